```python
import jax
import jax.numpy as jnp
from jax import lax
import numpy as np

D_MODEL = 1024
BATCH = 32
SEQ = 2048
DEPTH = 2

N_MIXERS = 2
MIX_WIDTH = D_MODEL
X_WIDTH = D_MODEL // 4
N_X_HEADS = 4
X_HEAD_DIM = X_WIDTH // N_X_HEADS
SEQ_MIX_WIDTH = MIX_WIDTH - X_WIDTH
LIN_HEAD_DIM = 128
N_LIN_HEADS = SEQ_MIX_WIDTH // LIN_HEAD_DIM
CONV_WIDTH = 4
CHUNK = 64
SB_HEAD_DIM = 64
N_SB_HEADS = SEQ_MIX_WIDTH // SB_HEAD_DIM
SB_BLOCK = 128
N_MEM = 256
D_FF = 4 * D_MODEL
EPS = 1e-6
N_A_LAYERS = (DEPTH + 1) // 2
N_B_LAYERS = DEPTH // 2
IN_A = 4 * SEQ_MIX_WIDTH + 2 * N_LIN_HEADS + X_WIDTH
IN_B = 3 * SEQ_MIX_WIDTH + X_WIDTH

kernel_name = 'hybrid_gdn_stickbreak_memory_trunk'


def rms_norm(x, g):
    xf = x.astype(jnp.float32)
    y = xf * lax.rsqrt(jnp.mean(xf * xf, axis=-1, keepdims=True) + EPS)
    return (y * g.astype(jnp.float32)).astype(x.dtype)


def l2norm(x):
    return x * lax.rsqrt(jnp.sum(x * x, axis=-1, keepdims=True) + EPS)


def causal_conv(x, w):
    c = x.shape[-1]
    return lax.conv_general_dilated(
        x, w[:, None, :].astype(x.dtype), window_strides=(1,),
        padding=[(CONV_WIDTH - 1, 0)], dimension_numbers=('NWC', 'WIO', 'NWC'),
        feature_group_count=c)


def gated_deltanet(p, conv_w, a_log, dt_bias, o_gain):
    B, S, _ = p.shape
    H, Dh, C, W = N_LIN_HEADS, LIN_HEAD_DIM, CHUNK, SEQ_MIX_WIDTH
    nc = S // C
    qkv = jax.nn.silu(causal_conv(p[..., :3 * W], conv_w)).astype(jnp.float32)
    gate = p[..., 3 * W:4 * W].astype(jnp.float32)
    beta = jax.nn.sigmoid(p[..., 4 * W:4 * W + H].astype(jnp.float32))
    g = -jnp.exp(a_log.astype(jnp.float32)) * jax.nn.softplus(
        p[..., 4 * W + H:].astype(jnp.float32) + dt_bias.astype(jnp.float32))
    q = l2norm(qkv[..., :W].reshape(B, S, H, Dh)) * (Dh ** -0.5)
    k = l2norm(qkv[..., W:2 * W].reshape(B, S, H, Dh))
    v = qkv[..., 2 * W:].reshape(B, S, H, Dh)

    def chunked(t):
        t = t.reshape((B, nc, C) + t.shape[2:])
        return jnp.moveaxis(jnp.moveaxis(t, 1, 0), 3, 2)

    q, k, v, beta = chunked(q), chunked(k), chunked(v), chunked(beta)
    gc = jnp.cumsum(chunked(g), axis=-1)
    idx = jnp.arange(C)
    causal = idx[:, None] >= idx[None, :]
    strict = idx[:, None] > idx[None, :]
    decay = jnp.exp(jnp.where(causal, gc[..., :, None] - gc[..., None, :], -jnp.inf))
    kb = k * beta[..., None]
    lower = jnp.where(strict, jnp.einsum('nbhcd,nbhsd->nbhcs', kb, k) * decay, 0.0)
    eye = jnp.eye(C, dtype=jnp.float32)
    rhs = jnp.concatenate([v * beta[..., None], kb * jnp.exp(gc)[..., None]], axis=-1)
    sol = lax.linalg.triangular_solve(eye + lower, rhs, left_side=True, lower=True,
                                      unit_diagonal=True)
    u, w = sol[..., :Dh], sol[..., Dh:]
    intra = jnp.einsum('nbhcd,nbhsd->nbhcs', q, k) * decay
    q_dec = q * jnp.exp(gc)[..., None]
    k_dec = k * jnp.exp(gc[..., -1:] - gc)[..., None]
    chunk_decay = jnp.exp(gc[..., -1])

    def step(state, inp):
        u_c, w_c, q_c, k_c, a_c, d_c = inp
        v_new = u_c - jnp.einsum('bhcd,bhde->bhce', w_c, state)
        o_c = jnp.einsum('bhcd,bhde->bhce', q_c, state) + jnp.einsum('bhcs,bhse->bhce', a_c, v_new)
        state = state * d_c[..., None, None] + jnp.einsum('bhcd,bhce->bhde', k_c, v_new)
        return state, o_c

    state0 = jnp.zeros((B, H, Dh, Dh), jnp.float32)
    _, o = lax.scan(step, state0, (u, w, q_dec, k_dec, intra, chunk_decay))
    o = jnp.swapaxes(jnp.moveaxis(o, 0, 1), 2, 3).reshape(B, S, H, Dh)
    o = o * lax.rsqrt(jnp.mean(o * o, axis=-1, keepdims=True) + EPS) * o_gain.astype(jnp.float32)
    o = o * jax.nn.silu(gate.reshape(B, S, H, Dh))
    return o.reshape(B, S, W)


def stick_breaking_attention(q, k, v):
    B, S, H, Dh = q.shape
    scale = Dh ** -0.5
    outs = []
    for blk in range(S // SB_BLOCK):
        t0 = blk * SB_BLOCK
        t1 = t0 + SB_BLOCK
        kb, vb = k[:, :t1], v[:, :t1]
        z = jnp.einsum('bthd,bshd->bhts', q[:, t0:t1], kb,
                       preferred_element_type=jnp.float32) * scale
        t_idx = t0 + jnp.arange(SB_BLOCK)[:, None]
        s_idx = jnp.arange(t1)[None, :]
        before = s_idx < t_idx
        log_beta = jax.nn.log_sigmoid(z)
        log_1m_beta = jnp.where(before, jax.nn.log_sigmoid(-z), 0.0)
        tail = lax.cumsum(log_1m_beta, axis=3, reverse=True) - log_1m_beta
        a = jnp.where(before, jnp.exp(log_beta + tail), 0.0)
        outs.append(jnp.einsum('bhts,bshd->bthd', a.astype(vb.dtype), vb))
    return jnp.concatenate(outs, axis=1).reshape(B, S, H * Dh)


def memory_attention(q, mem_kv):
    B, S, _ = q.shape
    q = q.reshape(B, S, N_X_HEADS, X_HEAD_DIM)
    k = mem_kv[..., :X_WIDTH].reshape(B, N_MEM, N_X_HEADS, X_HEAD_DIM)
    v = mem_kv[..., X_WIDTH:].reshape(B, N_MEM, N_X_HEADS, X_HEAD_DIM)
    s = jnp.einsum('bshd,bmhd->bhsm', q, k, preferred_element_type=jnp.float32) * (X_HEAD_DIM ** -0.5)
    p = jax.nn.softmax(s, axis=-1).astype(v.dtype)
    return jnp.einsum('bhsm,bmhd->bshd', p, v).reshape(B, S, X_WIDTH)


def _fwd_setup_inputs(seed: int = 0) -> dict:
    key = jax.random.key(seed)
    ks = jax.random.split(key, 18)
    f32 = jnp.float32
    nrm = lambda k, shape, scale: jax.random.normal(k, shape, f32) * scale
    gain = lambda k, shape: 1.0 + 0.1 * jax.random.normal(k, shape, f32)
    return {
        'x': nrm(ks[0], (BATCH, SEQ, D_MODEL), 1.0),
        'mem': nrm(ks[1], (BATCH, N_MEM, D_MODEL), 1.0),
        'mem_norm': gain(ks[2], (D_MODEL,)),
        'norm_pre_mix': gain(ks[3], (DEPTH, D_MODEL)),
        'norm_post_mix': gain(ks[4], (DEPTH, D_MODEL)),
        'norm_pre_mlp': gain(ks[5], (DEPTH, D_MODEL)),
        'norm_post_mlp': gain(ks[6], (DEPTH, D_MODEL)),
        'w_in_a': nrm(ks[7], (N_A_LAYERS, D_MODEL, IN_A), D_MODEL ** -0.5),
        'conv_w_a': nrm(ks[8], (N_A_LAYERS, CONV_WIDTH, 3 * SEQ_MIX_WIDTH), CONV_WIDTH ** -0.5),
        'a_log_a': jnp.log(jax.random.uniform(ks[9], (N_A_LAYERS, N_LIN_HEADS), f32, 0.01, 1.0)),
        'dt_bias_a': nrm(ks[10], (N_A_LAYERS, N_LIN_HEADS), 0.1),
        'onorm_a': gain(ks[11], (N_A_LAYERS, LIN_HEAD_DIM)),
        'w_in_b': nrm(ks[12], (N_B_LAYERS, D_MODEL, IN_B), D_MODEL ** -0.5),
        'w_mem_kv': nrm(ks[13], (DEPTH, D_MODEL, 2 * X_WIDTH), D_MODEL ** -0.5),
        'w_out': nrm(ks[14], (DEPTH, MIX_WIDTH, D_MODEL), MIX_WIDTH ** -0.5),
        'w_up': nrm(ks[15], (DEPTH, D_MODEL, D_FF), D_MODEL ** -0.5),
        'w_down': nrm(ks[16], (DEPTH, D_FF, D_MODEL), D_FF ** -0.5),
    }


def _fwd_reference(x, mem, mem_norm, norm_pre_mix, norm_post_mix, norm_pre_mlp, norm_post_mlp,
              w_in_a, conv_w_a, a_log_a, dt_bias_a, onorm_a, w_in_b, w_mem_kv, w_out,
              w_up, w_down):
    B, S, _ = x.shape
    mem_n = rms_norm(mem, mem_norm)
    for i in range(DEPTH):
        j = i // N_MIXERS
        h = rms_norm(x, norm_pre_mix[i])
        if i % N_MIXERS == 0:
            proj = h @ w_in_a[j]
            mix = gated_deltanet(proj[..., :IN_A - X_WIDTH], conv_w_a[j], a_log_a[j],
                                 dt_bias_a[j], onorm_a[j])
            mem_q = proj[..., IN_A - X_WIDTH:]
        else:
            proj = h @ w_in_b[j]
            W = SEQ_MIX_WIDTH
            q = proj[..., :W].reshape(B, S, N_SB_HEADS, SB_HEAD_DIM)
            k = proj[..., W:2 * W].reshape(B, S, N_SB_HEADS, SB_HEAD_DIM)
            v = proj[..., 2 * W:3 * W].reshape(B, S, N_SB_HEADS, SB_HEAD_DIM)
            mix = stick_breaking_attention(q, k, v)
            mem_q = proj[..., 3 * W:]
        cross = memory_attention(mem_q, mem_n @ w_mem_kv[i])
        y = jnp.concatenate([mix.astype(x.dtype), cross.astype(x.dtype)], axis=-1) @ w_out[i]
        x = x + rms_norm(y, norm_post_mix[i])
        h = rms_norm(x, norm_pre_mlp[i])
        y = jnp.square(jax.nn.relu(h @ w_up[i])) @ w_down[i]
        x = x + rms_norm(y, norm_post_mlp[i])
    return x


import jax as _jax
import jax.numpy as _jnp

TWIN_FORMAT = 'train_step'
FWD_PARAMS = ['x', 'mem', 'mem_norm', 'norm_pre_mix', 'norm_post_mix', 'norm_pre_mlp', 'norm_post_mlp', 'w_in_a', 'conv_w_a', 'a_log_a', 'dt_bias_a', 'onorm_a', 'w_in_b', 'w_mem_kv', 'w_out', 'w_up', 'w_down']
TWIN_WEIGHTS = ['mem_norm', 'norm_pre_mix', 'norm_post_mix', 'norm_pre_mlp', 'norm_post_mlp', 'w_in_a', 'conv_w_a', 'a_log_a', 'dt_bias_a', 'onorm_a', 'w_in_b', 'w_mem_kv', 'w_out', 'w_up', 'w_down']
TWIN_DIFF_INPUT = 'x'
TWIN_INPUTS = ['x', 'mem', 'mem_norm', 'norm_pre_mix', 'norm_post_mix', 'norm_pre_mlp', 'norm_post_mlp', 'w_in_a', 'conv_w_a', 'a_log_a', 'dt_bias_a', 'onorm_a', 'w_in_b', 'w_mem_kv', 'w_out', 'w_up', 'w_down', 'loss_target', 'm_mem_norm', 'm_norm_pre_mix', 'm_norm_post_mix', 'm_norm_pre_mlp', 'm_norm_post_mlp', 'm_w_in_a', 'm_conv_w_a', 'm_a_log_a', 'm_dt_bias_a', 'm_onorm_a', 'm_w_in_b', 'm_w_mem_kv', 'm_w_out', 'm_w_up', 'm_w_down', 'v_mem_norm', 'v_norm_pre_mix', 'v_norm_post_mix', 'v_norm_pre_mlp', 'v_norm_post_mlp', 'v_w_in_a', 'v_conv_w_a', 'v_a_log_a', 'v_dt_bias_a', 'v_onorm_a', 'v_w_in_b', 'v_w_mem_kv', 'v_w_out', 'v_w_up', 'v_w_down']
TWIN_OUTPUTS = ['loss', 'grad_x', 'grad_mem_norm', 'grad_norm_pre_mix', 'grad_norm_post_mix', 'grad_norm_pre_mlp', 'grad_norm_post_mlp', 'grad_w_in_a', 'grad_conv_w_a', 'grad_a_log_a', 'grad_dt_bias_a', 'grad_onorm_a', 'grad_w_in_b', 'grad_w_mem_kv', 'grad_w_out', 'grad_w_up', 'grad_w_down', 'delta_mem_norm', 'delta_norm_pre_mix', 'delta_norm_post_mix', 'delta_norm_pre_mlp', 'delta_norm_post_mlp', 'delta_w_in_a', 'delta_conv_w_a', 'delta_a_log_a', 'delta_dt_bias_a', 'delta_onorm_a', 'delta_w_in_b', 'delta_w_mem_kv', 'delta_w_out', 'delta_w_up', 'delta_w_down', 'new_m_mem_norm', 'new_m_norm_pre_mix', 'new_m_norm_post_mix', 'new_m_norm_pre_mlp', 'new_m_norm_post_mlp', 'new_m_w_in_a', 'new_m_conv_w_a', 'new_m_a_log_a', 'new_m_dt_bias_a', 'new_m_onorm_a', 'new_m_w_in_b', 'new_m_w_mem_kv', 'new_m_w_out', 'new_m_w_up', 'new_m_w_down', 'new_v_mem_norm', 'new_v_norm_pre_mix', 'new_v_norm_post_mix', 'new_v_norm_pre_mlp', 'new_v_norm_post_mlp', 'new_v_w_in_a', 'new_v_conv_w_a', 'new_v_a_log_a', 'new_v_dt_bias_a', 'new_v_onorm_a', 'new_v_w_in_b', 'new_v_w_mem_kv', 'new_v_w_out', 'new_v_w_up', 'new_v_w_down']
TWIN_LEAF_KINDS = {'loss': 'loss', 'grad_x': 'grad_x', 'grad_mem_norm': 'grad_w', 'grad_norm_pre_mix': 'grad_w', 'grad_norm_post_mix': 'grad_w', 'grad_norm_pre_mlp': 'grad_w', 'grad_norm_post_mlp': 'grad_w', 'grad_w_in_a': 'grad_w', 'grad_conv_w_a': 'grad_w', 'grad_a_log_a': 'grad_w', 'grad_dt_bias_a': 'grad_w', 'grad_onorm_a': 'grad_w', 'grad_w_in_b': 'grad_w', 'grad_w_mem_kv': 'grad_w', 'grad_w_out': 'grad_w', 'grad_w_up': 'grad_w', 'grad_w_down': 'grad_w', 'delta_mem_norm': 'delta_w', 'delta_norm_pre_mix': 'delta_w', 'delta_norm_post_mix': 'delta_w', 'delta_norm_pre_mlp': 'delta_w', 'delta_norm_post_mlp': 'delta_w', 'delta_w_in_a': 'delta_w', 'delta_conv_w_a': 'delta_w', 'delta_a_log_a': 'delta_w', 'delta_dt_bias_a': 'delta_w', 'delta_onorm_a': 'delta_w', 'delta_w_in_b': 'delta_w', 'delta_w_mem_kv': 'delta_w', 'delta_w_out': 'delta_w', 'delta_w_up': 'delta_w', 'delta_w_down': 'delta_w', 'new_m_mem_norm': 'new_m', 'new_m_norm_pre_mix': 'new_m', 'new_m_norm_post_mix': 'new_m', 'new_m_norm_pre_mlp': 'new_m', 'new_m_norm_post_mlp': 'new_m', 'new_m_w_in_a': 'new_m', 'new_m_conv_w_a': 'new_m', 'new_m_a_log_a': 'new_m', 'new_m_dt_bias_a': 'new_m', 'new_m_onorm_a': 'new_m', 'new_m_w_in_b': 'new_m', 'new_m_w_mem_kv': 'new_m', 'new_m_w_out': 'new_m', 'new_m_w_up': 'new_m', 'new_m_w_down': 'new_m', 'new_v_mem_norm': 'new_v', 'new_v_norm_pre_mix': 'new_v', 'new_v_norm_post_mix': 'new_v', 'new_v_norm_pre_mlp': 'new_v', 'new_v_norm_post_mlp': 'new_v', 'new_v_w_in_a': 'new_v', 'new_v_conv_w_a': 'new_v', 'new_v_a_log_a': 'new_v', 'new_v_dt_bias_a': 'new_v', 'new_v_onorm_a': 'new_v', 'new_v_w_in_b': 'new_v', 'new_v_w_mem_kv': 'new_v', 'new_v_w_out': 'new_v', 'new_v_w_up': 'new_v', 'new_v_w_down': 'new_v'}


def _forward(args):
    return _fwd_reference(*[args[k] for k in FWD_PARAMS])


def _output_shape():
    out = _jax.eval_shape(lambda: _forward(_fwd_setup_inputs(0)))
    return out.shape, out.dtype

N_MICROBATCH = 1
ADAM_LR = 0.001
ADAM_B1 = 0.9
ADAM_B2 = 0.999
ADAM_EPS = 1e-08
ADAM_WD = 0.01
ADAM_STEP = 10
PER_EXAMPLE_BATCH_AXIS = {'x': 0, 'mem': 0, 'loss_target': 0}
SHARED_INPUTS = []
_WEIGHT_DTYPES = {'mem_norm': _jnp.float32, 'norm_pre_mix': _jnp.float32, 'norm_post_mix': _jnp.float32, 'norm_pre_mlp': _jnp.float32, 'norm_post_mlp': _jnp.float32, 'w_in_a': _jnp.float32, 'conv_w_a': _jnp.float32, 'a_log_a': _jnp.float32, 'dt_bias_a': _jnp.float32, 'onorm_a': _jnp.float32, 'w_in_b': _jnp.float32, 'w_mem_kv': _jnp.float32, 'w_out': _jnp.float32, 'w_up': _jnp.float32, 'w_down': _jnp.float32}
MOMENT_SCALE = {'mem_norm': 2.055798e+00, 'norm_pre_mix': 9.039751e+00, 'norm_post_mix': 6.722050e+01, 'norm_pre_mlp': 6.518678e+00, 'norm_post_mlp': 7.044362e+01, 'w_in_a': 1.064124e+00, 'conv_w_a': 4.208733e+00, 'a_log_a': 3.731196e+01, 'dt_bias_a': 2.248029e+01, 'onorm_a': 3.193236e+01, 'w_in_b': 1.061096e+01, 'w_mem_kv': 2.175870e+00, 'w_out': 1.487127e+01, 'w_up': 3.174334e+00, 'w_down': 2.105424e+01}


def _to_microbatches(a, axis):
    t = _jnp.moveaxis(a, axis, 0)
    t = t.reshape((N_MICROBATCH, t.shape[0] // N_MICROBATCH) + t.shape[1:])
    return _jnp.moveaxis(t, 1, axis + 1)


def setup_inputs(seed: int = 0) -> dict:
    inp = _fwd_setup_inputs(seed)
    key = _jax.random.fold_in(_jax.random.key(seed), 7919)
    shape, _ = _output_shape()
    out = dict(inp)
    out["loss_target"] = _jax.random.normal(_jax.random.fold_in(key, 0), shape, _jnp.float32)
    for i, name in enumerate(TWIN_WEIGHTS):
        w = inp[name].astype(_jnp.float32)
        if MOMENT_SCALE is None:
            s = _jnp.sqrt(_jnp.mean(_jnp.square(w)) + 1e-30)
        else:
            s = MOMENT_SCALE[name]
        km, kv = _jax.random.split(_jax.random.fold_in(key, i + 1))
        out[name] = w
        out["m_" + name] = s * _jax.random.normal(km, w.shape, _jnp.float32)
        out["v_" + name] = (s * s) * _jax.random.uniform(kv, w.shape, _jnp.float32, 0.5, 1.5)
    if N_MICROBATCH > 1:
        for name, axis in PER_EXAMPLE_BATCH_AXIS.items():
            out[name] = _to_microbatches(out[name], axis)
    return {'x': out['x'], 'mem': out['mem'], 'mem_norm': out['mem_norm'], 'norm_pre_mix': out['norm_pre_mix'], 'norm_post_mix': out['norm_post_mix'], 'norm_pre_mlp': out['norm_pre_mlp'], 'norm_post_mlp': out['norm_post_mlp'], 'w_in_a': out['w_in_a'], 'conv_w_a': out['conv_w_a'], 'a_log_a': out['a_log_a'], 'dt_bias_a': out['dt_bias_a'], 'onorm_a': out['onorm_a'], 'w_in_b': out['w_in_b'], 'w_mem_kv': out['w_mem_kv'], 'w_out': out['w_out'], 'w_up': out['w_up'], 'w_down': out['w_down'], 'loss_target': out['loss_target'], 'm_mem_norm': out['m_mem_norm'], 'm_norm_pre_mix': out['m_norm_pre_mix'], 'm_norm_post_mix': out['m_norm_post_mix'], 'm_norm_pre_mlp': out['m_norm_pre_mlp'], 'm_norm_post_mlp': out['m_norm_post_mlp'], 'm_w_in_a': out['m_w_in_a'], 'm_conv_w_a': out['m_conv_w_a'], 'm_a_log_a': out['m_a_log_a'], 'm_dt_bias_a': out['m_dt_bias_a'], 'm_onorm_a': out['m_onorm_a'], 'm_w_in_b': out['m_w_in_b'], 'm_w_mem_kv': out['m_w_mem_kv'], 'm_w_out': out['m_w_out'], 'm_w_up': out['m_w_up'], 'm_w_down': out['m_w_down'], 'v_mem_norm': out['v_mem_norm'], 'v_norm_pre_mix': out['v_norm_pre_mix'], 'v_norm_post_mix': out['v_norm_post_mix'], 'v_norm_pre_mlp': out['v_norm_pre_mlp'], 'v_norm_post_mlp': out['v_norm_post_mlp'], 'v_w_in_a': out['v_w_in_a'], 'v_conv_w_a': out['v_conv_w_a'], 'v_a_log_a': out['v_a_log_a'], 'v_dt_bias_a': out['v_dt_bias_a'], 'v_onorm_a': out['v_onorm_a'], 'v_w_in_b': out['v_w_in_b'], 'v_w_mem_kv': out['v_w_mem_kv'], 'v_w_out': out['v_w_out'], 'v_w_up': out['v_w_up'], 'v_w_down': out['v_w_down']}


def _loss(weights, diff, rest, loss_target):
    with _jax.named_scope("forward"):
        args = {**rest, TWIN_DIFF_INPUT: diff, **{k: w.astype(_WEIGHT_DTYPES[k]) for k, w in weights.items()}}
        y = _forward(args)
    with _jax.named_scope("loss_head"):
        err = _jnp.square(y.astype(_jnp.float32) - loss_target)
        return 0.5 * _jnp.sum(_jnp.mean(err, axis=-1)) if err.ndim else 0.5 * err


def _adamw(w, g, m, v):
    m = ADAM_B1 * m + (1.0 - ADAM_B1) * g
    v = ADAM_B2 * v + (1.0 - ADAM_B2) * _jnp.square(g)
    m_hat = m / (1.0 - ADAM_B1 ** ADAM_STEP)
    v_hat = v / (1.0 - ADAM_B2 ** ADAM_STEP)
    delta = -ADAM_LR * (m_hat / (_jnp.sqrt(v_hat) + ADAM_EPS) + ADAM_WD * w)
    return delta, m, v


def reference(x, mem, mem_norm, norm_pre_mix, norm_post_mix, norm_pre_mlp, norm_post_mlp, w_in_a, conv_w_a, a_log_a, dt_bias_a, onorm_a, w_in_b, w_mem_kv, w_out, w_up, w_down, loss_target, m_mem_norm, m_norm_pre_mix, m_norm_post_mix, m_norm_pre_mlp, m_norm_post_mlp, m_w_in_a, m_conv_w_a, m_a_log_a, m_dt_bias_a, m_onorm_a, m_w_in_b, m_w_mem_kv, m_w_out, m_w_up, m_w_down, v_mem_norm, v_norm_pre_mix, v_norm_post_mix, v_norm_pre_mlp, v_norm_post_mlp, v_w_in_a, v_conv_w_a, v_a_log_a, v_dt_bias_a, v_onorm_a, v_w_in_b, v_w_mem_kv, v_w_out, v_w_up, v_w_down):
    given = dict(x=x, mem=mem, mem_norm=mem_norm, norm_pre_mix=norm_pre_mix, norm_post_mix=norm_post_mix, norm_pre_mlp=norm_pre_mlp, norm_post_mlp=norm_post_mlp, w_in_a=w_in_a, conv_w_a=conv_w_a, a_log_a=a_log_a, dt_bias_a=dt_bias_a, onorm_a=onorm_a, w_in_b=w_in_b, w_mem_kv=w_mem_kv, w_out=w_out, w_up=w_up, w_down=w_down, loss_target=loss_target, m_mem_norm=m_mem_norm, m_norm_pre_mix=m_norm_pre_mix, m_norm_post_mix=m_norm_post_mix, m_norm_pre_mlp=m_norm_pre_mlp, m_norm_post_mlp=m_norm_post_mlp, m_w_in_a=m_w_in_a, m_conv_w_a=m_conv_w_a, m_a_log_a=m_a_log_a, m_dt_bias_a=m_dt_bias_a, m_onorm_a=m_onorm_a, m_w_in_b=m_w_in_b, m_w_mem_kv=m_w_mem_kv, m_w_out=m_w_out, m_w_up=m_w_up, m_w_down=m_w_down, v_mem_norm=v_mem_norm, v_norm_pre_mix=v_norm_pre_mix, v_norm_post_mix=v_norm_post_mix, v_norm_pre_mlp=v_norm_pre_mlp, v_norm_post_mlp=v_norm_post_mlp, v_w_in_a=v_w_in_a, v_conv_w_a=v_conv_w_a, v_a_log_a=v_a_log_a, v_dt_bias_a=v_dt_bias_a, v_onorm_a=v_onorm_a, v_w_in_b=v_w_in_b, v_w_mem_kv=v_w_mem_kv, v_w_out=v_w_out, v_w_up=v_w_up, v_w_down=v_w_down)
    weights = {n: given[n] for n in TWIN_WEIGHTS}
    shared = {n: given[n] for n in SHARED_INPUTS}
    per_example = {n: given[n] for n in ['x', 'mem']}
    grad_fn = _jax.value_and_grad(_loss, argnums=(0, 1))

    def one_microbatch(ex, loss_target):
        ex = dict(ex)
        diff = ex.pop(TWIN_DIFF_INPUT)
        return grad_fn(weights, diff, {**shared, **ex}, loss_target)

    if N_MICROBATCH == 1:
        loss, (grad_w, grad_x) = one_microbatch(per_example, given["loss_target"])
    else:
        def body(carry, xs):
            loss_sum, grad_sum = carry
            l_k, (gw_k, gx_k) = one_microbatch(xs[0], xs[1])
            with _jax.named_scope("update"):
                return (loss_sum + l_k, _jax.tree.map(_jnp.add, grad_sum, gw_k)), gx_k

        init = (_jnp.zeros((), _jnp.float32), _jax.tree.map(_jnp.zeros_like, weights))
        (loss, grad_w), grad_x = _jax.lax.scan(body, init, (per_example, given["loss_target"]))
    with _jax.named_scope("update"):
        delta_w, new_m, new_v = {}, {}, {}
        for n in TWIN_WEIGHTS:
            delta_w[n], new_m[n], new_v[n] = _adamw(weights[n], grad_w[n], given["m_" + n], given["v_" + n])
    return (loss, grad_x, *[grad_w[n] for n in TWIN_WEIGHTS], *[delta_w[n] for n in TWIN_WEIGHTS],
            *[new_m[n] for n in TWIN_WEIGHTS], *[new_v[n] for n in TWIN_WEIGHTS])
```

```python
import functools
import math

import jax
import jax.numpy as jnp
from jax import lax
from jax.experimental import pallas as pl
from jax.experimental.pallas import tpu as pltpu

F32 = jnp.float32
_MXU_DTYPE = jnp.bfloat16
_WIRE_DTYPE = jnp.bfloat16
_HI = lax.Precision.HIGHEST

D_MODEL = 1024
N_DEV = 8
N_MEM = 256
X_WIDTH = 256
N_X_HEADS = 4
X_HEAD_DIM = 64
SEQ_MIX_WIDTH = 768
LIN_HEAD_DIM = 128
N_LIN_HEADS = 6
CONV_WIDTH = 4
CHUNK = 64
SB_HEAD_DIM = 64
SB_BLOCK = 128
D_FF = 4096
EPS = 1e-6
IN_A = 3340
IN_A_PAD = 3584
IN_B = 2560
SM_COL = 3328

ADAM_LR = 0.001
ADAM_B1 = 0.9
ADAM_B2 = 0.999
ADAM_EPS = 1e-08
ADAM_WD = 0.01
ADAM_STEP = 10

LANE = 128
VMEM_LIMIT = 56 * 1024 * 1024


def _cp(sem=None):
    return pltpu.CompilerParams(dimension_semantics=sem, vmem_limit_bytes=VMEM_LIMIT)


def _pick(n, target):
    if n <= target:
        return n
    best = None
    for t in range(LANE, target + 1, LANE):
        if n % t == 0:
            best = t
    assert best is not None, (n, target)
    return best


def _dot(a, b, ca=1, cb=0):
    return lax.dot_general(a.astype(_MXU_DTYPE), b.astype(_MXU_DTYPE), (((ca,), (cb,)), ((), ())),
                           preferred_element_type=F32)


def _bdot(a, b, ca, cb):
    return lax.dot_general(a.astype(_MXU_DTYPE), b.astype(_MXU_DTYPE), (((ca,), (cb,)), ((0,), (0,))),
                           preferred_element_type=F32)


def _bdot_hi(a, b):
    return lax.dot_general(a, b, (((2,), (1,)), ((0,), (0,))), precision=_HI, preferred_element_type=F32)


def _sigmoid(x):
    return 1.0 / (1.0 + jnp.exp(-x))


def _silu(x):
    return x * _sigmoid(x)


def _softplus(x):
    return jnp.maximum(x, 0.0) + jnp.log(1.0 + jnp.exp(-jnp.abs(x)))


def _rms(x, g):
    return x * lax.rsqrt(jnp.mean(x * x, axis=-1, keepdims=True) + EPS) * g


def _norm_fwd(x, g, resid=None, out_dtype=F32, name="norm_fwd"):
    T, D = x.shape
    tm = _pick(T, 512)
    has_resid = resid is not None

    def body(*refs):
        if has_resid:
            x_ref, g_ref, r_ref, o_ref = refs
        else:
            x_ref, g_ref, o_ref = refs
        y = _rms(x_ref[...].astype(F32), g_ref[...])
        if has_resid:
            y = r_ref[...] + y
        o_ref[...] = y.astype(out_dtype)

    row = pl.BlockSpec((tm, D), lambda i: (i, 0))
    in_specs = [row, pl.BlockSpec((1, D), lambda i: (0, 0))] + ([row] if has_resid else [])
    args = (x, g) + ((resid,) if has_resid else ())
    return pl.pallas_call(body, grid=(T // tm,), in_specs=in_specs, out_specs=row,
                          out_shape=jax.ShapeDtypeStruct((T, D), out_dtype),
                          compiler_params=_cp(("parallel",)), name=name)(*args)


def _norm_bwd(dy, x, g, resid=None, name="norm_bwd"):
    T, D = x.shape
    tm = _pick(T, 512)
    has_resid = resid is not None

    def body(*refs):
        if has_resid:
            dy_ref, x_ref, g_ref, r_ref, dx_ref, dg_ref = refs
        else:
            dy_ref, x_ref, g_ref, dx_ref, dg_ref = refs
        _, vjp = jax.vjp(_rms, x_ref[...].astype(F32), g_ref[...])
        dx, dg = vjp(dy_ref[...].astype(F32))
        if has_resid:
            dx = r_ref[...] + dx
        dx_ref[...] = dx

        @pl.when(pl.program_id(0) == 0)
        def _():
            dg_ref[...] = jnp.zeros_like(dg_ref)

        dg_ref[...] += dg

    row = pl.BlockSpec((tm, D), lambda i: (i, 0))
    vec = pl.BlockSpec((1, D), lambda i: (0, 0))
    in_specs = [row, row, vec] + ([row] if has_resid else [])
    args = (dy, x, g) + ((resid,) if has_resid else ())
    return pl.pallas_call(body, grid=(T // tm,), in_specs=in_specs, out_specs=(row, vec),
                          out_shape=(jax.ShapeDtypeStruct((T, D), F32), jax.ShapeDtypeStruct((1, D), F32)),
                          compiler_params=_cp(("arbitrary",)), name=name)(*args)


def _mm(a, b, *, ta=False, tb=False, out_dtypes=(F32,), epilogue=None, extras=(), name="mm",
        tm_t=1024, tn_t=1024, tk_t=512):
    M, K = (a.shape[1], a.shape[0]) if ta else a.shape
    N = b.shape[0] if tb else b.shape[1]
    assert (b.shape[1] if tb else b.shape[0]) == K, (a.shape, b.shape, ta, tb)
    tm, tn, tk = _pick(M, tm_t), _pick(N, tn_t), _pick(K, tk_t)
    nk = K // tk
    n_extra = len(extras)
    n_out = len(out_dtypes)

    def body(*refs):
        a_ref, b_ref = refs[0], refs[1]
        e_refs = refs[2:2 + n_extra]
        o_refs = refs[2 + n_extra:2 + n_extra + n_out]
        acc_ref = refs[-1]
        k = pl.program_id(2)

        @pl.when(k == 0)
        def _():
            acc_ref[...] = jnp.zeros_like(acc_ref)

        acc_ref[...] += _dot(a_ref[...], b_ref[...], 0 if ta else 1, 1 if tb else 0)

        @pl.when(k == nk - 1)
        def _():
            acc = acc_ref[...]
            outs = (acc,) if epilogue is None else epilogue(acc, *[e[...] for e in e_refs])
            for o_ref, o in zip(o_refs, outs):
                o_ref[...] = o.astype(o_ref.dtype)

    a_spec = pl.BlockSpec((tk, tm), lambda i, j, k: (k, i)) if ta else pl.BlockSpec((tm, tk), lambda i, j, k: (i, k))
    b_spec = pl.BlockSpec((tn, tk), lambda i, j, k: (j, k)) if tb else pl.BlockSpec((tk, tn), lambda i, j, k: (k, j))
    o_spec = pl.BlockSpec((tm, tn), lambda i, j, k: (i, j))
    outs = pl.pallas_call(
        body, grid=(M // tm, N // tn, nk),
        in_specs=[a_spec, b_spec] + [o_spec] * n_extra,
        out_specs=tuple([o_spec] * n_out),
        out_shape=tuple(jax.ShapeDtypeStruct((M, N), dt) for dt in out_dtypes),
        scratch_shapes=[pltpu.VMEM((tm, tn), F32)],
        compiler_params=_cp(("parallel", "parallel", "arbitrary")), name=name)(a, b, *extras)
    return outs[0] if n_out == 1 else outs


def _relu2_epilogue(acc):
    r = jnp.maximum(acc, 0.0)
    return r * r, r


def _drelu2_epilogue(acc, r):
    return (acc * (2.0 * r.astype(F32)),)


def _loss_head(x, target, name="loss_head"):
    T, D = x.shape
    tm = _pick(T, 512)

    def body(x_ref, t_ref, l_ref, dx_ref):
        e = x_ref[...] - t_ref[...]
        dx_ref[...] = e * (1.0 / D)

        @pl.when(pl.program_id(0) == 0)
        def _():
            l_ref[...] = jnp.zeros_like(l_ref)

        part = 0.5 * jnp.sum(jnp.mean(e * e, axis=-1, keepdims=True), axis=0, keepdims=True)
        l_ref[...] += jnp.broadcast_to(part, l_ref.shape)

    row = pl.BlockSpec((tm, D), lambda i: (i, 0))
    return pl.pallas_call(body, grid=(T // tm,), in_specs=[row, row],
                          out_specs=(pl.BlockSpec((1, LANE), lambda i: (0, 0)), row),
                          out_shape=(jax.ShapeDtypeStruct((1, LANE), F32), jax.ShapeDtypeStruct((T, D), F32)),
                          compiler_params=_cp(("arbitrary",)), name=name)(x, target)


def _shift_down(x, k, row):
    return jnp.where(row >= k, pltpu.roll(x, k, 0), 0.0)


def _shift_up(x, k, row, n):
    return jnp.where(row < n - k, pltpu.roll(x, n - k, 0), 0.0)


def _conv_taps(x, w, row):
    y = x * w[CONV_WIDTH - 1:CONV_WIDTH, :]
    for i in range(CONV_WIDTH - 1):
        y = y + _shift_down(x, CONV_WIDTH - 1 - i, row) * w[i:i + 1, :]
    return y


def _qkv_act(xc, j):
    s = _silu(xc)
    n = s * lax.rsqrt(jnp.sum(s * s, axis=-1, keepdims=True) + EPS)
    n = n * jnp.where(j < N_LIN_HEADS, LIN_HEAD_DIM ** -0.5, 1.0)
    return jnp.where(j < 2 * N_LIN_HEADS, n, s)


def _gdn_conv_fwd(proj, conv_w, B, S):
    nblk = 3 * N_LIN_HEADS

    def body(p_ref, w_ref, o_ref):
        j = pl.program_id(1)
        x = p_ref[...]
        row = lax.broadcasted_iota(jnp.int32, x.shape, 0)
        o_ref[...] = _qkv_act(_conv_taps(x, w_ref[...], row), j)

    blk = pl.BlockSpec((S, LANE), lambda b, j: (b, j))
    return pl.pallas_call(body, grid=(B, nblk),
                          in_specs=[blk, pl.BlockSpec((CONV_WIDTH, LANE), lambda b, j: (0, j))],
                          out_specs=blk, out_shape=jax.ShapeDtypeStruct((B * S, nblk * LANE), F32),
                          compiler_params=_cp(("parallel", "parallel")), name="gdn_conv_fwd")(proj, conv_w)


def _gdn_conv_bwd(dq, dk, dv, proj, conv_w, B, S):
    nblk = 3 * N_LIN_HEADS
    H = N_LIN_HEADS

    def body(dq_ref, dk_ref, dv_ref, p_ref, w_ref, dp_ref, dw_ref):
        j = pl.program_id(0)
        b = pl.program_id(1)
        x = p_ref[...]
        w = w_ref[...]
        row = lax.broadcasted_iota(jnp.int32, x.shape, 0)
        d_act = jnp.where(j < H, dq_ref[...], jnp.where(j < 2 * H, dk_ref[...], dv_ref[...]))
        _, vjp = jax.vjp(lambda t: _qkv_act(t, j), _conv_taps(x, w, row))
        (d_xc,) = vjp(d_act)
        dx = d_xc * w[CONV_WIDTH - 1:CONV_WIDTH, :]
        for i in range(CONV_WIDTH - 1):
            dx = dx + _shift_up(d_xc, CONV_WIDTH - 1 - i, row, S) * w[i:i + 1, :]
        dp_ref[...] = dx.astype(dp_ref.dtype)

        @pl.when(b == 0)
        def _():
            dw_ref[...] = jnp.zeros_like(dw_ref)

        for i in range(CONV_WIDTH):
            xs = x if i == CONV_WIDTH - 1 else _shift_down(x, CONV_WIDTH - 1 - i, row)
            dw_ref[i:i + 1, :] += jnp.sum(d_xc * xs, axis=0, keepdims=True)

    blk = pl.BlockSpec((S, LANE), lambda j, b: (b, j))
    wblk = pl.BlockSpec((CONV_WIDTH, LANE), lambda j, b: (0, j))
    return pl.pallas_call(
        body, grid=(nblk, B),
        in_specs=[pl.BlockSpec((S, LANE), lambda j, b: (b, jnp.clip(j, 0, H - 1))),
                  pl.BlockSpec((S, LANE), lambda j, b: (b, jnp.clip(j - H, 0, H - 1))),
                  pl.BlockSpec((S, LANE), lambda j, b: (b, jnp.clip(j - 2 * H, 0, H - 1))),
                  blk, wblk],
        out_specs=(blk, wblk),
        out_shape=(jax.ShapeDtypeStruct((B * S, nblk * LANE), _MXU_DTYPE),
                   jax.ShapeDtypeStruct((CONV_WIDTH, nblk * LANE), F32)),
        compiler_params=_cp(("parallel", "arbitrary")), name="gdn_conv_bwd")(dq, dk, dv, proj, conv_w)


def _chunk_cumsum(x, row):
    pos = row % CHUNK
    k = 1
    while k < CHUNK:
        x = x + jnp.where(pos >= k, pltpu.roll(x, k, 0), 0.0)
        k *= 2
    return x


def _chunk_rev_cumsum(x, row, n):
    pos = row % CHUNK
    k = 1
    while k < CHUNK:
        x = x + jnp.where(pos < CHUNK - k, pltpu.roll(x, n - k, 0), 0.0)
        k *= 2
    return x


def _gdn_gates_fwd(proj, a_log, dt_bias, B, S):
    H = N_LIN_HEADS

    def body(sm_ref, al_ref, dt_ref, beta_ref, gc_ref):
        sm = sm_ref[...]
        row = lax.broadcasted_iota(jnp.int32, (S, LANE), 0)
        for h in range(H):
            beta = _sigmoid(sm[:, h:h + 1])
            g = -jnp.exp(al_ref[0:1, h:h + 1]) * _softplus(sm[:, H + h:H + h + 1] + dt_ref[0:1, h:h + 1])
            beta_ref[:, h * LANE:(h + 1) * LANE] = jnp.broadcast_to(beta, (S, LANE))
            gc_ref[:, h * LANE:(h + 1) * LANE] = _chunk_cumsum(jnp.broadcast_to(g, (S, LANE)), row)

    vec = pl.BlockSpec((1, LANE), lambda b: (0, 0))
    wide = pl.BlockSpec((S, H * LANE), lambda b: (b, 0))
    return pl.pallas_call(body, grid=(B,),
                          in_specs=[pl.BlockSpec((S, LANE), lambda b: (b, SM_COL // LANE)), vec, vec],
                          out_specs=(wide, wide),
                          out_shape=(jax.ShapeDtypeStruct((B * S, H * LANE), F32),) * 2,
                          compiler_params=_cp(("parallel",)), name="gdn_gates_fwd")(proj, a_log, dt_bias)


def _gdn_gates_bwd(d_beta, d_gc, proj, a_log, dt_bias, B, S):
    H = N_LIN_HEADS

    def body(db_ref, dgc_ref, sm_ref, al_ref, dt_ref, dsm_ref, dal_ref, ddt_ref):
        sm = sm_ref[...]
        row = lax.broadcasted_iota(jnp.int32, (S, LANE), 0)
        lane = lax.broadcasted_iota(jnp.int32, (1, LANE), 1)
        dsm = jnp.zeros((S, LANE), F32)
        dal = jnp.zeros((1, LANE), F32)
        ddt = jnp.zeros((1, LANE), F32)
        for h in range(H):
            beta = _sigmoid(sm[:, h:h + 1])
            dbeta = jnp.sum(db_ref[:, h * LANE:(h + 1) * LANE], axis=-1, keepdims=True)
            d_bl = dbeta * beta * (1.0 - beta)
            dgc = jnp.sum(dgc_ref[:, h * LANE:(h + 1) * LANE], axis=-1, keepdims=True)
            dg = _chunk_rev_cumsum(jnp.broadcast_to(dgc, (S, LANE)), row, S)[:, 0:1]
            z = sm[:, H + h:H + h + 1] + dt_ref[0:1, h:h + 1]
            a = jnp.exp(al_ref[0:1, h:h + 1])
            g = -a * _softplus(z)
            d_al = dg * (-a) * _sigmoid(z)
            dsm = dsm + jnp.where(lane == h, d_bl, 0.0) + jnp.where(lane == H + h, d_al, 0.0)
            ddt = ddt + jnp.where(lane == h, jnp.sum(d_al, axis=0, keepdims=True), 0.0)
            dal = dal + jnp.where(lane == h, jnp.sum(dg * g, axis=0, keepdims=True), 0.0)
        dsm_ref[...] = dsm.astype(dsm_ref.dtype)

        @pl.when(pl.program_id(0) == 0)
        def _():
            dal_ref[...] = jnp.zeros_like(dal_ref)
            ddt_ref[...] = jnp.zeros_like(ddt_ref)

        dal_ref[...] += dal
        ddt_ref[...] += ddt

    vec = pl.BlockSpec((1, LANE), lambda b: (0, 0))
    wide = pl.BlockSpec((S, H * LANE), lambda b: (b, 0))
    return pl.pallas_call(body, grid=(B,),
                          in_specs=[wide, wide, pl.BlockSpec((S, LANE), lambda b: (b, SM_COL // LANE)), vec, vec],
                          out_specs=(pl.BlockSpec((S, LANE), lambda b: (b, 0)), vec, vec),
                          out_shape=(jax.ShapeDtypeStruct((B * S, LANE), _MXU_DTYPE),
                                     jax.ShapeDtypeStruct((1, LANE), F32), jax.ShapeDtypeStruct((1, LANE), F32)),
                          compiler_params=_cp(("arbitrary",)), name="gdn_gates_bwd")(d_beta, d_gc, proj, a_log, dt_bias)


PREP_ROWS = 512


def _prep_fn(q, k, v, beta, gc):
    R = q.shape[0]
    n = R // CHUNK
    q3, k3, v3, b3, g3 = [t.reshape(n, CHUNK, LIN_HEAD_DIM) for t in (q, k, v, beta, gc)]
    ri = lax.broadcasted_iota(jnp.int32, (CHUNK, CHUNK), 0)
    ci = lax.broadcasted_iota(jnp.int32, (CHUNK, CHUNK), 1)
    causal = (ri >= ci)[None]
    strict = (ri > ci)[None]
    gcol = g3[:, :, 0:1]
    grow = jnp.swapaxes(g3, 1, 2)[:, 0:1, :]
    decay = jnp.exp(jnp.where(causal, gcol - grow, -1e30))
    kb = k3 * b3
    lower = jnp.where(strict, _bdot(kb, k3, 2, 2) * decay, 0.0)
    p = -lower
    inv = jnp.where((ri == ci)[None], 1.0, 0.0) + p
    for _ in range(int(math.log2(CHUNK)) - 1):
        p = _bdot_hi(p, p)
        inv = inv + _bdot_hi(inv, p)
    eg = jnp.exp(g3)
    u = _bdot_hi(inv, v3 * b3)
    w = _bdot_hi(inv, kb * eg)
    intra = _bdot(q3, k3, 2, 2) * decay
    q_dec = q3 * eg
    k_dec = k3 * jnp.exp(g3[:, CHUNK - 1:CHUNK, :] - g3)
    return (u.reshape(R, LIN_HEAD_DIM), w.reshape(R, LIN_HEAD_DIM), q_dec.reshape(R, LIN_HEAD_DIM),
            k_dec.reshape(R, LIN_HEAD_DIM), intra.reshape(R, CHUNK))


def _prep_specs(S):
    H = N_LIN_HEADS
    R = min(PREP_ROWS, S)
    nr = S // R

    def col(off):
        return pl.BlockSpec((R, LANE), lambda b, h, r: (b * nr + r, off + h))

    intra = pl.BlockSpec((None, R, CHUNK), lambda b, h, r: (h, b * nr + r, 0))
    return R, nr, col, intra


def _gdn_prep_fwd(act, beta, gc, B, S):
    H = N_LIN_HEADS
    R, nr, col, intra_spec = _prep_specs(S)
    T = B * S

    def body(q_ref, k_ref, v_ref, b_ref, g_ref, u_ref, w_ref, qd_ref, kd_ref, a_ref):
        u, w, qd, kd, a = _prep_fn(q_ref[...], k_ref[...], v_ref[...], b_ref[...], g_ref[...])
        u_ref[...] = u
        w_ref[...] = w
        qd_ref[...] = qd
        kd_ref[...] = kd
        a_ref[...] = a

    wide = jax.ShapeDtypeStruct((T, H * LANE), F32)
    return pl.pallas_call(body, grid=(B, H, nr),
                          in_specs=[col(0), col(H), col(2 * H), col(0), col(0)],
                          out_specs=(col(0), col(0), col(0), col(0), intra_spec),
                          out_shape=(wide, wide, wide, wide, jax.ShapeDtypeStruct((H, T, CHUNK), F32)),
                          compiler_params=_cp(("parallel", "parallel", "parallel")),
                          name="gdn_prep_fwd")(act, act, act, beta, gc)


def _gdn_prep_bwd(act, beta, gc, du, dw, dqd, dkd, da, B, S):
    H = N_LIN_HEADS
    R, nr, col, intra_spec = _prep_specs(S)
    T = B * S

    def body(q_ref, k_ref, v_ref, b_ref, g_ref, du_ref, dw_ref, dqd_ref, dkd_ref, da_ref,
             dq_ref, dk_ref, dv_ref, db_ref, dg_ref):
        _, vjp = jax.vjp(_prep_fn, q_ref[...], k_ref[...], v_ref[...], b_ref[...], g_ref[...])
        dq, dk, dv, db, dg = vjp((du_ref[...], dw_ref[...], dqd_ref[...], dkd_ref[...], da_ref[...]))
        dq_ref[...] = dq
        dk_ref[...] = dk
        dv_ref[...] = dv
        db_ref[...] = db
        dg_ref[...] = dg

    wide = jax.ShapeDtypeStruct((T, H * LANE), F32)
    return pl.pallas_call(body, grid=(B, H, nr),
                          in_specs=[col(0), col(H), col(2 * H), col(0), col(0),
                                    col(0), col(0), col(0), col(0), intra_spec],
                          out_specs=(col(0),) * 5, out_shape=(wide,) * 5,
                          compiler_params=_cp(("parallel", "parallel", "parallel")),
                          name="gdn_prep_bwd")(act, act, act, beta, gc, du, dw, dqd, dkd, da)


def _scan_step(u, w, qd, kd, a, g_last, state):
    v_new = u - _dot(w, state)
    o = _dot(qd, state) + _dot(a, v_new)
    new_state = state * jnp.exp(g_last) + _dot(kd, v_new, 0, 0)
    return o, new_state


def _scan_specs(B, S):
    H = N_LIN_HEADS
    n = S // CHUNK
    col = pl.BlockSpec((S, LANE), lambda b, h: (b, h))
    intra = pl.BlockSpec((None, S, CHUNK), lambda b, h: (h, b, 0))
    st = pl.BlockSpec((None, None, n, LIN_HEAD_DIM, LIN_HEAD_DIM), lambda b, h: (b, h, 0, 0, 0))
    return n, col, intra, st


def _gdn_scan_fwd(u, w, qd, kd, a, gc, B, S):
    H = N_LIN_HEADS
    n, col, intra, st = _scan_specs(B, S)

    def body(u_ref, w_ref, qd_ref, kd_ref, a_ref, g_ref, o_ref, st_ref):
        def step(c, state):
            rows = pl.ds(pl.multiple_of(c * CHUNK, CHUNK), CHUNK)
            st_ref[c] = state.astype(st_ref.dtype)
            o, new_state = _scan_step(u_ref[rows, :], w_ref[rows, :], qd_ref[rows, :], kd_ref[rows, :],
                                      a_ref[rows, :], g_ref[rows, :][CHUNK - 1:CHUNK, :], state)
            o_ref[rows, :] = o
            return new_state

        lax.fori_loop(0, n, step, jnp.zeros((LIN_HEAD_DIM, LIN_HEAD_DIM), F32))

    return pl.pallas_call(body, grid=(B, H), in_specs=[col, col, col, col, intra, col],
                          out_specs=(col, st),
                          out_shape=(jax.ShapeDtypeStruct((B * S, H * LANE), F32),
                                     jax.ShapeDtypeStruct((B, H, n, LIN_HEAD_DIM, LIN_HEAD_DIM), _MXU_DTYPE)),
                          compiler_params=_cp(("parallel", "parallel")), name="gdn_scan_fwd")(u, w, qd, kd, a, gc)


def _gdn_scan_bwd(u, w, qd, kd, a, gc, states, do, B, S):
    H = N_LIN_HEADS
    n, col, intra, st = _scan_specs(B, S)
    T = B * S

    def body(u_ref, w_ref, qd_ref, kd_ref, a_ref, g_ref, st_ref, do_ref,
             du_ref, dw_ref, dqd_ref, dkd_ref, da_ref, dg_ref):
        last = lax.broadcasted_iota(jnp.int32, (CHUNK, LANE), 0) == CHUNK - 1

        def step(i, d_state):
            c = n - 1 - i
            rows = pl.ds(pl.multiple_of(c * CHUNK, CHUNK), CHUNK)
            _, vjp = jax.vjp(_scan_step, u_ref[rows, :], w_ref[rows, :], qd_ref[rows, :], kd_ref[rows, :],
                             a_ref[rows, :], g_ref[rows, :][CHUNK - 1:CHUNK, :], st_ref[c].astype(F32))
            du, dw, dqd, dkd, da, dgl, d_prev = vjp((do_ref[rows, :].astype(F32), d_state))
            du_ref[rows, :] = du
            dw_ref[rows, :] = dw
            dqd_ref[rows, :] = dqd
            dkd_ref[rows, :] = dkd
            da_ref[rows, :] = da
            dg_ref[rows, :] = jnp.where(last, dgl, 0.0)
            return d_prev

        lax.fori_loop(0, n, step, jnp.zeros((LIN_HEAD_DIM, LIN_HEAD_DIM), F32))

    wide = jax.ShapeDtypeStruct((T, H * LANE), F32)
    return pl.pallas_call(body, grid=(B, H), in_specs=[col, col, col, col, intra, col, st, col],
                          out_specs=(col, col, col, col, intra, col),
                          out_shape=(wide, wide, wide, wide, jax.ShapeDtypeStruct((H, T, CHUNK), F32), wide),
                          compiler_params=_cp(("parallel", "parallel")),
                          name="gdn_scan_bwd")(u, w, qd, kd, a, gc, states, do)


GATE_COL = 3 * SEQ_MIX_WIDTH


def _post_fn(o, gate, gain):
    return o * lax.rsqrt(jnp.mean(o * o, axis=-1, keepdims=True) + EPS) * gain * _silu(gate)


def _gdn_post_fwd(o, proj, onorm, T):
    H = N_LIN_HEADS
    tm = _pick(T, 1024)

    def body(o_ref, g_ref, n_ref, y_ref):
        y_ref[...] = _post_fn(o_ref[...], g_ref[...], n_ref[...]).astype(y_ref.dtype)

    col = pl.BlockSpec((tm, LANE), lambda i, h: (i, h))
    return pl.pallas_call(body, grid=(T // tm, H),
                          in_specs=[col, pl.BlockSpec((tm, LANE), lambda i, h: (i, GATE_COL // LANE + h)),
                                    pl.BlockSpec((1, LANE), lambda i, h: (0, 0))],
                          out_specs=col, out_shape=jax.ShapeDtypeStruct((T, H * LANE), _MXU_DTYPE),
                          compiler_params=_cp(("parallel", "parallel")), name="gdn_post_fwd")(o, proj, onorm)


def _gdn_post_bwd(d_cat, o, proj, onorm, T):
    H = N_LIN_HEADS
    tm = _pick(T, 1024)

    def body(dy_ref, o_ref, g_ref, n_ref, do_ref, dg_ref, dn_ref):
        _, vjp = jax.vjp(_post_fn, o_ref[...], g_ref[...], n_ref[...])
        do, dg, dn = vjp(dy_ref[...].astype(F32))
        do_ref[...] = do
        dg_ref[...] = dg.astype(dg_ref.dtype)

        @pl.when((pl.program_id(0) == 0) & (pl.program_id(1) == 0))
        def _():
            dn_ref[...] = jnp.zeros_like(dn_ref)

        dn_ref[...] += dn

    col = pl.BlockSpec((tm, LANE), lambda i, h: (i, h))
    vec = pl.BlockSpec((1, LANE), lambda i, h: (0, 0))
    return pl.pallas_call(body, grid=(T // tm, H),
                          in_specs=[col, col, pl.BlockSpec((tm, LANE), lambda i, h: (i, GATE_COL // LANE + h)), vec],
                          out_specs=(col, col, vec),
                          out_shape=(jax.ShapeDtypeStruct((T, H * LANE), F32),
                                     jax.ShapeDtypeStruct((T, H * LANE), _MXU_DTYPE),
                                     jax.ShapeDtypeStruct((1, LANE), F32)),
                          compiler_params=_cp(("arbitrary", "arbitrary")), name="gdn_post_bwd")(d_cat, o, proj, onorm)


def _log_sigmoid(z):
    return jnp.minimum(z, 0.0) - jnp.log(1.0 + jnp.exp(-jnp.abs(z)))


def _split_dot(x, m):
    hi = x.astype(_MXU_DTYPE)
    lo = x - hi.astype(F32)
    return _dot(hi, m) + _dot(lo, m)


def _sb_scores(q_m, k_j, scale, diag, tri):
    z = _dot(q_m, k_j, 1, 1) * scale
    lb = _log_sigmoid(z)
    valid = tri["col"] < tri["row"] + jnp.where(diag, 0, SB_BLOCK)
    return lb, jnp.where(valid, lb - z, 0.0), valid


def _sb_tri():
    ri = lax.broadcasted_iota(jnp.int32, (SB_BLOCK, SB_BLOCK), 0)
    ci = lax.broadcasted_iota(jnp.int32, (SB_BLOCK, SB_BLOCK), 1)
    return {
        "row": ri,
        "col": ci,
        "after_excl": (ri > ci).astype(_MXU_DTYPE),
        "upto_incl": (ri <= ci).astype(_MXU_DTYPE),
        "upto_excl": (ri < ci).astype(_MXU_DTYPE),
    }


def _sb_fwd(proj, B, S):
    W = SEQ_MIX_WIDTH
    P = W // LANE
    nb = S // SB_BLOCK
    scale = SB_HEAD_DIM ** -0.5

    def body(q_ref, k_ref, v_ref, o_ref, tot_ref):
        tri = _sb_tri()
        lane = lax.broadcasted_iota(jnp.int32, (1, LANE), 1)

        def q_loop(i, carry):
            qrows = pl.ds(pl.multiple_of(i * SB_BLOCK, SB_BLOCK), SB_BLOCK)
            q = q_ref[qrows, :].astype(F32)
            out = jnp.zeros((SB_BLOCK, LANE), F32)
            tot = jnp.zeros((SB_BLOCK, LANE), F32)
            for hh in range(LANE // SB_HEAD_DIM):
                hm = (lane // SB_HEAD_DIM) == hh
                q_m = jnp.where(hm, q, 0.0)

                def k_loop(t, c):
                    acc, r = c
                    j = i - t
                    krows = pl.ds(pl.multiple_of(j * SB_BLOCK, SB_BLOCK), SB_BLOCK)
                    lb, l1, valid = _sb_scores(q_m, k_ref[krows, :], scale, j == i, tri)
                    a = jnp.where(valid, jnp.exp(lb + r + _split_dot(l1, tri["after_excl"])), 0.0)
                    return acc + _dot(a, v_ref[krows, :]), r + jnp.sum(l1, axis=-1, keepdims=True)

                acc, r = lax.fori_loop(0, i + 1, k_loop, (jnp.zeros((SB_BLOCK, LANE), F32),
                                                          jnp.zeros((SB_BLOCK, 1), F32)))
                out = out + jnp.where(hm, acc, 0.0)
                tot = tot + jnp.where(hm, r, 0.0)
            o_ref[qrows, :] = out
            tot_ref[qrows, :] = tot
            return carry

        lax.fori_loop(0, nb, q_loop, 0)

    def col(off):
        return pl.BlockSpec((S, LANE), lambda b, p: (b, off + p))

    out = jax.ShapeDtypeStruct((B * S, W), F32)
    return pl.pallas_call(body, grid=(B, P), in_specs=[col(0), col(P), col(2 * P)], out_specs=(col(0), col(0)),
                          out_shape=(out, out),
                          compiler_params=_cp(("parallel", "parallel")), name="sb_fwd")(proj, proj, proj)


def _sb_bwd(proj, tot, d_cat, B, S):
    W = SEQ_MIX_WIDTH
    P = W // LANE
    nb = S // SB_BLOCK
    scale = SB_HEAD_DIM ** -0.5

    def body(q_ref, k_ref, v_ref, tot_ref, do_ref, dq_ref, dk_ref, dv_ref, dk_acc, dv_acc):
        tri = _sb_tri()
        lane = lax.broadcasted_iota(jnp.int32, (1, LANE), 1)
        dk_acc[...] = jnp.zeros_like(dk_acc)
        dv_acc[...] = jnp.zeros_like(dv_acc)

        def q_loop(i, carry):
            qrows = pl.ds(pl.multiple_of(i * SB_BLOCK, SB_BLOCK), SB_BLOCK)
            q = q_ref[qrows, :].astype(F32)
            do = do_ref[qrows, :].astype(F32)
            tot = tot_ref[qrows, :]
            dq_out = jnp.zeros((SB_BLOCK, LANE), F32)
            for hh in range(LANE // SB_HEAD_DIM):
                hm = (lane // SB_HEAD_DIM) == hh
                q_m = jnp.where(hm, q, 0.0)
                do_m = jnp.where(hm, do, 0.0)
                total = jnp.sum(jnp.where(lane == hh * SB_HEAD_DIM, tot, 0.0), axis=-1, keepdims=True)

                def k_loop(j, c):
                    dq_acc, p_l1, p_g = c
                    krows = pl.ds(pl.multiple_of(j * SB_BLOCK, SB_BLOCK), SB_BLOCK)
                    k_j = k_ref[krows, :]
                    lb, l1, valid = _sb_scores(q_m, k_j, scale, j == i, tri)
                    tail = total - p_l1 - _split_dot(l1, tri["upto_incl"])
                    a = jnp.where(valid, jnp.exp(lb + tail), 0.0)
                    g = _dot(do_m, v_ref[krows, :], 1, 1) * a
                    g_before = p_g + _split_dot(g, tri["upto_excl"])
                    sig = jnp.exp(lb)
                    dz = jnp.where(valid, g * (1.0 - sig) - g_before * sig, 0.0) * scale
                    dk_acc[krows, :] += jnp.where(hm, _dot(dz, q_m, 0, 0), 0.0)
                    dv_acc[krows, :] += jnp.where(hm, _dot(a, do_m, 0, 0), 0.0)
                    return (dq_acc + _dot(dz, k_j), p_l1 + jnp.sum(l1, axis=-1, keepdims=True),
                            p_g + jnp.sum(g, axis=-1, keepdims=True))

                zero_col = jnp.zeros((SB_BLOCK, 1), F32)
                dq_h, _, _ = lax.fori_loop(0, i + 1, k_loop, (jnp.zeros((SB_BLOCK, LANE), F32), zero_col, zero_col))
                dq_out = dq_out + jnp.where(hm, dq_h, 0.0)
            dq_ref[qrows, :] = dq_out.astype(dq_ref.dtype)
            return carry

        lax.fori_loop(0, nb, q_loop, 0)
        dk_ref[...] = dk_acc[...].astype(dk_ref.dtype)
        dv_ref[...] = dv_acc[...].astype(dv_ref.dtype)

    def col(off):
        return pl.BlockSpec((S, LANE), lambda b, p: (b, off + p))

    out = jax.ShapeDtypeStruct((B * S, W), _MXU_DTYPE)
    return pl.pallas_call(body, grid=(B, P), in_specs=[col(0), col(P), col(2 * P), col(0), col(0)],
                          out_specs=(col(0),) * 3, out_shape=(out,) * 3,
                          scratch_shapes=[pltpu.VMEM((S, LANE), F32), pltpu.VMEM((S, LANE), F32)],
                          compiler_params=_cp(("parallel", "parallel")), name="sb_bwd")(proj, proj, proj, tot, d_cat)


def _mem_fn(q, k, v):
    lane = lax.broadcasted_iota(jnp.int32, (1, X_WIDTH), 1)
    out = jnp.zeros(q.shape, F32)
    for h in range(N_X_HEADS):
        hm = (lane // X_HEAD_DIM) == h
        s = _dot(jnp.where(hm, q, 0.0), k, 1, 1) * (X_HEAD_DIM ** -0.5)
        e = jnp.exp(s - lax.stop_gradient(jnp.max(s, axis=-1, keepdims=True)))
        p = e / jnp.sum(e, axis=-1, keepdims=True)
        out = out + jnp.where(hm, _dot(p, v), 0.0)
    return out


def _mem_specs(S, q_col):
    ts = _pick(S, 1024)
    ns = S // ts
    qs = pl.BlockSpec((ts, X_WIDTH), lambda b, i: (b * ns + i, q_col // X_WIDTH))
    ks = pl.BlockSpec((N_MEM, X_WIDTH), lambda b, i: (b, 0))
    vs = pl.BlockSpec((N_MEM, X_WIDTH), lambda b, i: (b, 1))
    os = pl.BlockSpec((ts, X_WIDTH), lambda b, i: (b * ns + i, 0))
    return ts, ns, qs, ks, vs, os


def _mem_fwd(proj, q_col, mem_kv, B, S, name):
    ts, ns, qs, ks, vs, os = _mem_specs(S, q_col)

    def body(q_ref, k_ref, v_ref, o_ref):
        o_ref[...] = _mem_fn(q_ref[...].astype(F32), k_ref[...].astype(F32), v_ref[...].astype(F32))

    return pl.pallas_call(body, grid=(B, ns), in_specs=[qs, ks, vs], out_specs=os,
                          out_shape=jax.ShapeDtypeStruct((B * S, X_WIDTH), F32),
                          compiler_params=_cp(("parallel", "parallel")), name=name)(proj, mem_kv, mem_kv)


def _mem_bwd(proj, q_col, mem_kv, d_cat, B, S, name):
    ts, ns, qs, ks, vs, os = _mem_specs(S, q_col)

    def body(q_ref, k_ref, v_ref, do_ref, dq_ref, dk_ref, dv_ref):
        _, vjp = jax.vjp(_mem_fn, q_ref[...].astype(F32), k_ref[...].astype(F32), v_ref[...].astype(F32))
        dq, dk, dv = vjp(do_ref[...].astype(F32))
        dq_ref[...] = dq.astype(dq_ref.dtype)

        @pl.when(pl.program_id(1) == 0)
        def _():
            dk_ref[...] = jnp.zeros_like(dk_ref)
            dv_ref[...] = jnp.zeros_like(dv_ref)

        dk_ref[...] += dk
        dv_ref[...] += dv

    dos = pl.BlockSpec((ts, X_WIDTH), lambda b, i: (b * ns + i, SEQ_MIX_WIDTH // X_WIDTH))
    dq, dk, dv = pl.pallas_call(
        body, grid=(B, ns), in_specs=[qs, ks, vs, dos],
        out_specs=(os, pl.BlockSpec((N_MEM, X_WIDTH), lambda b, i: (b, 0)), pl.BlockSpec((N_MEM, X_WIDTH), lambda b, i: (b, 0))),
        out_shape=(jax.ShapeDtypeStruct((B * S, X_WIDTH), _MXU_DTYPE),
                   jax.ShapeDtypeStruct((B * N_MEM, X_WIDTH), F32), jax.ShapeDtypeStruct((B * N_MEM, X_WIDTH), F32)),
        compiler_params=_cp(("parallel", "arbitrary")), name=name)(proj, mem_kv, mem_kv, d_cat)
    return dq, dk, dv


def _peers():
    x, y, c = lax.axis_index("x"), lax.axis_index("y"), lax.axis_index("c")
    me = 4 * x + 2 * y + c
    out = []
    for fx, fy, fc in [(0, 0, 1), (1, 0, 0), (0, 1, 0), (1, 1, 0), (1, 0, 1), (0, 1, 1), (1, 1, 1)]:
        px, py, pc = x ^ fx, y ^ fy, c ^ fc
        out.append(((px, py, pc), 4 * px + 2 * py + pc))
    return me, out


ANY = pl.BlockSpec(memory_space=pl.ANY)


def _all_gather(shard):
    R = shard.shape[0]

    def body(x_ref, o_ref, send_sems, recv_sems, local_sem):
        me, peers = _peers()
        mine = pltpu.make_async_copy(x_ref, o_ref.at[me], local_sem)
        mine.start()
        copies = []
        for k, (dev, _) in enumerate(peers):
            cp = pltpu.make_async_remote_copy(src_ref=x_ref, dst_ref=o_ref.at[me], send_sem=send_sems.at[k],
                                              recv_sem=recv_sems.at[k], device_id=dev,
                                              device_id_type=pl.DeviceIdType.MESH)
            cp.start()
            copies.append(cp)
        for k, (dev, idx) in enumerate(peers):
            pltpu.make_async_remote_copy(src_ref=x_ref, dst_ref=o_ref.at[idx], send_sem=send_sems.at[k],
                                         recv_sem=recv_sems.at[k], device_id=dev,
                                         device_id_type=pl.DeviceIdType.MESH).wait_recv()
        for cp in copies:
            cp.wait_send()
        mine.wait()

    return pl.pallas_call(body, in_specs=[ANY], out_specs=ANY,
                          out_shape=jax.ShapeDtypeStruct((N_DEV, R, LANE), shard.dtype),
                          scratch_shapes=[pltpu.SemaphoreType.DMA((7,)), pltpu.SemaphoreType.DMA((7,)),
                                          pltpu.SemaphoreType.DMA],
                          compiler_params=pltpu.CompilerParams(has_side_effects=True),
                          name="all_gather_weights")(shard)


def _exchange(big, small):
    R = big.shape[1]
    K = small.shape[0]

    def body(b_ref, s_ref, ob_ref, os_ref, send_sems, recv_sems, local_sems):
        me, peers = _peers()
        own_b = pltpu.make_async_copy(b_ref.at[me], ob_ref.at[me], local_sems.at[0])
        own_s = pltpu.make_async_copy(s_ref, os_ref.at[me], local_sems.at[1])
        own_b.start()
        own_s.start()
        copies = []
        for k, (dev, idx) in enumerate(peers):
            for t, (src, dst) in enumerate([(b_ref.at[idx], ob_ref.at[me]), (s_ref, os_ref.at[me])]):
                cp = pltpu.make_async_remote_copy(src_ref=src, dst_ref=dst, send_sem=send_sems.at[2 * k + t],
                                                  recv_sem=recv_sems.at[2 * k + t], device_id=dev,
                                                  device_id_type=pl.DeviceIdType.MESH)
                cp.start()
                copies.append(cp)
        for k, (dev, idx) in enumerate(peers):
            for t, (src, dst) in enumerate([(b_ref.at[me], ob_ref.at[idx]), (s_ref, os_ref.at[idx])]):
                pltpu.make_async_remote_copy(src_ref=src, dst_ref=dst, send_sem=send_sems.at[2 * k + t],
                                             recv_sem=recv_sems.at[2 * k + t], device_id=dev,
                                             device_id_type=pl.DeviceIdType.MESH).wait_recv()
        for cp in copies:
            cp.wait_send()
        own_b.wait()
        own_s.wait()

    return pl.pallas_call(body, in_specs=[ANY, ANY], out_specs=(ANY, ANY),
                          out_shape=(jax.ShapeDtypeStruct((N_DEV, R, LANE), big.dtype),
                                     jax.ShapeDtypeStruct((N_DEV, K, LANE), small.dtype)),
                          scratch_shapes=[pltpu.SemaphoreType.DMA((14,)), pltpu.SemaphoreType.DMA((14,)),
                                          pltpu.SemaphoreType.DMA((2,))],
                          compiler_params=pltpu.CompilerParams(has_side_effects=True),
                          name="exchange_grads")(big, small)


def _adamw_math(w, g, m, v):
    m = ADAM_B1 * m + (1.0 - ADAM_B1) * g
    v = ADAM_B2 * v + (1.0 - ADAM_B2) * (g * g)
    m_hat = m / (1.0 - ADAM_B1 ** ADAM_STEP)
    v_hat = v / (1.0 - ADAM_B2 ** ADAM_STEP)
    delta = -ADAM_LR * (m_hat / (jnp.sqrt(v_hat) + ADAM_EPS) + ADAM_WD * w)
    return delta, m, v


def _adamw_shard(own, recv, w, m, v):
    R = own.shape[0]
    tr = _pick(R, 1024)

    def body(own_ref, recv_ref, w_ref, m_ref, v_ref, g_ref, d_ref, nm_ref, nv_ref):
        me, _ = _peers()
        g = own_ref[...]
        for p in range(N_DEV):
            g = g + jnp.where(p == me, 0.0, recv_ref[p].astype(F32))
        delta, nm, nv = _adamw_math(w_ref[...], g, m_ref[...], v_ref[...])
        g_ref[...] = g
        d_ref[...] = delta
        nm_ref[...] = nm
        nv_ref[...] = nv

    row = pl.BlockSpec((tr, LANE), lambda i: (i, 0))
    out = jax.ShapeDtypeStruct((R, LANE), F32)
    return pl.pallas_call(body, grid=(R // tr,),
                          in_specs=[row, pl.BlockSpec((N_DEV, tr, LANE), lambda i: (0, i, 0)), row, row, row],
                          out_specs=(row,) * 4, out_shape=(out,) * 4,
                          compiler_params=_cp(("parallel",)), name="adamw_shard")(own, recv, w, m, v)


def _adamw_replicated(parts, w, m, v):
    K = w.shape[0]

    def body(p_ref, w_ref, m_ref, v_ref, g_ref, d_ref, nm_ref, nv_ref):
        g = p_ref[0]
        for p in range(1, N_DEV):
            g = g + p_ref[p]
        delta, nm, nv = _adamw_math(w_ref[...], g, m_ref[...], v_ref[...])
        g_ref[...] = g
        d_ref[...] = delta
        nm_ref[...] = nm
        nv_ref[...] = nv

    out = jax.ShapeDtypeStruct((K, LANE), F32)
    return pl.pallas_call(body, out_shape=(out,) * 4, compiler_params=_cp(), name="adamw_replicated")(parts, w, m, v)


_SHARDED = (("w_in_a", 1), ("conv_w_a", 2), ("w_in_b", 2), ("w_mem_kv", 1), ("w_out", 1), ("w_up", 2), ("w_down", 1))
_REPLICATED = ("mem_norm", "norm_pre_mix", "norm_post_mix", "norm_pre_mlp", "norm_post_mlp", "a_log_a", "dt_bias_a", "onorm_a")
ROW_ALIGN = 16


def _rows(n_elems, align=ROW_ALIGN):
    r = -(-n_elems // LANE)
    return -(-r // align) * align


def _pack(arrays, align=ROW_ALIGN, total=None, lead=()):
    parts = []
    for a in arrays:
        n = math.prod(a.shape[len(lead):])
        flat = a.reshape(lead + (n,))
        r = _rows(n, align)
        flat = jnp.pad(flat, [(0, 0)] * len(lead) + [(0, r * LANE - n)])
        parts.append(flat.reshape(lead + (r, LANE)))
    out = jnp.concatenate(parts, axis=len(lead))
    if total is not None and out.shape[len(lead)] < total:
        out = jnp.pad(out, [(0, 0)] * len(lead) + [(0, total - out.shape[len(lead)]), (0, 0)])
    return out


def _unpack(flat, shapes, align=ROW_ALIGN, lead=()):
    outs = []
    r0 = 0
    for shp in shapes:
        n = math.prod(shp)
        r = _rows(n, align)
        part = lax.slice_in_dim(flat, r0, r0 + r, axis=len(lead))
        part = part.reshape(lead + (r * LANE,))
        part = lax.slice_in_dim(part, 0, n, axis=len(lead))
        outs.append(part.reshape(lead + tuple(shp)))
        r0 += r
    return outs


def _to_full(gathered, axis):
    g = jnp.moveaxis(gathered, 0, axis)
    shp = g.shape
    return g.reshape(shp[:axis] + (shp[axis] * shp[axis + 1],) + shp[axis + 2:])


def _to_blocks(full, axis):
    shp = full.shape
    g = full.reshape(shp[:axis] + (N_DEV, shp[axis] // N_DEV) + shp[axis + 1:])
    return jnp.moveaxis(g, axis, 0)


def _widen_in_a(w):
    main = w[:, :4 * SEQ_MIX_WIDTH]
    small = w[:, 4 * SEQ_MIX_WIDTH:4 * SEQ_MIX_WIDTH + 2 * N_LIN_HEADS]
    memq = w[:, 4 * SEQ_MIX_WIDTH + 2 * N_LIN_HEADS:]
    pad = jnp.zeros((w.shape[0], IN_A_PAD - IN_A), w.dtype)
    return jnp.concatenate([main, memq, small, pad], axis=1)


def _narrow_in_a(g):
    main = g[:, :4 * SEQ_MIX_WIDTH]
    memq = g[:, 4 * SEQ_MIX_WIDTH:4 * SEQ_MIX_WIDTH + X_WIDTH]
    small = g[:, SM_COL:SM_COL + 2 * N_LIN_HEADS]
    return jnp.concatenate([main, small, memq], axis=1)


def _row128(v):
    return jnp.pad(v.reshape(1, -1), ((0, 0), (0, LANE - v.shape[-1])))


def _local_step(x, mem, target, p):
    B, S, D = x.shape
    T = B * S
    md = _MXU_DTYPE
    x0 = x.reshape(T, D)
    tgt = target.reshape(T, D)
    memf = mem.reshape(B * N_MEM, D)
    vec = lambda a: a.reshape(1, -1)

    mem_n = _norm_fwd(memf, vec(p["mem_norm"]), out_dtype=md, name="norm_mem")
    w_in = [_widen_in_a(p["w_in_a"][0]), p["w_in_b"][0]]
    memq_col = [4 * SEQ_MIX_WIDTH, 3 * SEQ_MIX_WIDTH]
    alog = _row128(p["a_log_a"][0])
    dtb = _row128(p["dt_bias_a"][0])
    onorm = vec(p["onorm_a"][0])
    conv_w = p["conv_w_a"][0]
    saved = []
    xi = x0
    for i in range(2):
        s = {"x_in": xi}
        h1 = _norm_fwd(xi, vec(p["norm_pre_mix"][i]), out_dtype=md, name=f"norm_pre_mix{i}")
        proj = _mm(h1, w_in[i], out_dtypes=(F32 if i == 0 else md,), name=f"in_proj{i}")
        mem_kv = _mm(mem_n, p["w_mem_kv"][i], out_dtypes=(md,), name=f"mem_kv{i}")
        if i == 0:
            act = _gdn_conv_fwd(proj, conv_w, B, S)
            beta, gc = _gdn_gates_fwd(proj, alog, dtb, B, S)
            u, w, qd, kd, intra = _gdn_prep_fwd(act, beta, gc, B, S)
            o, states = _gdn_scan_fwd(u, w, qd, kd, intra, gc, B, S)
            mix = _gdn_post_fwd(o, proj, onorm, T)
            s.update(act=act, beta=beta, gc=gc, u=u, w=w, qd=qd, kd=kd, intra=intra, o=o, states=states)
        else:
            mix, tot = _sb_fwd(proj, B, S)
            s.update(tot=tot)
        cross = _mem_fwd(proj, memq_col[i], mem_kv, B, S, name=f"mem_fwd{i}")
        cat = jnp.concatenate([mix.astype(md), cross.astype(md)], axis=1)
        y = _mm(cat, p["w_out"][i], name=f"out_proj{i}")
        x_mid = _norm_fwd(y, vec(p["norm_post_mix"][i]), resid=xi, name=f"norm_post_mix{i}")
        h2 = _norm_fwd(x_mid, vec(p["norm_pre_mlp"][i]), out_dtype=md, name=f"norm_pre_mlp{i}")
        a_act, r = _mm(h2, p["w_up"][i], out_dtypes=(md, md), epilogue=_relu2_epilogue, name=f"up_proj{i}")
        y2 = _mm(a_act, p["w_down"][i], name=f"down_proj{i}")
        x_out = _norm_fwd(y2, vec(p["norm_post_mlp"][i]), resid=x_mid, name=f"norm_post_mlp{i}")
        s.update(h1=h1, proj=proj, mem_kv=mem_kv, cat=cat, y=y, x_mid=x_mid, h2=h2, a_act=a_act, r=r, y2=y2)
        saved.append(s)
        xi = x_out

    loss_row, dx = _loss_head(xi, tgt)

    g = {}
    d_mem_n = None
    gn = {k: [None, None] for k in ("norm_pre_mix", "norm_post_mix", "norm_pre_mlp", "norm_post_mlp")}
    g_w_mem_kv, g_w_out, g_w_up, g_w_down = [None, None], [None, None], [None, None], [None, None]
    for i in (1, 0):
        s = saved[i]
        d_y2, gn["norm_post_mlp"][i] = _norm_bwd(dx, s["y2"], vec(p["norm_post_mlp"][i]), name=f"norm_post_mlp_bwd{i}")
        g_w_down[i] = _mm(s["a_act"], d_y2, ta=True, name=f"down_proj_dw{i}")
        d_u = _mm(d_y2, p["w_down"][i], tb=True, out_dtypes=(md,), epilogue=_drelu2_epilogue, extras=(s["r"],),
                  name=f"down_proj_dx{i}")
        g_w_up[i] = _mm(s["h2"], d_u, ta=True, name=f"up_proj_dw{i}")
        d_h2 = _mm(d_u, p["w_up"][i], tb=True, name=f"up_proj_dx{i}")
        dx, gn["norm_pre_mlp"][i] = _norm_bwd(d_h2, s["x_mid"], vec(p["norm_pre_mlp"][i]), resid=dx,
                                              name=f"norm_pre_mlp_bwd{i}")
        d_y, gn["norm_post_mix"][i] = _norm_bwd(dx, s["y"], vec(p["norm_post_mix"][i]), name=f"norm_post_mix_bwd{i}")
        g_w_out[i] = _mm(s["cat"], d_y, ta=True, name=f"out_proj_dw{i}")
        d_cat = _mm(d_y, p["w_out"][i], tb=True, name=f"out_proj_dx{i}")
        d_memq, d_mk, d_mv = _mem_bwd(s["proj"], memq_col[i], s["mem_kv"], d_cat, B, S, name=f"mem_bwd{i}")
        d_mem_kv = jnp.concatenate([d_mk.astype(md), d_mv.astype(md)], axis=1)
        g_w_mem_kv[i] = _mm(mem_n, d_mem_kv, ta=True, name=f"mem_kv_dw{i}")
        d_mn = _mm(d_mem_kv, p["w_mem_kv"][i], tb=True, name=f"mem_kv_dx{i}")
        d_mem_n = d_mn if d_mem_n is None else d_mem_n + d_mn
        if i == 0:
            d_o, d_gate, g["onorm_a"] = _gdn_post_bwd(d_cat, s["o"], s["proj"], onorm, T)
            du, dw, dqd, dkd, da, dgc_s = _gdn_scan_bwd(s["u"], s["w"], s["qd"], s["kd"], s["intra"], s["gc"],
                                                         s["states"], d_o, B, S)
            dq, dk, dv, d_beta, d_gc = _gdn_prep_bwd(s["act"], s["beta"], s["gc"], du, dw, dqd, dkd, da, B, S)
            d_qkv, g["conv_w_a"] = _gdn_conv_bwd(dq, dk, dv, s["proj"], conv_w, B, S)
            d_sm, g["a_log_a"], g["dt_bias_a"] = _gdn_gates_bwd(d_beta, d_gc + dgc_s, s["proj"], alog, dtb, B, S)
            pad = jnp.zeros((T, IN_A_PAD - SM_COL - LANE), md)
            d_proj = jnp.concatenate([d_qkv, d_gate, d_memq, d_sm, pad], axis=1)
        else:
            dq, dk, dv = _sb_bwd(s["proj"], s["tot"], d_cat, B, S)
            d_proj = jnp.concatenate([dq, dk, dv, d_memq], axis=1)
        g_w_in = _mm(s["h1"], d_proj, ta=True, name=f"in_proj_dw{i}")
        d_h1 = _mm(d_proj, w_in[i], tb=True, name=f"in_proj_dx{i}")
        dx, gn["norm_pre_mix"][i] = _norm_bwd(d_h1, s["x_in"], vec(p["norm_pre_mix"][i]), resid=dx,
                                              name=f"norm_pre_mix_bwd{i}")
        if i == 0:
            g["w_in_a"] = _narrow_in_a(g_w_in)[None]
        else:
            g["w_in_b"] = g_w_in[None]
    _, g_mem_norm = _norm_bwd(d_mem_n, memf, vec(p["mem_norm"]), name="norm_mem_bwd")
    g["mem_norm"] = g_mem_norm.reshape(-1)
    for k, v in gn.items():
        g[k] = jnp.concatenate(v, axis=0)
    g["w_mem_kv"] = jnp.stack(g_w_mem_kv)
    g["w_out"] = jnp.stack(g_w_out)
    g["w_up"] = jnp.stack(g_w_up)
    g["w_down"] = jnp.stack(g_w_down)
    g["conv_w_a"] = g["conv_w_a"][None]
    g["a_log_a"] = g["a_log_a"][:, :N_LIN_HEADS]
    g["dt_bias_a"] = g["dt_bias_a"][:, :N_LIN_HEADS]
    return loss_row, dx.reshape(B, S, D), g


def kernel(x, mem, mem_norm, norm_pre_mix, norm_post_mix, norm_pre_mlp, norm_post_mlp, w_in_a, conv_w_a, a_log_a, dt_bias_a, onorm_a, w_in_b, w_mem_kv, w_out, w_up, w_down, loss_target, m_mem_norm, m_norm_pre_mix, m_norm_post_mix, m_norm_pre_mlp, m_norm_post_mlp, m_w_in_a, m_conv_w_a, m_a_log_a, m_dt_bias_a, m_onorm_a, m_w_in_b, m_w_mem_kv, m_w_out, m_w_up, m_w_down, v_mem_norm, v_norm_pre_mix, v_norm_post_mix, v_norm_pre_mlp, v_norm_post_mlp, v_w_in_a, v_conv_w_a, v_a_log_a, v_dt_bias_a, v_onorm_a, v_w_in_b, v_w_mem_kv, v_w_out, v_w_up, v_w_down):
    args = dict(locals())
    big_names = [n for n, _ in _SHARDED]
    axes = dict(_SHARDED)
    shard_shapes = [args[n].shape for n in big_names]
    rows_total = -(-sum(_rows(math.prod(s)) for s in shard_shapes) // 1024) * 1024

    exact = ("conv_w_a",) if _WIRE_DTYPE != F32 else ()
    wire = [lax.bitcast_convert_type(args[n], _WIRE_DTYPE) if n in exact else args[n].astype(_WIRE_DTYPE)
            for n in big_names]
    wire_shapes = [a.shape for a in wire]
    wire_rows = -(-sum(_rows(math.prod(s)) for s in wire_shapes) // ROW_ALIGN) * ROW_ALIGN
    gathered = _all_gather(_pack(wire, total=wire_rows))
    p = {}
    for n, b in zip(big_names, _unpack(gathered, wire_shapes, lead=(N_DEV,))):
        b = lax.bitcast_convert_type(b, F32) if n in exact else b.astype(_MXU_DTYPE)
        p[n] = _to_full(b, axes[n])
    for n in _REPLICATED:
        p[n] = args[n]
    w_flat = _pack([args[n] for n in big_names], total=rows_total)

    loss_row, grad_x, g = _local_step(x, mem, loss_target, p)
    loss = lax.psum(loss_row[0, 0], ("x", "y", "c"))

    g_blocks = _pack([_to_blocks(g[n], axes[n]) for n in big_names], total=rows_total, lead=(N_DEV,))
    rep_shapes = [args[n].shape for n in _REPLICATED]
    g_small = _pack([g[n] for n in _REPLICATED], align=8)
    recv_big, recv_small = _exchange(g_blocks.astype(_WIRE_DTYPE), g_small)

    me = 4 * lax.axis_index("x") + 2 * lax.axis_index("y") + lax.axis_index("c")
    own = lax.dynamic_index_in_dim(g_blocks, me, axis=0, keepdims=False)
    m_flat = _pack([args["m_" + n] for n in big_names], total=rows_total)
    v_flat = _pack([args["v_" + n] for n in big_names], total=rows_total)
    outs_big = [_unpack(f, shard_shapes) for f in _adamw_shard(own, recv_big, w_flat, m_flat, v_flat)]
    outs_small = [_unpack(f, rep_shapes, align=8) for f in _adamw_replicated(
        recv_small, _pack([args[n] for n in _REPLICATED], align=8),
        _pack([args["m_" + n] for n in _REPLICATED], align=8), _pack([args["v_" + n] for n in _REPLICATED], align=8))]

    order = ["mem_norm", "norm_pre_mix", "norm_post_mix", "norm_pre_mlp", "norm_post_mlp", "w_in_a", "conv_w_a",
             "a_log_a", "dt_bias_a", "onorm_a", "w_in_b", "w_mem_kv", "w_out", "w_up", "w_down"]
    result = [loss, grad_x]
    for kind in range(4):
        for n in order:
            if n in axes:
                result.append(outs_big[kind][big_names.index(n)])
            else:
                result.append(outs_small[kind][_REPLICATED.index(n)])
    return tuple(result)
```

```python
import functools
import math

import jax
import jax.numpy as jnp
from jax import lax
from jax.experimental import pallas as pl
from jax.experimental.pallas import tpu as pltpu

F32 = jnp.float32
_MXU_DTYPE = jnp.bfloat16
_WIRE_DTYPE = jnp.bfloat16
_HI = lax.Precision.HIGHEST

D_MODEL = 1024
N_DEV = 8
N_MEM = 256
X_WIDTH = 256
N_X_HEADS = 4
X_HEAD_DIM = 64
SEQ_MIX_WIDTH = 768
LIN_HEAD_DIM = 128
N_LIN_HEADS = 6
CONV_WIDTH = 4
CHUNK = 64
SB_HEAD_DIM = 64
SB_BLOCK = 128
D_FF = 4096
EPS = 1e-6
IN_A = 3340
IN_A_PAD = 3584
IN_B = 2560
SM_COL = 3328

ADAM_LR = 0.001
ADAM_B1 = 0.9
ADAM_B2 = 0.999
ADAM_EPS = 1e-08
ADAM_WD = 0.01
ADAM_STEP = 10

LANE = 128
VMEM_LIMIT = 56 * 1024 * 1024


def _cp(sem=None):
    return pltpu.CompilerParams(dimension_semantics=sem, vmem_limit_bytes=VMEM_LIMIT)


def _pick(n, target):
    if n <= target:
        return n
    best = None
    for t in range(LANE, target + 1, LANE):
        if n % t == 0:
            best = t
    assert best is not None, (n, target)
    return best


def _dot(a, b, ca=1, cb=0):
    return lax.dot_general(a.astype(_MXU_DTYPE), b.astype(_MXU_DTYPE), (((ca,), (cb,)), ((), ())),
                           preferred_element_type=F32)


def _bdot(a, b, ca, cb):
    return lax.dot_general(a.astype(_MXU_DTYPE), b.astype(_MXU_DTYPE), (((ca,), (cb,)), ((0,), (0,))),
                           preferred_element_type=F32)


def _bdot_hi(a, b):
    return lax.dot_general(a, b, (((2,), (1,)), ((0,), (0,))), precision=_HI, preferred_element_type=F32)


def _sigmoid(x):
    return 1.0 / (1.0 + jnp.exp(-x))


def _silu(x):
    return x * _sigmoid(x)


def _softplus(x):
    return jnp.maximum(x, 0.0) + jnp.log(1.0 + jnp.exp(-jnp.abs(x)))


def _rms(x, g):
    return x * lax.rsqrt(jnp.mean(x * x, axis=-1, keepdims=True) + EPS) * g


def _norm_fwd(x, g, resid=None, out_dtype=F32, name="norm_fwd"):
    T, D = x.shape
    tm = _pick(T, 512)
    has_resid = resid is not None

    def body(*refs):
        if has_resid:
            x_ref, g_ref, r_ref, o_ref = refs
        else:
            x_ref, g_ref, o_ref = refs
        y = _rms(x_ref[...].astype(F32), g_ref[...])
        if has_resid:
            y = r_ref[...] + y
        o_ref[...] = y.astype(out_dtype)

    row = pl.BlockSpec((tm, D), lambda i: (i, 0))
    in_specs = [row, pl.BlockSpec((1, D), lambda i: (0, 0))] + ([row] if has_resid else [])
    args = (x, g) + ((resid,) if has_resid else ())
    return pl.pallas_call(body, grid=(T // tm,), in_specs=in_specs, out_specs=row,
                          out_shape=jax.ShapeDtypeStruct((T, D), out_dtype),
                          compiler_params=_cp(("parallel",)), name=name)(*args)


def _norm_bwd(dy, x, g, resid=None, name="norm_bwd"):
    T, D = x.shape
    tm = _pick(T, 512)
    has_resid = resid is not None

    def body(*refs):
        if has_resid:
            dy_ref, x_ref, g_ref, r_ref, dx_ref, dg_ref = refs
        else:
            dy_ref, x_ref, g_ref, dx_ref, dg_ref = refs
        _, vjp = jax.vjp(_rms, x_ref[...].astype(F32), g_ref[...])
        dx, dg = vjp(dy_ref[...].astype(F32))
        if has_resid:
            dx = r_ref[...] + dx
        dx_ref[...] = dx

        @pl.when(pl.program_id(0) == 0)
        def _():
            dg_ref[...] = jnp.zeros_like(dg_ref)

        dg_ref[...] += dg

    row = pl.BlockSpec((tm, D), lambda i: (i, 0))
    vec = pl.BlockSpec((1, D), lambda i: (0, 0))
    in_specs = [row, row, vec] + ([row] if has_resid else [])
    args = (dy, x, g) + ((resid,) if has_resid else ())
    return pl.pallas_call(body, grid=(T // tm,), in_specs=in_specs, out_specs=(row, vec),
                          out_shape=(jax.ShapeDtypeStruct((T, D), F32), jax.ShapeDtypeStruct((1, D), F32)),
                          compiler_params=_cp(("arbitrary",)), name=name)(*args)


def _mm(a, b, *, ta=False, tb=False, out_dtypes=(F32,), epilogue=None, extras=(), name="mm",
        tm_t=1024, tn_t=1024, tk_t=512):
    M, K = (a.shape[1], a.shape[0]) if ta else a.shape
    N = b.shape[0] if tb else b.shape[1]
    assert (b.shape[1] if tb else b.shape[0]) == K, (a.shape, b.shape, ta, tb)
    tm, tn, tk = _pick(M, tm_t), _pick(N, tn_t), _pick(K, tk_t)
    nk = K // tk
    n_extra = len(extras)
    n_out = len(out_dtypes)

    def body(*refs):
        a_ref, b_ref = refs[0], refs[1]
        e_refs = refs[2:2 + n_extra]
        o_refs = refs[2 + n_extra:2 + n_extra + n_out]
        acc_ref = refs[-1]
        k = pl.program_id(2)

        @pl.when(k == 0)
        def _():
            acc_ref[...] = jnp.zeros_like(acc_ref)

        acc_ref[...] += _dot(a_ref[...], b_ref[...], 0 if ta else 1, 1 if tb else 0)

        @pl.when(k == nk - 1)
        def _():
            acc = acc_ref[...]
            outs = (acc,) if epilogue is None else epilogue(acc, *[e[...] for e in e_refs])
            for o_ref, o in zip(o_refs, outs):
                o_ref[...] = o.astype(o_ref.dtype)

    a_spec = pl.BlockSpec((tk, tm), lambda i, j, k: (k, i)) if ta else pl.BlockSpec((tm, tk), lambda i, j, k: (i, k))
    b_spec = pl.BlockSpec((tn, tk), lambda i, j, k: (j, k)) if tb else pl.BlockSpec((tk, tn), lambda i, j, k: (k, j))
    o_spec = pl.BlockSpec((tm, tn), lambda i, j, k: (i, j))
    outs = pl.pallas_call(
        body, grid=(M // tm, N // tn, nk),
        in_specs=[a_spec, b_spec] + [o_spec] * n_extra,
        out_specs=tuple([o_spec] * n_out),
        out_shape=tuple(jax.ShapeDtypeStruct((M, N), dt) for dt in out_dtypes),
        scratch_shapes=[pltpu.VMEM((tm, tn), F32)],
        compiler_params=_cp(("parallel", "parallel", "arbitrary")), name=name)(a, b, *extras)
    return outs[0] if n_out == 1 else outs


def _relu2_epilogue(acc):
    r = jnp.maximum(acc, 0.0)
    return r * r, r


def _drelu2_epilogue(acc, r):
    return (acc * (2.0 * r.astype(F32)),)


def _loss_head(x, target, name="loss_head"):
    T, D = x.shape
    tm = _pick(T, 512)

    def body(x_ref, t_ref, l_ref, dx_ref):
        e = x_ref[...] - t_ref[...]
        dx_ref[...] = e * (1.0 / D)

        @pl.when(pl.program_id(0) == 0)
        def _():
            l_ref[...] = jnp.zeros_like(l_ref)

        part = 0.5 * jnp.sum(jnp.mean(e * e, axis=-1, keepdims=True), axis=0, keepdims=True)
        l_ref[...] += jnp.broadcast_to(part, l_ref.shape)

    row = pl.BlockSpec((tm, D), lambda i: (i, 0))
    return pl.pallas_call(body, grid=(T // tm,), in_specs=[row, row],
                          out_specs=(pl.BlockSpec((1, LANE), lambda i: (0, 0)), row),
                          out_shape=(jax.ShapeDtypeStruct((1, LANE), F32), jax.ShapeDtypeStruct((T, D), F32)),
                          compiler_params=_cp(("arbitrary",)), name=name)(x, target)


def _shift_down(x, k, row):
    return jnp.where(row >= k, pltpu.roll(x, k, 0), 0.0)


def _shift_up(x, k, row, n):
    return jnp.where(row < n - k, pltpu.roll(x, n - k, 0), 0.0)


def _conv_taps(x, w, row):
    y = x * w[CONV_WIDTH - 1:CONV_WIDTH, :]
    for i in range(CONV_WIDTH - 1):
        y = y + _shift_down(x, CONV_WIDTH - 1 - i, row) * w[i:i + 1, :]
    return y


def _qkv_act(xc, j):
    s = _silu(xc)
    n = s * lax.rsqrt(jnp.sum(s * s, axis=-1, keepdims=True) + EPS)
    n = n * jnp.where(j < N_LIN_HEADS, LIN_HEAD_DIM ** -0.5, 1.0)
    return jnp.where(j < 2 * N_LIN_HEADS, n, s)


def _gdn_conv_fwd(proj, conv_w, B, S):
    nblk = 3 * N_LIN_HEADS

    def body(p_ref, w_ref, o_ref):
        j = pl.program_id(1)
        x = p_ref[...]
        row = lax.broadcasted_iota(jnp.int32, x.shape, 0)
        o_ref[...] = _qkv_act(_conv_taps(x, w_ref[...], row), j)

    blk = pl.BlockSpec((S, LANE), lambda b, j: (b, j))
    return pl.pallas_call(body, grid=(B, nblk),
                          in_specs=[blk, pl.BlockSpec((CONV_WIDTH, LANE), lambda b, j: (0, j))],
                          out_specs=blk, out_shape=jax.ShapeDtypeStruct((B * S, nblk * LANE), F32),
                          compiler_params=_cp(("parallel", "parallel")), name="gdn_conv_fwd")(proj, conv_w)


def _gdn_conv_bwd(dq, dk, dv, proj, conv_w, B, S):
    nblk = 3 * N_LIN_HEADS
    H = N_LIN_HEADS

    def body(dq_ref, dk_ref, dv_ref, p_ref, w_ref, dp_ref, dw_ref):
        j = pl.program_id(0)
        b = pl.program_id(1)
        x = p_ref[...]
        w = w_ref[...]
        row = lax.broadcasted_iota(jnp.int32, x.shape, 0)
        d_act = jnp.where(j < H, dq_ref[...], jnp.where(j < 2 * H, dk_ref[...], dv_ref[...]))
        _, vjp = jax.vjp(lambda t: _qkv_act(t, j), _conv_taps(x, w, row))
        (d_xc,) = vjp(d_act)
        dx = d_xc * w[CONV_WIDTH - 1:CONV_WIDTH, :]
        for i in range(CONV_WIDTH - 1):
            dx = dx + _shift_up(d_xc, CONV_WIDTH - 1 - i, row, S) * w[i:i + 1, :]
        dp_ref[...] = dx.astype(dp_ref.dtype)

        @pl.when(b == 0)
        def _():
            dw_ref[...] = jnp.zeros_like(dw_ref)

        for i in range(CONV_WIDTH):
            xs = x if i == CONV_WIDTH - 1 else _shift_down(x, CONV_WIDTH - 1 - i, row)
            dw_ref[i:i + 1, :] += jnp.sum(d_xc * xs, axis=0, keepdims=True)

    blk = pl.BlockSpec((S, LANE), lambda j, b: (b, j))
    wblk = pl.BlockSpec((CONV_WIDTH, LANE), lambda j, b: (0, j))
    return pl.pallas_call(
        body, grid=(nblk, B),
        in_specs=[pl.BlockSpec((S, LANE), lambda j, b: (b, jnp.clip(j, 0, H - 1))),
                  pl.BlockSpec((S, LANE), lambda j, b: (b, jnp.clip(j - H, 0, H - 1))),
                  pl.BlockSpec((S, LANE), lambda j, b: (b, jnp.clip(j - 2 * H, 0, H - 1))),
                  blk, wblk],
        out_specs=(blk, wblk),
        out_shape=(jax.ShapeDtypeStruct((B * S, nblk * LANE), _MXU_DTYPE),
                   jax.ShapeDtypeStruct((CONV_WIDTH, nblk * LANE), F32)),
        compiler_params=_cp(("parallel", "arbitrary")), name="gdn_conv_bwd")(dq, dk, dv, proj, conv_w)


def _chunk_cumsum(x, row):
    pos = row % CHUNK
    k = 1
    while k < CHUNK:
        x = x + jnp.where(pos >= k, pltpu.roll(x, k, 0), 0.0)
        k *= 2
    return x


def _chunk_rev_cumsum(x, row, n):
    pos = row % CHUNK
    k = 1
    while k < CHUNK:
        x = x + jnp.where(pos < CHUNK - k, pltpu.roll(x, n - k, 0), 0.0)
        k *= 2
    return x


def _gdn_gates_fwd(proj, a_log, dt_bias, B, S):
    H = N_LIN_HEADS

    def body(sm_ref, al_ref, dt_ref, beta_ref, gc_ref):
        sm = sm_ref[...]
        row = lax.broadcasted_iota(jnp.int32, (S, LANE), 0)
        for h in range(H):
            beta = _sigmoid(sm[:, h:h + 1])
            g = -jnp.exp(al_ref[0:1, h:h + 1]) * _softplus(sm[:, H + h:H + h + 1] + dt_ref[0:1, h:h + 1])
            beta_ref[:, h * LANE:(h + 1) * LANE] = jnp.broadcast_to(beta, (S, LANE))
            gc_ref[:, h * LANE:(h + 1) * LANE] = _chunk_cumsum(jnp.broadcast_to(g, (S, LANE)), row)

    vec = pl.BlockSpec((1, LANE), lambda b: (0, 0))
    wide = pl.BlockSpec((S, H * LANE), lambda b: (b, 0))
    return pl.pallas_call(body, grid=(B,),
                          in_specs=[pl.BlockSpec((S, LANE), lambda b: (b, SM_COL // LANE)), vec, vec],
                          out_specs=(wide, wide),
                          out_shape=(jax.ShapeDtypeStruct((B * S, H * LANE), F32),) * 2,
                          compiler_params=_cp(("parallel",)), name="gdn_gates_fwd")(proj, a_log, dt_bias)


def _gdn_gates_bwd(d_beta, d_gc, proj, a_log, dt_bias, B, S):
    H = N_LIN_HEADS

    def body(db_ref, dgc_ref, sm_ref, al_ref, dt_ref, dsm_ref, dal_ref, ddt_ref):
        sm = sm_ref[...]
        row = lax.broadcasted_iota(jnp.int32, (S, LANE), 0)
        lane = lax.broadcasted_iota(jnp.int32, (1, LANE), 1)
        dsm = jnp.zeros((S, LANE), F32)
        dal = jnp.zeros((1, LANE), F32)
        ddt = jnp.zeros((1, LANE), F32)
        for h in range(H):
            beta = _sigmoid(sm[:, h:h + 1])
            dbeta = jnp.sum(db_ref[:, h * LANE:(h + 1) * LANE], axis=-1, keepdims=True)
            d_bl = dbeta * beta * (1.0 - beta)
            dgc = jnp.sum(dgc_ref[:, h * LANE:(h + 1) * LANE], axis=-1, keepdims=True)
            dg = _chunk_rev_cumsum(jnp.broadcast_to(dgc, (S, LANE)), row, S)[:, 0:1]
            z = sm[:, H + h:H + h + 1] + dt_ref[0:1, h:h + 1]
            a = jnp.exp(al_ref[0:1, h:h + 1])
            g = -a * _softplus(z)
            d_al = dg * (-a) * _sigmoid(z)
            dsm = dsm + jnp.where(lane == h, d_bl, 0.0) + jnp.where(lane == H + h, d_al, 0.0)
            ddt = ddt + jnp.where(lane == h, jnp.sum(d_al, axis=0, keepdims=True), 0.0)
            dal = dal + jnp.where(lane == h, jnp.sum(dg * g, axis=0, keepdims=True), 0.0)
        dsm_ref[...] = dsm.astype(dsm_ref.dtype)

        @pl.when(pl.program_id(0) == 0)
        def _():
            dal_ref[...] = jnp.zeros_like(dal_ref)
            ddt_ref[...] = jnp.zeros_like(ddt_ref)

        dal_ref[...] += dal
        ddt_ref[...] += ddt

    vec = pl.BlockSpec((1, LANE), lambda b: (0, 0))
    wide = pl.BlockSpec((S, H * LANE), lambda b: (b, 0))
    return pl.pallas_call(body, grid=(B,),
                          in_specs=[wide, wide, pl.BlockSpec((S, LANE), lambda b: (b, SM_COL // LANE)), vec, vec],
                          out_specs=(pl.BlockSpec((S, LANE), lambda b: (b, 0)), vec, vec),
                          out_shape=(jax.ShapeDtypeStruct((B * S, LANE), _MXU_DTYPE),
                                     jax.ShapeDtypeStruct((1, LANE), F32), jax.ShapeDtypeStruct((1, LANE), F32)),
                          compiler_params=_cp(("arbitrary",)), name="gdn_gates_bwd")(d_beta, d_gc, proj, a_log, dt_bias)


PREP_ROWS = 512


@jax.custom_vjp
def _unit_lower_inverse(lower):
    n, C, _ = lower.shape
    ri = lax.broadcasted_iota(jnp.int32, (C, C), 0)
    ci = lax.broadcasted_iota(jnp.int32, (C, C), 1)
    p = -lower
    inv = jnp.where((ri == ci)[None], 1.0, 0.0) + p
    for _ in range(int(math.log2(C)) - 1):
        p = _bdot_hi(p, p)
        inv = inv + _bdot_hi(inv, p)
    return inv


def _unit_lower_inverse_fwd(lower):
    inv = _unit_lower_inverse(lower)
    return inv, inv


def _unit_lower_inverse_bwd(inv, d_inv):
    inv_t = jnp.swapaxes(inv, 1, 2)
    return (-_bdot_hi(_bdot_hi(inv_t, d_inv), inv_t),)


_unit_lower_inverse.defvjp(_unit_lower_inverse_fwd, _unit_lower_inverse_bwd)


def _prep_fn(q, k, v, beta, gc):
    R = q.shape[0]
    n = R // CHUNK
    q3, k3, v3, b3, g3 = [t.reshape(n, CHUNK, LIN_HEAD_DIM) for t in (q, k, v, beta, gc)]
    ri = lax.broadcasted_iota(jnp.int32, (CHUNK, CHUNK), 0)
    ci = lax.broadcasted_iota(jnp.int32, (CHUNK, CHUNK), 1)
    causal = (ri >= ci)[None]
    strict = (ri > ci)[None]
    gcol = g3[:, :, 0:1]
    grow = jnp.swapaxes(g3, 1, 2)[:, 0:1, :]
    decay = jnp.exp(jnp.where(causal, gcol - grow, -1e30))
    kb = k3 * b3
    lower = jnp.where(strict, _bdot(kb, k3, 2, 2) * decay, 0.0)
    inv = _unit_lower_inverse(lower)
    eg = jnp.exp(g3)
    sol = _bdot_hi(inv, jnp.concatenate([v3 * b3, kb * eg], axis=-1))
    u, w = sol[..., :LIN_HEAD_DIM], sol[..., LIN_HEAD_DIM:]
    intra = _bdot(q3, k3, 2, 2) * decay
    q_dec = q3 * eg
    k_dec = k3 * jnp.exp(g3[:, CHUNK - 1:CHUNK, :] - g3)
    return (u.reshape(R, LIN_HEAD_DIM), w.reshape(R, LIN_HEAD_DIM), q_dec.reshape(R, LIN_HEAD_DIM),
            k_dec.reshape(R, LIN_HEAD_DIM), intra.reshape(R, CHUNK))


def _prep_specs(S):
    H = N_LIN_HEADS
    R = min(PREP_ROWS, S)
    nr = S // R

    def col(off):
        return pl.BlockSpec((R, LANE), lambda b, h, r: (b * nr + r, off + h))

    intra = pl.BlockSpec((None, R, CHUNK), lambda b, h, r: (h, b * nr + r, 0))
    return R, nr, col, intra


def _gdn_prep_fwd(act, beta, gc, B, S):
    H = N_LIN_HEADS
    R, nr, col, intra_spec = _prep_specs(S)
    T = B * S

    def body(q_ref, k_ref, v_ref, b_ref, g_ref, u_ref, w_ref, qd_ref, kd_ref, a_ref):
        u, w, qd, kd, a = _prep_fn(q_ref[...], k_ref[...], v_ref[...], b_ref[...], g_ref[...])
        u_ref[...] = u
        w_ref[...] = w
        qd_ref[...] = qd
        kd_ref[...] = kd
        a_ref[...] = a

    wide = jax.ShapeDtypeStruct((T, H * LANE), F32)
    return pl.pallas_call(body, grid=(B, H, nr),
                          in_specs=[col(0), col(H), col(2 * H), col(0), col(0)],
                          out_specs=(col(0), col(0), col(0), col(0), intra_spec),
                          out_shape=(wide, wide, wide, wide, jax.ShapeDtypeStruct((H, T, CHUNK), F32)),
                          compiler_params=_cp(("parallel", "parallel", "parallel")),
                          name="gdn_prep_fwd")(act, act, act, beta, gc)


def _gdn_prep_bwd(act, beta, gc, du, dw, dqd, dkd, da, B, S):
    H = N_LIN_HEADS
    R, nr, col, intra_spec = _prep_specs(S)
    T = B * S

    def body(q_ref, k_ref, v_ref, b_ref, g_ref, du_ref, dw_ref, dqd_ref, dkd_ref, da_ref,
             dq_ref, dk_ref, dv_ref, db_ref, dg_ref):
        _, vjp = jax.vjp(_prep_fn, q_ref[...], k_ref[...], v_ref[...], b_ref[...], g_ref[...])
        dq, dk, dv, db, dg = vjp((du_ref[...], dw_ref[...], dqd_ref[...], dkd_ref[...], da_ref[...]))
        dq_ref[...] = dq
        dk_ref[...] = dk
        dv_ref[...] = dv
        db_ref[...] = db
        dg_ref[...] = dg

    wide = jax.ShapeDtypeStruct((T, H * LANE), F32)
    return pl.pallas_call(body, grid=(B, H, nr),
                          in_specs=[col(0), col(H), col(2 * H), col(0), col(0),
                                    col(0), col(0), col(0), col(0), intra_spec],
                          out_specs=(col(0),) * 5, out_shape=(wide,) * 5,
                          compiler_params=_cp(("parallel", "parallel", "parallel")),
                          name="gdn_prep_bwd")(act, act, act, beta, gc, du, dw, dqd, dkd, da)


def _scan_step(u, w, qd, kd, a, g_last, state):
    v_new = u - _dot(w, state)
    o = _dot(qd, state) + _dot(a, v_new)
    new_state = state * jnp.exp(g_last) + _dot(kd, v_new, 0, 0)
    return o, new_state


def _scan_specs(B, S):
    H = N_LIN_HEADS
    n = S // CHUNK
    col = pl.BlockSpec((S, LANE), lambda b, h: (b, h))
    intra = pl.BlockSpec((None, S, CHUNK), lambda b, h: (h, b, 0))
    st = pl.BlockSpec((None, None, n, LIN_HEAD_DIM, LIN_HEAD_DIM), lambda b, h: (b, h, 0, 0, 0))
    return n, col, intra, st


def _gdn_scan_fwd(u, w, qd, kd, a, gc, B, S):
    H = N_LIN_HEADS
    n, col, intra, st = _scan_specs(B, S)

    def body(u_ref, w_ref, qd_ref, kd_ref, a_ref, g_ref, o_ref, st_ref):
        def step(c, state):
            rows = pl.ds(pl.multiple_of(c * CHUNK, CHUNK), CHUNK)
            st_ref[c] = state.astype(st_ref.dtype)
            o, new_state = _scan_step(u_ref[rows, :], w_ref[rows, :], qd_ref[rows, :], kd_ref[rows, :],
                                      a_ref[rows, :], g_ref[rows, :][CHUNK - 1:CHUNK, :], state)
            o_ref[rows, :] = o
            return new_state

        lax.fori_loop(0, n, step, jnp.zeros((LIN_HEAD_DIM, LIN_HEAD_DIM), F32))

    return pl.pallas_call(body, grid=(B, H), in_specs=[col, col, col, col, intra, col],
                          out_specs=(col, st),
                          out_shape=(jax.ShapeDtypeStruct((B * S, H * LANE), F32),
                                     jax.ShapeDtypeStruct((B, H, n, LIN_HEAD_DIM, LIN_HEAD_DIM), _MXU_DTYPE)),
                          compiler_params=_cp(("parallel", "parallel")), name="gdn_scan_fwd")(u, w, qd, kd, a, gc)


def _gdn_scan_bwd(u, w, qd, kd, a, gc, states, do, B, S):
    H = N_LIN_HEADS
    n, col, intra, st = _scan_specs(B, S)
    T = B * S

    def body(u_ref, w_ref, qd_ref, kd_ref, a_ref, g_ref, st_ref, do_ref,
             du_ref, dw_ref, dqd_ref, dkd_ref, da_ref, dg_ref):
        last = lax.broadcasted_iota(jnp.int32, (CHUNK, LANE), 0) == CHUNK - 1

        def step(i, d_state):
            c = n - 1 - i
            rows = pl.ds(pl.multiple_of(c * CHUNK, CHUNK), CHUNK)
            _, vjp = jax.vjp(_scan_step, u_ref[rows, :], w_ref[rows, :], qd_ref[rows, :], kd_ref[rows, :],
                             a_ref[rows, :], g_ref[rows, :][CHUNK - 1:CHUNK, :], st_ref[c].astype(F32))
            du, dw, dqd, dkd, da, dgl, d_prev = vjp((do_ref[rows, :].astype(F32), d_state))
            du_ref[rows, :] = du
            dw_ref[rows, :] = dw
            dqd_ref[rows, :] = dqd
            dkd_ref[rows, :] = dkd
            da_ref[rows, :] = da
            dg_ref[rows, :] = jnp.where(last, dgl, 0.0)
            return d_prev

        lax.fori_loop(0, n, step, jnp.zeros((LIN_HEAD_DIM, LIN_HEAD_DIM), F32))

    wide = jax.ShapeDtypeStruct((T, H * LANE), F32)
    return pl.pallas_call(body, grid=(B, H), in_specs=[col, col, col, col, intra, col, st, col],
                          out_specs=(col, col, col, col, intra, col),
                          out_shape=(wide, wide, wide, wide, jax.ShapeDtypeStruct((H, T, CHUNK), F32), wide),
                          compiler_params=_cp(("parallel", "parallel")),
                          name="gdn_scan_bwd")(u, w, qd, kd, a, gc, states, do)


GATE_COL = 3 * SEQ_MIX_WIDTH


def _post_fn(o, gate, gain):
    return o * lax.rsqrt(jnp.mean(o * o, axis=-1, keepdims=True) + EPS) * gain * _silu(gate)


def _gdn_post_fwd(o, proj, onorm, T):
    H = N_LIN_HEADS
    tm = _pick(T, 1024)

    def body(o_ref, g_ref, n_ref, y_ref):
        y_ref[...] = _post_fn(o_ref[...], g_ref[...], n_ref[...]).astype(y_ref.dtype)

    col = pl.BlockSpec((tm, LANE), lambda i, h: (i, h))
    return pl.pallas_call(body, grid=(T // tm, H),
                          in_specs=[col, pl.BlockSpec((tm, LANE), lambda i, h: (i, GATE_COL // LANE + h)),
                                    pl.BlockSpec((1, LANE), lambda i, h: (0, 0))],
                          out_specs=col, out_shape=jax.ShapeDtypeStruct((T, H * LANE), _MXU_DTYPE),
                          compiler_params=_cp(("parallel", "parallel")), name="gdn_post_fwd")(o, proj, onorm)


def _gdn_post_bwd(d_cat, o, proj, onorm, T):
    H = N_LIN_HEADS
    tm = _pick(T, 1024)

    def body(dy_ref, o_ref, g_ref, n_ref, do_ref, dg_ref, dn_ref):
        _, vjp = jax.vjp(_post_fn, o_ref[...], g_ref[...], n_ref[...])
        do, dg, dn = vjp(dy_ref[...].astype(F32))
        do_ref[...] = do
        dg_ref[...] = dg.astype(dg_ref.dtype)

        @pl.when((pl.program_id(0) == 0) & (pl.program_id(1) == 0))
        def _():
            dn_ref[...] = jnp.zeros_like(dn_ref)

        dn_ref[...] += dn

    col = pl.BlockSpec((tm, LANE), lambda i, h: (i, h))
    vec = pl.BlockSpec((1, LANE), lambda i, h: (0, 0))
    return pl.pallas_call(body, grid=(T // tm, H),
                          in_specs=[col, col, pl.BlockSpec((tm, LANE), lambda i, h: (i, GATE_COL // LANE + h)), vec],
                          out_specs=(col, col, vec),
                          out_shape=(jax.ShapeDtypeStruct((T, H * LANE), F32),
                                     jax.ShapeDtypeStruct((T, H * LANE), _MXU_DTYPE),
                                     jax.ShapeDtypeStruct((1, LANE), F32)),
                          compiler_params=_cp(("arbitrary", "arbitrary")), name="gdn_post_bwd")(d_cat, o, proj, onorm)


def _log_sigmoid(z):
    return jnp.minimum(z, 0.0) - jnp.log(1.0 + jnp.exp(-jnp.abs(z)))


def _split_dot(x, m):
    hi = x.astype(_MXU_DTYPE)
    lo = x - hi.astype(F32)
    return _dot(hi, m) + _dot(lo, m)


SB_QB = 256


def _sb_scores(q2, k_j, scale, valid):
    z = _dot(q2, k_j, 1, 1) * scale
    lb = _log_sigmoid(z)
    return lb, jnp.where(valid, lb - z, 0.0)


def _sb_consts(QB):
    ri = lax.broadcasted_iota(jnp.int32, (SB_BLOCK, SB_BLOCK), 0)
    ci = lax.broadcasted_iota(jnp.int32, (SB_BLOCK, SB_BLOCK), 1)
    row = lax.broadcasted_iota(jnp.int32, (2 * QB, SB_BLOCK), 0)
    col = lax.broadcasted_iota(jnp.int32, (2 * QB, SB_BLOCK), 1)
    lane = lax.broadcasted_iota(jnp.int32, (1, LANE), 1)
    return {
        "qpos_minus_col": row % QB - col,
        "after_excl": (ri > ci).astype(_MXU_DTYPE),
        "upto_incl": (ri <= ci).astype(_MXU_DTYPE),
        "upto_excl": (ri < ci).astype(_MXU_DTYPE),
        "lane": lane,
        "head0": lane < SB_HEAD_DIM,
    }


def _sb_stack(x, c):
    return jnp.concatenate([jnp.where(c["head0"], x, 0.0), jnp.where(c["head0"], 0.0, x)], axis=0)


def _sb_unstack(x2, c, QB):
    return jnp.where(c["head0"], x2[:QB], x2[QB:])


def _sb_fwd(proj, B, S):
    W = SEQ_MIX_WIDTH
    P = W // LANE
    QB = min(SB_QB, S)
    KB = SB_BLOCK
    scale = SB_HEAD_DIM ** -0.5

    def body(q_ref, k_ref, v_ref, o_ref, tot_ref):
        c = _sb_consts(QB)

        def q_loop(i, carry):
            qrows = pl.ds(pl.multiple_of(i * QB, QB), QB)
            q2 = _sb_stack(q_ref[qrows, :].astype(F32), c)
            nkb = (i + 1) * (QB // KB)

            def k_loop(t, st):
                acc, r = st
                j = nkb - 1 - t
                krows = pl.ds(pl.multiple_of(j * KB, KB), KB)
                valid = c["qpos_minus_col"] > j * KB - i * QB
                lb, l1 = _sb_scores(q2, k_ref[krows, :], scale, valid)
                a = jnp.where(valid, jnp.exp(lb + r + _split_dot(l1, c["after_excl"])), 0.0)
                return acc + _dot(a, v_ref[krows, :]), r + jnp.sum(l1, axis=-1, keepdims=True)

            acc, r = lax.fori_loop(0, nkb, k_loop, (jnp.zeros((2 * QB, LANE), F32), jnp.zeros((2 * QB, 1), F32)))
            o_ref[qrows, :] = _sb_unstack(acc, c, QB)
            tot_ref[qrows, :] = _sb_unstack(jnp.broadcast_to(r, (2 * QB, LANE)), c, QB)
            return carry

        lax.fori_loop(0, S // QB, q_loop, 0)

    def col(off):
        return pl.BlockSpec((S, LANE), lambda b, p: (b, off + p))

    out = jax.ShapeDtypeStruct((B * S, W), F32)
    return pl.pallas_call(body, grid=(B, P), in_specs=[col(0), col(P), col(2 * P)], out_specs=(col(0), col(0)),
                          out_shape=(out, out),
                          compiler_params=_cp(("parallel", "parallel")), name="sb_fwd")(proj, proj, proj)


def _sb_bwd(proj, tot, d_cat, B, S):
    W = SEQ_MIX_WIDTH
    P = W // LANE
    QB = min(SB_QB, S)
    KB = SB_BLOCK
    scale = SB_HEAD_DIM ** -0.5

    def body(q_ref, k_ref, v_ref, tot_ref, do_ref, dq_ref, dk_ref, dv_ref, dk_acc, dv_acc):
        c = _sb_consts(QB)
        dk_acc[...] = jnp.zeros_like(dk_acc)
        dv_acc[...] = jnp.zeros_like(dv_acc)

        def q_loop(i, carry):
            qrows = pl.ds(pl.multiple_of(i * QB, QB), QB)
            q2 = _sb_stack(q_ref[qrows, :].astype(F32), c)
            do2 = _sb_stack(do_ref[qrows, :].astype(F32), c)
            tot = tot_ref[qrows, :]
            total = jnp.concatenate(
                [jnp.sum(jnp.where(c["lane"] == h * SB_HEAD_DIM, tot, 0.0), axis=-1, keepdims=True) for h in range(2)],
                axis=0)

            def k_loop(j, st):
                dq_acc, p_l1, p_g = st
                krows = pl.ds(pl.multiple_of(j * KB, KB), KB)
                k_j = k_ref[krows, :]
                valid = c["qpos_minus_col"] > j * KB - i * QB
                lb, l1 = _sb_scores(q2, k_j, scale, valid)
                tail = total - p_l1 - _split_dot(l1, c["upto_incl"])
                a = jnp.where(valid, jnp.exp(lb + tail), 0.0)
                g = _dot(do2, v_ref[krows, :], 1, 1) * a
                g_before = p_g + _split_dot(g, c["upto_excl"])
                sig = jnp.exp(lb)
                dz = jnp.where(valid, g * (1.0 - sig) - g_before * sig, 0.0) * scale
                dk_acc[krows, :] += _dot(dz, q2, 0, 0)
                dv_acc[krows, :] += _dot(a, do2, 0, 0)
                return (dq_acc + _dot(dz, k_j), p_l1 + jnp.sum(l1, axis=-1, keepdims=True),
                        p_g + jnp.sum(g, axis=-1, keepdims=True))

            zero_col = jnp.zeros((2 * QB, 1), F32)
            dq2, _, _ = lax.fori_loop(0, (i + 1) * (QB // KB), k_loop,
                                      (jnp.zeros((2 * QB, LANE), F32), zero_col, zero_col))
            dq_ref[qrows, :] = _sb_unstack(dq2, c, QB).astype(dq_ref.dtype)
            return carry

        lax.fori_loop(0, S // QB, q_loop, 0)
        dk_ref[...] = dk_acc[...].astype(dk_ref.dtype)
        dv_ref[...] = dv_acc[...].astype(dv_ref.dtype)

    def col(off):
        return pl.BlockSpec((S, LANE), lambda b, p: (b, off + p))

    out = jax.ShapeDtypeStruct((B * S, W), _MXU_DTYPE)
    return pl.pallas_call(body, grid=(B, P), in_specs=[col(0), col(P), col(2 * P), col(0), col(0)],
                          out_specs=(col(0),) * 3, out_shape=(out,) * 3,
                          scratch_shapes=[pltpu.VMEM((S, LANE), F32), pltpu.VMEM((S, LANE), F32)],
                          compiler_params=_cp(("parallel", "parallel")), name="sb_bwd")(proj, proj, proj, tot, d_cat)


def _mem_fn(q, k, v):
    lane = lax.broadcasted_iota(jnp.int32, (1, X_WIDTH), 1)
    out = jnp.zeros(q.shape, F32)
    for h in range(N_X_HEADS):
        hm = (lane // X_HEAD_DIM) == h
        s = _dot(jnp.where(hm, q, 0.0), k, 1, 1) * (X_HEAD_DIM ** -0.5)
        e = jnp.exp(s - lax.stop_gradient(jnp.max(s, axis=-1, keepdims=True)))
        p = e / jnp.sum(e, axis=-1, keepdims=True)
        out = out + jnp.where(hm, _dot(p, v), 0.0)
    return out


def _mem_specs(S, q_col):
    ts = _pick(S, 1024)
    ns = S // ts
    qs = pl.BlockSpec((ts, X_WIDTH), lambda b, i: (b * ns + i, q_col // X_WIDTH))
    ks = pl.BlockSpec((N_MEM, X_WIDTH), lambda b, i: (b, 0))
    vs = pl.BlockSpec((N_MEM, X_WIDTH), lambda b, i: (b, 1))
    os = pl.BlockSpec((ts, X_WIDTH), lambda b, i: (b * ns + i, 0))
    return ts, ns, qs, ks, vs, os


def _mem_fwd(proj, q_col, mem_kv, B, S, name):
    ts, ns, qs, ks, vs, os = _mem_specs(S, q_col)

    def body(q_ref, k_ref, v_ref, o_ref):
        o_ref[...] = _mem_fn(q_ref[...].astype(F32), k_ref[...].astype(F32), v_ref[...].astype(F32))

    return pl.pallas_call(body, grid=(B, ns), in_specs=[qs, ks, vs], out_specs=os,
                          out_shape=jax.ShapeDtypeStruct((B * S, X_WIDTH), F32),
                          compiler_params=_cp(("parallel", "parallel")), name=name)(proj, mem_kv, mem_kv)


def _mem_bwd(proj, q_col, mem_kv, d_cat, B, S, name):
    ts, ns, qs, ks, vs, os = _mem_specs(S, q_col)

    def body(q_ref, k_ref, v_ref, do_ref, dq_ref, dk_ref, dv_ref):
        _, vjp = jax.vjp(_mem_fn, q_ref[...].astype(F32), k_ref[...].astype(F32), v_ref[...].astype(F32))
        dq, dk, dv = vjp(do_ref[...].astype(F32))
        dq_ref[...] = dq.astype(dq_ref.dtype)

        @pl.when(pl.program_id(1) == 0)
        def _():
            dk_ref[...] = jnp.zeros_like(dk_ref)
            dv_ref[...] = jnp.zeros_like(dv_ref)

        dk_ref[...] += dk
        dv_ref[...] += dv

    dos = pl.BlockSpec((ts, X_WIDTH), lambda b, i: (b * ns + i, SEQ_MIX_WIDTH // X_WIDTH))
    dq, dk, dv = pl.pallas_call(
        body, grid=(B, ns), in_specs=[qs, ks, vs, dos],
        out_specs=(os, pl.BlockSpec((N_MEM, X_WIDTH), lambda b, i: (b, 0)), pl.BlockSpec((N_MEM, X_WIDTH), lambda b, i: (b, 0))),
        out_shape=(jax.ShapeDtypeStruct((B * S, X_WIDTH), _MXU_DTYPE),
                   jax.ShapeDtypeStruct((B * N_MEM, X_WIDTH), F32), jax.ShapeDtypeStruct((B * N_MEM, X_WIDTH), F32)),
        compiler_params=_cp(("parallel", "arbitrary")), name=name)(proj, mem_kv, mem_kv, d_cat)
    return dq, dk, dv


def _peers():
    x, y, c = lax.axis_index("x"), lax.axis_index("y"), lax.axis_index("c")
    me = 4 * x + 2 * y + c
    out = []
    for fx, fy, fc in [(0, 0, 1), (1, 0, 0), (0, 1, 0), (1, 1, 0), (1, 0, 1), (0, 1, 1), (1, 1, 1)]:
        px, py, pc = x ^ fx, y ^ fy, c ^ fc
        out.append(((px, py, pc), 4 * px + 2 * py + pc))
    return me, out


ANY = pl.BlockSpec(memory_space=pl.ANY)


def _all_gather(shard):
    R = shard.shape[0]

    def body(x_ref, o_ref, send_sems, recv_sems, local_sem):
        me, peers = _peers()
        mine = pltpu.make_async_copy(x_ref, o_ref.at[me], local_sem)
        mine.start()
        copies = []
        for k, (dev, _) in enumerate(peers):
            cp = pltpu.make_async_remote_copy(src_ref=x_ref, dst_ref=o_ref.at[me], send_sem=send_sems.at[k],
                                              recv_sem=recv_sems.at[k], device_id=dev,
                                              device_id_type=pl.DeviceIdType.MESH)
            cp.start()
            copies.append(cp)
        for k, (dev, idx) in enumerate(peers):
            pltpu.make_async_remote_copy(src_ref=x_ref, dst_ref=o_ref.at[idx], send_sem=send_sems.at[k],
                                         recv_sem=recv_sems.at[k], device_id=dev,
                                         device_id_type=pl.DeviceIdType.MESH).wait_recv()
        for cp in copies:
            cp.wait_send()
        mine.wait()

    return pl.pallas_call(body, in_specs=[ANY], out_specs=ANY,
                          out_shape=jax.ShapeDtypeStruct((N_DEV, R, LANE), shard.dtype),
                          scratch_shapes=[pltpu.SemaphoreType.DMA((7,)), pltpu.SemaphoreType.DMA((7,)),
                                          pltpu.SemaphoreType.DMA],
                          compiler_params=pltpu.CompilerParams(has_side_effects=True),
                          name="all_gather_weights")(shard)


def _exchange(big, small):
    R = big.shape[1]
    K = small.shape[0]

    def body(b_ref, s_ref, ob_ref, os_ref, send_sems, recv_sems, local_sems):
        me, peers = _peers()
        own_b = pltpu.make_async_copy(b_ref.at[me], ob_ref.at[me], local_sems.at[0])
        own_s = pltpu.make_async_copy(s_ref, os_ref.at[me], local_sems.at[1])
        own_b.start()
        own_s.start()
        copies = []
        for k, (dev, idx) in enumerate(peers):
            for t, (src, dst) in enumerate([(b_ref.at[idx], ob_ref.at[me]), (s_ref, os_ref.at[me])]):
                cp = pltpu.make_async_remote_copy(src_ref=src, dst_ref=dst, send_sem=send_sems.at[2 * k + t],
                                                  recv_sem=recv_sems.at[2 * k + t], device_id=dev,
                                                  device_id_type=pl.DeviceIdType.MESH)
                cp.start()
                copies.append(cp)
        for k, (dev, idx) in enumerate(peers):
            for t, (src, dst) in enumerate([(b_ref.at[me], ob_ref.at[idx]), (s_ref, os_ref.at[idx])]):
                pltpu.make_async_remote_copy(src_ref=src, dst_ref=dst, send_sem=send_sems.at[2 * k + t],
                                             recv_sem=recv_sems.at[2 * k + t], device_id=dev,
                                             device_id_type=pl.DeviceIdType.MESH).wait_recv()
        for cp in copies:
            cp.wait_send()
        own_b.wait()
        own_s.wait()

    return pl.pallas_call(body, in_specs=[ANY, ANY], out_specs=(ANY, ANY),
                          out_shape=(jax.ShapeDtypeStruct((N_DEV, R, LANE), big.dtype),
                                     jax.ShapeDtypeStruct((N_DEV, K, LANE), small.dtype)),
                          scratch_shapes=[pltpu.SemaphoreType.DMA((14,)), pltpu.SemaphoreType.DMA((14,)),
                                          pltpu.SemaphoreType.DMA((2,))],
                          compiler_params=pltpu.CompilerParams(has_side_effects=True),
                          name="exchange_grads")(big, small)


def _adamw_math(w, g, m, v):
    m = ADAM_B1 * m + (1.0 - ADAM_B1) * g
    v = ADAM_B2 * v + (1.0 - ADAM_B2) * (g * g)
    m_hat = m / (1.0 - ADAM_B1 ** ADAM_STEP)
    v_hat = v / (1.0 - ADAM_B2 ** ADAM_STEP)
    delta = -ADAM_LR * (m_hat / (jnp.sqrt(v_hat) + ADAM_EPS) + ADAM_WD * w)
    return delta, m, v


def _adamw_shard(own, recv, w, m, v):
    R = own.shape[0]
    tr = _pick(R, 1024)

    def body(own_ref, recv_ref, w_ref, m_ref, v_ref, g_ref, d_ref, nm_ref, nv_ref):
        me, _ = _peers()
        g = own_ref[...]
        for p in range(N_DEV):
            g = g + jnp.where(p == me, 0.0, recv_ref[p].astype(F32))
        delta, nm, nv = _adamw_math(w_ref[...], g, m_ref[...], v_ref[...])
        g_ref[...] = g
        d_ref[...] = delta
        nm_ref[...] = nm
        nv_ref[...] = nv

    row = pl.BlockSpec((tr, LANE), lambda i: (i, 0))
    out = jax.ShapeDtypeStruct((R, LANE), F32)
    return pl.pallas_call(body, grid=(R // tr,),
                          in_specs=[row, pl.BlockSpec((N_DEV, tr, LANE), lambda i: (0, i, 0)), row, row, row],
                          out_specs=(row,) * 4, out_shape=(out,) * 4,
                          compiler_params=_cp(("parallel",)), name="adamw_shard")(own, recv, w, m, v)


def _adamw_replicated(parts, w, m, v):
    K = w.shape[0]

    def body(p_ref, w_ref, m_ref, v_ref, g_ref, d_ref, nm_ref, nv_ref):
        g = p_ref[0]
        for p in range(1, N_DEV):
            g = g + p_ref[p]
        delta, nm, nv = _adamw_math(w_ref[...], g, m_ref[...], v_ref[...])
        g_ref[...] = g
        d_ref[...] = delta
        nm_ref[...] = nm
        nv_ref[...] = nv

    out = jax.ShapeDtypeStruct((K, LANE), F32)
    return pl.pallas_call(body, out_shape=(out,) * 4, compiler_params=_cp(), name="adamw_replicated")(parts, w, m, v)


_SHARDED = (("w_in_a", 1), ("conv_w_a", 2), ("w_in_b", 2), ("w_mem_kv", 1), ("w_out", 1), ("w_up", 2), ("w_down", 1))
_REPLICATED = ("mem_norm", "norm_pre_mix", "norm_post_mix", "norm_pre_mlp", "norm_post_mlp", "a_log_a", "dt_bias_a", "onorm_a")
ROW_ALIGN = 16


def _rows(n_elems, align=ROW_ALIGN):
    r = -(-n_elems // LANE)
    return -(-r // align) * align


def _pack(arrays, align=ROW_ALIGN, total=None, lead=()):
    parts = []
    for a in arrays:
        n = math.prod(a.shape[len(lead):])
        flat = a.reshape(lead + (n,))
        r = _rows(n, align)
        flat = jnp.pad(flat, [(0, 0)] * len(lead) + [(0, r * LANE - n)])
        parts.append(flat.reshape(lead + (r, LANE)))
    out = jnp.concatenate(parts, axis=len(lead))
    if total is not None and out.shape[len(lead)] < total:
        out = jnp.pad(out, [(0, 0)] * len(lead) + [(0, total - out.shape[len(lead)]), (0, 0)])
    return out


def _unpack(flat, shapes, align=ROW_ALIGN, lead=()):
    outs = []
    r0 = 0
    for shp in shapes:
        n = math.prod(shp)
        r = _rows(n, align)
        part = lax.slice_in_dim(flat, r0, r0 + r, axis=len(lead))
        part = part.reshape(lead + (r * LANE,))
        part = lax.slice_in_dim(part, 0, n, axis=len(lead))
        outs.append(part.reshape(lead + tuple(shp)))
        r0 += r
    return outs


def _to_full(gathered, axis):
    g = jnp.moveaxis(gathered, 0, axis)
    shp = g.shape
    return g.reshape(shp[:axis] + (shp[axis] * shp[axis + 1],) + shp[axis + 2:])


def _to_blocks(full, axis):
    shp = full.shape
    g = full.reshape(shp[:axis] + (N_DEV, shp[axis] // N_DEV) + shp[axis + 1:])
    return jnp.moveaxis(g, axis, 0)


def _widen_in_a(w):
    main = w[:, :4 * SEQ_MIX_WIDTH]
    small = w[:, 4 * SEQ_MIX_WIDTH:4 * SEQ_MIX_WIDTH + 2 * N_LIN_HEADS]
    memq = w[:, 4 * SEQ_MIX_WIDTH + 2 * N_LIN_HEADS:]
    pad = jnp.zeros((w.shape[0], IN_A_PAD - IN_A), w.dtype)
    return jnp.concatenate([main, memq, small, pad], axis=1)


def _narrow_in_a(g):
    main = g[:, :4 * SEQ_MIX_WIDTH]
    memq = g[:, 4 * SEQ_MIX_WIDTH:4 * SEQ_MIX_WIDTH + X_WIDTH]
    small = g[:, SM_COL:SM_COL + 2 * N_LIN_HEADS]
    return jnp.concatenate([main, small, memq], axis=1)


def _row128(v):
    return jnp.pad(v.reshape(1, -1), ((0, 0), (0, LANE - v.shape[-1])))


def _local_step(x, mem, target, p):
    B, S, D = x.shape
    T = B * S
    md = _MXU_DTYPE
    x0 = x.reshape(T, D)
    tgt = target.reshape(T, D)
    memf = mem.reshape(B * N_MEM, D)
    vec = lambda a: a.reshape(1, -1)

    mem_n = _norm_fwd(memf, vec(p["mem_norm"]), out_dtype=md, name="norm_mem")
    w_in = [_widen_in_a(p["w_in_a"][0]), p["w_in_b"][0]]
    memq_col = [4 * SEQ_MIX_WIDTH, 3 * SEQ_MIX_WIDTH]
    alog = _row128(p["a_log_a"][0])
    dtb = _row128(p["dt_bias_a"][0])
    onorm = vec(p["onorm_a"][0])
    conv_w = p["conv_w_a"][0]
    saved = []
    xi = x0
    for i in range(2):
        s = {"x_in": xi}
        h1 = _norm_fwd(xi, vec(p["norm_pre_mix"][i]), out_dtype=md, name=f"norm_pre_mix{i}")
        proj = _mm(h1, w_in[i], out_dtypes=(F32 if i == 0 else md,), name=f"in_proj{i}")
        mem_kv = _mm(mem_n, p["w_mem_kv"][i], out_dtypes=(md,), name=f"mem_kv{i}")
        if i == 0:
            act = _gdn_conv_fwd(proj, conv_w, B, S)
            beta, gc = _gdn_gates_fwd(proj, alog, dtb, B, S)
            u, w, qd, kd, intra = _gdn_prep_fwd(act, beta, gc, B, S)
            o, states = _gdn_scan_fwd(u, w, qd, kd, intra, gc, B, S)
            mix = _gdn_post_fwd(o, proj, onorm, T)
            s.update(act=act, beta=beta, gc=gc, u=u, w=w, qd=qd, kd=kd, intra=intra, o=o, states=states)
        else:
            mix, tot = _sb_fwd(proj, B, S)
            s.update(tot=tot)
        cross = _mem_fwd(proj, memq_col[i], mem_kv, B, S, name=f"mem_fwd{i}")
        cat = jnp.concatenate([mix.astype(md), cross.astype(md)], axis=1)
        y = _mm(cat, p["w_out"][i], name=f"out_proj{i}")
        x_mid = _norm_fwd(y, vec(p["norm_post_mix"][i]), resid=xi, name=f"norm_post_mix{i}")
        h2 = _norm_fwd(x_mid, vec(p["norm_pre_mlp"][i]), out_dtype=md, name=f"norm_pre_mlp{i}")
        a_act, r = _mm(h2, p["w_up"][i], out_dtypes=(md, md), epilogue=_relu2_epilogue, name=f"up_proj{i}")
        y2 = _mm(a_act, p["w_down"][i], name=f"down_proj{i}")
        x_out = _norm_fwd(y2, vec(p["norm_post_mlp"][i]), resid=x_mid, name=f"norm_post_mlp{i}")
        s.update(h1=h1, proj=proj, mem_kv=mem_kv, cat=cat, y=y, x_mid=x_mid, h2=h2, a_act=a_act, r=r, y2=y2)
        saved.append(s)
        xi = x_out

    loss_row, dx = _loss_head(xi, tgt)

    g = {}
    d_mem_n = None
    gn = {k: [None, None] for k in ("norm_pre_mix", "norm_post_mix", "norm_pre_mlp", "norm_post_mlp")}
    g_w_mem_kv, g_w_out, g_w_up, g_w_down = [None, None], [None, None], [None, None], [None, None]
    for i in (1, 0):
        s = saved[i]
        d_y2, gn["norm_post_mlp"][i] = _norm_bwd(dx, s["y2"], vec(p["norm_post_mlp"][i]), name=f"norm_post_mlp_bwd{i}")
        g_w_down[i] = _mm(s["a_act"], d_y2, ta=True, name=f"down_proj_dw{i}")
        d_u = _mm(d_y2, p["w_down"][i], tb=True, out_dtypes=(md,), epilogue=_drelu2_epilogue, extras=(s["r"],),
                  name=f"down_proj_dx{i}")
        g_w_up[i] = _mm(s["h2"], d_u, ta=True, name=f"up_proj_dw{i}")
        d_h2 = _mm(d_u, p["w_up"][i], tb=True, name=f"up_proj_dx{i}")
        dx, gn["norm_pre_mlp"][i] = _norm_bwd(d_h2, s["x_mid"], vec(p["norm_pre_mlp"][i]), resid=dx,
                                              name=f"norm_pre_mlp_bwd{i}")
        d_y, gn["norm_post_mix"][i] = _norm_bwd(dx, s["y"], vec(p["norm_post_mix"][i]), name=f"norm_post_mix_bwd{i}")
        g_w_out[i] = _mm(s["cat"], d_y, ta=True, name=f"out_proj_dw{i}")
        d_cat = _mm(d_y, p["w_out"][i], tb=True, name=f"out_proj_dx{i}")
        d_memq, d_mk, d_mv = _mem_bwd(s["proj"], memq_col[i], s["mem_kv"], d_cat, B, S, name=f"mem_bwd{i}")
        d_mem_kv = jnp.concatenate([d_mk.astype(md), d_mv.astype(md)], axis=1)
        g_w_mem_kv[i] = _mm(mem_n, d_mem_kv, ta=True, name=f"mem_kv_dw{i}")
        d_mn = _mm(d_mem_kv, p["w_mem_kv"][i], tb=True, name=f"mem_kv_dx{i}")
        d_mem_n = d_mn if d_mem_n is None else d_mem_n + d_mn
        if i == 0:
            d_o, d_gate, g["onorm_a"] = _gdn_post_bwd(d_cat, s["o"], s["proj"], onorm, T)
            du, dw, dqd, dkd, da, dgc_s = _gdn_scan_bwd(s["u"], s["w"], s["qd"], s["kd"], s["intra"], s["gc"],
                                                         s["states"], d_o, B, S)
            dq, dk, dv, d_beta, d_gc = _gdn_prep_bwd(s["act"], s["beta"], s["gc"], du, dw, dqd, dkd, da, B, S)
            d_qkv, g["conv_w_a"] = _gdn_conv_bwd(dq, dk, dv, s["proj"], conv_w, B, S)
            d_sm, g["a_log_a"], g["dt_bias_a"] = _gdn_gates_bwd(d_beta, d_gc + dgc_s, s["proj"], alog, dtb, B, S)
            pad = jnp.zeros((T, IN_A_PAD - SM_COL - LANE), md)
            d_proj = jnp.concatenate([d_qkv, d_gate, d_memq, d_sm, pad], axis=1)
        else:
            dq, dk, dv = _sb_bwd(s["proj"], s["tot"], d_cat, B, S)
            d_proj = jnp.concatenate([dq, dk, dv, d_memq], axis=1)
        g_w_in = _mm(s["h1"], d_proj, ta=True, name=f"in_proj_dw{i}")
        d_h1 = _mm(d_proj, w_in[i], tb=True, name=f"in_proj_dx{i}")
        dx, gn["norm_pre_mix"][i] = _norm_bwd(d_h1, s["x_in"], vec(p["norm_pre_mix"][i]), resid=dx,
                                              name=f"norm_pre_mix_bwd{i}")
        if i == 0:
            g["w_in_a"] = _narrow_in_a(g_w_in)[None]
        else:
            g["w_in_b"] = g_w_in[None]
    _, g_mem_norm = _norm_bwd(d_mem_n, memf, vec(p["mem_norm"]), name="norm_mem_bwd")
    g["mem_norm"] = g_mem_norm.reshape(-1)
    for k, v in gn.items():
        g[k] = jnp.concatenate(v, axis=0)
    g["w_mem_kv"] = jnp.stack(g_w_mem_kv)
    g["w_out"] = jnp.stack(g_w_out)
    g["w_up"] = jnp.stack(g_w_up)
    g["w_down"] = jnp.stack(g_w_down)
    g["conv_w_a"] = g["conv_w_a"][None]
    g["a_log_a"] = g["a_log_a"][:, :N_LIN_HEADS]
    g["dt_bias_a"] = g["dt_bias_a"][:, :N_LIN_HEADS]
    return loss_row, dx.reshape(B, S, D), g


def kernel(x, mem, mem_norm, norm_pre_mix, norm_post_mix, norm_pre_mlp, norm_post_mlp, w_in_a, conv_w_a, a_log_a, dt_bias_a, onorm_a, w_in_b, w_mem_kv, w_out, w_up, w_down, loss_target, m_mem_norm, m_norm_pre_mix, m_norm_post_mix, m_norm_pre_mlp, m_norm_post_mlp, m_w_in_a, m_conv_w_a, m_a_log_a, m_dt_bias_a, m_onorm_a, m_w_in_b, m_w_mem_kv, m_w_out, m_w_up, m_w_down, v_mem_norm, v_norm_pre_mix, v_norm_post_mix, v_norm_pre_mlp, v_norm_post_mlp, v_w_in_a, v_conv_w_a, v_a_log_a, v_dt_bias_a, v_onorm_a, v_w_in_b, v_w_mem_kv, v_w_out, v_w_up, v_w_down):
    args = dict(locals())
    big_names = [n for n, _ in _SHARDED]
    axes = dict(_SHARDED)
    shard_shapes = [args[n].shape for n in big_names]
    rows_total = -(-sum(_rows(math.prod(s)) for s in shard_shapes) // 1024) * 1024

    exact = ("conv_w_a",) if _WIRE_DTYPE != F32 else ()
    wire = [lax.bitcast_convert_type(args[n], _WIRE_DTYPE) if n in exact else args[n].astype(_WIRE_DTYPE)
            for n in big_names]
    wire_shapes = [a.shape for a in wire]
    wire_rows = -(-sum(_rows(math.prod(s)) for s in wire_shapes) // ROW_ALIGN) * ROW_ALIGN
    gathered = _all_gather(_pack(wire, total=wire_rows))
    p = {}
    for n, b in zip(big_names, _unpack(gathered, wire_shapes, lead=(N_DEV,))):
        b = lax.bitcast_convert_type(b, F32) if n in exact else b.astype(_MXU_DTYPE)
        p[n] = _to_full(b, axes[n])
    for n in _REPLICATED:
        p[n] = args[n]
    w_flat = _pack([args[n] for n in big_names], total=rows_total)

    loss_row, grad_x, g = _local_step(x, mem, loss_target, p)
    loss = lax.psum(loss_row[0, 0], ("x", "y", "c"))

    g_blocks = _pack([_to_blocks(g[n], axes[n]) for n in big_names], total=rows_total, lead=(N_DEV,))
    rep_shapes = [args[n].shape for n in _REPLICATED]
    g_small = _pack([g[n] for n in _REPLICATED], align=8)
    recv_big, recv_small = _exchange(g_blocks.astype(_WIRE_DTYPE), g_small)

    me = 4 * lax.axis_index("x") + 2 * lax.axis_index("y") + lax.axis_index("c")
    own = lax.dynamic_index_in_dim(g_blocks, me, axis=0, keepdims=False)
    m_flat = _pack([args["m_" + n] for n in big_names], total=rows_total)
    v_flat = _pack([args["v_" + n] for n in big_names], total=rows_total)
    outs_big = [_unpack(f, shard_shapes) for f in _adamw_shard(own, recv_big, w_flat, m_flat, v_flat)]
    outs_small = [_unpack(f, rep_shapes, align=8) for f in _adamw_replicated(
        recv_small, _pack([args[n] for n in _REPLICATED], align=8),
        _pack([args["m_" + n] for n in _REPLICATED], align=8), _pack([args["v_" + n] for n in _REPLICATED], align=8))]

    order = ["mem_norm", "norm_pre_mix", "norm_post_mix", "norm_pre_mlp", "norm_post_mlp", "w_in_a", "conv_w_a",
             "a_log_a", "dt_bias_a", "onorm_a", "w_in_b", "w_mem_kv", "w_out", "w_up", "w_down"]
    result = [loss, grad_x]
    for kind in range(4):
        for n in order:
            if n in axes:
                result.append(outs_big[kind][big_names.index(n)])
            else:
                result.append(outs_small[kind][_REPLICATED.index(n)])
    return tuple(result)
```

```python
import functools
import math

import jax
import jax.numpy as jnp
from jax import lax
from jax.experimental import pallas as pl
from jax.experimental.pallas import tpu as pltpu

F32 = jnp.float32
_MXU_DTYPE = jnp.bfloat16
_WIRE_DTYPE = jnp.bfloat16
_HI = lax.Precision.HIGHEST

D_MODEL = 1024
N_DEV = 8
N_MEM = 256
X_WIDTH = 256
N_X_HEADS = 4
X_HEAD_DIM = 64
SEQ_MIX_WIDTH = 768
LIN_HEAD_DIM = 128
N_LIN_HEADS = 6
CONV_WIDTH = 4
CHUNK = 64
SB_HEAD_DIM = 64
SB_BLOCK = 128
D_FF = 4096
EPS = 1e-6
IN_A = 3340
IN_A_PAD = 3584
IN_B = 2560
SM_COL = 3328

ADAM_LR = 0.001
ADAM_B1 = 0.9
ADAM_B2 = 0.999
ADAM_EPS = 1e-08
ADAM_WD = 0.01
ADAM_STEP = 10

LANE = 128
VMEM_LIMIT = 56 * 1024 * 1024


def _cp(sem=None):
    return pltpu.CompilerParams(dimension_semantics=sem, vmem_limit_bytes=VMEM_LIMIT)


def _pick(n, target):
    if n <= target:
        return n
    best = None
    for t in range(LANE, target + 1, LANE):
        if n % t == 0:
            best = t
    assert best is not None, (n, target)
    return best


def _dot(a, b, ca=1, cb=0):
    return lax.dot_general(a.astype(_MXU_DTYPE), b.astype(_MXU_DTYPE), (((ca,), (cb,)), ((), ())),
                           preferred_element_type=F32)


def _bdot(a, b, ca, cb):
    return lax.dot_general(a.astype(_MXU_DTYPE), b.astype(_MXU_DTYPE), (((ca,), (cb,)), ((0,), (0,))),
                           preferred_element_type=F32)


def _bdot_hi(a, b):
    return lax.dot_general(a, b, (((2,), (1,)), ((0,), (0,))), precision=_HI, preferred_element_type=F32)


def _sigmoid(x):
    return 1.0 / (1.0 + jnp.exp(-x))


def _silu(x):
    return x * _sigmoid(x)


def _softplus(x):
    return jnp.maximum(x, 0.0) + jnp.log(1.0 + jnp.exp(-jnp.abs(x)))


def _rms(x, g):
    return x * lax.rsqrt(jnp.mean(x * x, axis=-1, keepdims=True) + EPS) * g


def _norm_fwd(x, g, resid=None, out_dtype=F32, name="norm_fwd"):
    T, D = x.shape
    tm = _pick(T, 512)
    has_resid = resid is not None

    def body(*refs):
        if has_resid:
            x_ref, g_ref, r_ref, o_ref = refs
        else:
            x_ref, g_ref, o_ref = refs
        y = _rms(x_ref[...].astype(F32), g_ref[...])
        if has_resid:
            y = r_ref[...] + y
        o_ref[...] = y.astype(out_dtype)

    row = pl.BlockSpec((tm, D), lambda i: (i, 0))
    in_specs = [row, pl.BlockSpec((1, D), lambda i: (0, 0))] + ([row] if has_resid else [])
    args = (x, g) + ((resid,) if has_resid else ())
    return pl.pallas_call(body, grid=(T // tm,), in_specs=in_specs, out_specs=row,
                          out_shape=jax.ShapeDtypeStruct((T, D), out_dtype),
                          compiler_params=_cp(("parallel",)), name=name)(*args)


def _norm_bwd(dy, x, g, resid=None, name="norm_bwd"):
    T, D = x.shape
    tm = _pick(T, 512)
    has_resid = resid is not None

    def body(*refs):
        if has_resid:
            dy_ref, x_ref, g_ref, r_ref, dx_ref, dg_ref = refs
        else:
            dy_ref, x_ref, g_ref, dx_ref, dg_ref = refs
        _, vjp = jax.vjp(_rms, x_ref[...].astype(F32), g_ref[...])
        dx, dg = vjp(dy_ref[...].astype(F32))
        if has_resid:
            dx = r_ref[...] + dx
        dx_ref[...] = dx

        @pl.when(pl.program_id(0) == 0)
        def _():
            dg_ref[...] = jnp.zeros_like(dg_ref)

        dg_ref[...] += dg

    row = pl.BlockSpec((tm, D), lambda i: (i, 0))
    vec = pl.BlockSpec((1, D), lambda i: (0, 0))
    in_specs = [row, row, vec] + ([row] if has_resid else [])
    args = (dy, x, g) + ((resid,) if has_resid else ())
    return pl.pallas_call(body, grid=(T // tm,), in_specs=in_specs, out_specs=(row, vec),
                          out_shape=(jax.ShapeDtypeStruct((T, D), F32), jax.ShapeDtypeStruct((1, D), F32)),
                          compiler_params=_cp(("arbitrary",)), name=name)(*args)


def _mm(a, b, *, ta=False, tb=False, out_dtypes=(F32,), epilogue=None, extras=(), name="mm",
        tm_t=1024, tn_t=1024, tk_t=1024):
    M, K = (a.shape[1], a.shape[0]) if ta else a.shape
    N = b.shape[0] if tb else b.shape[1]
    assert (b.shape[1] if tb else b.shape[0]) == K, (a.shape, b.shape, ta, tb)
    tm, tn, tk = _pick(M, tm_t), _pick(N, tn_t), _pick(K, tk_t)
    nk = K // tk
    n_extra = len(extras)
    n_out = len(out_dtypes)

    def body(*refs):
        a_ref, b_ref = refs[0], refs[1]
        e_refs = refs[2:2 + n_extra]
        o_refs = refs[2 + n_extra:2 + n_extra + n_out]

        def finish(acc):
            outs = (acc,) if epilogue is None else epilogue(acc, *[e[...] for e in e_refs])
            for o_ref, o in zip(o_refs, outs):
                o_ref[...] = o.astype(o_ref.dtype)

        d = _dot(a_ref[...], b_ref[...], 0 if ta else 1, 1 if tb else 0)
        if nk == 1:
            finish(d)
            return
        acc_ref = refs[-1]
        k = pl.program_id(2)

        @pl.when(k == 0)
        def _():
            acc_ref[...] = d

        @pl.when((k > 0) & (k < nk - 1))
        def _():
            acc_ref[...] += d

        @pl.when(k == nk - 1)
        def _():
            finish(acc_ref[...] + d)

    a_spec = pl.BlockSpec((tk, tm), lambda i, j, k: (k, i)) if ta else pl.BlockSpec((tm, tk), lambda i, j, k: (i, k))
    b_spec = pl.BlockSpec((tn, tk), lambda i, j, k: (j, k)) if tb else pl.BlockSpec((tk, tn), lambda i, j, k: (k, j))
    o_spec = pl.BlockSpec((tm, tn), lambda i, j, k: (i, j))
    outs = pl.pallas_call(
        body, grid=(M // tm, N // tn, nk),
        in_specs=[a_spec, b_spec] + [o_spec] * n_extra,
        out_specs=tuple([o_spec] * n_out),
        out_shape=tuple(jax.ShapeDtypeStruct((M, N), dt) for dt in out_dtypes),
        scratch_shapes=[pltpu.VMEM((tm, tn), F32)] if nk > 1 else [],
        compiler_params=_cp(("parallel", "parallel", "arbitrary")), name=name)(a, b, *extras)
    return outs[0] if n_out == 1 else outs


def _relu2_epilogue(acc):
    r = jnp.maximum(acc, 0.0)
    return r * r, r


def _drelu2_epilogue(acc, r):
    return (acc * (2.0 * r.astype(F32)),)


def _loss_head(x, target, name="loss_head"):
    T, D = x.shape
    tm = _pick(T, 512)

    def body(x_ref, t_ref, l_ref, dx_ref):
        e = x_ref[...] - t_ref[...]
        dx_ref[...] = e * (1.0 / D)

        @pl.when(pl.program_id(0) == 0)
        def _():
            l_ref[...] = jnp.zeros_like(l_ref)

        part = 0.5 * jnp.sum(jnp.mean(e * e, axis=-1, keepdims=True), axis=0, keepdims=True)
        l_ref[...] += jnp.broadcast_to(part, l_ref.shape)

    row = pl.BlockSpec((tm, D), lambda i: (i, 0))
    return pl.pallas_call(body, grid=(T // tm,), in_specs=[row, row],
                          out_specs=(pl.BlockSpec((1, LANE), lambda i: (0, 0)), row),
                          out_shape=(jax.ShapeDtypeStruct((1, LANE), F32), jax.ShapeDtypeStruct((T, D), F32)),
                          compiler_params=_cp(("arbitrary",)), name=name)(x, target)


def _shift_down(x, k, row):
    return jnp.where(row >= k, pltpu.roll(x, k, 0), 0.0)


def _shift_up(x, k, row, n):
    return jnp.where(row < n - k, pltpu.roll(x, n - k, 0), 0.0)


def _conv_taps(x, w, row):
    y = x * w[CONV_WIDTH - 1:CONV_WIDTH, :]
    for i in range(CONV_WIDTH - 1):
        y = y + _shift_down(x, CONV_WIDTH - 1 - i, row) * w[i:i + 1, :]
    return y


def _qkv_act(xc, j):
    s = _silu(xc)
    n = s * lax.rsqrt(jnp.sum(s * s, axis=-1, keepdims=True) + EPS)
    n = n * jnp.where(j < N_LIN_HEADS, LIN_HEAD_DIM ** -0.5, 1.0)
    return jnp.where(j < 2 * N_LIN_HEADS, n, s)


def _gdn_conv_fwd(proj, conv_w, B, S):
    nblk = 3 * N_LIN_HEADS

    def body(p_ref, w_ref, o_ref):
        j = pl.program_id(1)
        x = p_ref[...]
        row = lax.broadcasted_iota(jnp.int32, x.shape, 0)
        o_ref[...] = _qkv_act(_conv_taps(x, w_ref[...], row), j)

    blk = pl.BlockSpec((S, LANE), lambda b, j: (b, j))
    return pl.pallas_call(body, grid=(B, nblk),
                          in_specs=[blk, pl.BlockSpec((CONV_WIDTH, LANE), lambda b, j: (0, j))],
                          out_specs=blk, out_shape=jax.ShapeDtypeStruct((B * S, nblk * LANE), F32),
                          compiler_params=_cp(("parallel", "parallel")), name="gdn_conv_fwd")(proj, conv_w)


def _gdn_conv_bwd(dq, dk, dv, proj, conv_w, B, S):
    nblk = 3 * N_LIN_HEADS
    H = N_LIN_HEADS

    def body(dq_ref, dk_ref, dv_ref, p_ref, w_ref, dp_ref, dw_ref):
        j = pl.program_id(0)
        b = pl.program_id(1)
        x = p_ref[...]
        w = w_ref[...]
        row = lax.broadcasted_iota(jnp.int32, x.shape, 0)
        d_act = jnp.where(j < H, dq_ref[...], jnp.where(j < 2 * H, dk_ref[...], dv_ref[...]))
        _, vjp = jax.vjp(lambda t: _qkv_act(t, j), _conv_taps(x, w, row))
        (d_xc,) = vjp(d_act)
        dx = d_xc * w[CONV_WIDTH - 1:CONV_WIDTH, :]
        for i in range(CONV_WIDTH - 1):
            dx = dx + _shift_up(d_xc, CONV_WIDTH - 1 - i, row, S) * w[i:i + 1, :]
        dp_ref[...] = dx.astype(dp_ref.dtype)

        @pl.when(b == 0)
        def _():
            dw_ref[...] = jnp.zeros_like(dw_ref)

        for i in range(CONV_WIDTH):
            xs = x if i == CONV_WIDTH - 1 else _shift_down(x, CONV_WIDTH - 1 - i, row)
            dw_ref[i:i + 1, :] += jnp.sum(d_xc * xs, axis=0, keepdims=True)

    blk = pl.BlockSpec((S, LANE), lambda j, b: (b, j))
    wblk = pl.BlockSpec((CONV_WIDTH, LANE), lambda j, b: (0, j))
    return pl.pallas_call(
        body, grid=(nblk, B),
        in_specs=[pl.BlockSpec((S, LANE), lambda j, b: (b, jnp.clip(j, 0, H - 1))),
                  pl.BlockSpec((S, LANE), lambda j, b: (b, jnp.clip(j - H, 0, H - 1))),
                  pl.BlockSpec((S, LANE), lambda j, b: (b, jnp.clip(j - 2 * H, 0, H - 1))),
                  blk, wblk],
        out_specs=(blk, wblk),
        out_shape=(jax.ShapeDtypeStruct((B * S, nblk * LANE), _MXU_DTYPE),
                   jax.ShapeDtypeStruct((CONV_WIDTH, nblk * LANE), F32)),
        compiler_params=_cp(("parallel", "arbitrary")), name="gdn_conv_bwd")(dq, dk, dv, proj, conv_w)


def _chunk_cumsum(x, row):
    pos = row % CHUNK
    k = 1
    while k < CHUNK:
        x = x + jnp.where(pos >= k, pltpu.roll(x, k, 0), 0.0)
        k *= 2
    return x


def _chunk_rev_cumsum(x, row, n):
    pos = row % CHUNK
    k = 1
    while k < CHUNK:
        x = x + jnp.where(pos < CHUNK - k, pltpu.roll(x, n - k, 0), 0.0)
        k *= 2
    return x


def _gdn_gates_fwd(proj, a_log, dt_bias, B, S):
    H = N_LIN_HEADS

    def body(sm_ref, al_ref, dt_ref, beta_ref, gc_ref):
        sm = sm_ref[...]
        row = lax.broadcasted_iota(jnp.int32, (S, LANE), 0)
        for h in range(H):
            beta = _sigmoid(sm[:, h:h + 1])
            g = -jnp.exp(al_ref[0:1, h:h + 1]) * _softplus(sm[:, H + h:H + h + 1] + dt_ref[0:1, h:h + 1])
            beta_ref[:, h * LANE:(h + 1) * LANE] = jnp.broadcast_to(beta, (S, LANE))
            gc_ref[:, h * LANE:(h + 1) * LANE] = _chunk_cumsum(jnp.broadcast_to(g, (S, LANE)), row)

    vec = pl.BlockSpec((1, LANE), lambda b: (0, 0))
    wide = pl.BlockSpec((S, H * LANE), lambda b: (b, 0))
    return pl.pallas_call(body, grid=(B,),
                          in_specs=[pl.BlockSpec((S, LANE), lambda b: (b, SM_COL // LANE)), vec, vec],
                          out_specs=(wide, wide),
                          out_shape=(jax.ShapeDtypeStruct((B * S, H * LANE), F32),) * 2,
                          compiler_params=_cp(("parallel",)), name="gdn_gates_fwd")(proj, a_log, dt_bias)


def _gdn_gates_bwd(d_beta, d_gc, proj, a_log, dt_bias, B, S):
    H = N_LIN_HEADS

    def body(db_ref, dgc_ref, sm_ref, al_ref, dt_ref, dsm_ref, dal_ref, ddt_ref):
        sm = sm_ref[...]
        row = lax.broadcasted_iota(jnp.int32, (S, LANE), 0)
        lane = lax.broadcasted_iota(jnp.int32, (1, LANE), 1)
        dsm = jnp.zeros((S, LANE), F32)
        dal = jnp.zeros((1, LANE), F32)
        ddt = jnp.zeros((1, LANE), F32)
        for h in range(H):
            beta = _sigmoid(sm[:, h:h + 1])
            dbeta = jnp.sum(db_ref[:, h * LANE:(h + 1) * LANE], axis=-1, keepdims=True)
            d_bl = dbeta * beta * (1.0 - beta)
            dgc = jnp.sum(dgc_ref[:, h * LANE:(h + 1) * LANE], axis=-1, keepdims=True)
            dg = _chunk_rev_cumsum(jnp.broadcast_to(dgc, (S, LANE)), row, S)[:, 0:1]
            z = sm[:, H + h:H + h + 1] + dt_ref[0:1, h:h + 1]
            a = jnp.exp(al_ref[0:1, h:h + 1])
            g = -a * _softplus(z)
            d_al = dg * (-a) * _sigmoid(z)
            dsm = dsm + jnp.where(lane == h, d_bl, 0.0) + jnp.where(lane == H + h, d_al, 0.0)
            ddt = ddt + jnp.where(lane == h, jnp.sum(d_al, axis=0, keepdims=True), 0.0)
            dal = dal + jnp.where(lane == h, jnp.sum(dg * g, axis=0, keepdims=True), 0.0)
        dsm_ref[...] = dsm.astype(dsm_ref.dtype)

        @pl.when(pl.program_id(0) == 0)
        def _():
            dal_ref[...] = jnp.zeros_like(dal_ref)
            ddt_ref[...] = jnp.zeros_like(ddt_ref)

        dal_ref[...] += dal
        ddt_ref[...] += ddt

    vec = pl.BlockSpec((1, LANE), lambda b: (0, 0))
    wide = pl.BlockSpec((S, H * LANE), lambda b: (b, 0))
    return pl.pallas_call(body, grid=(B,),
                          in_specs=[wide, wide, pl.BlockSpec((S, LANE), lambda b: (b, SM_COL // LANE)), vec, vec],
                          out_specs=(pl.BlockSpec((S, LANE), lambda b: (b, 0)), vec, vec),
                          out_shape=(jax.ShapeDtypeStruct((B * S, LANE), _MXU_DTYPE),
                                     jax.ShapeDtypeStruct((1, LANE), F32), jax.ShapeDtypeStruct((1, LANE), F32)),
                          compiler_params=_cp(("arbitrary",)), name="gdn_gates_bwd")(d_beta, d_gc, proj, a_log, dt_bias)


PREP_ROWS = 512


@jax.custom_vjp
def _unit_lower_inverse(lower):
    n, C, _ = lower.shape
    ri = lax.broadcasted_iota(jnp.int32, (C, C), 0)
    ci = lax.broadcasted_iota(jnp.int32, (C, C), 1)
    p = -lower
    inv = jnp.where((ri == ci)[None], 1.0, 0.0) + p
    for _ in range(int(math.log2(C)) - 1):
        p = _bdot_hi(p, p)
        inv = inv + _bdot_hi(inv, p)
    return inv


def _unit_lower_inverse_fwd(lower):
    inv = _unit_lower_inverse(lower)
    return inv, inv


def _unit_lower_inverse_bwd(inv, d_inv):
    inv_t = jnp.swapaxes(inv, 1, 2)
    return (-_bdot_hi(_bdot_hi(inv_t, d_inv), inv_t),)


_unit_lower_inverse.defvjp(_unit_lower_inverse_fwd, _unit_lower_inverse_bwd)


def _prep_fn(q, k, v, beta, gc):
    R = q.shape[0]
    n = R // CHUNK
    q3, k3, v3, b3, g3 = [t.reshape(n, CHUNK, LIN_HEAD_DIM) for t in (q, k, v, beta, gc)]
    ri = lax.broadcasted_iota(jnp.int32, (CHUNK, CHUNK), 0)
    ci = lax.broadcasted_iota(jnp.int32, (CHUNK, CHUNK), 1)
    causal = (ri >= ci)[None]
    strict = (ri > ci)[None]
    gcol = g3[:, :, 0:1]
    grow = jnp.swapaxes(g3, 1, 2)[:, 0:1, :]
    decay = jnp.exp(jnp.where(causal, gcol - grow, -1e30))
    kb = k3 * b3
    lower = jnp.where(strict, _bdot(kb, k3, 2, 2) * decay, 0.0)
    inv = _unit_lower_inverse(lower)
    eg = jnp.exp(g3)
    sol = _bdot_hi(inv, jnp.concatenate([v3 * b3, kb * eg], axis=-1))
    u, w = sol[..., :LIN_HEAD_DIM], sol[..., LIN_HEAD_DIM:]
    intra = _bdot(q3, k3, 2, 2) * decay
    q_dec = q3 * eg
    k_dec = k3 * jnp.exp(g3[:, CHUNK - 1:CHUNK, :] - g3)
    return (u.reshape(R, LIN_HEAD_DIM), w.reshape(R, LIN_HEAD_DIM), q_dec.reshape(R, LIN_HEAD_DIM),
            k_dec.reshape(R, LIN_HEAD_DIM), intra.reshape(R, CHUNK))


def _prep_specs(S):
    H = N_LIN_HEADS
    R = min(PREP_ROWS, S)
    nr = S // R

    def col(off):
        return pl.BlockSpec((R, LANE), lambda b, h, r: (b * nr + r, off + h))

    intra = pl.BlockSpec((None, R, CHUNK), lambda b, h, r: (h, b * nr + r, 0))
    return R, nr, col, intra


def _gdn_prep_fwd(act, beta, gc, B, S):
    H = N_LIN_HEADS
    R, nr, col, intra_spec = _prep_specs(S)
    T = B * S

    def body(q_ref, k_ref, v_ref, b_ref, g_ref, u_ref, w_ref, qd_ref, kd_ref, a_ref):
        u, w, qd, kd, a = _prep_fn(q_ref[...], k_ref[...], v_ref[...], b_ref[...], g_ref[...])
        u_ref[...] = u
        w_ref[...] = w
        qd_ref[...] = qd
        kd_ref[...] = kd
        a_ref[...] = a

    wide = jax.ShapeDtypeStruct((T, H * LANE), F32)
    return pl.pallas_call(body, grid=(B, H, nr),
                          in_specs=[col(0), col(H), col(2 * H), col(0), col(0)],
                          out_specs=(col(0), col(0), col(0), col(0), intra_spec),
                          out_shape=(wide, wide, wide, wide, jax.ShapeDtypeStruct((H, T, CHUNK), F32)),
                          compiler_params=_cp(("parallel", "parallel", "parallel")),
                          name="gdn_prep_fwd")(act, act, act, beta, gc)


def _gdn_prep_bwd(act, beta, gc, du, dw, dqd, dkd, da, B, S):
    H = N_LIN_HEADS
    R, nr, col, intra_spec = _prep_specs(S)
    T = B * S

    def body(q_ref, k_ref, v_ref, b_ref, g_ref, du_ref, dw_ref, dqd_ref, dkd_ref, da_ref,
             dq_ref, dk_ref, dv_ref, db_ref, dg_ref):
        _, vjp = jax.vjp(_prep_fn, q_ref[...], k_ref[...], v_ref[...], b_ref[...], g_ref[...])
        dq, dk, dv, db, dg = vjp((du_ref[...], dw_ref[...], dqd_ref[...], dkd_ref[...], da_ref[...]))
        dq_ref[...] = dq
        dk_ref[...] = dk
        dv_ref[...] = dv
        db_ref[...] = db
        dg_ref[...] = dg

    wide = jax.ShapeDtypeStruct((T, H * LANE), F32)
    return pl.pallas_call(body, grid=(B, H, nr),
                          in_specs=[col(0), col(H), col(2 * H), col(0), col(0),
                                    col(0), col(0), col(0), col(0), intra_spec],
                          out_specs=(col(0),) * 5, out_shape=(wide,) * 5,
                          compiler_params=_cp(("parallel", "parallel", "parallel")),
                          name="gdn_prep_bwd")(act, act, act, beta, gc, du, dw, dqd, dkd, da)


def _scan_step(u, w, qd, kd, a, g_last, state):
    v_new = u - _dot(w, state)
    o = _dot(qd, state) + _dot(a, v_new)
    new_state = state * jnp.exp(g_last) + _dot(kd, v_new, 0, 0)
    return o, new_state


def _scan_specs(B, S):
    H = N_LIN_HEADS
    n = S // CHUNK
    col = pl.BlockSpec((S, LANE), lambda b, h: (b, h))
    intra = pl.BlockSpec((None, S, CHUNK), lambda b, h: (h, b, 0))
    st = pl.BlockSpec((None, None, n, LIN_HEAD_DIM, LIN_HEAD_DIM), lambda b, h: (b, h, 0, 0, 0))
    return n, col, intra, st


def _gdn_scan_fwd(u, w, qd, kd, a, gc, B, S):
    H = N_LIN_HEADS
    n, col, intra, st = _scan_specs(B, S)

    def body(u_ref, w_ref, qd_ref, kd_ref, a_ref, g_ref, o_ref, st_ref):
        def step(c, state):
            rows = pl.ds(pl.multiple_of(c * CHUNK, CHUNK), CHUNK)
            st_ref[c] = state.astype(st_ref.dtype)
            o, new_state = _scan_step(u_ref[rows, :], w_ref[rows, :], qd_ref[rows, :], kd_ref[rows, :],
                                      a_ref[rows, :], g_ref[rows, :][CHUNK - 1:CHUNK, :], state)
            o_ref[rows, :] = o
            return new_state

        lax.fori_loop(0, n, step, jnp.zeros((LIN_HEAD_DIM, LIN_HEAD_DIM), F32))

    return pl.pallas_call(body, grid=(B, H), in_specs=[col, col, col, col, intra, col],
                          out_specs=(col, st),
                          out_shape=(jax.ShapeDtypeStruct((B * S, H * LANE), F32),
                                     jax.ShapeDtypeStruct((B, H, n, LIN_HEAD_DIM, LIN_HEAD_DIM), _MXU_DTYPE)),
                          compiler_params=_cp(("parallel", "parallel")), name="gdn_scan_fwd")(u, w, qd, kd, a, gc)


def _gdn_scan_bwd(u, w, qd, kd, a, gc, states, do, B, S):
    H = N_LIN_HEADS
    n, col, intra, st = _scan_specs(B, S)
    T = B * S

    def body(u_ref, w_ref, qd_ref, kd_ref, a_ref, g_ref, st_ref, do_ref,
             du_ref, dw_ref, dqd_ref, dkd_ref, da_ref, dg_ref):
        last = lax.broadcasted_iota(jnp.int32, (CHUNK, LANE), 0) == CHUNK - 1

        def step(i, d_state):
            c = n - 1 - i
            rows = pl.ds(pl.multiple_of(c * CHUNK, CHUNK), CHUNK)
            _, vjp = jax.vjp(_scan_step, u_ref[rows, :], w_ref[rows, :], qd_ref[rows, :], kd_ref[rows, :],
                             a_ref[rows, :], g_ref[rows, :][CHUNK - 1:CHUNK, :], st_ref[c].astype(F32))
            du, dw, dqd, dkd, da, dgl, d_prev = vjp((do_ref[rows, :].astype(F32), d_state))
            du_ref[rows, :] = du
            dw_ref[rows, :] = dw
            dqd_ref[rows, :] = dqd
            dkd_ref[rows, :] = dkd
            da_ref[rows, :] = da
            dg_ref[rows, :] = jnp.where(last, dgl, 0.0)
            return d_prev

        lax.fori_loop(0, n, step, jnp.zeros((LIN_HEAD_DIM, LIN_HEAD_DIM), F32))

    wide = jax.ShapeDtypeStruct((T, H * LANE), F32)
    return pl.pallas_call(body, grid=(B, H), in_specs=[col, col, col, col, intra, col, st, col],
                          out_specs=(col, col, col, col, intra, col),
                          out_shape=(wide, wide, wide, wide, jax.ShapeDtypeStruct((H, T, CHUNK), F32), wide),
                          compiler_params=_cp(("parallel", "parallel")),
                          name="gdn_scan_bwd")(u, w, qd, kd, a, gc, states, do)


GATE_COL = 3 * SEQ_MIX_WIDTH


def _post_fn(o, gate, gain):
    return o * lax.rsqrt(jnp.mean(o * o, axis=-1, keepdims=True) + EPS) * gain * _silu(gate)


def _gdn_post_fwd(o, proj, onorm, T):
    H = N_LIN_HEADS
    tm = _pick(T, 1024)

    def body(o_ref, g_ref, n_ref, y_ref):
        y_ref[...] = _post_fn(o_ref[...], g_ref[...], n_ref[...]).astype(y_ref.dtype)

    col = pl.BlockSpec((tm, LANE), lambda i, h: (i, h))
    return pl.pallas_call(body, grid=(T // tm, H),
                          in_specs=[col, pl.BlockSpec((tm, LANE), lambda i, h: (i, GATE_COL // LANE + h)),
                                    pl.BlockSpec((1, LANE), lambda i, h: (0, 0))],
                          out_specs=col, out_shape=jax.ShapeDtypeStruct((T, H * LANE), _MXU_DTYPE),
                          compiler_params=_cp(("parallel", "parallel")), name="gdn_post_fwd")(o, proj, onorm)


def _gdn_post_bwd(d_cat, o, proj, onorm, T):
    H = N_LIN_HEADS
    tm = _pick(T, 1024)

    def body(dy_ref, o_ref, g_ref, n_ref, do_ref, dg_ref, dn_ref):
        _, vjp = jax.vjp(_post_fn, o_ref[...], g_ref[...], n_ref[...])
        do, dg, dn = vjp(dy_ref[...].astype(F32))
        do_ref[...] = do
        dg_ref[...] = dg.astype(dg_ref.dtype)

        @pl.when((pl.program_id(0) == 0) & (pl.program_id(1) == 0))
        def _():
            dn_ref[...] = jnp.zeros_like(dn_ref)

        dn_ref[...] += dn

    col = pl.BlockSpec((tm, LANE), lambda i, h: (i, h))
    vec = pl.BlockSpec((1, LANE), lambda i, h: (0, 0))
    return pl.pallas_call(body, grid=(T // tm, H),
                          in_specs=[col, col, pl.BlockSpec((tm, LANE), lambda i, h: (i, GATE_COL // LANE + h)), vec],
                          out_specs=(col, col, vec),
                          out_shape=(jax.ShapeDtypeStruct((T, H * LANE), F32),
                                     jax.ShapeDtypeStruct((T, H * LANE), _MXU_DTYPE),
                                     jax.ShapeDtypeStruct((1, LANE), F32)),
                          compiler_params=_cp(("arbitrary", "arbitrary")), name="gdn_post_bwd")(d_cat, o, proj, onorm)


def _log_sigmoid(z):
    return jnp.minimum(z, 0.0) - jnp.log(1.0 + jnp.exp(-jnp.abs(z)))


def _split_dot(x, m):
    hi = x.astype(_MXU_DTYPE)
    lo = x - hi.astype(F32)
    return _dot(hi, m) + _dot(lo, m)


SB_QB = 256
SB_RC = 512


def _sb_scores(q2, k_j, scale, valid):
    z = _dot(q2, k_j, 1, 1) * scale
    lb = _log_sigmoid(z)
    return lb, jnp.where(valid, lb - z, 0.0)


def _sb_consts(QB):
    ri = lax.broadcasted_iota(jnp.int32, (SB_BLOCK, SB_BLOCK), 0)
    ci = lax.broadcasted_iota(jnp.int32, (SB_BLOCK, SB_BLOCK), 1)
    rc = min(SB_RC, 2 * QB)
    assert rc == 2 * QB or QB % rc == 0
    row = lax.broadcasted_iota(jnp.int32, (rc, SB_BLOCK), 0)
    col = lax.broadcasted_iota(jnp.int32, (rc, SB_BLOCK), 1)
    lane = lax.broadcasted_iota(jnp.int32, (1, LANE), 1)
    return {
        "rc": rc,
        "chunks": [(ch * rc, (ch * rc) % QB) for ch in range(2 * QB // rc)],
        "row_minus_col": row % QB - col,
        "after_excl": (ri > ci).astype(_MXU_DTYPE),
        "upto_incl": (ri <= ci).astype(_MXU_DTYPE),
        "upto_excl": (ri < ci).astype(_MXU_DTYPE),
        "lane": lane,
        "head0": lane < SB_HEAD_DIM,
    }


def _sb_stack(x, c):
    return jnp.concatenate([jnp.where(c["head0"], x, 0.0), jnp.where(c["head0"], 0.0, x)], axis=0)


def _sb_unstack(x2, c, QB):
    return jnp.where(c["head0"], x2[:QB], x2[QB:])


def _sb_fwd(proj, B, S):
    W = SEQ_MIX_WIDTH
    P = W // LANE
    QB = min(SB_QB, S)
    KB = SB_BLOCK
    scale = SB_HEAD_DIM ** -0.5

    def body(q_ref, k_ref, v_ref, o_ref, tot_ref):
        c = _sb_consts(QB)
        rc = c["rc"]

        def q_loop(i, carry):
            qrows = pl.ds(pl.multiple_of(i * QB, QB), QB)
            q2 = _sb_stack(q_ref[qrows, :].astype(F32), c)
            nkb = (i + 1) * (QB // KB)

            def k_loop(t, st):
                j = nkb - 1 - t
                krows = pl.ds(pl.multiple_of(j * KB, KB), KB)
                k_j, v_j = k_ref[krows, :], v_ref[krows, :]
                out = []
                for (r0, qpos), (acc, r) in zip(c["chunks"], st):
                    valid = c["row_minus_col"] > j * KB - i * QB - qpos
                    lb, l1 = _sb_scores(q2[r0:r0 + rc], k_j, scale, valid)
                    a = jnp.where(valid, jnp.exp(lb + r + _split_dot(l1, c["after_excl"])), 0.0)
                    out.append((acc + _dot(a, v_j), r + jnp.sum(l1, axis=-1, keepdims=True)))
                return tuple(out)

            st = lax.fori_loop(0, nkb, k_loop,
                               tuple((jnp.zeros((rc, LANE), F32), jnp.zeros((rc, 1), F32)) for _ in c["chunks"]))
            acc = jnp.concatenate([a for a, _ in st], axis=0)
            r = jnp.concatenate([jnp.broadcast_to(r, (rc, LANE)) for _, r in st], axis=0)
            o_ref[qrows, :] = _sb_unstack(acc, c, QB)
            tot_ref[qrows, :] = _sb_unstack(r, c, QB)
            return carry

        lax.fori_loop(0, S // QB, q_loop, 0)

    def col(off):
        return pl.BlockSpec((S, LANE), lambda b, p: (b, off + p))

    out = jax.ShapeDtypeStruct((B * S, W), F32)
    return pl.pallas_call(body, grid=(B, P), in_specs=[col(0), col(P), col(2 * P)], out_specs=(col(0), col(0)),
                          out_shape=(out, out),
                          compiler_params=_cp(("parallel", "parallel")), name="sb_fwd")(proj, proj, proj)


def _sb_bwd(proj, tot, d_cat, B, S):
    W = SEQ_MIX_WIDTH
    P = W // LANE
    QB = min(SB_QB, S)
    KB = SB_BLOCK
    scale = SB_HEAD_DIM ** -0.5

    def body(q_ref, k_ref, v_ref, tot_ref, do_ref, dq_ref, dk_ref, dv_ref, dk_acc, dv_acc):
        c = _sb_consts(QB)
        rc = c["rc"]
        dk_acc[...] = jnp.zeros_like(dk_acc)
        dv_acc[...] = jnp.zeros_like(dv_acc)

        def q_loop(i, carry):
            qrows = pl.ds(pl.multiple_of(i * QB, QB), QB)
            q2 = _sb_stack(q_ref[qrows, :].astype(F32), c)
            do2 = _sb_stack(do_ref[qrows, :].astype(F32), c)
            tot = tot_ref[qrows, :]
            total = jnp.concatenate(
                [jnp.sum(jnp.where(c["lane"] == h * SB_HEAD_DIM, tot, 0.0), axis=-1, keepdims=True) for h in range(2)],
                axis=0)

            def k_loop(j, st):
                krows = pl.ds(pl.multiple_of(j * KB, KB), KB)
                k_j, v_j = k_ref[krows, :], v_ref[krows, :]
                out, dzs, aa = [], [], []
                for (r0, qpos), (dq_acc, p_l1, p_g) in zip(c["chunks"], st):
                    valid = c["row_minus_col"] > j * KB - i * QB - qpos
                    lb, l1 = _sb_scores(q2[r0:r0 + rc], k_j, scale, valid)
                    tail = total[r0:r0 + rc] - p_l1 - _split_dot(l1, c["upto_incl"])
                    a = jnp.where(valid, jnp.exp(lb + tail), 0.0)
                    g = _dot(do2[r0:r0 + rc], v_j, 1, 1) * a
                    g_before = p_g + _split_dot(g, c["upto_excl"])
                    sig = jnp.exp(lb)
                    dz = jnp.where(valid, g * (1.0 - sig) - g_before * sig, 0.0) * scale
                    dzs.append(dz.astype(_MXU_DTYPE))
                    aa.append(a.astype(_MXU_DTYPE))
                    out.append((dq_acc + _dot(dzs[-1], k_j), p_l1 + jnp.sum(l1, axis=-1, keepdims=True),
                                p_g + jnp.sum(g, axis=-1, keepdims=True)))
                dk_acc[krows, :] += _dot(jnp.concatenate(dzs, axis=0), q2, 0, 0)
                dv_acc[krows, :] += _dot(jnp.concatenate(aa, axis=0), do2, 0, 0)
                return tuple(out)

            zero_col = jnp.zeros((rc, 1), F32)
            st = lax.fori_loop(0, (i + 1) * (QB // KB), k_loop,
                               tuple((jnp.zeros((rc, LANE), F32), zero_col, zero_col) for _ in c["chunks"]))
            dq2 = jnp.concatenate([d for d, _, _ in st], axis=0)
            dq_ref[qrows, :] = _sb_unstack(dq2, c, QB).astype(dq_ref.dtype)
            return carry

        lax.fori_loop(0, S // QB, q_loop, 0)
        dk_ref[...] = dk_acc[...].astype(dk_ref.dtype)
        dv_ref[...] = dv_acc[...].astype(dv_ref.dtype)

    def col(off):
        return pl.BlockSpec((S, LANE), lambda b, p: (b, off + p))

    out = jax.ShapeDtypeStruct((B * S, W), _MXU_DTYPE)
    return pl.pallas_call(body, grid=(B, P), in_specs=[col(0), col(P), col(2 * P), col(0), col(0)],
                          out_specs=(col(0),) * 3, out_shape=(out,) * 3,
                          scratch_shapes=[pltpu.VMEM((S, LANE), F32), pltpu.VMEM((S, LANE), F32)],
                          compiler_params=_cp(("parallel", "parallel")), name="sb_bwd")(proj, proj, proj, tot, d_cat)


def _mem_fn(q, k, v):
    lane = lax.broadcasted_iota(jnp.int32, (1, X_WIDTH), 1)
    out = jnp.zeros(q.shape, F32)
    for h in range(N_X_HEADS):
        hm = (lane // X_HEAD_DIM) == h
        s = _dot(jnp.where(hm, q, 0.0), k, 1, 1) * (X_HEAD_DIM ** -0.5)
        e = jnp.exp(s - lax.stop_gradient(jnp.max(s, axis=-1, keepdims=True)))
        p = e / jnp.sum(e, axis=-1, keepdims=True)
        out = out + jnp.where(hm, _dot(p, v), 0.0)
    return out


def _mem_specs(S, q_col):
    ts = _pick(S, 1024)
    ns = S // ts
    qs = pl.BlockSpec((ts, X_WIDTH), lambda b, i: (b * ns + i, q_col // X_WIDTH))
    ks = pl.BlockSpec((N_MEM, X_WIDTH), lambda b, i: (b, 0))
    vs = pl.BlockSpec((N_MEM, X_WIDTH), lambda b, i: (b, 1))
    os = pl.BlockSpec((ts, X_WIDTH), lambda b, i: (b * ns + i, 0))
    return ts, ns, qs, ks, vs, os


def _mem_fwd(proj, q_col, mem_kv, B, S, name):
    ts, ns, qs, ks, vs, os = _mem_specs(S, q_col)

    def body(q_ref, k_ref, v_ref, o_ref):
        o_ref[...] = _mem_fn(q_ref[...].astype(F32), k_ref[...].astype(F32), v_ref[...].astype(F32))

    return pl.pallas_call(body, grid=(B, ns), in_specs=[qs, ks, vs], out_specs=os,
                          out_shape=jax.ShapeDtypeStruct((B * S, X_WIDTH), F32),
                          compiler_params=_cp(("parallel", "parallel")), name=name)(proj, mem_kv, mem_kv)


def _mem_bwd(proj, q_col, mem_kv, d_cat, B, S, name):
    ts, ns, qs, ks, vs, os = _mem_specs(S, q_col)

    def body(q_ref, k_ref, v_ref, do_ref, dq_ref, dk_ref, dv_ref):
        _, vjp = jax.vjp(_mem_fn, q_ref[...].astype(F32), k_ref[...].astype(F32), v_ref[...].astype(F32))
        dq, dk, dv = vjp(do_ref[...].astype(F32))
        dq_ref[...] = dq.astype(dq_ref.dtype)

        @pl.when(pl.program_id(1) == 0)
        def _():
            dk_ref[...] = jnp.zeros_like(dk_ref)
            dv_ref[...] = jnp.zeros_like(dv_ref)

        dk_ref[...] += dk
        dv_ref[...] += dv

    dos = pl.BlockSpec((ts, X_WIDTH), lambda b, i: (b * ns + i, SEQ_MIX_WIDTH // X_WIDTH))
    dq, dk, dv = pl.pallas_call(
        body, grid=(B, ns), in_specs=[qs, ks, vs, dos],
        out_specs=(os, pl.BlockSpec((N_MEM, X_WIDTH), lambda b, i: (b, 0)), pl.BlockSpec((N_MEM, X_WIDTH), lambda b, i: (b, 0))),
        out_shape=(jax.ShapeDtypeStruct((B * S, X_WIDTH), _MXU_DTYPE),
                   jax.ShapeDtypeStruct((B * N_MEM, X_WIDTH), F32), jax.ShapeDtypeStruct((B * N_MEM, X_WIDTH), F32)),
        compiler_params=_cp(("parallel", "arbitrary")), name=name)(proj, mem_kv, mem_kv, d_cat)
    return dq, dk, dv


def _peers():
    x, y, c = lax.axis_index("x"), lax.axis_index("y"), lax.axis_index("c")
    me = 4 * x + 2 * y + c
    out = []
    for fx, fy, fc in [(0, 0, 1), (1, 0, 0), (0, 1, 0), (1, 1, 0), (1, 0, 1), (0, 1, 1), (1, 1, 1)]:
        px, py, pc = x ^ fx, y ^ fy, c ^ fc
        out.append(((px, py, pc), 4 * px + 2 * py + pc))
    return me, out


ANY = pl.BlockSpec(memory_space=pl.ANY)


def _remote(src, dst, send_sems, recv_sems, k, dev):
    return pltpu.make_async_remote_copy(src_ref=src, dst_ref=dst, send_sem=send_sems.at[k], recv_sem=recv_sems.at[k],
                                        device_id=dev, device_id_type=pl.DeviceIdType.MESH)


def _place():
    x, y, c = lax.axis_index("x"), lax.axis_index("y"), lax.axis_index("c")
    return x, y, c, [(1 - x, y), (x, 1 - y), (1 - x, 1 - y)]


def _all_gather(shard):
    R = shard.shape[0]

    def body(x_ref, o_ref, send_sems, recv_sems, local_sem):
        x, y, c, chips = _place()
        me, sibling = (x, y, c), (x, y, 1 - c)

        def slot(px, py, pc):
            return o_ref.at[4 * px + 2 * py + pc]

        def copy(k, block, to, src=None):
            return _remote(slot(*block) if src is None else src, slot(*block), send_sems, recv_sems, k, to)

        mine = pltpu.make_async_copy(x_ref, slot(*me), local_sem)
        mine.start()
        first = [copy(0, me, sibling, src=x_ref)] + [copy(1 + j, me, (*chip, c), src=x_ref) for j, chip in enumerate(chips)]
        for cp in first:
            cp.start()
        passed = [copy(4 + j, (*chip, c), sibling) for j, chip in enumerate(chips)]
        for j, chip in enumerate(chips):
            copy(1 + j, (*chip, c), me).wait_recv()
            passed[j].start()
        copy(0, sibling, me).wait_recv()
        for j, chip in enumerate(chips):
            copy(4 + j, (*chip, 1 - c), me).wait_recv()
        for cp in first + passed:
            cp.wait_send()
        mine.wait()

    return pl.pallas_call(body, in_specs=[ANY], out_specs=ANY,
                          out_shape=jax.ShapeDtypeStruct((N_DEV, R, LANE), shard.dtype),
                          scratch_shapes=[pltpu.SemaphoreType.DMA((7,)), pltpu.SemaphoreType.DMA((7,)),
                                          pltpu.SemaphoreType.DMA],
                          compiler_params=pltpu.CompilerParams(has_side_effects=True),
                          name="all_gather_weights")(shard)


N_CHIP = 4


def _exchange_sibling(big):
    R = big.shape[1]

    def body(b_ref, o_ref, send_sems, recv_sems):
        x, y, c, _ = _place()
        copies = [_remote(b_ref.at[2 * k + (1 - c)], o_ref.at[k], send_sems, recv_sems, k, (x, y, 1 - c))
                  for k in range(N_CHIP)]
        for cp in copies:
            cp.start()
        for cp in copies:
            cp.wait()

    return pl.pallas_call(body, in_specs=[ANY], out_specs=ANY,
                          out_shape=jax.ShapeDtypeStruct((N_CHIP, R, LANE), big.dtype),
                          scratch_shapes=[pltpu.SemaphoreType.DMA((N_CHIP,)), pltpu.SemaphoreType.DMA((N_CHIP,))],
                          compiler_params=pltpu.CompilerParams(has_side_effects=True),
                          name="exchange_sibling")(big)


def _partial_sum(g4, recv):
    R = g4.shape[2]
    tr = _pick(R, 1024)

    def body(g_ref, r_ref, pw_ref, po_ref):
        x, y, c, _ = _place()
        g = jnp.where(c == 0, g_ref[0], g_ref[1]) + r_ref[...].astype(F32)
        pw_ref[...] = g.astype(pw_ref.dtype)

        @pl.when(pl.program_id(1) == 2 * x + y)
        def _():
            po_ref[...] = g

    return pl.pallas_call(body, grid=(R // tr, N_CHIP),
                          in_specs=[pl.BlockSpec((None, 2, tr, LANE), lambda i, k: (k, 0, i, 0)),
                                    pl.BlockSpec((None, tr, LANE), lambda i, k: (k, i, 0))],
                          out_specs=(pl.BlockSpec((None, tr, LANE), lambda i, k: (k, i, 0)),
                                     pl.BlockSpec((tr, LANE), lambda i, k: (i, 0))),
                          out_shape=(jax.ShapeDtypeStruct((N_CHIP, R, LANE), recv.dtype),
                                     jax.ShapeDtypeStruct((R, LANE), F32)),
                          compiler_params=_cp(("parallel", "arbitrary")), name="partial_sum")(g4, recv)


def _exchange_chips(part, small):
    R = part.shape[1]
    K = small.shape[0]

    def body(p_ref, s_ref, ob_ref, os_ref, send_sems, recv_sems, local_sems):
        x, y, c, chips = _place()
        my_chip = 2 * x + y
        me, peers = _peers()
        own_b = pltpu.make_async_copy(p_ref.at[my_chip], ob_ref.at[my_chip], local_sems.at[0])
        own_s = pltpu.make_async_copy(s_ref, os_ref.at[me], local_sems.at[1])
        own_b.start()
        own_s.start()
        copies = [_remote(p_ref.at[2 * px + py], ob_ref.at[my_chip], send_sems, recv_sems, j, (px, py, c))
                  for j, (px, py) in enumerate(chips)]
        copies += [_remote(s_ref, os_ref.at[me], send_sems, recv_sems, 3 + k, dev) for k, (dev, _) in enumerate(peers)]
        for cp in copies:
            cp.start()
        for j, (px, py) in enumerate(chips):
            _remote(p_ref.at[my_chip], ob_ref.at[2 * px + py], send_sems, recv_sems, j, (px, py, c)).wait_recv()
        for k, (dev, idx) in enumerate(peers):
            _remote(s_ref, os_ref.at[idx], send_sems, recv_sems, 3 + k, dev).wait_recv()
        for cp in copies:
            cp.wait_send()
        own_b.wait()
        own_s.wait()

    return pl.pallas_call(body, in_specs=[ANY, ANY], out_specs=(ANY, ANY),
                          out_shape=(jax.ShapeDtypeStruct((N_CHIP, R, LANE), part.dtype),
                                     jax.ShapeDtypeStruct((N_DEV, K, LANE), small.dtype)),
                          scratch_shapes=[pltpu.SemaphoreType.DMA((10,)), pltpu.SemaphoreType.DMA((10,)),
                                          pltpu.SemaphoreType.DMA((2,))],
                          compiler_params=pltpu.CompilerParams(has_side_effects=True),
                          name="exchange_chips")(part, small)


def _adamw_math(w, g, m, v):
    m = ADAM_B1 * m + (1.0 - ADAM_B1) * g
    v = ADAM_B2 * v + (1.0 - ADAM_B2) * (g * g)
    m_hat = m / (1.0 - ADAM_B1 ** ADAM_STEP)
    v_hat = v / (1.0 - ADAM_B2 ** ADAM_STEP)
    delta = -ADAM_LR * (m_hat / (jnp.sqrt(v_hat) + ADAM_EPS) + ADAM_WD * w)
    return delta, m, v


def _adamw_shard(own, recv, w, m, v):
    R = own.shape[0]
    tr = _pick(R, 1024)

    def body(own_ref, recv_ref, w_ref, m_ref, v_ref, g_ref, d_ref, nm_ref, nv_ref):
        x, y, _, _ = _place()
        g = own_ref[...]
        for k in range(N_CHIP):
            g = g + jnp.where(k == 2 * x + y, 0.0, recv_ref[k].astype(F32))
        delta, nm, nv = _adamw_math(w_ref[...], g, m_ref[...], v_ref[...])
        g_ref[...] = g
        d_ref[...] = delta
        nm_ref[...] = nm
        nv_ref[...] = nv

    row = pl.BlockSpec((tr, LANE), lambda i: (i, 0))
    out = jax.ShapeDtypeStruct((R, LANE), F32)
    return pl.pallas_call(body, grid=(R // tr,),
                          in_specs=[row, pl.BlockSpec((N_CHIP, tr, LANE), lambda i: (0, i, 0)), row, row, row],
                          out_specs=(row,) * 4, out_shape=(out,) * 4,
                          compiler_params=_cp(("parallel",)), name="adamw_shard")(own, recv, w, m, v)


def _adamw_replicated(parts, w, m, v):
    K = w.shape[0]

    def body(p_ref, w_ref, m_ref, v_ref, g_ref, d_ref, nm_ref, nv_ref):
        g = p_ref[0]
        for p in range(1, N_DEV):
            g = g + p_ref[p]
        delta, nm, nv = _adamw_math(w_ref[...], g, m_ref[...], v_ref[...])
        g_ref[...] = g
        d_ref[...] = delta
        nm_ref[...] = nm
        nv_ref[...] = nv

    out = jax.ShapeDtypeStruct((K, LANE), F32)
    return pl.pallas_call(body, out_shape=(out,) * 4, compiler_params=_cp(), name="adamw_replicated")(parts, w, m, v)


_SHARDED = (("w_in_a", 1), ("conv_w_a", 2), ("w_in_b", 2), ("w_mem_kv", 1), ("w_out", 1), ("w_up", 2), ("w_down", 1))
_REPLICATED = ("mem_norm", "norm_pre_mix", "norm_post_mix", "norm_pre_mlp", "norm_post_mlp", "a_log_a", "dt_bias_a", "onorm_a")
ROW_ALIGN = 16


def _rows(n_elems, align=ROW_ALIGN):
    r = -(-n_elems // LANE)
    return -(-r // align) * align


def _pack(arrays, align=ROW_ALIGN, total=None, lead=()):
    parts = []
    for a in arrays:
        n = math.prod(a.shape[len(lead):])
        flat = a.reshape(lead + (n,))
        r = _rows(n, align)
        flat = jnp.pad(flat, [(0, 0)] * len(lead) + [(0, r * LANE - n)])
        parts.append(flat.reshape(lead + (r, LANE)))
    out = jnp.concatenate(parts, axis=len(lead))
    if total is not None and out.shape[len(lead)] < total:
        out = jnp.pad(out, [(0, 0)] * len(lead) + [(0, total - out.shape[len(lead)]), (0, 0)])
    return out


def _unpack(flat, shapes, align=ROW_ALIGN, lead=()):
    outs = []
    r0 = 0
    for shp in shapes:
        n = math.prod(shp)
        r = _rows(n, align)
        part = lax.slice_in_dim(flat, r0, r0 + r, axis=len(lead))
        part = part.reshape(lead + (r * LANE,))
        part = lax.slice_in_dim(part, 0, n, axis=len(lead))
        outs.append(part.reshape(lead + tuple(shp)))
        r0 += r
    return outs


def _to_full(gathered, axis):
    g = jnp.moveaxis(gathered, 0, axis)
    shp = g.shape
    return g.reshape(shp[:axis] + (shp[axis] * shp[axis + 1],) + shp[axis + 2:])


def _to_blocks(full, axis):
    shp = full.shape
    g = full.reshape(shp[:axis] + (N_DEV, shp[axis] // N_DEV) + shp[axis + 1:])
    return jnp.moveaxis(g, axis, 0)


def _widen_in_a(w):
    main = w[:, :4 * SEQ_MIX_WIDTH]
    small = w[:, 4 * SEQ_MIX_WIDTH:4 * SEQ_MIX_WIDTH + 2 * N_LIN_HEADS]
    memq = w[:, 4 * SEQ_MIX_WIDTH + 2 * N_LIN_HEADS:]
    pad = jnp.zeros((w.shape[0], IN_A_PAD - IN_A), w.dtype)
    return jnp.concatenate([main, memq, small, pad], axis=1)


def _narrow_in_a(g):
    main = g[:, :4 * SEQ_MIX_WIDTH]
    memq = g[:, 4 * SEQ_MIX_WIDTH:4 * SEQ_MIX_WIDTH + X_WIDTH]
    small = g[:, SM_COL:SM_COL + 2 * N_LIN_HEADS]
    return jnp.concatenate([main, small, memq], axis=1)


def _row128(v):
    return jnp.pad(v.reshape(1, -1), ((0, 0), (0, LANE - v.shape[-1])))


def _local_step(x, mem, target, p):
    B, S, D = x.shape
    T = B * S
    md = _MXU_DTYPE
    x0 = x.reshape(T, D)
    tgt = target.reshape(T, D)
    memf = mem.reshape(B * N_MEM, D)
    vec = lambda a: a.reshape(1, -1)

    mem_n = _norm_fwd(memf, vec(p["mem_norm"]), out_dtype=md, name="norm_mem")
    w_in = [_widen_in_a(p["w_in_a"][0]), p["w_in_b"][0]]
    memq_col = [4 * SEQ_MIX_WIDTH, 3 * SEQ_MIX_WIDTH]
    alog = _row128(p["a_log_a"][0])
    dtb = _row128(p["dt_bias_a"][0])
    onorm = vec(p["onorm_a"][0])
    conv_w = p["conv_w_a"][0]
    saved = []
    xi = x0
    for i in range(2):
        s = {"x_in": xi}
        h1 = _norm_fwd(xi, vec(p["norm_pre_mix"][i]), out_dtype=md, name=f"norm_pre_mix{i}")
        proj = _mm(h1, w_in[i], out_dtypes=(F32 if i == 0 else md,), name=f"in_proj{i}")
        mem_kv = _mm(mem_n, p["w_mem_kv"][i], out_dtypes=(md,), name=f"mem_kv{i}")
        if i == 0:
            act = _gdn_conv_fwd(proj, conv_w, B, S)
            beta, gc = _gdn_gates_fwd(proj, alog, dtb, B, S)
            u, w, qd, kd, intra = _gdn_prep_fwd(act, beta, gc, B, S)
            o, states = _gdn_scan_fwd(u, w, qd, kd, intra, gc, B, S)
            mix = _gdn_post_fwd(o, proj, onorm, T)
            s.update(act=act, beta=beta, gc=gc, u=u, w=w, qd=qd, kd=kd, intra=intra, o=o, states=states)
        else:
            mix, tot = _sb_fwd(proj, B, S)
            s.update(tot=tot)
        cross = _mem_fwd(proj, memq_col[i], mem_kv, B, S, name=f"mem_fwd{i}")
        cat = jnp.concatenate([mix.astype(md), cross.astype(md)], axis=1)
        y = _mm(cat, p["w_out"][i], name=f"out_proj{i}")
        x_mid = _norm_fwd(y, vec(p["norm_post_mix"][i]), resid=xi, name=f"norm_post_mix{i}")
        h2 = _norm_fwd(x_mid, vec(p["norm_pre_mlp"][i]), out_dtype=md, name=f"norm_pre_mlp{i}")
        a_act, r = _mm(h2, p["w_up"][i], out_dtypes=(md, md), epilogue=_relu2_epilogue, name=f"up_proj{i}")
        y2 = _mm(a_act, p["w_down"][i], name=f"down_proj{i}")
        x_out = _norm_fwd(y2, vec(p["norm_post_mlp"][i]), resid=x_mid, name=f"norm_post_mlp{i}")
        s.update(h1=h1, proj=proj, mem_kv=mem_kv, cat=cat, y=y, x_mid=x_mid, h2=h2, a_act=a_act, r=r, y2=y2)
        saved.append(s)
        xi = x_out

    loss_row, dx = _loss_head(xi, tgt)

    g = {}
    d_mem_n = None
    gn = {k: [None, None] for k in ("norm_pre_mix", "norm_post_mix", "norm_pre_mlp", "norm_post_mlp")}
    g_w_mem_kv, g_w_out, g_w_up, g_w_down = [None, None], [None, None], [None, None], [None, None]
    for i in (1, 0):
        s = saved[i]
        d_y2, gn["norm_post_mlp"][i] = _norm_bwd(dx, s["y2"], vec(p["norm_post_mlp"][i]), name=f"norm_post_mlp_bwd{i}")
        g_w_down[i] = _mm(s["a_act"], d_y2, ta=True, name=f"down_proj_dw{i}")
        d_u = _mm(d_y2, p["w_down"][i], tb=True, out_dtypes=(md,), epilogue=_drelu2_epilogue, extras=(s["r"],),
                  name=f"down_proj_dx{i}")
        g_w_up[i] = _mm(s["h2"], d_u, ta=True, name=f"up_proj_dw{i}")
        d_h2 = _mm(d_u, p["w_up"][i], tb=True, name=f"up_proj_dx{i}")
        dx, gn["norm_pre_mlp"][i] = _norm_bwd(d_h2, s["x_mid"], vec(p["norm_pre_mlp"][i]), resid=dx,
                                              name=f"norm_pre_mlp_bwd{i}")
        d_y, gn["norm_post_mix"][i] = _norm_bwd(dx, s["y"], vec(p["norm_post_mix"][i]), name=f"norm_post_mix_bwd{i}")
        g_w_out[i] = _mm(s["cat"], d_y, ta=True, name=f"out_proj_dw{i}")
        d_cat = _mm(d_y, p["w_out"][i], tb=True, name=f"out_proj_dx{i}")
        d_memq, d_mk, d_mv = _mem_bwd(s["proj"], memq_col[i], s["mem_kv"], d_cat, B, S, name=f"mem_bwd{i}")
        d_mem_kv = jnp.concatenate([d_mk.astype(md), d_mv.astype(md)], axis=1)
        g_w_mem_kv[i] = _mm(mem_n, d_mem_kv, ta=True, name=f"mem_kv_dw{i}")
        d_mn = _mm(d_mem_kv, p["w_mem_kv"][i], tb=True, name=f"mem_kv_dx{i}")
        d_mem_n = d_mn if d_mem_n is None else d_mem_n + d_mn
        if i == 0:
            d_o, d_gate, g["onorm_a"] = _gdn_post_bwd(d_cat, s["o"], s["proj"], onorm, T)
            du, dw, dqd, dkd, da, dgc_s = _gdn_scan_bwd(s["u"], s["w"], s["qd"], s["kd"], s["intra"], s["gc"],
                                                         s["states"], d_o, B, S)
            dq, dk, dv, d_beta, d_gc = _gdn_prep_bwd(s["act"], s["beta"], s["gc"], du, dw, dqd, dkd, da, B, S)
            d_qkv, g["conv_w_a"] = _gdn_conv_bwd(dq, dk, dv, s["proj"], conv_w, B, S)
            d_sm, g["a_log_a"], g["dt_bias_a"] = _gdn_gates_bwd(d_beta, d_gc + dgc_s, s["proj"], alog, dtb, B, S)
            pad = jnp.zeros((T, IN_A_PAD - SM_COL - LANE), md)
            d_proj = jnp.concatenate([d_qkv, d_gate, d_memq, d_sm, pad], axis=1)
        else:
            dq, dk, dv = _sb_bwd(s["proj"], s["tot"], d_cat, B, S)
            d_proj = jnp.concatenate([dq, dk, dv, d_memq], axis=1)
        g_w_in = _mm(s["h1"], d_proj, ta=True, name=f"in_proj_dw{i}")
        d_h1 = _mm(d_proj, w_in[i], tb=True, name=f"in_proj_dx{i}")
        dx, gn["norm_pre_mix"][i] = _norm_bwd(d_h1, s["x_in"], vec(p["norm_pre_mix"][i]), resid=dx,
                                              name=f"norm_pre_mix_bwd{i}")
        if i == 0:
            g["w_in_a"] = _narrow_in_a(g_w_in)[None]
        else:
            g["w_in_b"] = g_w_in[None]
    _, g_mem_norm = _norm_bwd(d_mem_n, memf, vec(p["mem_norm"]), name="norm_mem_bwd")
    g["mem_norm"] = g_mem_norm.reshape(-1)
    for k, v in gn.items():
        g[k] = jnp.concatenate(v, axis=0)
    g["w_mem_kv"] = jnp.stack(g_w_mem_kv)
    g["w_out"] = jnp.stack(g_w_out)
    g["w_up"] = jnp.stack(g_w_up)
    g["w_down"] = jnp.stack(g_w_down)
    g["conv_w_a"] = g["conv_w_a"][None]
    g["a_log_a"] = g["a_log_a"][:, :N_LIN_HEADS]
    g["dt_bias_a"] = g["dt_bias_a"][:, :N_LIN_HEADS]
    return loss_row, dx.reshape(B, S, D), g


def kernel(x, mem, mem_norm, norm_pre_mix, norm_post_mix, norm_pre_mlp, norm_post_mlp, w_in_a, conv_w_a, a_log_a, dt_bias_a, onorm_a, w_in_b, w_mem_kv, w_out, w_up, w_down, loss_target, m_mem_norm, m_norm_pre_mix, m_norm_post_mix, m_norm_pre_mlp, m_norm_post_mlp, m_w_in_a, m_conv_w_a, m_a_log_a, m_dt_bias_a, m_onorm_a, m_w_in_b, m_w_mem_kv, m_w_out, m_w_up, m_w_down, v_mem_norm, v_norm_pre_mix, v_norm_post_mix, v_norm_pre_mlp, v_norm_post_mlp, v_w_in_a, v_conv_w_a, v_a_log_a, v_dt_bias_a, v_onorm_a, v_w_in_b, v_w_mem_kv, v_w_out, v_w_up, v_w_down):
    args = dict(locals())
    big_names = [n for n, _ in _SHARDED]
    axes = dict(_SHARDED)
    shard_shapes = [args[n].shape for n in big_names]
    rows_total = -(-sum(_rows(math.prod(s)) for s in shard_shapes) // 1024) * 1024

    exact = ("conv_w_a",) if _WIRE_DTYPE != F32 else ()
    wire = [lax.bitcast_convert_type(args[n], _WIRE_DTYPE) if n in exact else args[n].astype(_WIRE_DTYPE)
            for n in big_names]
    wire_shapes = [a.shape for a in wire]
    wire_rows = -(-sum(_rows(math.prod(s)) for s in wire_shapes) // ROW_ALIGN) * ROW_ALIGN
    gathered = _all_gather(_pack(wire, total=wire_rows))
    p = {}
    for n, b in zip(big_names, _unpack(gathered, wire_shapes, lead=(N_DEV,))):
        b = lax.bitcast_convert_type(b, F32) if n in exact else b.astype(_MXU_DTYPE)
        p[n] = _to_full(b, axes[n])
    for n in _REPLICATED:
        p[n] = args[n]
    w_flat = _pack([args[n] for n in big_names], total=rows_total)

    loss_row, grad_x, g = _local_step(x, mem, loss_target, p)
    loss = lax.psum(loss_row[0, 0], ("x", "y", "c"))

    g_blocks = _pack([_to_blocks(g[n], axes[n]) for n in big_names], total=rows_total, lead=(N_DEV,))
    rep_shapes = [args[n].shape for n in _REPLICATED]
    g_small = _pack([g[n] for n in _REPLICATED], align=8)
    recv_sib = _exchange_sibling(g_blocks.astype(_WIRE_DTYPE))
    part, own = _partial_sum(g_blocks.reshape(N_CHIP, 2, rows_total, LANE), recv_sib)
    recv_big, recv_small = _exchange_chips(part, g_small)

    m_flat = _pack([args["m_" + n] for n in big_names], total=rows_total)
    v_flat = _pack([args["v_" + n] for n in big_names], total=rows_total)
    outs_big = [_unpack(f, shard_shapes) for f in _adamw_shard(own, recv_big, w_flat, m_flat, v_flat)]
    outs_small = [_unpack(f, rep_shapes, align=8) for f in _adamw_replicated(
        recv_small, _pack([args[n] for n in _REPLICATED], align=8),
        _pack([args["m_" + n] for n in _REPLICATED], align=8), _pack([args["v_" + n] for n in _REPLICATED], align=8))]

    order = ["mem_norm", "norm_pre_mix", "norm_post_mix", "norm_pre_mlp", "norm_post_mlp", "w_in_a", "conv_w_a",
             "a_log_a", "dt_bias_a", "onorm_a", "w_in_b", "w_mem_kv", "w_out", "w_up", "w_down"]
    result = [loss, grad_x]
    for kind in range(4):
        for n in order:
            if n in axes:
                result.append(outs_big[kind][big_names.index(n)])
            else:
                result.append(outs_small[kind][_REPLICATED.index(n)])
    return tuple(result)
```

```python
import functools
import math

import jax
import jax.numpy as jnp
from jax import lax
from jax.experimental import pallas as pl
from jax.experimental.pallas import tpu as pltpu

F32 = jnp.float32
_MXU_DTYPE = jnp.bfloat16
_WIRE_DTYPE = jnp.bfloat16
_HI = lax.Precision.HIGH

D_MODEL = 1024
N_DEV = 8
N_MEM = 256
X_WIDTH = 256
N_X_HEADS = 4
X_HEAD_DIM = 64
SEQ_MIX_WIDTH = 768
LIN_HEAD_DIM = 128
N_LIN_HEADS = 6
CONV_WIDTH = 4
CHUNK = 64
SB_HEAD_DIM = 64
SB_BLOCK = 128
D_FF = 4096
EPS = 1e-6
IN_A = 3340
IN_A_PAD = 3584
IN_B = 2560
SM_COL = 3328

ADAM_LR = 0.001
ADAM_B1 = 0.9
ADAM_B2 = 0.999
ADAM_EPS = 1e-08
ADAM_WD = 0.01
ADAM_STEP = 10

LANE = 128
VMEM_LIMIT = 56 * 1024 * 1024


def _cp(sem=None):
    return pltpu.CompilerParams(dimension_semantics=sem, vmem_limit_bytes=VMEM_LIMIT)


def _pick(n, target):
    if n <= target:
        return n
    best = None
    for t in range(LANE, target + 1, LANE):
        if n % t == 0:
            best = t
    assert best is not None, (n, target)
    return best


def _dot(a, b, ca=1, cb=0):
    return lax.dot_general(a.astype(_MXU_DTYPE), b.astype(_MXU_DTYPE), (((ca,), (cb,)), ((), ())),
                           preferred_element_type=F32)


def _bdot(a, b, ca, cb):
    return lax.dot_general(a.astype(_MXU_DTYPE), b.astype(_MXU_DTYPE), (((ca,), (cb,)), ((0,), (0,))),
                           preferred_element_type=F32)


def _bdot_hi(a, b):
    return lax.dot_general(a, b, (((2,), (1,)), ((0,), (0,))), precision=_HI, preferred_element_type=F32)


def _sigmoid(x):
    return 1.0 / (1.0 + jnp.exp(-x))


def _silu(x):
    return x * _sigmoid(x)


def _softplus(x):
    return jnp.maximum(x, 0.0) + jnp.log(1.0 + jnp.exp(-jnp.abs(x)))


def _rms(x, g):
    return x * lax.rsqrt(jnp.mean(x * x, axis=-1, keepdims=True) + EPS) * g


def _norm_fwd(x, g, resid=None, out_dtype=F32, name="norm_fwd"):
    T, D = x.shape
    tm = _pick(T, 512)
    has_resid = resid is not None

    def body(*refs):
        if has_resid:
            x_ref, g_ref, r_ref, o_ref = refs
        else:
            x_ref, g_ref, o_ref = refs
        y = _rms(x_ref[...].astype(F32), g_ref[...])
        if has_resid:
            y = r_ref[...] + y
        o_ref[...] = y.astype(out_dtype)

    row = pl.BlockSpec((tm, D), lambda i: (i, 0))
    in_specs = [row, pl.BlockSpec((1, D), lambda i: (0, 0))] + ([row] if has_resid else [])
    args = (x, g) + ((resid,) if has_resid else ())
    return pl.pallas_call(body, grid=(T // tm,), in_specs=in_specs, out_specs=row,
                          out_shape=jax.ShapeDtypeStruct((T, D), out_dtype),
                          compiler_params=_cp(("parallel",)), name=name)(*args)


def _norm_bwd(dy, x, g, resid=None, name="norm_bwd"):
    T, D = x.shape
    tm = _pick(T, 512)
    has_resid = resid is not None

    def body(*refs):
        if has_resid:
            dy_ref, x_ref, g_ref, r_ref, dx_ref, dg_ref = refs
        else:
            dy_ref, x_ref, g_ref, dx_ref, dg_ref = refs
        _, vjp = jax.vjp(_rms, x_ref[...].astype(F32), g_ref[...])
        dx, dg = vjp(dy_ref[...].astype(F32))
        if has_resid:
            dx = r_ref[...] + dx
        dx_ref[...] = dx

        @pl.when(pl.program_id(0) == 0)
        def _():
            dg_ref[...] = jnp.zeros_like(dg_ref)

        dg_ref[...] += dg

    row = pl.BlockSpec((tm, D), lambda i: (i, 0))
    vec = pl.BlockSpec((1, D), lambda i: (0, 0))
    in_specs = [row, row, vec] + ([row] if has_resid else [])
    args = (dy, x, g) + ((resid,) if has_resid else ())
    return pl.pallas_call(body, grid=(T // tm,), in_specs=in_specs, out_specs=(row, vec),
                          out_shape=(jax.ShapeDtypeStruct((T, D), F32), jax.ShapeDtypeStruct((1, D), F32)),
                          compiler_params=_cp(("arbitrary",)), name=name)(*args)


def _mm(a, b, *, ta=False, tb=False, out_dtypes=(F32,), epilogue=None, extras=(), name="mm",
        tm_t=1024, tn_t=1024, tk_t=1024):
    M, K = (a.shape[1], a.shape[0]) if ta else a.shape
    N = b.shape[0] if tb else b.shape[1]
    assert (b.shape[1] if tb else b.shape[0]) == K, (a.shape, b.shape, ta, tb)
    tm, tn, tk = _pick(M, tm_t), _pick(N, tn_t), _pick(K, tk_t)
    nk = K // tk
    n_extra = len(extras)
    n_out = len(out_dtypes)

    def body(*refs):
        a_ref, b_ref = refs[0], refs[1]
        e_refs = refs[2:2 + n_extra]
        o_refs = refs[2 + n_extra:2 + n_extra + n_out]

        def finish(acc):
            outs = (acc,) if epilogue is None else epilogue(acc, *[e[...] for e in e_refs])
            for o_ref, o in zip(o_refs, outs):
                o_ref[...] = o.astype(o_ref.dtype)

        d = _dot(a_ref[...], b_ref[...], 0 if ta else 1, 1 if tb else 0)
        if nk == 1:
            finish(d)
            return
        acc_ref = refs[-1]
        k = pl.program_id(2)

        @pl.when(k == 0)
        def _():
            acc_ref[...] = d

        @pl.when((k > 0) & (k < nk - 1))
        def _():
            acc_ref[...] += d

        @pl.when(k == nk - 1)
        def _():
            finish(acc_ref[...] + d)

    a_spec = pl.BlockSpec((tk, tm), lambda i, j, k: (k, i)) if ta else pl.BlockSpec((tm, tk), lambda i, j, k: (i, k))
    b_spec = pl.BlockSpec((tn, tk), lambda i, j, k: (j, k)) if tb else pl.BlockSpec((tk, tn), lambda i, j, k: (k, j))
    o_spec = pl.BlockSpec((tm, tn), lambda i, j, k: (i, j))
    outs = pl.pallas_call(
        body, grid=(M // tm, N // tn, nk),
        in_specs=[a_spec, b_spec] + [o_spec] * n_extra,
        out_specs=tuple([o_spec] * n_out),
        out_shape=tuple(jax.ShapeDtypeStruct((M, N), dt) for dt in out_dtypes),
        scratch_shapes=[pltpu.VMEM((tm, tn), F32)] if nk > 1 else [],
        compiler_params=_cp(("parallel", "parallel", "arbitrary")), name=name)(a, b, *extras)
    return outs[0] if n_out == 1 else outs


def _relu2_epilogue(acc):
    r = jnp.maximum(acc, 0.0)
    return r * r, r


def _drelu2_epilogue(acc, r):
    return (acc * (2.0 * r.astype(F32)),)


def _loss_head(x, target, name="loss_head"):
    T, D = x.shape
    tm = _pick(T, 512)

    def body(x_ref, t_ref, l_ref, dx_ref):
        e = x_ref[...] - t_ref[...]
        dx_ref[...] = e * (1.0 / D)

        @pl.when(pl.program_id(0) == 0)
        def _():
            l_ref[...] = jnp.zeros_like(l_ref)

        part = 0.5 * jnp.sum(jnp.mean(e * e, axis=-1, keepdims=True), axis=0, keepdims=True)
        l_ref[...] += jnp.broadcast_to(part, l_ref.shape)

    row = pl.BlockSpec((tm, D), lambda i: (i, 0))
    return pl.pallas_call(body, grid=(T // tm,), in_specs=[row, row],
                          out_specs=(pl.BlockSpec((1, LANE), lambda i: (0, 0)), row),
                          out_shape=(jax.ShapeDtypeStruct((1, LANE), F32), jax.ShapeDtypeStruct((T, D), F32)),
                          compiler_params=_cp(("arbitrary",)), name=name)(x, target)


def _shift_down(x, k, row):
    return jnp.where(row >= k, pltpu.roll(x, k, 0), 0.0)


def _shift_up(x, k, row, n):
    return jnp.where(row < n - k, pltpu.roll(x, n - k, 0), 0.0)


def _conv_taps(x, w, row):
    y = x * w[CONV_WIDTH - 1:CONV_WIDTH, :]
    for i in range(CONV_WIDTH - 1):
        y = y + _shift_down(x, CONV_WIDTH - 1 - i, row) * w[i:i + 1, :]
    return y


def _qkv_act(xc, j):
    s = _silu(xc)
    n = s * lax.rsqrt(jnp.sum(s * s, axis=-1, keepdims=True) + EPS)
    n = n * jnp.where(j < N_LIN_HEADS, LIN_HEAD_DIM ** -0.5, 1.0)
    return jnp.where(j < 2 * N_LIN_HEADS, n, s)


def _gdn_conv_fwd(proj, conv_w, B, S):
    nblk = 3 * N_LIN_HEADS

    def body(p_ref, w_ref, o_ref):
        j = pl.program_id(1)
        x = p_ref[...]
        row = lax.broadcasted_iota(jnp.int32, x.shape, 0)
        o_ref[...] = _qkv_act(_conv_taps(x, w_ref[...], row), j)

    blk = pl.BlockSpec((S, LANE), lambda b, j: (b, j))
    return pl.pallas_call(body, grid=(B, nblk),
                          in_specs=[blk, pl.BlockSpec((CONV_WIDTH, LANE), lambda b, j: (0, j))],
                          out_specs=blk, out_shape=jax.ShapeDtypeStruct((B * S, nblk * LANE), F32),
                          compiler_params=_cp(("parallel", "parallel")), name="gdn_conv_fwd")(proj, conv_w)


def _gdn_conv_bwd(dq, dk, dv, proj, conv_w, B, S):
    nblk = 3 * N_LIN_HEADS
    H = N_LIN_HEADS

    def body(dq_ref, dk_ref, dv_ref, p_ref, w_ref, dp_ref, dw_ref):
        j = pl.program_id(0)
        b = pl.program_id(1)
        x = p_ref[...]
        w = w_ref[...]
        row = lax.broadcasted_iota(jnp.int32, x.shape, 0)
        d_act = jnp.where(j < H, dq_ref[...], jnp.where(j < 2 * H, dk_ref[...], dv_ref[...]))
        _, vjp = jax.vjp(lambda t: _qkv_act(t, j), _conv_taps(x, w, row))
        (d_xc,) = vjp(d_act)
        dx = d_xc * w[CONV_WIDTH - 1:CONV_WIDTH, :]
        for i in range(CONV_WIDTH - 1):
            dx = dx + _shift_up(d_xc, CONV_WIDTH - 1 - i, row, S) * w[i:i + 1, :]
        dp_ref[...] = dx.astype(dp_ref.dtype)

        @pl.when(b == 0)
        def _():
            dw_ref[...] = jnp.zeros_like(dw_ref)

        for i in range(CONV_WIDTH):
            xs = x if i == CONV_WIDTH - 1 else _shift_down(x, CONV_WIDTH - 1 - i, row)
            dw_ref[i:i + 1, :] += jnp.sum(d_xc * xs, axis=0, keepdims=True)

    blk = pl.BlockSpec((S, LANE), lambda j, b: (b, j))
    wblk = pl.BlockSpec((CONV_WIDTH, LANE), lambda j, b: (0, j))
    return pl.pallas_call(
        body, grid=(nblk, B),
        in_specs=[pl.BlockSpec((S, LANE), lambda j, b: (b, jnp.clip(j, 0, H - 1))),
                  pl.BlockSpec((S, LANE), lambda j, b: (b, jnp.clip(j - H, 0, H - 1))),
                  pl.BlockSpec((S, LANE), lambda j, b: (b, jnp.clip(j - 2 * H, 0, H - 1))),
                  blk, wblk],
        out_specs=(blk, wblk),
        out_shape=(jax.ShapeDtypeStruct((B * S, nblk * LANE), _MXU_DTYPE),
                   jax.ShapeDtypeStruct((CONV_WIDTH, nblk * LANE), F32)),
        compiler_params=_cp(("parallel", "arbitrary")), name="gdn_conv_bwd")(dq, dk, dv, proj, conv_w)


def _chunk_cumsum(x, row):
    pos = row % CHUNK
    k = 1
    while k < CHUNK:
        x = x + jnp.where(pos >= k, pltpu.roll(x, k, 0), 0.0)
        k *= 2
    return x


def _chunk_rev_cumsum(x, row, n):
    pos = row % CHUNK
    k = 1
    while k < CHUNK:
        x = x + jnp.where(pos < CHUNK - k, pltpu.roll(x, n - k, 0), 0.0)
        k *= 2
    return x


def _gdn_gates_fwd(proj, a_log, dt_bias, B, S):
    H = N_LIN_HEADS

    def body(sm_ref, al_ref, dt_ref, beta_ref, gc_ref):
        sm = sm_ref[...]
        row = lax.broadcasted_iota(jnp.int32, (S, LANE), 0)
        for h in range(H):
            beta = _sigmoid(sm[:, h:h + 1])
            g = -jnp.exp(al_ref[0:1, h:h + 1]) * _softplus(sm[:, H + h:H + h + 1] + dt_ref[0:1, h:h + 1])
            beta_ref[:, h * LANE:(h + 1) * LANE] = jnp.broadcast_to(beta, (S, LANE))
            gc_ref[:, h * LANE:(h + 1) * LANE] = _chunk_cumsum(jnp.broadcast_to(g, (S, LANE)), row)

    vec = pl.BlockSpec((1, LANE), lambda b: (0, 0))
    wide = pl.BlockSpec((S, H * LANE), lambda b: (b, 0))
    return pl.pallas_call(body, grid=(B,),
                          in_specs=[pl.BlockSpec((S, LANE), lambda b: (b, SM_COL // LANE)), vec, vec],
                          out_specs=(wide, wide),
                          out_shape=(jax.ShapeDtypeStruct((B * S, H * LANE), F32),) * 2,
                          compiler_params=_cp(("parallel",)), name="gdn_gates_fwd")(proj, a_log, dt_bias)


def _gdn_gates_bwd(d_beta, d_gc, proj, a_log, dt_bias, B, S):
    H = N_LIN_HEADS

    def body(db_ref, dgc_ref, sm_ref, al_ref, dt_ref, dsm_ref, dal_ref, ddt_ref):
        sm = sm_ref[...]
        row = lax.broadcasted_iota(jnp.int32, (S, LANE), 0)
        lane = lax.broadcasted_iota(jnp.int32, (1, LANE), 1)
        dsm = jnp.zeros((S, LANE), F32)
        dal = jnp.zeros((1, LANE), F32)
        ddt = jnp.zeros((1, LANE), F32)
        for h in range(H):
            beta = _sigmoid(sm[:, h:h + 1])
            dbeta = jnp.sum(db_ref[:, h * LANE:(h + 1) * LANE], axis=-1, keepdims=True)
            d_bl = dbeta * beta * (1.0 - beta)
            dgc = jnp.sum(dgc_ref[:, h * LANE:(h + 1) * LANE], axis=-1, keepdims=True)
            dg = _chunk_rev_cumsum(jnp.broadcast_to(dgc, (S, LANE)), row, S)[:, 0:1]
            z = sm[:, H + h:H + h + 1] + dt_ref[0:1, h:h + 1]
            a = jnp.exp(al_ref[0:1, h:h + 1])
            g = -a * _softplus(z)
            d_al = dg * (-a) * _sigmoid(z)
            dsm = dsm + jnp.where(lane == h, d_bl, 0.0) + jnp.where(lane == H + h, d_al, 0.0)
            ddt = ddt + jnp.where(lane == h, jnp.sum(d_al, axis=0, keepdims=True), 0.0)
            dal = dal + jnp.where(lane == h, jnp.sum(dg * g, axis=0, keepdims=True), 0.0)
        dsm_ref[...] = dsm.astype(dsm_ref.dtype)

        @pl.when(pl.program_id(0) == 0)
        def _():
            dal_ref[...] = jnp.zeros_like(dal_ref)
            ddt_ref[...] = jnp.zeros_like(ddt_ref)

        dal_ref[...] += dal
        ddt_ref[...] += ddt

    vec = pl.BlockSpec((1, LANE), lambda b: (0, 0))
    wide = pl.BlockSpec((S, H * LANE), lambda b: (b, 0))
    return pl.pallas_call(body, grid=(B,),
                          in_specs=[wide, wide, pl.BlockSpec((S, LANE), lambda b: (b, SM_COL // LANE)), vec, vec],
                          out_specs=(pl.BlockSpec((S, LANE), lambda b: (b, 0)), vec, vec),
                          out_shape=(jax.ShapeDtypeStruct((B * S, LANE), _MXU_DTYPE),
                                     jax.ShapeDtypeStruct((1, LANE), F32), jax.ShapeDtypeStruct((1, LANE), F32)),
                          compiler_params=_cp(("arbitrary",)), name="gdn_gates_bwd")(d_beta, d_gc, proj, a_log, dt_bias)


PREP_ROWS = 512


@jax.custom_vjp
def _unit_lower_inverse(lower):
    n, C, _ = lower.shape
    ri = lax.broadcasted_iota(jnp.int32, (C, C), 0)
    ci = lax.broadcasted_iota(jnp.int32, (C, C), 1)
    p = -lower
    inv = jnp.where((ri == ci)[None], 1.0, 0.0) + p
    for _ in range(int(math.log2(C)) - 1):
        p = _bdot_hi(p, p)
        inv = inv + _bdot_hi(inv, p)
    return inv


def _unit_lower_inverse_fwd(lower):
    inv = _unit_lower_inverse(lower)
    return inv, inv


def _unit_lower_inverse_bwd(inv, d_inv):
    inv_t = jnp.swapaxes(inv, 1, 2)
    return (-_bdot_hi(_bdot_hi(inv_t, d_inv), inv_t),)


_unit_lower_inverse.defvjp(_unit_lower_inverse_fwd, _unit_lower_inverse_bwd)


def _prep_fn(q, k, v, beta, gc):
    R = q.shape[0]
    n = R // CHUNK
    q3, k3, v3, b3, g3 = [t.reshape(n, CHUNK, LIN_HEAD_DIM) for t in (q, k, v, beta, gc)]
    ri = lax.broadcasted_iota(jnp.int32, (CHUNK, CHUNK), 0)
    ci = lax.broadcasted_iota(jnp.int32, (CHUNK, CHUNK), 1)
    causal = (ri >= ci)[None]
    strict = (ri > ci)[None]
    gcol = g3[:, :, 0:1]
    grow = jnp.swapaxes(g3, 1, 2)[:, 0:1, :]
    decay = jnp.exp(jnp.where(causal, gcol - grow, -1e30))
    kb = k3 * b3
    lower = jnp.where(strict, _bdot(kb, k3, 2, 2) * decay, 0.0)
    inv = _unit_lower_inverse(lower)
    eg = jnp.exp(g3)
    sol = _bdot_hi(inv, jnp.concatenate([v3 * b3, kb * eg], axis=-1))
    u, w = sol[..., :LIN_HEAD_DIM], sol[..., LIN_HEAD_DIM:]
    intra = _bdot(q3, k3, 2, 2) * decay
    q_dec = q3 * eg
    k_dec = k3 * jnp.exp(g3[:, CHUNK - 1:CHUNK, :] - g3)
    return (u.reshape(R, LIN_HEAD_DIM), w.reshape(R, LIN_HEAD_DIM), q_dec.reshape(R, LIN_HEAD_DIM),
            k_dec.reshape(R, LIN_HEAD_DIM), intra.reshape(R, CHUNK))


def _prep_specs(S):
    H = N_LIN_HEADS
    R = min(PREP_ROWS, S)
    nr = S // R

    def col(off):
        return pl.BlockSpec((R, LANE), lambda b, h, r: (b * nr + r, off + h))

    intra = pl.BlockSpec((None, R, CHUNK), lambda b, h, r: (h, b * nr + r, 0))
    return R, nr, col, intra


def _gdn_prep_fwd(act, beta, gc, B, S):
    H = N_LIN_HEADS
    R, nr, col, intra_spec = _prep_specs(S)
    T = B * S

    def body(q_ref, k_ref, v_ref, b_ref, g_ref, u_ref, w_ref, qd_ref, kd_ref, a_ref):
        u, w, qd, kd, a = _prep_fn(q_ref[...], k_ref[...], v_ref[...], b_ref[...], g_ref[...])
        u_ref[...] = u
        w_ref[...] = w
        qd_ref[...] = qd
        kd_ref[...] = kd
        a_ref[...] = a

    wide = jax.ShapeDtypeStruct((T, H * LANE), F32)
    return pl.pallas_call(body, grid=(B, H, nr),
                          in_specs=[col(0), col(H), col(2 * H), col(0), col(0)],
                          out_specs=(col(0), col(0), col(0), col(0), intra_spec),
                          out_shape=(wide, wide, wide, wide, jax.ShapeDtypeStruct((H, T, CHUNK), F32)),
                          compiler_params=_cp(("parallel", "parallel", "parallel")),
                          name="gdn_prep_fwd")(act, act, act, beta, gc)


def _gdn_prep_bwd(act, beta, gc, du, dw, dqd, dkd, da, B, S):
    H = N_LIN_HEADS
    R, nr, col, intra_spec = _prep_specs(S)
    T = B * S

    def body(q_ref, k_ref, v_ref, b_ref, g_ref, du_ref, dw_ref, dqd_ref, dkd_ref, da_ref,
             dq_ref, dk_ref, dv_ref, db_ref, dg_ref):
        _, vjp = jax.vjp(_prep_fn, q_ref[...], k_ref[...], v_ref[...], b_ref[...], g_ref[...])
        dq, dk, dv, db, dg = vjp((du_ref[...], dw_ref[...], dqd_ref[...], dkd_ref[...], da_ref[...]))
        dq_ref[...] = dq
        dk_ref[...] = dk
        dv_ref[...] = dv
        db_ref[...] = db
        dg_ref[...] = dg

    wide = jax.ShapeDtypeStruct((T, H * LANE), F32)
    return pl.pallas_call(body, grid=(B, H, nr),
                          in_specs=[col(0), col(H), col(2 * H), col(0), col(0),
                                    col(0), col(0), col(0), col(0), intra_spec],
                          out_specs=(col(0),) * 5, out_shape=(wide,) * 5,
                          compiler_params=_cp(("parallel", "parallel", "parallel")),
                          name="gdn_prep_bwd")(act, act, act, beta, gc, du, dw, dqd, dkd, da)


def _scan_step(u, w, qd, kd, a, g_last, state):
    v_new = u - _dot(w, state)
    o = _dot(qd, state) + _dot(a, v_new)
    new_state = state * jnp.exp(g_last) + _dot(kd, v_new, 0, 0)
    return o, new_state


SCAN_HEADS = 3
SCAN_ROWS = 512


def _scan_specs(B, S, reverse):
    HP = SCAN_HEADS
    R = min(SCAN_ROWS, S)
    nr = S // R

    def blk(r):
        return nr - 1 - r if reverse else r

    col = pl.BlockSpec((R, HP * LANE), lambda b, h, r: (b * nr + blk(r), h))
    intra = pl.BlockSpec((HP, R, CHUNK), lambda b, h, r: (h, b * nr + blk(r), 0))
    st = pl.BlockSpec((None, HP, R // CHUNK, LIN_HEAD_DIM, LIN_HEAD_DIM), lambda b, h, r: (b, h, blk(r), 0, 0))
    return R, nr, col, intra, st


def _gdn_scan_fwd(u, w, qd, kd, a, gc, B, S):
    H = N_LIN_HEADS
    R, nr, col, intra, st = _scan_specs(B, S, reverse=False)

    def body(u_ref, w_ref, qd_ref, kd_ref, a_ref, g_ref, o_ref, st_ref, carry_ref):
        @pl.when(pl.program_id(2) == 0)
        def _():
            carry_ref[...] = jnp.zeros_like(carry_ref)

        def step(c, states):
            rows = pl.ds(pl.multiple_of(c * CHUNK, CHUNK), CHUNK)
            new_states = []
            for hh, state in enumerate(states):
                cols = slice(hh * LANE, (hh + 1) * LANE)
                st_ref[hh, c] = state.astype(st_ref.dtype)
                o, new_state = _scan_step(u_ref[rows, cols], w_ref[rows, cols], qd_ref[rows, cols], kd_ref[rows, cols],
                                          a_ref[hh, rows, :], g_ref[rows, cols][CHUNK - 1:CHUNK, :], state)
                o_ref[rows, cols] = o
                new_states.append(new_state)
            return tuple(new_states)

        states = lax.fori_loop(0, R // CHUNK, step, tuple(carry_ref[hh] for hh in range(SCAN_HEADS)))
        for hh, state in enumerate(states):
            carry_ref[hh] = state

    return pl.pallas_call(body, grid=(B, H // SCAN_HEADS, nr), in_specs=[col, col, col, col, intra, col],
                          out_specs=(col, st),
                          out_shape=(jax.ShapeDtypeStruct((B * S, H * LANE), F32),
                                     jax.ShapeDtypeStruct((B, H, S // CHUNK, LIN_HEAD_DIM, LIN_HEAD_DIM), _MXU_DTYPE)),
                          scratch_shapes=[pltpu.VMEM((SCAN_HEADS, LIN_HEAD_DIM, LIN_HEAD_DIM), F32)],
                          compiler_params=_cp(("parallel", "parallel", "arbitrary")),
                          name="gdn_scan_fwd")(u, w, qd, kd, a, gc)


def _gdn_scan_bwd(u, w, qd, kd, a, gc, states, do, B, S):
    H = N_LIN_HEADS
    R, nr, col, intra, st = _scan_specs(B, S, reverse=True)
    T = B * S
    n = R // CHUNK

    def body(u_ref, w_ref, qd_ref, kd_ref, a_ref, g_ref, st_ref, do_ref,
             du_ref, dw_ref, dqd_ref, dkd_ref, da_ref, dg_ref, carry_ref):
        last = lax.broadcasted_iota(jnp.int32, (CHUNK, LANE), 0) == CHUNK - 1

        @pl.when(pl.program_id(2) == 0)
        def _():
            carry_ref[...] = jnp.zeros_like(carry_ref)

        def step(i, d_states):
            c = n - 1 - i
            rows = pl.ds(pl.multiple_of(c * CHUNK, CHUNK), CHUNK)
            d_prevs = []
            for hh, d_state in enumerate(d_states):
                cols = slice(hh * LANE, (hh + 1) * LANE)
                _, vjp = jax.vjp(_scan_step, u_ref[rows, cols], w_ref[rows, cols], qd_ref[rows, cols], kd_ref[rows, cols],
                                 a_ref[hh, rows, :], g_ref[rows, cols][CHUNK - 1:CHUNK, :], st_ref[hh, c].astype(F32))
                du, dw, dqd, dkd, da, dgl, d_prev = vjp((do_ref[rows, cols].astype(F32), d_state))
                du_ref[rows, cols] = du
                dw_ref[rows, cols] = dw
                dqd_ref[rows, cols] = dqd
                dkd_ref[rows, cols] = dkd
                da_ref[hh, rows, :] = da
                dg_ref[rows, cols] = jnp.where(last, dgl, 0.0)
                d_prevs.append(d_prev)
            return tuple(d_prevs)

        d_states = lax.fori_loop(0, n, step, tuple(carry_ref[hh] for hh in range(SCAN_HEADS)))
        for hh, d_state in enumerate(d_states):
            carry_ref[hh] = d_state

    wide = jax.ShapeDtypeStruct((T, H * LANE), F32)
    return pl.pallas_call(body, grid=(B, H // SCAN_HEADS, nr), in_specs=[col, col, col, col, intra, col, st, col],
                          out_specs=(col, col, col, col, intra, col),
                          out_shape=(wide, wide, wide, wide, jax.ShapeDtypeStruct((H, T, CHUNK), F32), wide),
                          scratch_shapes=[pltpu.VMEM((SCAN_HEADS, LIN_HEAD_DIM, LIN_HEAD_DIM), F32)],
                          compiler_params=_cp(("parallel", "parallel", "arbitrary")),
                          name="gdn_scan_bwd")(u, w, qd, kd, a, gc, states, do)


GATE_COL = 3 * SEQ_MIX_WIDTH


def _post_fn(o, gate, gain):
    return o * lax.rsqrt(jnp.mean(o * o, axis=-1, keepdims=True) + EPS) * gain * _silu(gate)


def _gdn_post_fwd(o, proj, onorm, T):
    H = N_LIN_HEADS
    tm = _pick(T, 1024)

    def body(o_ref, g_ref, n_ref, y_ref):
        y_ref[...] = _post_fn(o_ref[...], g_ref[...], n_ref[...]).astype(y_ref.dtype)

    col = pl.BlockSpec((tm, LANE), lambda i, h: (i, h))
    return pl.pallas_call(body, grid=(T // tm, H),
                          in_specs=[col, pl.BlockSpec((tm, LANE), lambda i, h: (i, GATE_COL // LANE + h)),
                                    pl.BlockSpec((1, LANE), lambda i, h: (0, 0))],
                          out_specs=col, out_shape=jax.ShapeDtypeStruct((T, H * LANE), _MXU_DTYPE),
                          compiler_params=_cp(("parallel", "parallel")), name="gdn_post_fwd")(o, proj, onorm)


def _gdn_post_bwd(d_cat, o, proj, onorm, T):
    H = N_LIN_HEADS
    tm = _pick(T, 1024)

    def body(dy_ref, o_ref, g_ref, n_ref, do_ref, dg_ref, dn_ref):
        _, vjp = jax.vjp(_post_fn, o_ref[...], g_ref[...], n_ref[...])
        do, dg, dn = vjp(dy_ref[...].astype(F32))
        do_ref[...] = do
        dg_ref[...] = dg.astype(dg_ref.dtype)

        @pl.when((pl.program_id(0) == 0) & (pl.program_id(1) == 0))
        def _():
            dn_ref[...] = jnp.zeros_like(dn_ref)

        dn_ref[...] += dn

    col = pl.BlockSpec((tm, LANE), lambda i, h: (i, h))
    vec = pl.BlockSpec((1, LANE), lambda i, h: (0, 0))
    return pl.pallas_call(body, grid=(T // tm, H),
                          in_specs=[col, col, pl.BlockSpec((tm, LANE), lambda i, h: (i, GATE_COL // LANE + h)), vec],
                          out_specs=(col, col, vec),
                          out_shape=(jax.ShapeDtypeStruct((T, H * LANE), F32),
                                     jax.ShapeDtypeStruct((T, H * LANE), _MXU_DTYPE),
                                     jax.ShapeDtypeStruct((1, LANE), F32)),
                          compiler_params=_cp(("arbitrary", "arbitrary")), name="gdn_post_bwd")(d_cat, o, proj, onorm)


def _log_sigmoid(z):
    return jnp.minimum(z, 0.0) - jnp.log(1.0 + jnp.exp(-jnp.abs(z)))


def _split_dot(x, m):
    hi = x.astype(_MXU_DTYPE)
    lo = x - hi.astype(F32)
    return _dot(hi, m) + _dot(lo, m)


SB_QB = 256
SB_RC = 512


def _sb_scores(q2, k_j, scale, valid):
    z = _dot(q2, k_j, 1, 1) * scale
    lb = _log_sigmoid(z)
    return lb, jnp.where(valid, lb - z, 0.0)


def _sb_consts(QB):
    ri = lax.broadcasted_iota(jnp.int32, (SB_BLOCK, SB_BLOCK), 0)
    ci = lax.broadcasted_iota(jnp.int32, (SB_BLOCK, SB_BLOCK), 1)
    rc = min(SB_RC, 2 * QB)
    assert rc == 2 * QB or QB % rc == 0
    row = lax.broadcasted_iota(jnp.int32, (rc, SB_BLOCK), 0)
    col = lax.broadcasted_iota(jnp.int32, (rc, SB_BLOCK), 1)
    lane = lax.broadcasted_iota(jnp.int32, (1, LANE), 1)
    return {
        "rc": rc,
        "chunks": [(ch * rc, (ch * rc) % QB) for ch in range(2 * QB // rc)],
        "row_minus_col": row % QB - col,
        "after_excl": (ri > ci).astype(_MXU_DTYPE),
        "upto_incl": (ri <= ci).astype(_MXU_DTYPE),
        "upto_excl": (ri < ci).astype(_MXU_DTYPE),
        "lane": lane,
        "head0": lane < SB_HEAD_DIM,
    }


def _sb_stack(x, c):
    return jnp.concatenate([jnp.where(c["head0"], x, 0.0), jnp.where(c["head0"], 0.0, x)], axis=0)


def _sb_unstack(x2, c, QB):
    return jnp.where(c["head0"], x2[:QB], x2[QB:])


def _sb_fwd(proj, B, S):
    W = SEQ_MIX_WIDTH
    P = W // LANE
    QB = min(SB_QB, S)
    KB = SB_BLOCK
    scale = SB_HEAD_DIM ** -0.5

    def body(q_ref, k_ref, v_ref, o_ref, tot_ref):
        c = _sb_consts(QB)
        rc = c["rc"]

        def q_loop(i, carry):
            qrows = pl.ds(pl.multiple_of(i * QB, QB), QB)
            q2 = _sb_stack(q_ref[qrows, :].astype(F32), c)
            nkb = (i + 1) * (QB // KB)

            def scores(j):
                krows = pl.ds(pl.multiple_of(j * KB, KB), KB)
                valid = c["row_minus_col"] > j * KB - i * QB
                lb, l1 = _sb_scores(q2, k_ref[krows, :], scale, valid)
                return jnp.where(valid, lb, -1e30), l1

            def k_loop(t, st):
                acc, r, lbm, l1 = st
                j = nkb - 1 - t
                nxt = scores(jnp.maximum(j - 1, 0))
                krows = pl.ds(pl.multiple_of(j * KB, KB), KB)
                a = jnp.exp(lbm + r + _split_dot(l1, c["after_excl"]))
                return (acc + _dot(a, v_ref[krows, :]), r + jnp.sum(l1, axis=-1, keepdims=True)) + nxt

            acc, r, _, _ = lax.fori_loop(0, nkb, k_loop, (jnp.zeros((2 * QB, LANE), F32), jnp.zeros((2 * QB, 1), F32))
                                         + scores(nkb - 1))
            o_ref[qrows, :] = _sb_unstack(acc, c, QB)
            tot_ref[qrows, :] = _sb_unstack(jnp.broadcast_to(r, (2 * QB, LANE)), c, QB)
            return carry

        lax.fori_loop(0, S // QB, q_loop, 0)

    def col(off):
        return pl.BlockSpec((S, LANE), lambda b, p: (b, off + p))

    out = jax.ShapeDtypeStruct((B * S, W), F32)
    return pl.pallas_call(body, grid=(B, P), in_specs=[col(0), col(P), col(2 * P)], out_specs=(col(0), col(0)),
                          out_shape=(out, out),
                          compiler_params=_cp(("parallel", "parallel")), name="sb_fwd")(proj, proj, proj)


def _sb_bwd(proj, tot, d_cat, B, S):
    W = SEQ_MIX_WIDTH
    P = W // LANE
    QB = min(SB_QB, S)
    KB = SB_BLOCK
    scale = SB_HEAD_DIM ** -0.5

    def body(q_ref, k_ref, v_ref, tot_ref, do_ref, dq_ref, dk_ref, dv_ref, dk_acc, dv_acc):
        c = _sb_consts(QB)
        rc = c["rc"]
        dk_acc[...] = jnp.zeros_like(dk_acc)
        dv_acc[...] = jnp.zeros_like(dv_acc)

        def q_loop(i, carry):
            qrows = pl.ds(pl.multiple_of(i * QB, QB), QB)
            q2 = _sb_stack(q_ref[qrows, :].astype(F32), c)
            do2 = _sb_stack(do_ref[qrows, :].astype(F32), c)
            tot = tot_ref[qrows, :]
            total = jnp.concatenate(
                [jnp.sum(jnp.where(c["lane"] == h * SB_HEAD_DIM, tot, 0.0), axis=-1, keepdims=True) for h in range(2)],
                axis=0)

            nkb = (i + 1) * (QB // KB)

            def scores(j):
                krows = pl.ds(pl.multiple_of(j * KB, KB), KB)
                valid = c["row_minus_col"] > j * KB - i * QB
                lb, l1 = _sb_scores(q2, k_ref[krows, :], scale, valid)
                return jnp.where(valid, lb, -1e30), l1, _dot(do2, v_ref[krows, :], 1, 1)

            def k_loop(j, st):
                dq_acc, p_l1, p_g, lbm, l1, da = st
                nxt = scores(jnp.minimum(j + 1, nkb - 1))
                krows = pl.ds(pl.multiple_of(j * KB, KB), KB)
                tail = total - p_l1 - _split_dot(l1, c["upto_incl"])
                a = jnp.exp(lbm + tail)
                g = da * a
                g_before = p_g + _split_dot(g, c["upto_excl"])
                sig = jnp.exp(lbm)
                dz = ((g * (1.0 - sig) - g_before * sig) * scale).astype(_MXU_DTYPE)
                dk_acc[krows, :] += _dot(dz, q2, 0, 0)
                dv_acc[krows, :] += _dot(a, do2, 0, 0)
                return (dq_acc + _dot(dz, k_ref[krows, :]), p_l1 + jnp.sum(l1, axis=-1, keepdims=True),
                        p_g + jnp.sum(g, axis=-1, keepdims=True)) + nxt

            zero_col = jnp.zeros((2 * QB, 1), F32)
            dq2 = lax.fori_loop(0, nkb, k_loop, (jnp.zeros((2 * QB, LANE), F32), zero_col, zero_col) + scores(0))[0]
            dq_ref[qrows, :] = _sb_unstack(dq2, c, QB).astype(dq_ref.dtype)
            return carry

        lax.fori_loop(0, S // QB, q_loop, 0)
        dk_ref[...] = dk_acc[...].astype(dk_ref.dtype)
        dv_ref[...] = dv_acc[...].astype(dv_ref.dtype)

    def col(off):
        return pl.BlockSpec((S, LANE), lambda b, p: (b, off + p))

    out = jax.ShapeDtypeStruct((B * S, W), _MXU_DTYPE)
    return pl.pallas_call(body, grid=(B, P), in_specs=[col(0), col(P), col(2 * P), col(0), col(0)],
                          out_specs=(col(0),) * 3, out_shape=(out,) * 3,
                          scratch_shapes=[pltpu.VMEM((S, LANE), F32), pltpu.VMEM((S, LANE), F32)],
                          compiler_params=_cp(("parallel", "parallel")), name="sb_bwd")(proj, proj, proj, tot, d_cat)


def _mem_fn(q, k, v):
    lane = lax.broadcasted_iota(jnp.int32, (1, X_WIDTH), 1)
    out = jnp.zeros(q.shape, F32)
    for h in range(N_X_HEADS):
        hm = (lane // X_HEAD_DIM) == h
        s = _dot(jnp.where(hm, q, 0.0), k, 1, 1) * (X_HEAD_DIM ** -0.5)
        e = jnp.exp(s - lax.stop_gradient(jnp.max(s, axis=-1, keepdims=True)))
        p = e / jnp.sum(e, axis=-1, keepdims=True)
        out = out + jnp.where(hm, _dot(p, v), 0.0)
    return out


def _mem_specs(S, q_col):
    ts = _pick(S, 1024)
    ns = S // ts
    qs = pl.BlockSpec((ts, X_WIDTH), lambda b, i: (b * ns + i, q_col // X_WIDTH))
    ks = pl.BlockSpec((N_MEM, X_WIDTH), lambda b, i: (b, 0))
    vs = pl.BlockSpec((N_MEM, X_WIDTH), lambda b, i: (b, 1))
    os = pl.BlockSpec((ts, X_WIDTH), lambda b, i: (b * ns + i, 0))
    return ts, ns, qs, ks, vs, os


def _mem_fwd(proj, q_col, mem_kv, B, S, name):
    ts, ns, qs, ks, vs, os = _mem_specs(S, q_col)

    def body(q_ref, k_ref, v_ref, o_ref):
        o_ref[...] = _mem_fn(q_ref[...].astype(F32), k_ref[...].astype(F32), v_ref[...].astype(F32))

    return pl.pallas_call(body, grid=(B, ns), in_specs=[qs, ks, vs], out_specs=os,
                          out_shape=jax.ShapeDtypeStruct((B * S, X_WIDTH), F32),
                          compiler_params=_cp(("parallel", "parallel")), name=name)(proj, mem_kv, mem_kv)


def _mem_bwd(proj, q_col, mem_kv, d_cat, B, S, name):
    ts, ns, qs, ks, vs, os = _mem_specs(S, q_col)

    def body(q_ref, k_ref, v_ref, do_ref, dq_ref, dk_ref, dv_ref):
        _, vjp = jax.vjp(_mem_fn, q_ref[...].astype(F32), k_ref[...].astype(F32), v_ref[...].astype(F32))
        dq, dk, dv = vjp(do_ref[...].astype(F32))
        dq_ref[...] = dq.astype(dq_ref.dtype)

        @pl.when(pl.program_id(1) == 0)
        def _():
            dk_ref[...] = jnp.zeros_like(dk_ref)
            dv_ref[...] = jnp.zeros_like(dv_ref)

        dk_ref[...] += dk
        dv_ref[...] += dv

    dos = pl.BlockSpec((ts, X_WIDTH), lambda b, i: (b * ns + i, SEQ_MIX_WIDTH // X_WIDTH))
    dq, dk, dv = pl.pallas_call(
        body, grid=(B, ns), in_specs=[qs, ks, vs, dos],
        out_specs=(os, pl.BlockSpec((N_MEM, X_WIDTH), lambda b, i: (b, 0)), pl.BlockSpec((N_MEM, X_WIDTH), lambda b, i: (b, 0))),
        out_shape=(jax.ShapeDtypeStruct((B * S, X_WIDTH), _MXU_DTYPE),
                   jax.ShapeDtypeStruct((B * N_MEM, X_WIDTH), F32), jax.ShapeDtypeStruct((B * N_MEM, X_WIDTH), F32)),
        compiler_params=_cp(("parallel", "arbitrary")), name=name)(proj, mem_kv, mem_kv, d_cat)
    return dq, dk, dv


def _peers():
    x, y, c = lax.axis_index("x"), lax.axis_index("y"), lax.axis_index("c")
    me = 4 * x + 2 * y + c
    out = []
    for fx, fy, fc in [(0, 0, 1), (1, 0, 0), (0, 1, 0), (1, 1, 0), (1, 0, 1), (0, 1, 1), (1, 1, 1)]:
        px, py, pc = x ^ fx, y ^ fy, c ^ fc
        out.append(((px, py, pc), 4 * px + 2 * py + pc))
    return me, out


ANY = pl.BlockSpec(memory_space=pl.ANY)


def _remote(src, dst, send_sems, recv_sems, k, dev):
    return pltpu.make_async_remote_copy(src_ref=src, dst_ref=dst, send_sem=send_sems.at[k], recv_sem=recv_sems.at[k],
                                        device_id=dev, device_id_type=pl.DeviceIdType.MESH)


def _place():
    x, y, c = lax.axis_index("x"), lax.axis_index("y"), lax.axis_index("c")
    return x, y, c, [(1 - x, y), (x, 1 - y), (1 - x, 1 - y)]


def _all_gather(shard):
    R = shard.shape[0]

    def body(x_ref, o_ref, send_sems, recv_sems, local_sem):
        x, y, c, chips = _place()
        me, sibling = (x, y, c), (x, y, 1 - c)

        def slot(px, py, pc):
            return o_ref.at[4 * px + 2 * py + pc]

        def copy(k, block, to, src=None):
            return _remote(slot(*block) if src is None else src, slot(*block), send_sems, recv_sems, k, to)

        mine = pltpu.make_async_copy(x_ref, slot(*me), local_sem)
        mine.start()
        first = [copy(0, me, sibling, src=x_ref)] + [copy(1 + j, me, (*chip, c), src=x_ref) for j, chip in enumerate(chips)]
        for cp in first:
            cp.start()
        passed = [copy(4 + j, (*chip, c), sibling) for j, chip in enumerate(chips)]
        for j, chip in enumerate(chips):
            copy(1 + j, (*chip, c), me).wait_recv()
            passed[j].start()
        copy(0, sibling, me).wait_recv()
        for j, chip in enumerate(chips):
            copy(4 + j, (*chip, 1 - c), me).wait_recv()
        for cp in first + passed:
            cp.wait_send()
        mine.wait()

    return pl.pallas_call(body, in_specs=[ANY], out_specs=ANY,
                          out_shape=jax.ShapeDtypeStruct((N_DEV, R, LANE), shard.dtype),
                          scratch_shapes=[pltpu.SemaphoreType.DMA((7,)), pltpu.SemaphoreType.DMA((7,)),
                                          pltpu.SemaphoreType.DMA],
                          compiler_params=pltpu.CompilerParams(has_side_effects=True),
                          name="all_gather_weights")(shard)


N_CHIP = 4


def _exchange_sibling(big):
    R = big.shape[1]

    def body(b_ref, o_ref, send_sems, recv_sems):
        x, y, c, _ = _place()
        copies = [_remote(b_ref.at[2 * k + (1 - c)], o_ref.at[k], send_sems, recv_sems, k, (x, y, 1 - c))
                  for k in range(N_CHIP)]
        for cp in copies:
            cp.start()
        for cp in copies:
            cp.wait()

    return pl.pallas_call(body, in_specs=[ANY], out_specs=ANY,
                          out_shape=jax.ShapeDtypeStruct((N_CHIP, R, LANE), big.dtype),
                          scratch_shapes=[pltpu.SemaphoreType.DMA((N_CHIP,)), pltpu.SemaphoreType.DMA((N_CHIP,))],
                          compiler_params=pltpu.CompilerParams(has_side_effects=True),
                          name="exchange_sibling")(big)


def _partial_sum(g4, recv):
    R = g4.shape[2]
    tr = _pick(R, 1024)

    def body(g_ref, r_ref, pw_ref, po_ref):
        x, y, c, _ = _place()
        g = jnp.where(c == 0, g_ref[0], g_ref[1]) + r_ref[...].astype(F32)
        pw_ref[...] = g.astype(pw_ref.dtype)

        @pl.when(pl.program_id(1) == 2 * x + y)
        def _():
            po_ref[...] = g

    return pl.pallas_call(body, grid=(R // tr, N_CHIP),
                          in_specs=[pl.BlockSpec((None, 2, tr, LANE), lambda i, k: (k, 0, i, 0)),
                                    pl.BlockSpec((None, tr, LANE), lambda i, k: (k, i, 0))],
                          out_specs=(pl.BlockSpec((None, tr, LANE), lambda i, k: (k, i, 0)),
                                     pl.BlockSpec((tr, LANE), lambda i, k: (i, 0))),
                          out_shape=(jax.ShapeDtypeStruct((N_CHIP, R, LANE), recv.dtype),
                                     jax.ShapeDtypeStruct((R, LANE), F32)),
                          compiler_params=_cp(("parallel", "arbitrary")), name="partial_sum")(g4, recv)


def _exchange_chips(part, small):
    R = part.shape[1]
    K = small.shape[0]

    def body(p_ref, s_ref, ob_ref, os_ref, send_sems, recv_sems, local_sems):
        x, y, c, chips = _place()
        my_chip = 2 * x + y
        me, peers = _peers()
        own_b = pltpu.make_async_copy(p_ref.at[my_chip], ob_ref.at[my_chip], local_sems.at[0])
        own_s = pltpu.make_async_copy(s_ref, os_ref.at[me], local_sems.at[1])
        own_b.start()
        own_s.start()
        copies = [_remote(p_ref.at[2 * px + py], ob_ref.at[my_chip], send_sems, recv_sems, j, (px, py, c))
                  for j, (px, py) in enumerate(chips)]
        copies += [_remote(s_ref, os_ref.at[me], send_sems, recv_sems, 3 + k, dev) for k, (dev, _) in enumerate(peers)]
        for cp in copies:
            cp.start()
        for j, (px, py) in enumerate(chips):
            _remote(p_ref.at[my_chip], ob_ref.at[2 * px + py], send_sems, recv_sems, j, (px, py, c)).wait_recv()
        for k, (dev, idx) in enumerate(peers):
            _remote(s_ref, os_ref.at[idx], send_sems, recv_sems, 3 + k, dev).wait_recv()
        for cp in copies:
            cp.wait_send()
        own_b.wait()
        own_s.wait()

    return pl.pallas_call(body, in_specs=[ANY, ANY], out_specs=(ANY, ANY),
                          out_shape=(jax.ShapeDtypeStruct((N_CHIP, R, LANE), part.dtype),
                                     jax.ShapeDtypeStruct((N_DEV, K, LANE), small.dtype)),
                          scratch_shapes=[pltpu.SemaphoreType.DMA((10,)), pltpu.SemaphoreType.DMA((10,)),
                                          pltpu.SemaphoreType.DMA((2,))],
                          compiler_params=pltpu.CompilerParams(has_side_effects=True),
                          name="exchange_chips")(part, small)


def _adamw_math(w, g, m, v):
    m = ADAM_B1 * m + (1.0 - ADAM_B1) * g
    v = ADAM_B2 * v + (1.0 - ADAM_B2) * (g * g)
    m_hat = m / (1.0 - ADAM_B1 ** ADAM_STEP)
    v_hat = v / (1.0 - ADAM_B2 ** ADAM_STEP)
    delta = -ADAM_LR * (m_hat / (jnp.sqrt(v_hat) + ADAM_EPS) + ADAM_WD * w)
    return delta, m, v


def _adamw_shard(own, recv, w, m, v):
    R = own.shape[0]
    tr = _pick(R, 1024)

    def body(own_ref, recv_ref, w_ref, m_ref, v_ref, g_ref, d_ref, nm_ref, nv_ref):
        x, y, _, _ = _place()
        g = own_ref[...]
        for k in range(N_CHIP):
            g = g + jnp.where(k == 2 * x + y, 0.0, recv_ref[k].astype(F32))
        delta, nm, nv = _adamw_math(w_ref[...], g, m_ref[...], v_ref[...])
        g_ref[...] = g
        d_ref[...] = delta
        nm_ref[...] = nm
        nv_ref[...] = nv

    row = pl.BlockSpec((tr, LANE), lambda i: (i, 0))
    out = jax.ShapeDtypeStruct((R, LANE), F32)
    return pl.pallas_call(body, grid=(R // tr,),
                          in_specs=[row, pl.BlockSpec((N_CHIP, tr, LANE), lambda i: (0, i, 0)), row, row, row],
                          out_specs=(row,) * 4, out_shape=(out,) * 4,
                          compiler_params=_cp(("parallel",)), name="adamw_shard")(own, recv, w, m, v)


def _adamw_replicated(parts, w, m, v):
    K = w.shape[0]

    def body(p_ref, w_ref, m_ref, v_ref, g_ref, d_ref, nm_ref, nv_ref):
        g = p_ref[0]
        for p in range(1, N_DEV):
            g = g + p_ref[p]
        delta, nm, nv = _adamw_math(w_ref[...], g, m_ref[...], v_ref[...])
        g_ref[...] = g
        d_ref[...] = delta
        nm_ref[...] = nm
        nv_ref[...] = nv

    out = jax.ShapeDtypeStruct((K, LANE), F32)
    return pl.pallas_call(body, out_shape=(out,) * 4, compiler_params=_cp(), name="adamw_replicated")(parts, w, m, v)


_SHARDED = (("w_in_a", 1), ("conv_w_a", 2), ("w_in_b", 2), ("w_mem_kv", 1), ("w_out", 1), ("w_up", 2), ("w_down", 1))
_REPLICATED = ("mem_norm", "norm_pre_mix", "norm_post_mix", "norm_pre_mlp", "norm_post_mlp", "a_log_a", "dt_bias_a", "onorm_a")
ROW_ALIGN = 16


def _rows(n_elems, align=ROW_ALIGN):
    r = -(-n_elems // LANE)
    return -(-r // align) * align


def _pack(arrays, align=ROW_ALIGN, total=None, lead=()):
    parts = []
    for a in arrays:
        n = math.prod(a.shape[len(lead):])
        flat = a.reshape(lead + (n,))
        r = _rows(n, align)
        flat = jnp.pad(flat, [(0, 0)] * len(lead) + [(0, r * LANE - n)])
        parts.append(flat.reshape(lead + (r, LANE)))
    out = jnp.concatenate(parts, axis=len(lead))
    if total is not None and out.shape[len(lead)] < total:
        out = jnp.pad(out, [(0, 0)] * len(lead) + [(0, total - out.shape[len(lead)]), (0, 0)])
    return out


def _unpack(flat, shapes, align=ROW_ALIGN, lead=()):
    outs = []
    r0 = 0
    for shp in shapes:
        n = math.prod(shp)
        r = _rows(n, align)
        part = lax.slice_in_dim(flat, r0, r0 + r, axis=len(lead))
        part = part.reshape(lead + (r * LANE,))
        part = lax.slice_in_dim(part, 0, n, axis=len(lead))
        outs.append(part.reshape(lead + tuple(shp)))
        r0 += r
    return outs


def _to_full(gathered, axis):
    g = jnp.moveaxis(gathered, 0, axis)
    shp = g.shape
    return g.reshape(shp[:axis] + (shp[axis] * shp[axis + 1],) + shp[axis + 2:])


def _to_blocks(full, axis):
    shp = full.shape
    g = full.reshape(shp[:axis] + (N_DEV, shp[axis] // N_DEV) + shp[axis + 1:])
    return jnp.moveaxis(g, axis, 0)


def _widen_in_a(w):
    main = w[:, :4 * SEQ_MIX_WIDTH]
    small = w[:, 4 * SEQ_MIX_WIDTH:4 * SEQ_MIX_WIDTH + 2 * N_LIN_HEADS]
    memq = w[:, 4 * SEQ_MIX_WIDTH + 2 * N_LIN_HEADS:]
    pad = jnp.zeros((w.shape[0], IN_A_PAD - IN_A), w.dtype)
    return jnp.concatenate([main, memq, small, pad], axis=1)


def _narrow_in_a(g):
    main = g[:, :4 * SEQ_MIX_WIDTH]
    memq = g[:, 4 * SEQ_MIX_WIDTH:4 * SEQ_MIX_WIDTH + X_WIDTH]
    small = g[:, SM_COL:SM_COL + 2 * N_LIN_HEADS]
    return jnp.concatenate([main, small, memq], axis=1)


def _row128(v):
    return jnp.pad(v.reshape(1, -1), ((0, 0), (0, LANE - v.shape[-1])))


def _local_step(x, mem, target, p):
    B, S, D = x.shape
    T = B * S
    md = _MXU_DTYPE
    x0 = x.reshape(T, D)
    tgt = target.reshape(T, D)
    memf = mem.reshape(B * N_MEM, D)
    vec = lambda a: a.reshape(1, -1)

    mem_n = _norm_fwd(memf, vec(p["mem_norm"]), out_dtype=md, name="norm_mem")
    w_in = [_widen_in_a(p["w_in_a"][0]), p["w_in_b"][0]]
    memq_col = [4 * SEQ_MIX_WIDTH, 3 * SEQ_MIX_WIDTH]
    alog = _row128(p["a_log_a"][0])
    dtb = _row128(p["dt_bias_a"][0])
    onorm = vec(p["onorm_a"][0])
    conv_w = p["conv_w_a"][0]
    saved = []
    xi = x0
    for i in range(2):
        s = {"x_in": xi}
        h1 = _norm_fwd(xi, vec(p["norm_pre_mix"][i]), out_dtype=md, name=f"norm_pre_mix{i}")
        proj = _mm(h1, w_in[i], out_dtypes=(F32 if i == 0 else md,), name=f"in_proj{i}")
        mem_kv = _mm(mem_n, p["w_mem_kv"][i], out_dtypes=(md,), name=f"mem_kv{i}")
        if i == 0:
            act = _gdn_conv_fwd(proj, conv_w, B, S)
            beta, gc = _gdn_gates_fwd(proj, alog, dtb, B, S)
            u, w, qd, kd, intra = _gdn_prep_fwd(act, beta, gc, B, S)
            o, states = _gdn_scan_fwd(u, w, qd, kd, intra, gc, B, S)
            mix = _gdn_post_fwd(o, proj, onorm, T)
            s.update(act=act, beta=beta, gc=gc, u=u, w=w, qd=qd, kd=kd, intra=intra, o=o, states=states)
        else:
            mix, tot = _sb_fwd(proj, B, S)
            s.update(tot=tot)
        cross = _mem_fwd(proj, memq_col[i], mem_kv, B, S, name=f"mem_fwd{i}")
        cat = jnp.concatenate([mix.astype(md), cross.astype(md)], axis=1)
        y = _mm(cat, p["w_out"][i], name=f"out_proj{i}")
        x_mid = _norm_fwd(y, vec(p["norm_post_mix"][i]), resid=xi, name=f"norm_post_mix{i}")
        h2 = _norm_fwd(x_mid, vec(p["norm_pre_mlp"][i]), out_dtype=md, name=f"norm_pre_mlp{i}")
        a_act, r = _mm(h2, p["w_up"][i], out_dtypes=(md, md), epilogue=_relu2_epilogue, name=f"up_proj{i}")
        y2 = _mm(a_act, p["w_down"][i], name=f"down_proj{i}")
        x_out = _norm_fwd(y2, vec(p["norm_post_mlp"][i]), resid=x_mid, name=f"norm_post_mlp{i}")
        s.update(h1=h1, proj=proj, mem_kv=mem_kv, cat=cat, y=y, x_mid=x_mid, h2=h2, a_act=a_act, r=r, y2=y2)
        saved.append(s)
        xi = x_out

    loss_row, dx = _loss_head(xi, tgt)

    g = {}
    d_mem_n = None
    gn = {k: [None, None] for k in ("norm_pre_mix", "norm_post_mix", "norm_pre_mlp", "norm_post_mlp")}
    g_w_mem_kv, g_w_out, g_w_up, g_w_down = [None, None], [None, None], [None, None], [None, None]
    for i in (1, 0):
        s = saved[i]
        d_y2, gn["norm_post_mlp"][i] = _norm_bwd(dx, s["y2"], vec(p["norm_post_mlp"][i]), name=f"norm_post_mlp_bwd{i}")
        g_w_down[i] = _mm(s["a_act"], d_y2, ta=True, name=f"down_proj_dw{i}")
        d_u = _mm(d_y2, p["w_down"][i], tb=True, out_dtypes=(md,), epilogue=_drelu2_epilogue, extras=(s["r"],),
                  name=f"down_proj_dx{i}")
        g_w_up[i] = _mm(s["h2"], d_u, ta=True, name=f"up_proj_dw{i}")
        d_h2 = _mm(d_u, p["w_up"][i], tb=True, name=f"up_proj_dx{i}")
        dx, gn["norm_pre_mlp"][i] = _norm_bwd(d_h2, s["x_mid"], vec(p["norm_pre_mlp"][i]), resid=dx,
                                              name=f"norm_pre_mlp_bwd{i}")
        d_y, gn["norm_post_mix"][i] = _norm_bwd(dx, s["y"], vec(p["norm_post_mix"][i]), name=f"norm_post_mix_bwd{i}")
        g_w_out[i] = _mm(s["cat"], d_y, ta=True, name=f"out_proj_dw{i}")
        d_cat = _mm(d_y, p["w_out"][i], tb=True, name=f"out_proj_dx{i}")
        d_memq, d_mk, d_mv = _mem_bwd(s["proj"], memq_col[i], s["mem_kv"], d_cat, B, S, name=f"mem_bwd{i}")
        d_mem_kv = jnp.concatenate([d_mk.astype(md), d_mv.astype(md)], axis=1)
        g_w_mem_kv[i] = _mm(mem_n, d_mem_kv, ta=True, name=f"mem_kv_dw{i}")
        d_mn = _mm(d_mem_kv, p["w_mem_kv"][i], tb=True, name=f"mem_kv_dx{i}")
        d_mem_n = d_mn if d_mem_n is None else d_mem_n + d_mn
        if i == 0:
            d_o, d_gate, g["onorm_a"] = _gdn_post_bwd(d_cat, s["o"], s["proj"], onorm, T)
            du, dw, dqd, dkd, da, dgc_s = _gdn_scan_bwd(s["u"], s["w"], s["qd"], s["kd"], s["intra"], s["gc"],
                                                         s["states"], d_o, B, S)
            dq, dk, dv, d_beta, d_gc = _gdn_prep_bwd(s["act"], s["beta"], s["gc"], du, dw, dqd, dkd, da, B, S)
            d_qkv, g["conv_w_a"] = _gdn_conv_bwd(dq, dk, dv, s["proj"], conv_w, B, S)
            d_sm, g["a_log_a"], g["dt_bias_a"] = _gdn_gates_bwd(d_beta, d_gc + dgc_s, s["proj"], alog, dtb, B, S)
            pad = jnp.zeros((T, IN_A_PAD - SM_COL - LANE), md)
            d_proj = jnp.concatenate([d_qkv, d_gate, d_memq, d_sm, pad], axis=1)
        else:
            dq, dk, dv = _sb_bwd(s["proj"], s["tot"], d_cat, B, S)
            d_proj = jnp.concatenate([dq, dk, dv, d_memq], axis=1)
        g_w_in = _mm(s["h1"], d_proj, ta=True, name=f"in_proj_dw{i}")
        d_h1 = _mm(d_proj, w_in[i], tb=True, name=f"in_proj_dx{i}")
        dx, gn["norm_pre_mix"][i] = _norm_bwd(d_h1, s["x_in"], vec(p["norm_pre_mix"][i]), resid=dx,
                                              name=f"norm_pre_mix_bwd{i}")
        if i == 0:
            g["w_in_a"] = _narrow_in_a(g_w_in)[None]
        else:
            g["w_in_b"] = g_w_in[None]
    _, g_mem_norm = _norm_bwd(d_mem_n, memf, vec(p["mem_norm"]), name="norm_mem_bwd")
    g["mem_norm"] = g_mem_norm.reshape(-1)
    for k, v in gn.items():
        g[k] = jnp.concatenate(v, axis=0)
    g["w_mem_kv"] = jnp.stack(g_w_mem_kv)
    g["w_out"] = jnp.stack(g_w_out)
    g["w_up"] = jnp.stack(g_w_up)
    g["w_down"] = jnp.stack(g_w_down)
    g["conv_w_a"] = g["conv_w_a"][None]
    g["a_log_a"] = g["a_log_a"][:, :N_LIN_HEADS]
    g["dt_bias_a"] = g["dt_bias_a"][:, :N_LIN_HEADS]
    return loss_row, dx.reshape(B, S, D), g


def kernel(x, mem, mem_norm, norm_pre_mix, norm_post_mix, norm_pre_mlp, norm_post_mlp, w_in_a, conv_w_a, a_log_a, dt_bias_a, onorm_a, w_in_b, w_mem_kv, w_out, w_up, w_down, loss_target, m_mem_norm, m_norm_pre_mix, m_norm_post_mix, m_norm_pre_mlp, m_norm_post_mlp, m_w_in_a, m_conv_w_a, m_a_log_a, m_dt_bias_a, m_onorm_a, m_w_in_b, m_w_mem_kv, m_w_out, m_w_up, m_w_down, v_mem_norm, v_norm_pre_mix, v_norm_post_mix, v_norm_pre_mlp, v_norm_post_mlp, v_w_in_a, v_conv_w_a, v_a_log_a, v_dt_bias_a, v_onorm_a, v_w_in_b, v_w_mem_kv, v_w_out, v_w_up, v_w_down):
    args = dict(locals())
    big_names = [n for n, _ in _SHARDED]
    axes = dict(_SHARDED)
    shard_shapes = [args[n].shape for n in big_names]
    rows_total = -(-sum(_rows(math.prod(s)) for s in shard_shapes) // 1024) * 1024

    exact = ("conv_w_a",) if _WIRE_DTYPE != F32 else ()
    wire = [lax.bitcast_convert_type(args[n], _WIRE_DTYPE) if n in exact else args[n].astype(_WIRE_DTYPE)
            for n in big_names]
    wire_shapes = [a.shape for a in wire]
    wire_rows = -(-sum(_rows(math.prod(s)) for s in wire_shapes) // ROW_ALIGN) * ROW_ALIGN
    gathered = _all_gather(_pack(wire, total=wire_rows))
    p = {}
    for n, b in zip(big_names, _unpack(gathered, wire_shapes, lead=(N_DEV,))):
        b = lax.bitcast_convert_type(b, F32) if n in exact else b.astype(_MXU_DTYPE)
        p[n] = _to_full(b, axes[n])
    for n in _REPLICATED:
        p[n] = args[n]
    w_flat = _pack([args[n] for n in big_names], total=rows_total)

    loss_row, grad_x, g = _local_step(x, mem, loss_target, p)
    loss = lax.psum(loss_row[0, 0], ("x", "y", "c"))

    g_blocks = _pack([_to_blocks(g[n], axes[n]) for n in big_names], total=rows_total, lead=(N_DEV,))
    rep_shapes = [args[n].shape for n in _REPLICATED]
    g_small = _pack([g[n] for n in _REPLICATED], align=8)
    recv_sib = _exchange_sibling(g_blocks.astype(_WIRE_DTYPE))
    part, own = _partial_sum(g_blocks.reshape(N_CHIP, 2, rows_total, LANE), recv_sib)
    recv_big, recv_small = _exchange_chips(part, g_small)

    m_flat = _pack([args["m_" + n] for n in big_names], total=rows_total)
    v_flat = _pack([args["v_" + n] for n in big_names], total=rows_total)
    outs_big = [_unpack(f, shard_shapes) for f in _adamw_shard(own, recv_big, w_flat, m_flat, v_flat)]
    outs_small = [_unpack(f, rep_shapes, align=8) for f in _adamw_replicated(
        recv_small, _pack([args[n] for n in _REPLICATED], align=8),
        _pack([args["m_" + n] for n in _REPLICATED], align=8), _pack([args["v_" + n] for n in _REPLICATED], align=8))]

    order = ["mem_norm", "norm_pre_mix", "norm_post_mix", "norm_pre_mlp", "norm_post_mlp", "w_in_a", "conv_w_a",
             "a_log_a", "dt_bias_a", "onorm_a", "w_in_b", "w_mem_kv", "w_out", "w_up", "w_down"]
    result = [loss, grad_x]
    for kind in range(4):
        for n in order:
            if n in axes:
                result.append(outs_big[kind][big_names.index(n)])
            else:
                result.append(outs_small[kind][_REPLICATED.index(n)])
    return tuple(result)
```

```python
import functools
import math

import jax
import jax.numpy as jnp
from jax import lax
from jax.experimental import pallas as pl
from jax.experimental.pallas import tpu as pltpu

F32 = jnp.float32
_MXU_DTYPE = jnp.bfloat16
_WIRE_DTYPE = jnp.bfloat16
_HI = lax.Precision.HIGH

D_MODEL = 1024
N_DEV = 8
N_MEM = 256
X_WIDTH = 256
N_X_HEADS = 4
X_HEAD_DIM = 64
SEQ_MIX_WIDTH = 768
LIN_HEAD_DIM = 128
N_LIN_HEADS = 6
CONV_WIDTH = 4
CHUNK = 64
SB_HEAD_DIM = 64
SB_BLOCK = 128
D_FF = 4096
EPS = 1e-6
IN_A = 3340
IN_A_PAD = 3584
IN_B = 2560
SM_COL = 3328

ADAM_LR = 0.001
ADAM_B1 = 0.9
ADAM_B2 = 0.999
ADAM_EPS = 1e-08
ADAM_WD = 0.01
ADAM_STEP = 10

LANE = 128
VMEM_LIMIT = 56 * 1024 * 1024


def _cp(sem=None):
    return pltpu.CompilerParams(dimension_semantics=sem, vmem_limit_bytes=VMEM_LIMIT)


def _pick(n, target):
    if n <= target:
        return n
    best = None
    for t in range(LANE, target + 1, LANE):
        if n % t == 0:
            best = t
    assert best is not None, (n, target)
    return best


def _dot(a, b, ca=1, cb=0):
    return lax.dot_general(a.astype(_MXU_DTYPE), b.astype(_MXU_DTYPE), (((ca,), (cb,)), ((), ())),
                           preferred_element_type=F32)


def _bdot(a, b, ca, cb):
    return lax.dot_general(a.astype(_MXU_DTYPE), b.astype(_MXU_DTYPE), (((ca,), (cb,)), ((0,), (0,))),
                           preferred_element_type=F32)


def _bdot_hi(a, b):
    return lax.dot_general(a, b, (((2,), (1,)), ((0,), (0,))), precision=_HI, preferred_element_type=F32)


def _sigmoid(x):
    return 1.0 / (1.0 + jnp.exp(-x))


def _silu(x):
    return x * _sigmoid(x)


def _softplus(x):
    return jnp.maximum(x, 0.0) + jnp.log(1.0 + jnp.exp(-jnp.abs(x)))


def _rms(x, g):
    return x * lax.rsqrt(jnp.mean(x * x, axis=-1, keepdims=True) + EPS) * g


def _norm_fwd(x, g, resid=None, out_dtype=F32, name="norm_fwd"):
    T, D = x.shape
    tm = _pick(T, 512)
    has_resid = resid is not None

    def body(*refs):
        if has_resid:
            x_ref, g_ref, r_ref, o_ref = refs
        else:
            x_ref, g_ref, o_ref = refs
        y = _rms(x_ref[...].astype(F32), g_ref[...])
        if has_resid:
            y = r_ref[...] + y
        o_ref[...] = y.astype(out_dtype)

    row = pl.BlockSpec((tm, D), lambda i: (i, 0))
    in_specs = [row, pl.BlockSpec((1, D), lambda i: (0, 0))] + ([row] if has_resid else [])
    args = (x, g) + ((resid,) if has_resid else ())
    return pl.pallas_call(body, grid=(T // tm,), in_specs=in_specs, out_specs=row,
                          out_shape=jax.ShapeDtypeStruct((T, D), out_dtype),
                          compiler_params=_cp(("parallel",)), name=name)(*args)


def _norm_bwd(dy, x, g, resid=None, name="norm_bwd"):
    T, D = x.shape
    tm = _pick(T, 512)
    has_resid = resid is not None

    def body(*refs):
        if has_resid:
            dy_ref, x_ref, g_ref, r_ref, dx_ref, dg_ref = refs
        else:
            dy_ref, x_ref, g_ref, dx_ref, dg_ref = refs
        _, vjp = jax.vjp(_rms, x_ref[...].astype(F32), g_ref[...])
        dx, dg = vjp(dy_ref[...].astype(F32))
        if has_resid:
            dx = r_ref[...] + dx
        dx_ref[...] = dx

        @pl.when(pl.program_id(0) == 0)
        def _():
            dg_ref[...] = jnp.zeros_like(dg_ref)

        dg_ref[...] += dg

    row = pl.BlockSpec((tm, D), lambda i: (i, 0))
    vec = pl.BlockSpec((1, D), lambda i: (0, 0))
    in_specs = [row, row, vec] + ([row] if has_resid else [])
    args = (dy, x, g) + ((resid,) if has_resid else ())
    return pl.pallas_call(body, grid=(T // tm,), in_specs=in_specs, out_specs=(row, vec),
                          out_shape=(jax.ShapeDtypeStruct((T, D), F32), jax.ShapeDtypeStruct((1, D), F32)),
                          compiler_params=_cp(("arbitrary",)), name=name)(*args)


def _mm(a, b, *, ta=False, tb=False, out_dtypes=(F32,), epilogue=None, extras=(), name="mm",
        tm_t=1024, tn_t=1024, tk_t=1024):
    M, K = (a.shape[1], a.shape[0]) if ta else a.shape
    N = b.shape[0] if tb else b.shape[1]
    assert (b.shape[1] if tb else b.shape[0]) == K, (a.shape, b.shape, ta, tb)
    tm, tn, tk = _pick(M, tm_t), _pick(N, tn_t), _pick(K, tk_t)
    nk = K // tk
    n_extra = len(extras)
    n_out = len(out_dtypes)

    def body(*refs):
        a_ref, b_ref = refs[0], refs[1]
        e_refs = refs[2:2 + n_extra]
        o_refs = refs[2 + n_extra:2 + n_extra + n_out]

        def finish(acc):
            outs = (acc,) if epilogue is None else epilogue(acc, *[e[...] for e in e_refs])
            for o_ref, o in zip(o_refs, outs):
                o_ref[...] = o.astype(o_ref.dtype)

        d = _dot(a_ref[...], b_ref[...], 0 if ta else 1, 1 if tb else 0)
        if nk == 1:
            finish(d)
            return
        acc_ref = refs[-1]
        k = pl.program_id(2)

        @pl.when(k == 0)
        def _():
            acc_ref[...] = d

        @pl.when((k > 0) & (k < nk - 1))
        def _():
            acc_ref[...] += d

        @pl.when(k == nk - 1)
        def _():
            finish(acc_ref[...] + d)

    a_spec = pl.BlockSpec((tk, tm), lambda i, j, k: (k, i)) if ta else pl.BlockSpec((tm, tk), lambda i, j, k: (i, k))
    b_spec = pl.BlockSpec((tn, tk), lambda i, j, k: (j, k)) if tb else pl.BlockSpec((tk, tn), lambda i, j, k: (k, j))
    o_spec = pl.BlockSpec((tm, tn), lambda i, j, k: (i, j))
    outs = pl.pallas_call(
        body, grid=(M // tm, N // tn, nk),
        in_specs=[a_spec, b_spec] + [o_spec] * n_extra,
        out_specs=tuple([o_spec] * n_out),
        out_shape=tuple(jax.ShapeDtypeStruct((M, N), dt) for dt in out_dtypes),
        scratch_shapes=[pltpu.VMEM((tm, tn), F32)] if nk > 1 else [],
        compiler_params=_cp(("parallel", "parallel", "arbitrary")), name=name)(a, b, *extras)
    return outs[0] if n_out == 1 else outs


def _relu2_epilogue(acc):
    r = jnp.maximum(acc, 0.0)
    return r * r, r


def _drelu2_epilogue(acc, r):
    return (acc * (2.0 * r.astype(F32)),)


def _loss_head(x, target, name="loss_head"):
    T, D = x.shape
    tm = _pick(T, 512)

    def body(x_ref, t_ref, l_ref, dx_ref):
        e = x_ref[...] - t_ref[...]
        dx_ref[...] = e * (1.0 / D)

        @pl.when(pl.program_id(0) == 0)
        def _():
            l_ref[...] = jnp.zeros_like(l_ref)

        part = 0.5 * jnp.sum(jnp.mean(e * e, axis=-1, keepdims=True), axis=0, keepdims=True)
        l_ref[...] += jnp.broadcast_to(part, l_ref.shape)

    row = pl.BlockSpec((tm, D), lambda i: (i, 0))
    return pl.pallas_call(body, grid=(T // tm,), in_specs=[row, row],
                          out_specs=(pl.BlockSpec((1, LANE), lambda i: (0, 0)), row),
                          out_shape=(jax.ShapeDtypeStruct((1, LANE), F32), jax.ShapeDtypeStruct((T, D), F32)),
                          compiler_params=_cp(("arbitrary",)), name=name)(x, target)


def _shift_down(x, k, row):
    return jnp.where(row >= k, pltpu.roll(x, k, 0), 0.0)


def _shift_up(x, k, row, n):
    return jnp.where(row < n - k, pltpu.roll(x, n - k, 0), 0.0)


def _conv_taps(x, w, row):
    y = x * w[CONV_WIDTH - 1:CONV_WIDTH, :]
    for i in range(CONV_WIDTH - 1):
        y = y + _shift_down(x, CONV_WIDTH - 1 - i, row) * w[i:i + 1, :]
    return y


def _qkv_act(xc, j):
    s = _silu(xc)
    n = s * lax.rsqrt(jnp.sum(s * s, axis=-1, keepdims=True) + EPS)
    n = n * jnp.where(j < N_LIN_HEADS, LIN_HEAD_DIM ** -0.5, 1.0)
    return jnp.where(j < 2 * N_LIN_HEADS, n, s)


def _gdn_conv_fwd(proj, conv_w, B, S):
    nblk = 3 * N_LIN_HEADS

    def body(p_ref, w_ref, o_ref):
        j = pl.program_id(1)
        x = p_ref[...]
        row = lax.broadcasted_iota(jnp.int32, x.shape, 0)
        o_ref[...] = _qkv_act(_conv_taps(x, w_ref[...], row), j)

    blk = pl.BlockSpec((S, LANE), lambda b, j: (b, j))
    return pl.pallas_call(body, grid=(B, nblk),
                          in_specs=[blk, pl.BlockSpec((CONV_WIDTH, LANE), lambda b, j: (0, j))],
                          out_specs=blk, out_shape=jax.ShapeDtypeStruct((B * S, nblk * LANE), F32),
                          compiler_params=_cp(("parallel", "parallel")), name="gdn_conv_fwd")(proj, conv_w)


def _gdn_conv_bwd(dq, dk, dv, proj, conv_w, B, S):
    nblk = 3 * N_LIN_HEADS
    H = N_LIN_HEADS

    def body(dq_ref, dk_ref, dv_ref, p_ref, w_ref, dp_ref, dw_ref):
        j = pl.program_id(0)
        b = pl.program_id(1)
        x = p_ref[...]
        w = w_ref[...]
        row = lax.broadcasted_iota(jnp.int32, x.shape, 0)
        d_act = jnp.where(j < H, dq_ref[...], jnp.where(j < 2 * H, dk_ref[...], dv_ref[...]))
        _, vjp = jax.vjp(lambda t: _qkv_act(t, j), _conv_taps(x, w, row))
        (d_xc,) = vjp(d_act)
        dx = d_xc * w[CONV_WIDTH - 1:CONV_WIDTH, :]
        for i in range(CONV_WIDTH - 1):
            dx = dx + _shift_up(d_xc, CONV_WIDTH - 1 - i, row, S) * w[i:i + 1, :]
        dp_ref[...] = dx.astype(dp_ref.dtype)

        @pl.when(b == 0)
        def _():
            dw_ref[...] = jnp.zeros_like(dw_ref)

        for i in range(CONV_WIDTH):
            xs = x if i == CONV_WIDTH - 1 else _shift_down(x, CONV_WIDTH - 1 - i, row)
            dw_ref[i:i + 1, :] += jnp.sum(d_xc * xs, axis=0, keepdims=True)

    blk = pl.BlockSpec((S, LANE), lambda j, b: (b, j))
    wblk = pl.BlockSpec((CONV_WIDTH, LANE), lambda j, b: (0, j))
    return pl.pallas_call(
        body, grid=(nblk, B),
        in_specs=[pl.BlockSpec((S, LANE), lambda j, b: (b, jnp.clip(j, 0, H - 1))),
                  pl.BlockSpec((S, LANE), lambda j, b: (b, jnp.clip(j - H, 0, H - 1))),
                  pl.BlockSpec((S, LANE), lambda j, b: (b, jnp.clip(j - 2 * H, 0, H - 1))),
                  blk, wblk],
        out_specs=(blk, wblk),
        out_shape=(jax.ShapeDtypeStruct((B * S, nblk * LANE), _MXU_DTYPE),
                   jax.ShapeDtypeStruct((CONV_WIDTH, nblk * LANE), F32)),
        compiler_params=_cp(("parallel", "arbitrary")), name="gdn_conv_bwd")(dq, dk, dv, proj, conv_w)


def _chunk_cumsum(x, row):
    pos = row % CHUNK
    k = 1
    while k < CHUNK:
        x = x + jnp.where(pos >= k, pltpu.roll(x, k, 0), 0.0)
        k *= 2
    return x


def _chunk_rev_cumsum(x, row, n):
    pos = row % CHUNK
    k = 1
    while k < CHUNK:
        x = x + jnp.where(pos < CHUNK - k, pltpu.roll(x, n - k, 0), 0.0)
        k *= 2
    return x


def _gdn_gates_fwd(proj, a_log, dt_bias, B, S):
    H = N_LIN_HEADS

    def body(sm_ref, al_ref, dt_ref, beta_ref, gc_ref):
        sm = sm_ref[...]
        row = lax.broadcasted_iota(jnp.int32, (S, LANE), 0)
        for h in range(H):
            beta = _sigmoid(sm[:, h:h + 1])
            g = -jnp.exp(al_ref[0:1, h:h + 1]) * _softplus(sm[:, H + h:H + h + 1] + dt_ref[0:1, h:h + 1])
            beta_ref[:, h * LANE:(h + 1) * LANE] = jnp.broadcast_to(beta, (S, LANE))
            gc_ref[:, h * LANE:(h + 1) * LANE] = _chunk_cumsum(jnp.broadcast_to(g, (S, LANE)), row)

    vec = pl.BlockSpec((1, LANE), lambda b: (0, 0))
    wide = pl.BlockSpec((S, H * LANE), lambda b: (b, 0))
    return pl.pallas_call(body, grid=(B,),
                          in_specs=[pl.BlockSpec((S, LANE), lambda b: (b, SM_COL // LANE)), vec, vec],
                          out_specs=(wide, wide),
                          out_shape=(jax.ShapeDtypeStruct((B * S, H * LANE), F32),) * 2,
                          compiler_params=_cp(("parallel",)), name="gdn_gates_fwd")(proj, a_log, dt_bias)


def _gdn_gates_bwd(d_beta, d_gc, proj, a_log, dt_bias, B, S):
    H = N_LIN_HEADS

    def body(db_ref, dgc_ref, sm_ref, al_ref, dt_ref, dsm_ref, dal_ref, ddt_ref):
        sm = sm_ref[...]
        row = lax.broadcasted_iota(jnp.int32, (S, LANE), 0)
        lane = lax.broadcasted_iota(jnp.int32, (1, LANE), 1)
        dsm = jnp.zeros((S, LANE), F32)
        dal = jnp.zeros((1, LANE), F32)
        ddt = jnp.zeros((1, LANE), F32)
        for h in range(H):
            beta = _sigmoid(sm[:, h:h + 1])
            dbeta = jnp.sum(db_ref[:, h * LANE:(h + 1) * LANE], axis=-1, keepdims=True)
            d_bl = dbeta * beta * (1.0 - beta)
            dgc = jnp.sum(dgc_ref[:, h * LANE:(h + 1) * LANE], axis=-1, keepdims=True)
            dg = _chunk_rev_cumsum(jnp.broadcast_to(dgc, (S, LANE)), row, S)[:, 0:1]
            z = sm[:, H + h:H + h + 1] + dt_ref[0:1, h:h + 1]
            a = jnp.exp(al_ref[0:1, h:h + 1])
            g = -a * _softplus(z)
            d_al = dg * (-a) * _sigmoid(z)
            dsm = dsm + jnp.where(lane == h, d_bl, 0.0) + jnp.where(lane == H + h, d_al, 0.0)
            ddt = ddt + jnp.where(lane == h, jnp.sum(d_al, axis=0, keepdims=True), 0.0)
            dal = dal + jnp.where(lane == h, jnp.sum(dg * g, axis=0, keepdims=True), 0.0)
        dsm_ref[...] = dsm.astype(dsm_ref.dtype)

        @pl.when(pl.program_id(0) == 0)
        def _():
            dal_ref[...] = jnp.zeros_like(dal_ref)
            ddt_ref[...] = jnp.zeros_like(ddt_ref)

        dal_ref[...] += dal
        ddt_ref[...] += ddt

    vec = pl.BlockSpec((1, LANE), lambda b: (0, 0))
    wide = pl.BlockSpec((S, H * LANE), lambda b: (b, 0))
    return pl.pallas_call(body, grid=(B,),
                          in_specs=[wide, wide, pl.BlockSpec((S, LANE), lambda b: (b, SM_COL // LANE)), vec, vec],
                          out_specs=(pl.BlockSpec((S, LANE), lambda b: (b, 0)), vec, vec),
                          out_shape=(jax.ShapeDtypeStruct((B * S, LANE), _MXU_DTYPE),
                                     jax.ShapeDtypeStruct((1, LANE), F32), jax.ShapeDtypeStruct((1, LANE), F32)),
                          compiler_params=_cp(("arbitrary",)), name="gdn_gates_bwd")(d_beta, d_gc, proj, a_log, dt_bias)


PREP_ROWS = 512


@jax.custom_vjp
def _unit_lower_inverse(lower):
    n, C, _ = lower.shape
    ri = lax.broadcasted_iota(jnp.int32, (C, C), 0)
    ci = lax.broadcasted_iota(jnp.int32, (C, C), 1)
    p = -lower
    inv = jnp.where((ri == ci)[None], 1.0, 0.0) + p
    for _ in range(int(math.log2(C)) - 1):
        p = _bdot_hi(p, p)
        inv = inv + _bdot_hi(inv, p)
    return inv


def _unit_lower_inverse_fwd(lower):
    inv = _unit_lower_inverse(lower)
    return inv, inv


def _unit_lower_inverse_bwd(inv, d_inv):
    inv_t = jnp.swapaxes(inv, 1, 2)
    return (-_bdot_hi(_bdot_hi(inv_t, d_inv), inv_t),)


_unit_lower_inverse.defvjp(_unit_lower_inverse_fwd, _unit_lower_inverse_bwd)


def _prep_fn(q, k, v, beta, gc):
    R = q.shape[0]
    n = R // CHUNK
    q3, k3, v3, b3, g3 = [t.reshape(n, CHUNK, LIN_HEAD_DIM) for t in (q, k, v, beta, gc)]
    ri = lax.broadcasted_iota(jnp.int32, (CHUNK, CHUNK), 0)
    ci = lax.broadcasted_iota(jnp.int32, (CHUNK, CHUNK), 1)
    causal = (ri >= ci)[None]
    strict = (ri > ci)[None]
    gcol = g3[:, :, 0:1]
    grow = jnp.swapaxes(g3, 1, 2)[:, 0:1, :]
    decay = jnp.exp(jnp.where(causal, gcol - grow, -1e30))
    kb = k3 * b3
    lower = jnp.where(strict, _bdot(kb, k3, 2, 2) * decay, 0.0)
    inv = _unit_lower_inverse(lower)
    eg = jnp.exp(g3)
    sol = _bdot_hi(inv, jnp.concatenate([v3 * b3, kb * eg], axis=-1))
    u, w = sol[..., :LIN_HEAD_DIM], sol[..., LIN_HEAD_DIM:]
    intra = _bdot(q3, k3, 2, 2) * decay
    q_dec = q3 * eg
    k_dec = k3 * jnp.exp(g3[:, CHUNK - 1:CHUNK, :] - g3)
    return (u.reshape(R, LIN_HEAD_DIM), w.reshape(R, LIN_HEAD_DIM), q_dec.reshape(R, LIN_HEAD_DIM),
            k_dec.reshape(R, LIN_HEAD_DIM), intra.reshape(R, CHUNK))


def _prep_specs(S):
    H = N_LIN_HEADS
    R = min(PREP_ROWS, S)
    nr = S // R

    def col(off):
        return pl.BlockSpec((R, LANE), lambda b, h, r: (b * nr + r, off + h))

    intra = pl.BlockSpec((None, R, CHUNK), lambda b, h, r: (h, b * nr + r, 0))
    return R, nr, col, intra


def _gdn_prep_fwd(act, beta, gc, B, S):
    H = N_LIN_HEADS
    R, nr, col, intra_spec = _prep_specs(S)
    T = B * S

    def body(q_ref, k_ref, v_ref, b_ref, g_ref, u_ref, w_ref, qd_ref, kd_ref, a_ref):
        u, w, qd, kd, a = _prep_fn(q_ref[...], k_ref[...], v_ref[...], b_ref[...], g_ref[...])
        u_ref[...] = u
        w_ref[...] = w
        qd_ref[...] = qd
        kd_ref[...] = kd
        a_ref[...] = a

    wide = jax.ShapeDtypeStruct((T, H * LANE), F32)
    return pl.pallas_call(body, grid=(B, H, nr),
                          in_specs=[col(0), col(H), col(2 * H), col(0), col(0)],
                          out_specs=(col(0), col(0), col(0), col(0), intra_spec),
                          out_shape=(wide, wide, wide, wide, jax.ShapeDtypeStruct((H, T, CHUNK), F32)),
                          compiler_params=_cp(("parallel", "parallel", "parallel")),
                          name="gdn_prep_fwd")(act, act, act, beta, gc)


def _gdn_prep_bwd(act, beta, gc, du, dw, dqd, dkd, da, B, S):
    H = N_LIN_HEADS
    R, nr, col, intra_spec = _prep_specs(S)
    T = B * S

    def body(q_ref, k_ref, v_ref, b_ref, g_ref, du_ref, dw_ref, dqd_ref, dkd_ref, da_ref,
             dq_ref, dk_ref, dv_ref, db_ref, dg_ref):
        _, vjp = jax.vjp(_prep_fn, q_ref[...], k_ref[...], v_ref[...], b_ref[...], g_ref[...])
        dq, dk, dv, db, dg = vjp((du_ref[...], dw_ref[...], dqd_ref[...], dkd_ref[...], da_ref[...]))
        dq_ref[...] = dq
        dk_ref[...] = dk
        dv_ref[...] = dv
        db_ref[...] = db
        dg_ref[...] = dg

    wide = jax.ShapeDtypeStruct((T, H * LANE), F32)
    return pl.pallas_call(body, grid=(B, H, nr),
                          in_specs=[col(0), col(H), col(2 * H), col(0), col(0),
                                    col(0), col(0), col(0), col(0), intra_spec],
                          out_specs=(col(0),) * 5, out_shape=(wide,) * 5,
                          compiler_params=_cp(("parallel", "parallel", "parallel")),
                          name="gdn_prep_bwd")(act, act, act, beta, gc, du, dw, dqd, dkd, da)


def _scan_step(u, w, qd, kd, a, g_last, state):
    v_new = u - _dot(w, state)
    o = _dot(qd, state) + _dot(a, v_new)
    new_state = state * jnp.exp(g_last) + _dot(kd, v_new, 0, 0)
    return o, new_state


SCAN_HEADS = 3
SCAN_ROWS = 512


def _scan_specs(B, S, reverse):
    HP = SCAN_HEADS
    R = min(SCAN_ROWS, S)
    nr = S // R

    def blk(r):
        return nr - 1 - r if reverse else r

    col = pl.BlockSpec((R, HP * LANE), lambda b, h, r: (b * nr + blk(r), h))
    intra = pl.BlockSpec((HP, R, CHUNK), lambda b, h, r: (h, b * nr + blk(r), 0))
    st = pl.BlockSpec((None, HP, R // CHUNK, LIN_HEAD_DIM, LIN_HEAD_DIM), lambda b, h, r: (b, h, blk(r), 0, 0))
    return R, nr, col, intra, st


def _gdn_scan_fwd(u, w, qd, kd, a, gc, B, S):
    H = N_LIN_HEADS
    R, nr, col, intra, st = _scan_specs(B, S, reverse=False)

    def body(u_ref, w_ref, qd_ref, kd_ref, a_ref, g_ref, o_ref, st_ref, carry_ref):
        @pl.when(pl.program_id(2) == 0)
        def _():
            carry_ref[...] = jnp.zeros_like(carry_ref)

        def step(c, states):
            rows = pl.ds(pl.multiple_of(c * CHUNK, CHUNK), CHUNK)
            new_states = []
            for hh, state in enumerate(states):
                cols = slice(hh * LANE, (hh + 1) * LANE)
                st_ref[hh, c] = state.astype(st_ref.dtype)
                o, new_state = _scan_step(u_ref[rows, cols], w_ref[rows, cols], qd_ref[rows, cols], kd_ref[rows, cols],
                                          a_ref[hh, rows, :], g_ref[rows, cols][CHUNK - 1:CHUNK, :], state)
                o_ref[rows, cols] = o
                new_states.append(new_state)
            return tuple(new_states)

        states = lax.fori_loop(0, R // CHUNK, step, tuple(carry_ref[hh] for hh in range(SCAN_HEADS)))
        for hh, state in enumerate(states):
            carry_ref[hh] = state

    return pl.pallas_call(body, grid=(B, H // SCAN_HEADS, nr), in_specs=[col, col, col, col, intra, col],
                          out_specs=(col, st),
                          out_shape=(jax.ShapeDtypeStruct((B * S, H * LANE), F32),
                                     jax.ShapeDtypeStruct((B, H, S // CHUNK, LIN_HEAD_DIM, LIN_HEAD_DIM), _MXU_DTYPE)),
                          scratch_shapes=[pltpu.VMEM((SCAN_HEADS, LIN_HEAD_DIM, LIN_HEAD_DIM), F32)],
                          compiler_params=_cp(("parallel", "parallel", "arbitrary")),
                          name="gdn_scan_fwd")(u, w, qd, kd, a, gc)


def _gdn_scan_bwd(u, w, qd, kd, a, gc, states, do, B, S):
    H = N_LIN_HEADS
    R, nr, col, intra, st = _scan_specs(B, S, reverse=True)
    T = B * S
    n = R // CHUNK

    def body(u_ref, w_ref, qd_ref, kd_ref, a_ref, g_ref, st_ref, do_ref,
             du_ref, dw_ref, dqd_ref, dkd_ref, da_ref, dg_ref, carry_ref):
        last = lax.broadcasted_iota(jnp.int32, (CHUNK, LANE), 0) == CHUNK - 1

        @pl.when(pl.program_id(2) == 0)
        def _():
            carry_ref[...] = jnp.zeros_like(carry_ref)

        def step(i, d_states):
            c = n - 1 - i
            rows = pl.ds(pl.multiple_of(c * CHUNK, CHUNK), CHUNK)
            d_prevs = []
            for hh, d_state in enumerate(d_states):
                cols = slice(hh * LANE, (hh + 1) * LANE)
                _, vjp = jax.vjp(_scan_step, u_ref[rows, cols], w_ref[rows, cols], qd_ref[rows, cols], kd_ref[rows, cols],
                                 a_ref[hh, rows, :], g_ref[rows, cols][CHUNK - 1:CHUNK, :], st_ref[hh, c].astype(F32))
                du, dw, dqd, dkd, da, dgl, d_prev = vjp((do_ref[rows, cols].astype(F32), d_state))
                du_ref[rows, cols] = du
                dw_ref[rows, cols] = dw
                dqd_ref[rows, cols] = dqd
                dkd_ref[rows, cols] = dkd
                da_ref[hh, rows, :] = da
                dg_ref[rows, cols] = jnp.where(last, dgl, 0.0)
                d_prevs.append(d_prev)
            return tuple(d_prevs)

        d_states = lax.fori_loop(0, n, step, tuple(carry_ref[hh] for hh in range(SCAN_HEADS)))
        for hh, d_state in enumerate(d_states):
            carry_ref[hh] = d_state

    wide = jax.ShapeDtypeStruct((T, H * LANE), F32)
    return pl.pallas_call(body, grid=(B, H // SCAN_HEADS, nr), in_specs=[col, col, col, col, intra, col, st, col],
                          out_specs=(col, col, col, col, intra, col),
                          out_shape=(wide, wide, wide, wide, jax.ShapeDtypeStruct((H, T, CHUNK), F32), wide),
                          scratch_shapes=[pltpu.VMEM((SCAN_HEADS, LIN_HEAD_DIM, LIN_HEAD_DIM), F32)],
                          compiler_params=_cp(("parallel", "parallel", "arbitrary")),
                          name="gdn_scan_bwd")(u, w, qd, kd, a, gc, states, do)


GATE_COL = 3 * SEQ_MIX_WIDTH


def _post_fn(o, gate, gain):
    return o * lax.rsqrt(jnp.mean(o * o, axis=-1, keepdims=True) + EPS) * gain * _silu(gate)


def _gdn_post_fwd(o, proj, onorm, T):
    H = N_LIN_HEADS
    tm = _pick(T, 1024)

    def body(o_ref, g_ref, n_ref, y_ref):
        y_ref[...] = _post_fn(o_ref[...], g_ref[...], n_ref[...]).astype(y_ref.dtype)

    col = pl.BlockSpec((tm, LANE), lambda i, h: (i, h))
    return pl.pallas_call(body, grid=(T // tm, H),
                          in_specs=[col, pl.BlockSpec((tm, LANE), lambda i, h: (i, GATE_COL // LANE + h)),
                                    pl.BlockSpec((1, LANE), lambda i, h: (0, 0))],
                          out_specs=col, out_shape=jax.ShapeDtypeStruct((T, H * LANE), _MXU_DTYPE),
                          compiler_params=_cp(("parallel", "parallel")), name="gdn_post_fwd")(o, proj, onorm)


def _gdn_post_bwd(d_cat, o, proj, onorm, T):
    H = N_LIN_HEADS
    tm = _pick(T, 1024)

    def body(dy_ref, o_ref, g_ref, n_ref, do_ref, dg_ref, dn_ref):
        _, vjp = jax.vjp(_post_fn, o_ref[...], g_ref[...], n_ref[...])
        do, dg, dn = vjp(dy_ref[...].astype(F32))
        do_ref[...] = do
        dg_ref[...] = dg.astype(dg_ref.dtype)

        @pl.when((pl.program_id(0) == 0) & (pl.program_id(1) == 0))
        def _():
            dn_ref[...] = jnp.zeros_like(dn_ref)

        dn_ref[...] += dn

    col = pl.BlockSpec((tm, LANE), lambda i, h: (i, h))
    vec = pl.BlockSpec((1, LANE), lambda i, h: (0, 0))
    return pl.pallas_call(body, grid=(T // tm, H),
                          in_specs=[col, col, pl.BlockSpec((tm, LANE), lambda i, h: (i, GATE_COL // LANE + h)), vec],
                          out_specs=(col, col, vec),
                          out_shape=(jax.ShapeDtypeStruct((T, H * LANE), F32),
                                     jax.ShapeDtypeStruct((T, H * LANE), _MXU_DTYPE),
                                     jax.ShapeDtypeStruct((1, LANE), F32)),
                          compiler_params=_cp(("arbitrary", "arbitrary")), name="gdn_post_bwd")(d_cat, o, proj, onorm)


def _log_sigmoid(z):
    return jnp.minimum(z, 0.0) - jnp.log(1.0 + jnp.exp(-jnp.abs(z)))


def _split_dot(x, m):
    hi = x.astype(_MXU_DTYPE)
    lo = x - hi.astype(F32)
    return _dot(hi, m) + _dot(lo, m)


SB_QB = 256
SB_RC = 512


def _sb_scores(q2, k_j, scale, valid):
    z = _dot(q2, k_j, 1, 1) * scale
    lb = _log_sigmoid(z)
    return lb, jnp.where(valid, lb - z, 0.0)


def _sb_consts(QB):
    ri = lax.broadcasted_iota(jnp.int32, (SB_BLOCK, SB_BLOCK), 0)
    ci = lax.broadcasted_iota(jnp.int32, (SB_BLOCK, SB_BLOCK), 1)
    rc = min(SB_RC, 2 * QB)
    assert rc == 2 * QB or QB % rc == 0
    row = lax.broadcasted_iota(jnp.int32, (rc, SB_BLOCK), 0)
    col = lax.broadcasted_iota(jnp.int32, (rc, SB_BLOCK), 1)
    lane = lax.broadcasted_iota(jnp.int32, (1, LANE), 1)
    return {
        "rc": rc,
        "chunks": [(ch * rc, (ch * rc) % QB) for ch in range(2 * QB // rc)],
        "row_minus_col": row % QB - col,
        "after_excl": (ri > ci).astype(_MXU_DTYPE),
        "upto_incl": (ri <= ci).astype(_MXU_DTYPE),
        "upto_excl": (ri < ci).astype(_MXU_DTYPE),
        "lane": lane,
        "head0": lane < SB_HEAD_DIM,
    }


def _sb_stack(x, c):
    return jnp.concatenate([jnp.where(c["head0"], x, 0.0), jnp.where(c["head0"], 0.0, x)], axis=0)


def _sb_unstack(x2, c, QB):
    return jnp.where(c["head0"], x2[:QB], x2[QB:])


def _sb_fwd(proj, B, S):
    W = SEQ_MIX_WIDTH
    P = W // LANE
    QB = min(SB_QB, S)
    KB = SB_BLOCK
    scale = SB_HEAD_DIM ** -0.5

    def body(q_ref, k_ref, v_ref, o_ref, tot_ref):
        c = _sb_consts(QB)
        rc = c["rc"]

        def q_loop(i, carry):
            qrows = pl.ds(pl.multiple_of(i * QB, QB), QB)
            q2 = _sb_stack(q_ref[qrows, :].astype(F32), c)
            nkb = (i + 1) * (QB // KB)

            def scores(j):
                krows = pl.ds(pl.multiple_of(j * KB, KB), KB)
                valid = c["row_minus_col"] > j * KB - i * QB
                lb, l1 = _sb_scores(q2, k_ref[krows, :], scale, valid)
                return jnp.where(valid, lb, -1e30), l1

            def k_loop(t, st):
                acc, r, lbm, l1 = st
                j = nkb - 1 - t
                nxt = scores(jnp.maximum(j - 1, 0))
                krows = pl.ds(pl.multiple_of(j * KB, KB), KB)
                a = jnp.exp(lbm + r + _split_dot(l1, c["after_excl"]))
                return (acc + _dot(a, v_ref[krows, :]), r + jnp.sum(l1, axis=-1, keepdims=True)) + nxt

            acc, r, _, _ = lax.fori_loop(0, nkb, k_loop, (jnp.zeros((2 * QB, LANE), F32), jnp.zeros((2 * QB, 1), F32))
                                         + scores(nkb - 1))
            o_ref[qrows, :] = _sb_unstack(acc, c, QB)
            tot_ref[qrows, :] = _sb_unstack(jnp.broadcast_to(r, (2 * QB, LANE)), c, QB)
            return carry

        lax.fori_loop(0, S // QB, q_loop, 0)

    def col(off):
        return pl.BlockSpec((S, LANE), lambda b, p: (b, off + p))

    out = jax.ShapeDtypeStruct((B * S, W), F32)
    return pl.pallas_call(body, grid=(B, P), in_specs=[col(0), col(P), col(2 * P)], out_specs=(col(0), col(0)),
                          out_shape=(out, out),
                          compiler_params=_cp(("parallel", "parallel")), name="sb_fwd")(proj, proj, proj)


def _sb_bwd(proj, tot, d_cat, B, S):
    W = SEQ_MIX_WIDTH
    P = W // LANE
    QB = min(SB_QB, S)
    KB = SB_BLOCK
    scale = SB_HEAD_DIM ** -0.5

    def body(q_ref, k_ref, v_ref, tot_ref, do_ref, dq_ref, dk_ref, dv_ref, dk_acc, dv_acc):
        c = _sb_consts(QB)
        rc = c["rc"]
        dk_acc[...] = jnp.zeros_like(dk_acc)
        dv_acc[...] = jnp.zeros_like(dv_acc)

        def q_loop(i, carry):
            qrows = pl.ds(pl.multiple_of(i * QB, QB), QB)
            q2 = _sb_stack(q_ref[qrows, :].astype(F32), c)
            do2 = _sb_stack(do_ref[qrows, :].astype(F32), c)
            q2_t = q2.T.astype(_MXU_DTYPE)
            do2_t = do2.T.astype(_MXU_DTYPE)
            tot = tot_ref[qrows, :]
            total = jnp.concatenate(
                [jnp.sum(jnp.where(c["lane"] == h * SB_HEAD_DIM, tot, 0.0), axis=-1, keepdims=True) for h in range(2)],
                axis=0)

            nkb = (i + 1) * (QB // KB)

            def scores(j):
                krows = pl.ds(pl.multiple_of(j * KB, KB), KB)
                valid = c["row_minus_col"] > j * KB - i * QB
                lb, l1 = _sb_scores(q2, k_ref[krows, :], scale, valid)
                return jnp.where(valid, lb, -1e30), l1, _dot(do2, v_ref[krows, :], 1, 1)

            def k_loop(j, st):
                dq_acc, p_l1, p_g, lbm, l1, da = st
                nxt = scores(jnp.minimum(j + 1, nkb - 1))
                krows = pl.ds(pl.multiple_of(j * KB, KB), KB)
                tail = total - p_l1 - _split_dot(l1, c["upto_incl"])
                a = jnp.exp(lbm + tail)
                g = da * a
                g_before = p_g + _split_dot(g, c["upto_excl"])
                sig = jnp.exp(lbm)
                dz = ((g * (1.0 - sig) - g_before * sig) * scale).astype(_MXU_DTYPE)
                dk_acc[j] += _dot(q2_t, dz)
                dv_acc[j] += _dot(do2_t, a)
                return (dq_acc + _dot(dz, k_ref[krows, :]), p_l1 + jnp.sum(l1, axis=-1, keepdims=True),
                        p_g + jnp.sum(g, axis=-1, keepdims=True)) + nxt

            zero_col = jnp.zeros((2 * QB, 1), F32)
            dq2 = lax.fori_loop(0, nkb, k_loop, (jnp.zeros((2 * QB, LANE), F32), zero_col, zero_col) + scores(0))[0]
            dq_ref[qrows, :] = _sb_unstack(dq2, c, QB).astype(dq_ref.dtype)
            return carry

        lax.fori_loop(0, S // QB, q_loop, 0)
        for j in range(S // KB):
            dk_ref[j * KB:(j + 1) * KB, :] = dk_acc[j].T.astype(dk_ref.dtype)
            dv_ref[j * KB:(j + 1) * KB, :] = dv_acc[j].T.astype(dv_ref.dtype)

    def col(off):
        return pl.BlockSpec((S, LANE), lambda b, p: (b, off + p))

    out = jax.ShapeDtypeStruct((B * S, W), _MXU_DTYPE)
    return pl.pallas_call(body, grid=(B, P), in_specs=[col(0), col(P), col(2 * P), col(0), col(0)],
                          out_specs=(col(0),) * 3, out_shape=(out,) * 3,
                          scratch_shapes=[pltpu.VMEM((S // KB, LANE, KB), F32), pltpu.VMEM((S // KB, LANE, KB), F32)],
                          compiler_params=_cp(("parallel", "parallel")), name="sb_bwd")(proj, proj, proj, tot, d_cat)


def _mem_fn(q, k, v):
    lane = lax.broadcasted_iota(jnp.int32, (1, X_WIDTH), 1)
    out = jnp.zeros(q.shape, F32)
    for h in range(N_X_HEADS):
        hm = (lane // X_HEAD_DIM) == h
        s = _dot(jnp.where(hm, q, 0.0), k, 1, 1) * (X_HEAD_DIM ** -0.5)
        e = jnp.exp(s - lax.stop_gradient(jnp.max(s, axis=-1, keepdims=True)))
        p = e / jnp.sum(e, axis=-1, keepdims=True)
        out = out + jnp.where(hm, _dot(p, v), 0.0)
    return out


def _mem_specs(S, q_col):
    ts = _pick(S, 1024)
    ns = S // ts
    qs = pl.BlockSpec((ts, X_WIDTH), lambda b, i: (b * ns + i, q_col // X_WIDTH))
    ks = pl.BlockSpec((N_MEM, X_WIDTH), lambda b, i: (b, 0))
    vs = pl.BlockSpec((N_MEM, X_WIDTH), lambda b, i: (b, 1))
    os = pl.BlockSpec((ts, X_WIDTH), lambda b, i: (b * ns + i, 0))
    return ts, ns, qs, ks, vs, os


def _mem_fwd(proj, q_col, mem_kv, B, S, name):
    ts, ns, qs, ks, vs, os = _mem_specs(S, q_col)

    def body(q_ref, k_ref, v_ref, o_ref):
        o_ref[...] = _mem_fn(q_ref[...].astype(F32), k_ref[...].astype(F32), v_ref[...].astype(F32))

    return pl.pallas_call(body, grid=(B, ns), in_specs=[qs, ks, vs], out_specs=os,
                          out_shape=jax.ShapeDtypeStruct((B * S, X_WIDTH), F32),
                          compiler_params=_cp(("parallel", "parallel")), name=name)(proj, mem_kv, mem_kv)


def _mem_bwd(proj, q_col, mem_kv, d_cat, B, S, name):
    ts, ns, qs, ks, vs, os = _mem_specs(S, q_col)

    def body(q_ref, k_ref, v_ref, do_ref, dq_ref, dk_ref, dv_ref):
        _, vjp = jax.vjp(_mem_fn, q_ref[...].astype(F32), k_ref[...].astype(F32), v_ref[...].astype(F32))
        dq, dk, dv = vjp(do_ref[...].astype(F32))
        dq_ref[...] = dq.astype(dq_ref.dtype)

        @pl.when(pl.program_id(1) == 0)
        def _():
            dk_ref[...] = jnp.zeros_like(dk_ref)
            dv_ref[...] = jnp.zeros_like(dv_ref)

        dk_ref[...] += dk
        dv_ref[...] += dv

    dos = pl.BlockSpec((ts, X_WIDTH), lambda b, i: (b * ns + i, SEQ_MIX_WIDTH // X_WIDTH))
    dq, dk, dv = pl.pallas_call(
        body, grid=(B, ns), in_specs=[qs, ks, vs, dos],
        out_specs=(os, pl.BlockSpec((N_MEM, X_WIDTH), lambda b, i: (b, 0)), pl.BlockSpec((N_MEM, X_WIDTH), lambda b, i: (b, 0))),
        out_shape=(jax.ShapeDtypeStruct((B * S, X_WIDTH), _MXU_DTYPE),
                   jax.ShapeDtypeStruct((B * N_MEM, X_WIDTH), F32), jax.ShapeDtypeStruct((B * N_MEM, X_WIDTH), F32)),
        compiler_params=_cp(("parallel", "arbitrary")), name=name)(proj, mem_kv, mem_kv, d_cat)
    return dq, dk, dv


def _peers():
    x, y, c = lax.axis_index("x"), lax.axis_index("y"), lax.axis_index("c")
    me = 4 * x + 2 * y + c
    out = []
    for fx, fy, fc in [(0, 0, 1), (1, 0, 0), (0, 1, 0), (1, 1, 0), (1, 0, 1), (0, 1, 1), (1, 1, 1)]:
        px, py, pc = x ^ fx, y ^ fy, c ^ fc
        out.append(((px, py, pc), 4 * px + 2 * py + pc))
    return me, out


ANY = pl.BlockSpec(memory_space=pl.ANY)


def _remote(src, dst, send_sems, recv_sems, k, dev):
    return pltpu.make_async_remote_copy(src_ref=src, dst_ref=dst, send_sem=send_sems.at[k], recv_sem=recv_sems.at[k],
                                        device_id=dev, device_id_type=pl.DeviceIdType.MESH)


def _place():
    x, y, c = lax.axis_index("x"), lax.axis_index("y"), lax.axis_index("c")
    return x, y, c, [(1 - x, y), (x, 1 - y), (1 - x, 1 - y)]


def _all_gather(shard):
    R = shard.shape[0]

    def body(x_ref, o_ref, send_sems, recv_sems, local_sem):
        x, y, c, chips = _place()
        me, sibling = (x, y, c), (x, y, 1 - c)

        def slot(px, py, pc):
            return o_ref.at[4 * px + 2 * py + pc]

        def copy(k, block, to, src=None):
            return _remote(slot(*block) if src is None else src, slot(*block), send_sems, recv_sems, k, to)

        mine = pltpu.make_async_copy(x_ref, slot(*me), local_sem)
        mine.start()
        first = [copy(0, me, sibling, src=x_ref)] + [copy(1 + j, me, (*chip, c), src=x_ref) for j, chip in enumerate(chips)]
        for cp in first:
            cp.start()
        passed = [copy(4 + j, (*chip, c), sibling) for j, chip in enumerate(chips)]
        for j, chip in enumerate(chips):
            copy(1 + j, (*chip, c), me).wait_recv()
            passed[j].start()
        copy(0, sibling, me).wait_recv()
        for j, chip in enumerate(chips):
            copy(4 + j, (*chip, 1 - c), me).wait_recv()
        for cp in first + passed:
            cp.wait_send()
        mine.wait()

    return pl.pallas_call(body, in_specs=[ANY], out_specs=ANY,
                          out_shape=jax.ShapeDtypeStruct((N_DEV, R, LANE), shard.dtype),
                          scratch_shapes=[pltpu.SemaphoreType.DMA((7,)), pltpu.SemaphoreType.DMA((7,)),
                                          pltpu.SemaphoreType.DMA],
                          compiler_params=pltpu.CompilerParams(has_side_effects=True),
                          name="all_gather_weights")(shard)


N_CHIP = 4


def _exchange_sibling(big):
    R = big.shape[1]

    def body(b_ref, o_ref, send_sems, recv_sems):
        x, y, c, _ = _place()
        copies = [_remote(b_ref.at[2 * k + (1 - c)], o_ref.at[k], send_sems, recv_sems, k, (x, y, 1 - c))
                  for k in range(N_CHIP)]
        for cp in copies:
            cp.start()
        for cp in copies:
            cp.wait()

    return pl.pallas_call(body, in_specs=[ANY], out_specs=ANY,
                          out_shape=jax.ShapeDtypeStruct((N_CHIP, R, LANE), big.dtype),
                          scratch_shapes=[pltpu.SemaphoreType.DMA((N_CHIP,)), pltpu.SemaphoreType.DMA((N_CHIP,))],
                          compiler_params=pltpu.CompilerParams(has_side_effects=True),
                          name="exchange_sibling")(big)


def _partial_sum(g4, recv):
    R = g4.shape[2]
    tr = _pick(R, 1024)

    def body(g_ref, r_ref, pw_ref, po_ref):
        x, y, c, _ = _place()
        g = jnp.where(c == 0, g_ref[0], g_ref[1]).astype(F32) + r_ref[...].astype(F32)
        pw_ref[...] = g.astype(pw_ref.dtype)

        @pl.when(pl.program_id(1) == 2 * x + y)
        def _():
            po_ref[...] = g

    return pl.pallas_call(body, grid=(R // tr, N_CHIP),
                          in_specs=[pl.BlockSpec((None, 2, tr, LANE), lambda i, k: (k, 0, i, 0)),
                                    pl.BlockSpec((None, tr, LANE), lambda i, k: (k, i, 0))],
                          out_specs=(pl.BlockSpec((None, tr, LANE), lambda i, k: (k, i, 0)),
                                     pl.BlockSpec((tr, LANE), lambda i, k: (i, 0))),
                          out_shape=(jax.ShapeDtypeStruct((N_CHIP, R, LANE), recv.dtype),
                                     jax.ShapeDtypeStruct((R, LANE), F32)),
                          compiler_params=_cp(("parallel", "arbitrary")), name="partial_sum")(g4, recv)


def _exchange_chips(part, small):
    R = part.shape[1]
    K = small.shape[0]

    def body(p_ref, s_ref, ob_ref, os_ref, send_sems, recv_sems, local_sems):
        x, y, c, chips = _place()
        my_chip = 2 * x + y
        me, peers = _peers()
        own_b = pltpu.make_async_copy(p_ref.at[my_chip], ob_ref.at[my_chip], local_sems.at[0])
        own_s = pltpu.make_async_copy(s_ref, os_ref.at[me], local_sems.at[1])
        own_b.start()
        own_s.start()
        copies = [_remote(p_ref.at[2 * px + py], ob_ref.at[my_chip], send_sems, recv_sems, j, (px, py, c))
                  for j, (px, py) in enumerate(chips)]
        copies += [_remote(s_ref, os_ref.at[me], send_sems, recv_sems, 3 + k, dev) for k, (dev, _) in enumerate(peers)]
        for cp in copies:
            cp.start()
        for j, (px, py) in enumerate(chips):
            _remote(p_ref.at[my_chip], ob_ref.at[2 * px + py], send_sems, recv_sems, j, (px, py, c)).wait_recv()
        for k, (dev, idx) in enumerate(peers):
            _remote(s_ref, os_ref.at[idx], send_sems, recv_sems, 3 + k, dev).wait_recv()
        for cp in copies:
            cp.wait_send()
        own_b.wait()
        own_s.wait()

    return pl.pallas_call(body, in_specs=[ANY, ANY], out_specs=(ANY, ANY),
                          out_shape=(jax.ShapeDtypeStruct((N_CHIP, R, LANE), part.dtype),
                                     jax.ShapeDtypeStruct((N_DEV, K, LANE), small.dtype)),
                          scratch_shapes=[pltpu.SemaphoreType.DMA((10,)), pltpu.SemaphoreType.DMA((10,)),
                                          pltpu.SemaphoreType.DMA((2,))],
                          compiler_params=pltpu.CompilerParams(has_side_effects=True),
                          name="exchange_chips")(part, small)


def _adamw_math(w, g, m, v):
    m = ADAM_B1 * m + (1.0 - ADAM_B1) * g
    v = ADAM_B2 * v + (1.0 - ADAM_B2) * (g * g)
    m_hat = m / (1.0 - ADAM_B1 ** ADAM_STEP)
    v_hat = v / (1.0 - ADAM_B2 ** ADAM_STEP)
    delta = -ADAM_LR * (m_hat / (jnp.sqrt(v_hat) + ADAM_EPS) + ADAM_WD * w)
    return delta, m, v


def _adamw_shard(own, recv, w, m, v):
    R = own.shape[0]
    tr = _pick(R, 1024)

    def body(own_ref, recv_ref, w_ref, m_ref, v_ref, g_ref, d_ref, nm_ref, nv_ref):
        x, y, _, _ = _place()
        g = own_ref[...]
        for k in range(N_CHIP):
            g = g + jnp.where(k == 2 * x + y, 0.0, recv_ref[k].astype(F32))
        delta, nm, nv = _adamw_math(w_ref[...], g, m_ref[...], v_ref[...])
        g_ref[...] = g
        d_ref[...] = delta
        nm_ref[...] = nm
        nv_ref[...] = nv

    row = pl.BlockSpec((tr, LANE), lambda i: (i, 0))
    out = jax.ShapeDtypeStruct((R, LANE), F32)
    return pl.pallas_call(body, grid=(R // tr,),
                          in_specs=[row, pl.BlockSpec((N_CHIP, tr, LANE), lambda i: (0, i, 0)), row, row, row],
                          out_specs=(row,) * 4, out_shape=(out,) * 4,
                          compiler_params=_cp(("parallel",)), name="adamw_shard")(own, recv, w, m, v)


def _adamw_replicated(parts, w, m, v):
    K = w.shape[0]

    def body(p_ref, w_ref, m_ref, v_ref, g_ref, d_ref, nm_ref, nv_ref):
        g = p_ref[0]
        for p in range(1, N_DEV):
            g = g + p_ref[p]
        delta, nm, nv = _adamw_math(w_ref[...], g, m_ref[...], v_ref[...])
        g_ref[...] = g
        d_ref[...] = delta
        nm_ref[...] = nm
        nv_ref[...] = nv

    out = jax.ShapeDtypeStruct((K, LANE), F32)
    return pl.pallas_call(body, out_shape=(out,) * 4, compiler_params=_cp(), name="adamw_replicated")(parts, w, m, v)


_SHARDED = (("w_in_a", 1), ("conv_w_a", 2), ("w_in_b", 2), ("w_mem_kv", 1), ("w_out", 1), ("w_up", 2), ("w_down", 1))
_REPLICATED = ("mem_norm", "norm_pre_mix", "norm_post_mix", "norm_pre_mlp", "norm_post_mlp", "a_log_a", "dt_bias_a", "onorm_a")
ROW_ALIGN = 16


def _rows(n_elems, align=ROW_ALIGN):
    r = -(-n_elems // LANE)
    return -(-r // align) * align


def _pack(arrays, align=ROW_ALIGN, total=None, lead=()):
    parts = []
    for a in arrays:
        n = math.prod(a.shape[len(lead):])
        flat = a.reshape(lead + (n,))
        r = _rows(n, align)
        flat = jnp.pad(flat, [(0, 0)] * len(lead) + [(0, r * LANE - n)])
        parts.append(flat.reshape(lead + (r, LANE)))
    used = sum(part.shape[len(lead)] for part in parts)
    if total is not None and used < total:
        parts.append(jnp.zeros(lead + (total - used, LANE), parts[0].dtype))
    return jnp.concatenate(parts, axis=len(lead))


def _unpack(flat, shapes, align=ROW_ALIGN, lead=()):
    outs = []
    r0 = 0
    for shp in shapes:
        n = math.prod(shp)
        r = _rows(n, align)
        part = lax.slice_in_dim(flat, r0, r0 + r, axis=len(lead))
        part = part.reshape(lead + (r * LANE,))
        part = lax.slice_in_dim(part, 0, n, axis=len(lead))
        outs.append(part.reshape(lead + tuple(shp)))
        r0 += r
    return outs


def _to_full(gathered, axis):
    g = jnp.moveaxis(gathered, 0, axis)
    shp = g.shape
    return g.reshape(shp[:axis] + (shp[axis] * shp[axis + 1],) + shp[axis + 2:])


def _to_blocks(full, axis):
    shp = full.shape
    g = full.reshape(shp[:axis] + (N_DEV, shp[axis] // N_DEV) + shp[axis + 1:])
    return jnp.moveaxis(g, axis, 0)


def _widen_in_a(w):
    main = w[:, :4 * SEQ_MIX_WIDTH]
    small = w[:, 4 * SEQ_MIX_WIDTH:4 * SEQ_MIX_WIDTH + 2 * N_LIN_HEADS]
    memq = w[:, 4 * SEQ_MIX_WIDTH + 2 * N_LIN_HEADS:]
    pad = jnp.zeros((w.shape[0], IN_A_PAD - IN_A), w.dtype)
    return jnp.concatenate([main, memq, small, pad], axis=1)


def _narrow_in_a(g):
    main = g[:, :4 * SEQ_MIX_WIDTH]
    memq = g[:, 4 * SEQ_MIX_WIDTH:4 * SEQ_MIX_WIDTH + X_WIDTH]
    small = g[:, SM_COL:SM_COL + 2 * N_LIN_HEADS]
    return jnp.concatenate([main, small, memq], axis=1)


def _row128(v):
    return jnp.pad(v.reshape(1, -1), ((0, 0), (0, LANE - v.shape[-1])))


def _local_step(x, mem, target, p):
    B, S, D = x.shape
    T = B * S
    md = _MXU_DTYPE
    x0 = x.reshape(T, D)
    tgt = target.reshape(T, D)
    memf = mem.reshape(B * N_MEM, D)
    vec = lambda a: a.reshape(1, -1)

    mem_n = _norm_fwd(memf, vec(p["mem_norm"]), out_dtype=md, name="norm_mem")
    w_in = [_widen_in_a(p["w_in_a"][0]), p["w_in_b"][0]]
    memq_col = [4 * SEQ_MIX_WIDTH, 3 * SEQ_MIX_WIDTH]
    alog = _row128(p["a_log_a"][0])
    dtb = _row128(p["dt_bias_a"][0])
    onorm = vec(p["onorm_a"][0])
    conv_w = p["conv_w_a"][0]
    saved = []
    xi = x0
    for i in range(2):
        s = {"x_in": xi}
        h1 = _norm_fwd(xi, vec(p["norm_pre_mix"][i]), out_dtype=md, name=f"norm_pre_mix{i}")
        proj = _mm(h1, w_in[i], out_dtypes=(F32 if i == 0 else md,), name=f"in_proj{i}")
        mem_kv = _mm(mem_n, p["w_mem_kv"][i], out_dtypes=(md,), name=f"mem_kv{i}")
        if i == 0:
            act = _gdn_conv_fwd(proj, conv_w, B, S)
            beta, gc = _gdn_gates_fwd(proj, alog, dtb, B, S)
            u, w, qd, kd, intra = _gdn_prep_fwd(act, beta, gc, B, S)
            o, states = _gdn_scan_fwd(u, w, qd, kd, intra, gc, B, S)
            mix = _gdn_post_fwd(o, proj, onorm, T)
            s.update(act=act, beta=beta, gc=gc, u=u, w=w, qd=qd, kd=kd, intra=intra, o=o, states=states)
        else:
            mix, tot = _sb_fwd(proj, B, S)
            s.update(tot=tot)
        cross = _mem_fwd(proj, memq_col[i], mem_kv, B, S, name=f"mem_fwd{i}")
        cat = jnp.concatenate([mix.astype(md), cross.astype(md)], axis=1)
        y = _mm(cat, p["w_out"][i], name=f"out_proj{i}")
        x_mid = _norm_fwd(y, vec(p["norm_post_mix"][i]), resid=xi, name=f"norm_post_mix{i}")
        h2 = _norm_fwd(x_mid, vec(p["norm_pre_mlp"][i]), out_dtype=md, name=f"norm_pre_mlp{i}")
        a_act, r = _mm(h2, p["w_up"][i], out_dtypes=(md, md), epilogue=_relu2_epilogue, name=f"up_proj{i}")
        y2 = _mm(a_act, p["w_down"][i], name=f"down_proj{i}")
        x_out = _norm_fwd(y2, vec(p["norm_post_mlp"][i]), resid=x_mid, name=f"norm_post_mlp{i}")
        s.update(h1=h1, proj=proj, mem_kv=mem_kv, cat=cat, y=y, x_mid=x_mid, h2=h2, a_act=a_act, r=r, y2=y2)
        saved.append(s)
        xi = x_out

    loss_row, dx = _loss_head(xi, tgt)

    g = {}
    d_mem_n = None
    gn = {k: [None, None] for k in ("norm_pre_mix", "norm_post_mix", "norm_pre_mlp", "norm_post_mlp")}
    g_w_mem_kv, g_w_out, g_w_up, g_w_down = [None, None], [None, None], [None, None], [None, None]
    for i in (1, 0):
        s = saved[i]
        d_y2, gn["norm_post_mlp"][i] = _norm_bwd(dx, s["y2"], vec(p["norm_post_mlp"][i]), name=f"norm_post_mlp_bwd{i}")
        g_w_down[i] = _mm(s["a_act"], d_y2, ta=True, out_dtypes=(_WIRE_DTYPE,), name=f"down_proj_dw{i}")
        d_u = _mm(d_y2, p["w_down"][i], tb=True, out_dtypes=(md,), epilogue=_drelu2_epilogue, extras=(s["r"],),
                  name=f"down_proj_dx{i}")
        g_w_up[i] = _mm(s["h2"], d_u, ta=True, out_dtypes=(_WIRE_DTYPE,), name=f"up_proj_dw{i}")
        d_h2 = _mm(d_u, p["w_up"][i], tb=True, name=f"up_proj_dx{i}")
        dx, gn["norm_pre_mlp"][i] = _norm_bwd(d_h2, s["x_mid"], vec(p["norm_pre_mlp"][i]), resid=dx,
                                              name=f"norm_pre_mlp_bwd{i}")
        d_y, gn["norm_post_mix"][i] = _norm_bwd(dx, s["y"], vec(p["norm_post_mix"][i]), name=f"norm_post_mix_bwd{i}")
        g_w_out[i] = _mm(s["cat"], d_y, ta=True, out_dtypes=(_WIRE_DTYPE,), name=f"out_proj_dw{i}")
        d_cat = _mm(d_y, p["w_out"][i], tb=True, name=f"out_proj_dx{i}")
        d_memq, d_mk, d_mv = _mem_bwd(s["proj"], memq_col[i], s["mem_kv"], d_cat, B, S, name=f"mem_bwd{i}")
        d_mem_kv = jnp.concatenate([d_mk.astype(md), d_mv.astype(md)], axis=1)
        g_w_mem_kv[i] = _mm(mem_n, d_mem_kv, ta=True, out_dtypes=(_WIRE_DTYPE,), name=f"mem_kv_dw{i}")
        d_mn = _mm(d_mem_kv, p["w_mem_kv"][i], tb=True, name=f"mem_kv_dx{i}")
        d_mem_n = d_mn if d_mem_n is None else d_mem_n + d_mn
        if i == 0:
            d_o, d_gate, g["onorm_a"] = _gdn_post_bwd(d_cat, s["o"], s["proj"], onorm, T)
            du, dw, dqd, dkd, da, dgc_s = _gdn_scan_bwd(s["u"], s["w"], s["qd"], s["kd"], s["intra"], s["gc"],
                                                         s["states"], d_o, B, S)
            dq, dk, dv, d_beta, d_gc = _gdn_prep_bwd(s["act"], s["beta"], s["gc"], du, dw, dqd, dkd, da, B, S)
            d_qkv, g["conv_w_a"] = _gdn_conv_bwd(dq, dk, dv, s["proj"], conv_w, B, S)
            d_sm, g["a_log_a"], g["dt_bias_a"] = _gdn_gates_bwd(d_beta, d_gc + dgc_s, s["proj"], alog, dtb, B, S)
            pad = jnp.zeros((T, IN_A_PAD - SM_COL - LANE), md)
            d_proj = jnp.concatenate([d_qkv, d_gate, d_memq, d_sm, pad], axis=1)
        else:
            dq, dk, dv = _sb_bwd(s["proj"], s["tot"], d_cat, B, S)
            d_proj = jnp.concatenate([dq, dk, dv, d_memq], axis=1)
        g_w_in = _mm(s["h1"], d_proj, ta=True, out_dtypes=(_WIRE_DTYPE,), name=f"in_proj_dw{i}")
        d_h1 = _mm(d_proj, w_in[i], tb=True, name=f"in_proj_dx{i}")
        dx, gn["norm_pre_mix"][i] = _norm_bwd(d_h1, s["x_in"], vec(p["norm_pre_mix"][i]), resid=dx,
                                              name=f"norm_pre_mix_bwd{i}")
        if i == 0:
            g["w_in_a"] = _narrow_in_a(g_w_in)[None]
        else:
            g["w_in_b"] = g_w_in[None]
    _, g_mem_norm = _norm_bwd(d_mem_n, memf, vec(p["mem_norm"]), name="norm_mem_bwd")
    g["mem_norm"] = g_mem_norm.reshape(-1)
    for k, v in gn.items():
        g[k] = jnp.concatenate(v, axis=0)
    g["w_mem_kv"] = jnp.stack(g_w_mem_kv)
    g["w_out"] = jnp.stack(g_w_out)
    g["w_up"] = jnp.stack(g_w_up)
    g["w_down"] = jnp.stack(g_w_down)
    g["conv_w_a"] = g["conv_w_a"][None]
    g["a_log_a"] = g["a_log_a"][:, :N_LIN_HEADS]
    g["dt_bias_a"] = g["dt_bias_a"][:, :N_LIN_HEADS]
    return loss_row, dx.reshape(B, S, D), g


def kernel(x, mem, mem_norm, norm_pre_mix, norm_post_mix, norm_pre_mlp, norm_post_mlp, w_in_a, conv_w_a, a_log_a, dt_bias_a, onorm_a, w_in_b, w_mem_kv, w_out, w_up, w_down, loss_target, m_mem_norm, m_norm_pre_mix, m_norm_post_mix, m_norm_pre_mlp, m_norm_post_mlp, m_w_in_a, m_conv_w_a, m_a_log_a, m_dt_bias_a, m_onorm_a, m_w_in_b, m_w_mem_kv, m_w_out, m_w_up, m_w_down, v_mem_norm, v_norm_pre_mix, v_norm_post_mix, v_norm_pre_mlp, v_norm_post_mlp, v_w_in_a, v_conv_w_a, v_a_log_a, v_dt_bias_a, v_onorm_a, v_w_in_b, v_w_mem_kv, v_w_out, v_w_up, v_w_down):
    args = dict(locals())
    big_names = [n for n, _ in _SHARDED]
    axes = dict(_SHARDED)
    shard_shapes = [args[n].shape for n in big_names]
    rows_total = -(-sum(_rows(math.prod(s)) for s in shard_shapes) // 1024) * 1024

    exact = ("conv_w_a",) if _WIRE_DTYPE != F32 else ()
    wire = [lax.bitcast_convert_type(args[n], _WIRE_DTYPE) if n in exact else args[n].astype(_WIRE_DTYPE)
            for n in big_names]
    wire_shapes = [a.shape for a in wire]
    wire_rows = -(-sum(_rows(math.prod(s)) for s in wire_shapes) // ROW_ALIGN) * ROW_ALIGN
    gathered = _all_gather(_pack(wire, total=wire_rows))
    p = {}
    for n, b in zip(big_names, _unpack(gathered, wire_shapes, lead=(N_DEV,))):
        b = lax.bitcast_convert_type(b, F32) if n in exact else b.astype(_MXU_DTYPE)
        p[n] = _to_full(b, axes[n])
    for n in _REPLICATED:
        p[n] = args[n]
    w_flat = _pack([args[n] for n in big_names], total=rows_total)

    loss_row, grad_x, g = _local_step(x, mem, loss_target, p)
    loss = lax.psum(loss_row[0, 0], ("x", "y", "c"))

    g_blocks = _pack([_to_blocks(g[n], axes[n]).astype(_WIRE_DTYPE) for n in big_names], total=rows_total, lead=(N_DEV,))
    rep_shapes = [args[n].shape for n in _REPLICATED]
    g_small = _pack([g[n] for n in _REPLICATED], align=8)
    recv_sib = _exchange_sibling(g_blocks)
    part, own = _partial_sum(g_blocks.reshape(N_CHIP, 2, rows_total, LANE), recv_sib)
    recv_big, recv_small = _exchange_chips(part, g_small)

    m_flat = _pack([args["m_" + n] for n in big_names], total=rows_total)
    v_flat = _pack([args["v_" + n] for n in big_names], total=rows_total)
    outs_big = [_unpack(f, shard_shapes) for f in _adamw_shard(own, recv_big, w_flat, m_flat, v_flat)]
    outs_small = [_unpack(f, rep_shapes, align=8) for f in _adamw_replicated(
        recv_small, _pack([args[n] for n in _REPLICATED], align=8),
        _pack([args["m_" + n] for n in _REPLICATED], align=8), _pack([args["v_" + n] for n in _REPLICATED], align=8))]

    order = ["mem_norm", "norm_pre_mix", "norm_post_mix", "norm_pre_mlp", "norm_post_mlp", "w_in_a", "conv_w_a",
             "a_log_a", "dt_bias_a", "onorm_a", "w_in_b", "w_mem_kv", "w_out", "w_up", "w_down"]
    result = [loss, grad_x]
    for kind in range(4):
        for n in order:
            if n in axes:
                result.append(outs_big[kind][big_names.index(n)])
            else:
                result.append(outs_small[kind][_REPLICATED.index(n)])
    return tuple(result)
```

```python
import functools
import math

import jax
import jax.numpy as jnp
from jax import lax
from jax.experimental import pallas as pl
from jax.experimental.pallas import tpu as pltpu

F32 = jnp.float32
_MXU_DTYPE = jnp.bfloat16
_WIRE_DTYPE = jnp.bfloat16
_HI = lax.Precision.HIGH

D_MODEL = 1024
N_DEV = 8
N_MEM = 256
X_WIDTH = 256
N_X_HEADS = 4
X_HEAD_DIM = 64
SEQ_MIX_WIDTH = 768
LIN_HEAD_DIM = 128
N_LIN_HEADS = 6
CONV_WIDTH = 4
CHUNK = 64
SB_HEAD_DIM = 64
SB_BLOCK = 128
D_FF = 4096
EPS = 1e-6
IN_A = 3340
IN_A_PAD = 3584
IN_B = 2560
SM_COL = 3328

ADAM_LR = 0.001
ADAM_B1 = 0.9
ADAM_B2 = 0.999
ADAM_EPS = 1e-08
ADAM_WD = 0.01
ADAM_STEP = 10

LANE = 128
VMEM_LIMIT = 56 * 1024 * 1024


def _cp(sem=None):
    return pltpu.CompilerParams(dimension_semantics=sem, vmem_limit_bytes=VMEM_LIMIT)


def _pick(n, target):
    if n <= target:
        return n
    best = None
    for t in range(LANE, target + 1, LANE):
        if n % t == 0:
            best = t
    assert best is not None, (n, target)
    return best


def _dot(a, b, ca=1, cb=0):
    return lax.dot_general(a.astype(_MXU_DTYPE), b.astype(_MXU_DTYPE), (((ca,), (cb,)), ((), ())),
                           preferred_element_type=F32)


def _bdot(a, b, ca, cb):
    return lax.dot_general(a.astype(_MXU_DTYPE), b.astype(_MXU_DTYPE), (((ca,), (cb,)), ((0,), (0,))),
                           preferred_element_type=F32)


def _bdot_hi(a, b):
    return lax.dot_general(a, b, (((2,), (1,)), ((0,), (0,))), precision=_HI, preferred_element_type=F32)


def _sigmoid(x):
    return 1.0 / (1.0 + jnp.exp(-x))


def _silu(x):
    return x * _sigmoid(x)


def _softplus(x):
    return jnp.maximum(x, 0.0) + jnp.log(1.0 + jnp.exp(-jnp.abs(x)))


def _rms(x, g):
    return x * lax.rsqrt(jnp.mean(x * x, axis=-1, keepdims=True) + EPS) * g


def _norm_fwd(x, g, resid=None, out_dtype=F32, name="norm_fwd"):
    T, D = x.shape
    tm = _pick(T, 512)
    has_resid = resid is not None

    def body(*refs):
        if has_resid:
            x_ref, g_ref, r_ref, o_ref = refs
        else:
            x_ref, g_ref, o_ref = refs
        y = _rms(x_ref[...].astype(F32), g_ref[...])
        if has_resid:
            y = r_ref[...] + y
        o_ref[...] = y.astype(out_dtype)

    row = pl.BlockSpec((tm, D), lambda i: (i, 0))
    in_specs = [row, pl.BlockSpec((1, D), lambda i: (0, 0))] + ([row] if has_resid else [])
    args = (x, g) + ((resid,) if has_resid else ())
    return pl.pallas_call(body, grid=(T // tm,), in_specs=in_specs, out_specs=row,
                          out_shape=jax.ShapeDtypeStruct((T, D), out_dtype),
                          compiler_params=_cp(("parallel",)), name=name)(*args)


def _norm_bwd(dy, x, g, resid=None, name="norm_bwd"):
    T, D = x.shape
    tm = _pick(T, 512)
    has_resid = resid is not None

    def body(*refs):
        if has_resid:
            dy_ref, x_ref, g_ref, r_ref, dx_ref, dg_ref = refs
        else:
            dy_ref, x_ref, g_ref, dx_ref, dg_ref = refs
        _, vjp = jax.vjp(_rms, x_ref[...].astype(F32), g_ref[...])
        dx, dg = vjp(dy_ref[...].astype(F32))
        if has_resid:
            dx = r_ref[...] + dx
        dx_ref[...] = dx

        @pl.when(pl.program_id(0) == 0)
        def _():
            dg_ref[...] = jnp.zeros_like(dg_ref)

        dg_ref[...] += dg

    row = pl.BlockSpec((tm, D), lambda i: (i, 0))
    vec = pl.BlockSpec((1, D), lambda i: (0, 0))
    in_specs = [row, row, vec] + ([row] if has_resid else [])
    args = (dy, x, g) + ((resid,) if has_resid else ())
    return pl.pallas_call(body, grid=(T // tm,), in_specs=in_specs, out_specs=(row, vec),
                          out_shape=(jax.ShapeDtypeStruct((T, D), F32), jax.ShapeDtypeStruct((1, D), F32)),
                          compiler_params=_cp(("arbitrary",)), name=name)(*args)


def _mm(a, b, *, ta=False, tb=False, out_dtypes=(F32,), epilogue=None, extras=(), name="mm",
        tm_t=1024, tn_t=1024, tk_t=1024):
    M, K = (a.shape[1], a.shape[0]) if ta else a.shape
    N = b.shape[0] if tb else b.shape[1]
    assert (b.shape[1] if tb else b.shape[0]) == K, (a.shape, b.shape, ta, tb)
    tm, tn, tk = _pick(M, tm_t), _pick(N, tn_t), _pick(K, tk_t)
    nk = K // tk
    n_extra = len(extras)
    n_out = len(out_dtypes)

    def body(*refs):
        a_ref, b_ref = refs[0], refs[1]
        e_refs = refs[2:2 + n_extra]
        o_refs = refs[2 + n_extra:2 + n_extra + n_out]

        def finish(acc):
            outs = (acc,) if epilogue is None else epilogue(acc, *[e[...] for e in e_refs])
            for o_ref, o in zip(o_refs, outs):
                o_ref[...] = o.astype(o_ref.dtype)

        d = _dot(a_ref[...], b_ref[...], 0 if ta else 1, 1 if tb else 0)
        if nk == 1:
            finish(d)
            return
        acc_ref = refs[-1]
        k = pl.program_id(2)

        @pl.when(k == 0)
        def _():
            acc_ref[...] = d

        @pl.when((k > 0) & (k < nk - 1))
        def _():
            acc_ref[...] += d

        @pl.when(k == nk - 1)
        def _():
            finish(acc_ref[...] + d)

    a_spec = pl.BlockSpec((tk, tm), lambda i, j, k: (k, i)) if ta else pl.BlockSpec((tm, tk), lambda i, j, k: (i, k))
    b_spec = pl.BlockSpec((tn, tk), lambda i, j, k: (j, k)) if tb else pl.BlockSpec((tk, tn), lambda i, j, k: (k, j))
    o_spec = pl.BlockSpec((tm, tn), lambda i, j, k: (i, j))
    outs = pl.pallas_call(
        body, grid=(M // tm, N // tn, nk),
        in_specs=[a_spec, b_spec] + [o_spec] * n_extra,
        out_specs=tuple([o_spec] * n_out),
        out_shape=tuple(jax.ShapeDtypeStruct((M, N), dt) for dt in out_dtypes),
        scratch_shapes=[pltpu.VMEM((tm, tn), F32)] if nk > 1 else [],
        compiler_params=_cp(("parallel", "parallel", "arbitrary")), name=name)(a, b, *extras)
    return outs[0] if n_out == 1 else outs


def _relu2_epilogue(acc):
    r = jnp.maximum(acc, 0.0)
    return r * r, r


def _drelu2_epilogue(acc, r):
    return (acc * (2.0 * r.astype(F32)),)


def _loss_head(x, target, name="loss_head"):
    T, D = x.shape
    tm = _pick(T, 512)

    def body(x_ref, t_ref, l_ref, dx_ref):
        e = x_ref[...] - t_ref[...]
        dx_ref[...] = e * (1.0 / D)

        @pl.when(pl.program_id(0) == 0)
        def _():
            l_ref[...] = jnp.zeros_like(l_ref)

        part = 0.5 * jnp.sum(jnp.mean(e * e, axis=-1, keepdims=True), axis=0, keepdims=True)
        l_ref[...] += jnp.broadcast_to(part, l_ref.shape)

    row = pl.BlockSpec((tm, D), lambda i: (i, 0))
    return pl.pallas_call(body, grid=(T // tm,), in_specs=[row, row],
                          out_specs=(pl.BlockSpec((1, LANE), lambda i: (0, 0)), row),
                          out_shape=(jax.ShapeDtypeStruct((1, LANE), F32), jax.ShapeDtypeStruct((T, D), F32)),
                          compiler_params=_cp(("arbitrary",)), name=name)(x, target)


def _shift_down(x, k, row):
    return jnp.where(row >= k, pltpu.roll(x, k, 0), 0.0)


def _shift_up(x, k, row, n):
    return jnp.where(row < n - k, pltpu.roll(x, n - k, 0), 0.0)


def _conv_taps(x, w, row):
    y = x * w[CONV_WIDTH - 1:CONV_WIDTH, :]
    for i in range(CONV_WIDTH - 1):
        y = y + _shift_down(x, CONV_WIDTH - 1 - i, row) * w[i:i + 1, :]
    return y


def _qkv_act(xc, j):
    s = _silu(xc)
    n = s * lax.rsqrt(jnp.sum(s * s, axis=-1, keepdims=True) + EPS)
    n = n * jnp.where(j < N_LIN_HEADS, LIN_HEAD_DIM ** -0.5, 1.0)
    return jnp.where(j < 2 * N_LIN_HEADS, n, s)


def _gdn_conv_fwd(proj, conv_w, B, S):
    nblk = 3 * N_LIN_HEADS

    def body(p_ref, w_ref, o_ref):
        j = pl.program_id(1)
        x = p_ref[...]
        row = lax.broadcasted_iota(jnp.int32, x.shape, 0)
        o_ref[...] = _qkv_act(_conv_taps(x, w_ref[...], row), j)

    blk = pl.BlockSpec((S, LANE), lambda b, j: (b, j))
    return pl.pallas_call(body, grid=(B, nblk),
                          in_specs=[blk, pl.BlockSpec((CONV_WIDTH, LANE), lambda b, j: (0, j))],
                          out_specs=blk, out_shape=jax.ShapeDtypeStruct((B * S, nblk * LANE), F32),
                          compiler_params=_cp(("parallel", "parallel")), name="gdn_conv_fwd")(proj, conv_w)


def _gdn_conv_bwd(dq, dk, dv, proj, conv_w, B, S):
    nblk = 3 * N_LIN_HEADS
    H = N_LIN_HEADS

    def body(dq_ref, dk_ref, dv_ref, p_ref, w_ref, dp_ref, dw_ref):
        j = pl.program_id(0)
        b = pl.program_id(1)
        x = p_ref[...]
        w = w_ref[...]
        row = lax.broadcasted_iota(jnp.int32, x.shape, 0)
        d_act = jnp.where(j < H, dq_ref[...], jnp.where(j < 2 * H, dk_ref[...], dv_ref[...]))
        _, vjp = jax.vjp(lambda t: _qkv_act(t, j), _conv_taps(x, w, row))
        (d_xc,) = vjp(d_act)
        dx = d_xc * w[CONV_WIDTH - 1:CONV_WIDTH, :]
        for i in range(CONV_WIDTH - 1):
            dx = dx + _shift_up(d_xc, CONV_WIDTH - 1 - i, row, S) * w[i:i + 1, :]
        dp_ref[...] = dx.astype(dp_ref.dtype)

        @pl.when(b == 0)
        def _():
            dw_ref[...] = jnp.zeros_like(dw_ref)

        for i in range(CONV_WIDTH):
            xs = x if i == CONV_WIDTH - 1 else _shift_down(x, CONV_WIDTH - 1 - i, row)
            dw_ref[i:i + 1, :] += jnp.sum(d_xc * xs, axis=0, keepdims=True)

    blk = pl.BlockSpec((S, LANE), lambda j, b: (b, j))
    wblk = pl.BlockSpec((CONV_WIDTH, LANE), lambda j, b: (0, j))
    return pl.pallas_call(
        body, grid=(nblk, B),
        in_specs=[pl.BlockSpec((S, LANE), lambda j, b: (b, jnp.clip(j, 0, H - 1))),
                  pl.BlockSpec((S, LANE), lambda j, b: (b, jnp.clip(j - H, 0, H - 1))),
                  pl.BlockSpec((S, LANE), lambda j, b: (b, jnp.clip(j - 2 * H, 0, H - 1))),
                  blk, wblk],
        out_specs=(blk, wblk),
        out_shape=(jax.ShapeDtypeStruct((B * S, nblk * LANE), _MXU_DTYPE),
                   jax.ShapeDtypeStruct((CONV_WIDTH, nblk * LANE), F32)),
        compiler_params=_cp(("parallel", "arbitrary")), name="gdn_conv_bwd")(dq, dk, dv, proj, conv_w)


def _chunk_cumsum(x, row):
    pos = row % CHUNK
    k = 1
    while k < CHUNK:
        x = x + jnp.where(pos >= k, pltpu.roll(x, k, 0), 0.0)
        k *= 2
    return x


def _chunk_rev_cumsum(x, row, n):
    pos = row % CHUNK
    k = 1
    while k < CHUNK:
        x = x + jnp.where(pos < CHUNK - k, pltpu.roll(x, n - k, 0), 0.0)
        k *= 2
    return x


def _gdn_gates_fwd(proj, a_log, dt_bias, B, S):
    H = N_LIN_HEADS

    def body(sm_ref, al_ref, dt_ref, beta_ref, gc_ref):
        sm = sm_ref[...]
        row = lax.broadcasted_iota(jnp.int32, (S, LANE), 0)
        for h in range(H):
            beta = _sigmoid(sm[:, h:h + 1])
            g = -jnp.exp(al_ref[0:1, h:h + 1]) * _softplus(sm[:, H + h:H + h + 1] + dt_ref[0:1, h:h + 1])
            beta_ref[:, h * LANE:(h + 1) * LANE] = jnp.broadcast_to(beta, (S, LANE))
            gc_ref[:, h * LANE:(h + 1) * LANE] = _chunk_cumsum(jnp.broadcast_to(g, (S, LANE)), row)

    vec = pl.BlockSpec((1, LANE), lambda b: (0, 0))
    wide = pl.BlockSpec((S, H * LANE), lambda b: (b, 0))
    return pl.pallas_call(body, grid=(B,),
                          in_specs=[pl.BlockSpec((S, LANE), lambda b: (b, SM_COL // LANE)), vec, vec],
                          out_specs=(wide, wide),
                          out_shape=(jax.ShapeDtypeStruct((B * S, H * LANE), F32),) * 2,
                          compiler_params=_cp(("parallel",)), name="gdn_gates_fwd")(proj, a_log, dt_bias)


def _gdn_gates_bwd(d_beta, d_gc, proj, a_log, dt_bias, B, S):
    H = N_LIN_HEADS

    def body(db_ref, dgc_ref, sm_ref, al_ref, dt_ref, dsm_ref, dal_ref, ddt_ref):
        sm = sm_ref[...]
        row = lax.broadcasted_iota(jnp.int32, (S, LANE), 0)
        lane = lax.broadcasted_iota(jnp.int32, (1, LANE), 1)
        dsm = jnp.zeros((S, LANE), F32)
        dal = jnp.zeros((1, LANE), F32)
        ddt = jnp.zeros((1, LANE), F32)
        for h in range(H):
            beta = _sigmoid(sm[:, h:h + 1])
            dbeta = jnp.sum(db_ref[:, h * LANE:(h + 1) * LANE], axis=-1, keepdims=True)
            d_bl = dbeta * beta * (1.0 - beta)
            dgc = jnp.sum(dgc_ref[:, h * LANE:(h + 1) * LANE], axis=-1, keepdims=True)
            dg = _chunk_rev_cumsum(jnp.broadcast_to(dgc, (S, LANE)), row, S)[:, 0:1]
            z = sm[:, H + h:H + h + 1] + dt_ref[0:1, h:h + 1]
            a = jnp.exp(al_ref[0:1, h:h + 1])
            g = -a * _softplus(z)
            d_al = dg * (-a) * _sigmoid(z)
            dsm = dsm + jnp.where(lane == h, d_bl, 0.0) + jnp.where(lane == H + h, d_al, 0.0)
            ddt = ddt + jnp.where(lane == h, jnp.sum(d_al, axis=0, keepdims=True), 0.0)
            dal = dal + jnp.where(lane == h, jnp.sum(dg * g, axis=0, keepdims=True), 0.0)
        dsm_ref[...] = dsm.astype(dsm_ref.dtype)

        @pl.when(pl.program_id(0) == 0)
        def _():
            dal_ref[...] = jnp.zeros_like(dal_ref)
            ddt_ref[...] = jnp.zeros_like(ddt_ref)

        dal_ref[...] += dal
        ddt_ref[...] += ddt

    vec = pl.BlockSpec((1, LANE), lambda b: (0, 0))
    wide = pl.BlockSpec((S, H * LANE), lambda b: (b, 0))
    return pl.pallas_call(body, grid=(B,),
                          in_specs=[wide, wide, pl.BlockSpec((S, LANE), lambda b: (b, SM_COL // LANE)), vec, vec],
                          out_specs=(pl.BlockSpec((S, LANE), lambda b: (b, 0)), vec, vec),
                          out_shape=(jax.ShapeDtypeStruct((B * S, LANE), _MXU_DTYPE),
                                     jax.ShapeDtypeStruct((1, LANE), F32), jax.ShapeDtypeStruct((1, LANE), F32)),
                          compiler_params=_cp(("arbitrary",)), name="gdn_gates_bwd")(d_beta, d_gc, proj, a_log, dt_bias)


PREP_ROWS = 512


@jax.custom_vjp
def _unit_lower_inverse(lower):
    n, C, _ = lower.shape
    ri = lax.broadcasted_iota(jnp.int32, (C, C), 0)
    ci = lax.broadcasted_iota(jnp.int32, (C, C), 1)
    p = -lower
    inv = jnp.where((ri == ci)[None], 1.0, 0.0) + p
    for _ in range(int(math.log2(C)) - 1):
        p = _bdot_hi(p, p)
        inv = inv + _bdot_hi(inv, p)
    return inv


def _unit_lower_inverse_fwd(lower):
    inv = _unit_lower_inverse(lower)
    return inv, inv


def _unit_lower_inverse_bwd(inv, d_inv):
    inv_t = jnp.swapaxes(inv, 1, 2)
    return (-_bdot_hi(_bdot_hi(inv_t, d_inv), inv_t),)


_unit_lower_inverse.defvjp(_unit_lower_inverse_fwd, _unit_lower_inverse_bwd)


def _prep_fn(q, k, v, beta, gc):
    R = q.shape[0]
    n = R // CHUNK
    q3, k3, v3, b3, g3 = [t.reshape(n, CHUNK, LIN_HEAD_DIM) for t in (q, k, v, beta, gc)]
    ri = lax.broadcasted_iota(jnp.int32, (CHUNK, CHUNK), 0)
    ci = lax.broadcasted_iota(jnp.int32, (CHUNK, CHUNK), 1)
    causal = (ri >= ci)[None]
    strict = (ri > ci)[None]
    gcol = g3[:, :, 0:1]
    grow = jnp.swapaxes(g3, 1, 2)[:, 0:1, :]
    decay = jnp.exp(jnp.where(causal, gcol - grow, -1e30))
    kb = k3 * b3
    lower = jnp.where(strict, _bdot(kb, k3, 2, 2) * decay, 0.0)
    inv = _unit_lower_inverse(lower)
    eg = jnp.exp(g3)
    sol = _bdot_hi(inv, jnp.concatenate([v3 * b3, kb * eg], axis=-1))
    u, w = sol[..., :LIN_HEAD_DIM], sol[..., LIN_HEAD_DIM:]
    intra = _bdot(q3, k3, 2, 2) * decay
    q_dec = q3 * eg
    k_dec = k3 * jnp.exp(g3[:, CHUNK - 1:CHUNK, :] - g3)
    return (u.reshape(R, LIN_HEAD_DIM), w.reshape(R, LIN_HEAD_DIM), q_dec.reshape(R, LIN_HEAD_DIM),
            k_dec.reshape(R, LIN_HEAD_DIM), intra.reshape(R, CHUNK))


def _prep_specs(S):
    H = N_LIN_HEADS
    R = min(PREP_ROWS, S)
    nr = S // R

    def col(off):
        return pl.BlockSpec((R, LANE), lambda b, h, r: (b * nr + r, off + h))

    intra = pl.BlockSpec((None, R, CHUNK), lambda b, h, r: (h, b * nr + r, 0))
    return R, nr, col, intra


def _gdn_prep_fwd(act, beta, gc, B, S):
    H = N_LIN_HEADS
    R, nr, col, intra_spec = _prep_specs(S)
    T = B * S

    def body(q_ref, k_ref, v_ref, b_ref, g_ref, u_ref, w_ref, qd_ref, kd_ref, a_ref):
        u, w, qd, kd, a = _prep_fn(q_ref[...], k_ref[...], v_ref[...], b_ref[...], g_ref[...])
        u_ref[...] = u
        w_ref[...] = w
        qd_ref[...] = qd
        kd_ref[...] = kd
        a_ref[...] = a

    wide = jax.ShapeDtypeStruct((T, H * LANE), F32)
    return pl.pallas_call(body, grid=(B, H, nr),
                          in_specs=[col(0), col(H), col(2 * H), col(0), col(0)],
                          out_specs=(col(0), col(0), col(0), col(0), intra_spec),
                          out_shape=(wide, wide, wide, wide, jax.ShapeDtypeStruct((H, T, CHUNK), F32)),
                          compiler_params=_cp(("parallel", "parallel", "parallel")),
                          name="gdn_prep_fwd")(act, act, act, beta, gc)


def _gdn_prep_bwd(act, beta, gc, du, dw, dqd, dkd, da, B, S):
    H = N_LIN_HEADS
    R, nr, col, intra_spec = _prep_specs(S)
    T = B * S

    def body(q_ref, k_ref, v_ref, b_ref, g_ref, du_ref, dw_ref, dqd_ref, dkd_ref, da_ref,
             dq_ref, dk_ref, dv_ref, db_ref, dg_ref):
        _, vjp = jax.vjp(_prep_fn, q_ref[...], k_ref[...], v_ref[...], b_ref[...], g_ref[...])
        dq, dk, dv, db, dg = vjp((du_ref[...], dw_ref[...], dqd_ref[...], dkd_ref[...], da_ref[...]))
        dq_ref[...] = dq
        dk_ref[...] = dk
        dv_ref[...] = dv
        db_ref[...] = db
        dg_ref[...] = dg

    wide = jax.ShapeDtypeStruct((T, H * LANE), F32)
    return pl.pallas_call(body, grid=(B, H, nr),
                          in_specs=[col(0), col(H), col(2 * H), col(0), col(0),
                                    col(0), col(0), col(0), col(0), intra_spec],
                          out_specs=(col(0),) * 5, out_shape=(wide,) * 5,
                          compiler_params=_cp(("parallel", "parallel", "parallel")),
                          name="gdn_prep_bwd")(act, act, act, beta, gc, du, dw, dqd, dkd, da)


def _scan_step(u, w, qd, kd, a, g_last, state):
    v_new = u - _dot(w, state)
    o = _dot(qd, state) + _dot(a, v_new)
    new_state = state * jnp.exp(g_last) + _dot(kd, v_new, 0, 0)
    return o, new_state


SCAN_HEADS = 3
SCAN_ROWS = 512


def _scan_specs(B, S, reverse):
    HP = SCAN_HEADS
    R = min(SCAN_ROWS, S)
    nr = S // R

    def blk(r):
        return nr - 1 - r if reverse else r

    col = pl.BlockSpec((R, HP * LANE), lambda b, h, r: (b * nr + blk(r), h))
    intra = pl.BlockSpec((HP, R, CHUNK), lambda b, h, r: (h, b * nr + blk(r), 0))
    st = pl.BlockSpec((None, HP, R // CHUNK, LIN_HEAD_DIM, LIN_HEAD_DIM), lambda b, h, r: (b, h, blk(r), 0, 0))
    return R, nr, col, intra, st


def _gdn_scan_fwd(u, w, qd, kd, a, gc, B, S):
    H = N_LIN_HEADS
    R, nr, col, intra, st = _scan_specs(B, S, reverse=False)

    def body(u_ref, w_ref, qd_ref, kd_ref, a_ref, g_ref, o_ref, st_ref, carry_ref):
        @pl.when(pl.program_id(2) == 0)
        def _():
            carry_ref[...] = jnp.zeros_like(carry_ref)

        def step(c, states):
            rows = pl.ds(pl.multiple_of(c * CHUNK, CHUNK), CHUNK)
            new_states = []
            for hh, state in enumerate(states):
                cols = slice(hh * LANE, (hh + 1) * LANE)
                st_ref[hh, c] = state.astype(st_ref.dtype)
                o, new_state = _scan_step(u_ref[rows, cols], w_ref[rows, cols], qd_ref[rows, cols], kd_ref[rows, cols],
                                          a_ref[hh, rows, :], g_ref[rows, cols][CHUNK - 1:CHUNK, :], state)
                o_ref[rows, cols] = o
                new_states.append(new_state)
            return tuple(new_states)

        states = lax.fori_loop(0, R // CHUNK, step, tuple(carry_ref[hh] for hh in range(SCAN_HEADS)))
        for hh, state in enumerate(states):
            carry_ref[hh] = state

    return pl.pallas_call(body, grid=(B, H // SCAN_HEADS, nr), in_specs=[col, col, col, col, intra, col],
                          out_specs=(col, st),
                          out_shape=(jax.ShapeDtypeStruct((B * S, H * LANE), F32),
                                     jax.ShapeDtypeStruct((B, H, S // CHUNK, LIN_HEAD_DIM, LIN_HEAD_DIM), _MXU_DTYPE)),
                          scratch_shapes=[pltpu.VMEM((SCAN_HEADS, LIN_HEAD_DIM, LIN_HEAD_DIM), F32)],
                          compiler_params=_cp(("parallel", "parallel", "arbitrary")),
                          name="gdn_scan_fwd")(u, w, qd, kd, a, gc)


def _gdn_scan_bwd(u, w, qd, kd, a, gc, states, do, B, S):
    H = N_LIN_HEADS
    R, nr, col, intra, st = _scan_specs(B, S, reverse=True)
    T = B * S
    n = R // CHUNK

    def body(u_ref, w_ref, qd_ref, kd_ref, a_ref, g_ref, st_ref, do_ref,
             du_ref, dw_ref, dqd_ref, dkd_ref, da_ref, dg_ref, carry_ref):
        last = lax.broadcasted_iota(jnp.int32, (CHUNK, LANE), 0) == CHUNK - 1

        @pl.when(pl.program_id(2) == 0)
        def _():
            carry_ref[...] = jnp.zeros_like(carry_ref)

        def step(i, d_states):
            c = n - 1 - i
            rows = pl.ds(pl.multiple_of(c * CHUNK, CHUNK), CHUNK)
            d_prevs = []
            for hh, d_state in enumerate(d_states):
                cols = slice(hh * LANE, (hh + 1) * LANE)
                _, vjp = jax.vjp(_scan_step, u_ref[rows, cols], w_ref[rows, cols], qd_ref[rows, cols], kd_ref[rows, cols],
                                 a_ref[hh, rows, :], g_ref[rows, cols][CHUNK - 1:CHUNK, :], st_ref[hh, c].astype(F32))
                du, dw, dqd, dkd, da, dgl, d_prev = vjp((do_ref[rows, cols].astype(F32), d_state))
                du_ref[rows, cols] = du
                dw_ref[rows, cols] = dw
                dqd_ref[rows, cols] = dqd
                dkd_ref[rows, cols] = dkd
                da_ref[hh, rows, :] = da
                dg_ref[rows, cols] = jnp.where(last, dgl, 0.0)
                d_prevs.append(d_prev)
            return tuple(d_prevs)

        d_states = lax.fori_loop(0, n, step, tuple(carry_ref[hh] for hh in range(SCAN_HEADS)))
        for hh, d_state in enumerate(d_states):
            carry_ref[hh] = d_state

    wide = jax.ShapeDtypeStruct((T, H * LANE), F32)
    return pl.pallas_call(body, grid=(B, H // SCAN_HEADS, nr), in_specs=[col, col, col, col, intra, col, st, col],
                          out_specs=(col, col, col, col, intra, col),
                          out_shape=(wide, wide, wide, wide, jax.ShapeDtypeStruct((H, T, CHUNK), F32), wide),
                          scratch_shapes=[pltpu.VMEM((SCAN_HEADS, LIN_HEAD_DIM, LIN_HEAD_DIM), F32)],
                          compiler_params=_cp(("parallel", "parallel", "arbitrary")),
                          name="gdn_scan_bwd")(u, w, qd, kd, a, gc, states, do)


GATE_COL = 3 * SEQ_MIX_WIDTH


def _post_fn(o, gate, gain):
    return o * lax.rsqrt(jnp.mean(o * o, axis=-1, keepdims=True) + EPS) * gain * _silu(gate)


def _gdn_post_fwd(o, proj, onorm, T):
    H = N_LIN_HEADS
    tm = _pick(T, 1024)

    def body(o_ref, g_ref, n_ref, y_ref):
        y_ref[...] = _post_fn(o_ref[...], g_ref[...], n_ref[...]).astype(y_ref.dtype)

    col = pl.BlockSpec((tm, LANE), lambda i, h: (i, h))
    return pl.pallas_call(body, grid=(T // tm, H),
                          in_specs=[col, pl.BlockSpec((tm, LANE), lambda i, h: (i, GATE_COL // LANE + h)),
                                    pl.BlockSpec((1, LANE), lambda i, h: (0, 0))],
                          out_specs=col, out_shape=jax.ShapeDtypeStruct((T, H * LANE), _MXU_DTYPE),
                          compiler_params=_cp(("parallel", "parallel")), name="gdn_post_fwd")(o, proj, onorm)


def _gdn_post_bwd(d_cat, o, proj, onorm, T):
    H = N_LIN_HEADS
    tm = _pick(T, 1024)

    def body(dy_ref, o_ref, g_ref, n_ref, do_ref, dg_ref, dn_ref):
        _, vjp = jax.vjp(_post_fn, o_ref[...], g_ref[...], n_ref[...])
        do, dg, dn = vjp(dy_ref[...].astype(F32))
        do_ref[...] = do
        dg_ref[...] = dg.astype(dg_ref.dtype)

        @pl.when((pl.program_id(0) == 0) & (pl.program_id(1) == 0))
        def _():
            dn_ref[...] = jnp.zeros_like(dn_ref)

        dn_ref[...] += dn

    col = pl.BlockSpec((tm, LANE), lambda i, h: (i, h))
    vec = pl.BlockSpec((1, LANE), lambda i, h: (0, 0))
    return pl.pallas_call(body, grid=(T // tm, H),
                          in_specs=[col, col, pl.BlockSpec((tm, LANE), lambda i, h: (i, GATE_COL // LANE + h)), vec],
                          out_specs=(col, col, vec),
                          out_shape=(jax.ShapeDtypeStruct((T, H * LANE), F32),
                                     jax.ShapeDtypeStruct((T, H * LANE), _MXU_DTYPE),
                                     jax.ShapeDtypeStruct((1, LANE), F32)),
                          compiler_params=_cp(("arbitrary", "arbitrary")), name="gdn_post_bwd")(d_cat, o, proj, onorm)


def _log_sigmoid(z):
    return jnp.minimum(z, 0.0) - jnp.log(1.0 + jnp.exp(-jnp.abs(z)))


def _split_dot(x, m):
    hi = x.astype(_MXU_DTYPE)
    lo = x - hi.astype(F32)
    return _dot(hi, m) + _dot(lo, m)


SB_QB = 256
SB_RC = 512


def _sb_scores(q2, k_j, scale, valid):
    z = _dot(q2, k_j, 1, 1) * scale
    lb = _log_sigmoid(z)
    return lb, jnp.where(valid, lb - z, 0.0)


def _sb_consts(QB):
    ri = lax.broadcasted_iota(jnp.int32, (SB_BLOCK, SB_BLOCK), 0)
    ci = lax.broadcasted_iota(jnp.int32, (SB_BLOCK, SB_BLOCK), 1)
    rc = min(SB_RC, 2 * QB)
    assert rc == 2 * QB or QB % rc == 0
    row = lax.broadcasted_iota(jnp.int32, (rc, SB_BLOCK), 0)
    col = lax.broadcasted_iota(jnp.int32, (rc, SB_BLOCK), 1)
    lane = lax.broadcasted_iota(jnp.int32, (1, LANE), 1)
    return {
        "rc": rc,
        "chunks": [(ch * rc, (ch * rc) % QB) for ch in range(2 * QB // rc)],
        "row_minus_col": row % QB - col,
        "after_excl": (ri > ci).astype(_MXU_DTYPE),
        "upto_incl": (ri <= ci).astype(_MXU_DTYPE),
        "upto_excl": (ri < ci).astype(_MXU_DTYPE),
        "lane": lane,
        "head0": lane < SB_HEAD_DIM,
    }


def _sb_stack(x, c):
    return jnp.concatenate([jnp.where(c["head0"], x, 0.0), jnp.where(c["head0"], 0.0, x)], axis=0)


def _sb_unstack(x2, c, QB):
    return jnp.where(c["head0"], x2[:QB], x2[QB:])


SB_DEAD = -110.0


def _sb_first_shape(S, QB):
    return (-(-(S // QB) // 8) * 8, LANE)


def _sb_fwd(proj, B, S):
    W = SEQ_MIX_WIDTH
    P = W // LANE
    QB = min(SB_QB, S)
    KB = SB_BLOCK
    scale = SB_HEAD_DIM ** -0.5
    fshape = _sb_first_shape(S, QB)

    def body(q_ref, k_ref, v_ref, o_ref, tot_ref, first_ref):
        c = _sb_consts(QB)
        frow = lax.broadcasted_iota(jnp.int32, fshape, 0)

        def q_loop(i, first):
            qrows = pl.ds(pl.multiple_of(i * QB, QB), QB)
            q2 = _sb_stack(q_ref[qrows, :].astype(F32), c)
            nkb = (i + 1) * (QB // KB)

            def scores(j):
                krows = pl.ds(pl.multiple_of(j * KB, KB), KB)
                valid = c["row_minus_col"] > j * KB - i * QB
                lb, l1 = _sb_scores(q2, k_ref[krows, :], scale, valid)
                return jnp.where(valid, lb, -1e30), l1

            def k_cond(st):
                return (st[0] < nkb) & (st[1] > 0)

            def k_body(st):
                t, _, acc, r, lbm, l1 = st
                j = nkb - 1 - t
                nxt = scores(jnp.maximum(j - 1, 0))
                krows = pl.ds(pl.multiple_of(j * KB, KB), KB)
                a = jnp.exp(lbm + r + _split_dot(l1, c["after_excl"]))
                r = r + jnp.sum(l1, axis=-1, keepdims=True)
                alive = (jnp.max(r) > SB_DEAD).astype(jnp.int32)
                return (t + 1, alive, acc + _dot(a, v_ref[krows, :]), r) + nxt

            t, _, acc, r, _, _ = lax.while_loop(
                k_cond, k_body, (jnp.int32(0), jnp.int32(1), jnp.zeros((2 * QB, LANE), F32), jnp.zeros((2 * QB, 1), F32))
                + scores(nkb - 1))
            o_ref[qrows, :] = _sb_unstack(acc, c, QB)
            tot_ref[qrows, :] = _sb_unstack(jnp.broadcast_to(r, (2 * QB, LANE)), c, QB)
            return jnp.where(frow == i, (nkb - t).astype(F32), first)

        first_ref[...] = lax.fori_loop(0, S // QB, q_loop, jnp.zeros(fshape, F32))

    def col(off):
        return pl.BlockSpec((S, LANE), lambda b, p: (b, off + p))

    out = jax.ShapeDtypeStruct((B * S, W), F32)
    return pl.pallas_call(body, grid=(B, P), in_specs=[col(0), col(P), col(2 * P)],
                          out_specs=(col(0), col(0), pl.BlockSpec((None, None) + fshape, lambda b, p: (b, p, 0, 0))),
                          out_shape=(out, out, jax.ShapeDtypeStruct((B, P) + fshape, F32)),
                          compiler_params=_cp(("parallel", "parallel")), name="sb_fwd")(proj, proj, proj)


def _sb_bwd(proj, tot, first, d_cat, B, S):
    W = SEQ_MIX_WIDTH
    P = W // LANE
    QB = min(SB_QB, S)
    KB = SB_BLOCK
    scale = SB_HEAD_DIM ** -0.5

    fshape = _sb_first_shape(S, QB)

    def body(q_ref, k_ref, v_ref, tot_ref, first_ref, do_ref, dq_ref, dk_ref, dv_ref, dk_acc, dv_acc):
        c = _sb_consts(QB)
        frow = lax.broadcasted_iota(jnp.int32, fshape, 0)
        dk_acc[...] = jnp.zeros_like(dk_acc)
        dv_acc[...] = jnp.zeros_like(dv_acc)

        def q_loop(i, carry):
            qrows = pl.ds(pl.multiple_of(i * QB, QB), QB)
            q2 = _sb_stack(q_ref[qrows, :].astype(F32), c)
            do2 = _sb_stack(do_ref[qrows, :].astype(F32), c)
            q2_t = q2.T.astype(_MXU_DTYPE)
            do2_t = do2.T.astype(_MXU_DTYPE)
            tot = tot_ref[qrows, :]
            total = jnp.concatenate(
                [jnp.sum(jnp.where(c["lane"] == h * SB_HEAD_DIM, tot, 0.0), axis=-1, keepdims=True) for h in range(2)],
                axis=0)

            nkb = (i + 1) * (QB // KB)
            j0 = jnp.clip(jnp.max(jnp.where(frow == i, first_ref[...], 0.0)).astype(jnp.int32), 0, nkb - 1)

            def scores(j):
                krows = pl.ds(pl.multiple_of(j * KB, KB), KB)
                valid = c["row_minus_col"] > j * KB - i * QB
                lb, l1 = _sb_scores(q2, k_ref[krows, :], scale, valid)
                return jnp.where(valid, lb, -1e30), l1, _dot(do2, v_ref[krows, :], 1, 1)

            def k_loop(j, st):
                dq_acc, p_l1, p_g, lbm, l1, da = st
                nxt = scores(jnp.minimum(j + 1, nkb - 1))
                krows = pl.ds(pl.multiple_of(j * KB, KB), KB)
                tail = total - p_l1 - _split_dot(l1, c["upto_incl"])
                a = jnp.exp(lbm + tail)
                g = da * a
                g_before = p_g + _dot(g, c["upto_excl"])
                sig = jnp.exp(lbm)
                dz = ((g * (1.0 - sig) - g_before * sig) * scale).astype(_MXU_DTYPE)
                dk_acc[j] += _dot(q2_t, dz)
                dv_acc[j] += _dot(do2_t, a)
                return (dq_acc + _dot(dz, k_ref[krows, :]), p_l1 + jnp.sum(l1, axis=-1, keepdims=True),
                        p_g + jnp.sum(g, axis=-1, keepdims=True)) + nxt

            zero_col = jnp.zeros((2 * QB, 1), F32)
            dq2 = lax.fori_loop(j0, nkb, k_loop, (jnp.zeros((2 * QB, LANE), F32), zero_col, zero_col) + scores(j0))[0]
            dq_ref[qrows, :] = _sb_unstack(dq2, c, QB).astype(dq_ref.dtype)
            return carry

        lax.fori_loop(0, S // QB, q_loop, 0)
        for j in range(S // KB):
            dk_ref[j * KB:(j + 1) * KB, :] = dk_acc[j].T.astype(dk_ref.dtype)
            dv_ref[j * KB:(j + 1) * KB, :] = dv_acc[j].T.astype(dv_ref.dtype)

    def col(off):
        return pl.BlockSpec((S, LANE), lambda b, p: (b, off + p))

    out = jax.ShapeDtypeStruct((B * S, W), _MXU_DTYPE)
    return pl.pallas_call(body, grid=(B, P),
                          in_specs=[col(0), col(P), col(2 * P), col(0),
                                    pl.BlockSpec((None, None) + fshape, lambda b, p: (b, p, 0, 0)), col(0)],
                          out_specs=(col(0),) * 3, out_shape=(out,) * 3,
                          scratch_shapes=[pltpu.VMEM((S // KB, LANE, KB), F32), pltpu.VMEM((S // KB, LANE, KB), F32)],
                          compiler_params=_cp(("parallel", "parallel")), name="sb_bwd")(proj, proj, proj, tot, first, d_cat)


def _mem_fn(q, k, v):
    lane = lax.broadcasted_iota(jnp.int32, (1, X_WIDTH), 1)
    out = jnp.zeros(q.shape, F32)
    for h in range(N_X_HEADS):
        hm = (lane // X_HEAD_DIM) == h
        s = _dot(jnp.where(hm, q, 0.0), k, 1, 1) * (X_HEAD_DIM ** -0.5)
        e = jnp.exp(s - lax.stop_gradient(jnp.max(s, axis=-1, keepdims=True)))
        p = e / jnp.sum(e, axis=-1, keepdims=True)
        out = out + jnp.where(hm, _dot(p, v), 0.0)
    return out


def _mem_specs(S, q_col):
    ts = _pick(S, 1024)
    ns = S // ts
    qs = pl.BlockSpec((ts, X_WIDTH), lambda b, i: (b * ns + i, q_col // X_WIDTH))
    ks = pl.BlockSpec((N_MEM, X_WIDTH), lambda b, i: (b, 0))
    vs = pl.BlockSpec((N_MEM, X_WIDTH), lambda b, i: (b, 1))
    os = pl.BlockSpec((ts, X_WIDTH), lambda b, i: (b * ns + i, 0))
    return ts, ns, qs, ks, vs, os


def _mem_fwd(proj, q_col, mem_kv, B, S, name):
    ts, ns, qs, ks, vs, os = _mem_specs(S, q_col)

    def body(q_ref, k_ref, v_ref, o_ref):
        o_ref[...] = _mem_fn(q_ref[...].astype(F32), k_ref[...].astype(F32), v_ref[...].astype(F32))

    return pl.pallas_call(body, grid=(B, ns), in_specs=[qs, ks, vs], out_specs=os,
                          out_shape=jax.ShapeDtypeStruct((B * S, X_WIDTH), F32),
                          compiler_params=_cp(("parallel", "parallel")), name=name)(proj, mem_kv, mem_kv)


def _mem_bwd(proj, q_col, mem_kv, d_cat, B, S, name):
    ts, ns, qs, ks, vs, os = _mem_specs(S, q_col)

    def body(q_ref, k_ref, v_ref, do_ref, dq_ref, dk_ref, dv_ref):
        _, vjp = jax.vjp(_mem_fn, q_ref[...].astype(F32), k_ref[...].astype(F32), v_ref[...].astype(F32))
        dq, dk, dv = vjp(do_ref[...].astype(F32))
        dq_ref[...] = dq.astype(dq_ref.dtype)

        @pl.when(pl.program_id(1) == 0)
        def _():
            dk_ref[...] = jnp.zeros_like(dk_ref)
            dv_ref[...] = jnp.zeros_like(dv_ref)

        dk_ref[...] += dk
        dv_ref[...] += dv

    dos = pl.BlockSpec((ts, X_WIDTH), lambda b, i: (b * ns + i, SEQ_MIX_WIDTH // X_WIDTH))
    dq, dk, dv = pl.pallas_call(
        body, grid=(B, ns), in_specs=[qs, ks, vs, dos],
        out_specs=(os, pl.BlockSpec((N_MEM, X_WIDTH), lambda b, i: (b, 0)), pl.BlockSpec((N_MEM, X_WIDTH), lambda b, i: (b, 0))),
        out_shape=(jax.ShapeDtypeStruct((B * S, X_WIDTH), _MXU_DTYPE),
                   jax.ShapeDtypeStruct((B * N_MEM, X_WIDTH), F32), jax.ShapeDtypeStruct((B * N_MEM, X_WIDTH), F32)),
        compiler_params=_cp(("parallel", "arbitrary")), name=name)(proj, mem_kv, mem_kv, d_cat)
    return dq, dk, dv


def _peers():
    x, y, c = lax.axis_index("x"), lax.axis_index("y"), lax.axis_index("c")
    me = 4 * x + 2 * y + c
    out = []
    for fx, fy, fc in [(0, 0, 1), (1, 0, 0), (0, 1, 0), (1, 1, 0), (1, 0, 1), (0, 1, 1), (1, 1, 1)]:
        px, py, pc = x ^ fx, y ^ fy, c ^ fc
        out.append(((px, py, pc), 4 * px + 2 * py + pc))
    return me, out


ANY = pl.BlockSpec(memory_space=pl.ANY)


def _remote(src, dst, send_sems, recv_sems, k, dev):
    return pltpu.make_async_remote_copy(src_ref=src, dst_ref=dst, send_sem=send_sems.at[k], recv_sem=recv_sems.at[k],
                                        device_id=dev, device_id_type=pl.DeviceIdType.MESH)


def _place():
    x, y, c = lax.axis_index("x"), lax.axis_index("y"), lax.axis_index("c")
    return x, y, c, [(1 - x, y), (x, 1 - y), (1 - x, 1 - y)]


def _all_gather(shard):
    R = shard.shape[0]

    def body(x_ref, o_ref, send_sems, recv_sems, local_sem):
        x, y, c, chips = _place()
        me, sibling = (x, y, c), (x, y, 1 - c)

        def slot(px, py, pc):
            return o_ref.at[4 * px + 2 * py + pc]

        def copy(k, block, to, src=None):
            return _remote(slot(*block) if src is None else src, slot(*block), send_sems, recv_sems, k, to)

        mine = pltpu.make_async_copy(x_ref, slot(*me), local_sem)
        mine.start()
        first = [copy(0, me, sibling, src=x_ref)] + [copy(1 + j, me, (*chip, c), src=x_ref) for j, chip in enumerate(chips)]
        for cp in first:
            cp.start()
        passed = [copy(4 + j, (*chip, c), sibling) for j, chip in enumerate(chips)]
        for j, chip in enumerate(chips):
            copy(1 + j, (*chip, c), me).wait_recv()
            passed[j].start()
        copy(0, sibling, me).wait_recv()
        for j, chip in enumerate(chips):
            copy(4 + j, (*chip, 1 - c), me).wait_recv()
        for cp in first + passed:
            cp.wait_send()
        mine.wait()

    return pl.pallas_call(body, in_specs=[ANY], out_specs=ANY,
                          out_shape=jax.ShapeDtypeStruct((N_DEV, R, LANE), shard.dtype),
                          scratch_shapes=[pltpu.SemaphoreType.DMA((7,)), pltpu.SemaphoreType.DMA((7,)),
                                          pltpu.SemaphoreType.DMA],
                          compiler_params=pltpu.CompilerParams(has_side_effects=True),
                          name="all_gather_weights")(shard)


N_CHIP = 4


def _exchange_sibling(big):
    R = big.shape[1]

    def body(b_ref, o_ref, send_sems, recv_sems):
        x, y, c, _ = _place()
        copies = [_remote(b_ref.at[2 * k + (1 - c)], o_ref.at[k], send_sems, recv_sems, k, (x, y, 1 - c))
                  for k in range(N_CHIP)]
        for cp in copies:
            cp.start()
        for cp in copies:
            cp.wait()

    return pl.pallas_call(body, in_specs=[ANY], out_specs=ANY,
                          out_shape=jax.ShapeDtypeStruct((N_CHIP, R, LANE), big.dtype),
                          scratch_shapes=[pltpu.SemaphoreType.DMA((N_CHIP,)), pltpu.SemaphoreType.DMA((N_CHIP,))],
                          compiler_params=pltpu.CompilerParams(has_side_effects=True),
                          name="exchange_sibling")(big)


def _partial_sum(g4, recv):
    R = g4.shape[2]
    tr = _pick(R, 1024)

    def body(g_ref, r_ref, pw_ref, po_ref):
        x, y, c, _ = _place()
        g = jnp.where(c == 0, g_ref[0], g_ref[1]).astype(F32) + r_ref[...].astype(F32)
        pw_ref[...] = g.astype(pw_ref.dtype)

        @pl.when(pl.program_id(1) == 2 * x + y)
        def _():
            po_ref[...] = g

    return pl.pallas_call(body, grid=(R // tr, N_CHIP),
                          in_specs=[pl.BlockSpec((None, 2, tr, LANE), lambda i, k: (k, 0, i, 0)),
                                    pl.BlockSpec((None, tr, LANE), lambda i, k: (k, i, 0))],
                          out_specs=(pl.BlockSpec((None, tr, LANE), lambda i, k: (k, i, 0)),
                                     pl.BlockSpec((tr, LANE), lambda i, k: (i, 0))),
                          out_shape=(jax.ShapeDtypeStruct((N_CHIP, R, LANE), recv.dtype),
                                     jax.ShapeDtypeStruct((R, LANE), F32)),
                          compiler_params=_cp(("parallel", "arbitrary")), name="partial_sum")(g4, recv)


def _exchange_chips(part, small):
    R = part.shape[1]
    K = small.shape[0]

    def body(p_ref, s_ref, ob_ref, os_ref, send_sems, recv_sems, local_sems):
        x, y, c, chips = _place()
        my_chip = 2 * x + y
        me, peers = _peers()
        own_b = pltpu.make_async_copy(p_ref.at[my_chip], ob_ref.at[my_chip], local_sems.at[0])
        own_s = pltpu.make_async_copy(s_ref, os_ref.at[me], local_sems.at[1])
        own_b.start()
        own_s.start()
        copies = [_remote(p_ref.at[2 * px + py], ob_ref.at[my_chip], send_sems, recv_sems, j, (px, py, c))
                  for j, (px, py) in enumerate(chips)]
        copies += [_remote(s_ref, os_ref.at[me], send_sems, recv_sems, 3 + k, dev) for k, (dev, _) in enumerate(peers)]
        for cp in copies:
            cp.start()
        for j, (px, py) in enumerate(chips):
            _remote(p_ref.at[my_chip], ob_ref.at[2 * px + py], send_sems, recv_sems, j, (px, py, c)).wait_recv()
        for k, (dev, idx) in enumerate(peers):
            _remote(s_ref, os_ref.at[idx], send_sems, recv_sems, 3 + k, dev).wait_recv()
        for cp in copies:
            cp.wait_send()
        own_b.wait()
        own_s.wait()

    return pl.pallas_call(body, in_specs=[ANY, ANY], out_specs=(ANY, ANY),
                          out_shape=(jax.ShapeDtypeStruct((N_CHIP, R, LANE), part.dtype),
                                     jax.ShapeDtypeStruct((N_DEV, K, LANE), small.dtype)),
                          scratch_shapes=[pltpu.SemaphoreType.DMA((10,)), pltpu.SemaphoreType.DMA((10,)),
                                          pltpu.SemaphoreType.DMA((2,))],
                          compiler_params=pltpu.CompilerParams(has_side_effects=True),
                          name="exchange_chips")(part, small)


def _adamw_math(w, g, m, v):
    m = ADAM_B1 * m + (1.0 - ADAM_B1) * g
    v = ADAM_B2 * v + (1.0 - ADAM_B2) * (g * g)
    m_hat = m / (1.0 - ADAM_B1 ** ADAM_STEP)
    v_hat = v / (1.0 - ADAM_B2 ** ADAM_STEP)
    delta = -ADAM_LR * (m_hat / (jnp.sqrt(v_hat) + ADAM_EPS) + ADAM_WD * w)
    return delta, m, v


def _adamw_shard(own, recv, w, m, v):
    R = own.shape[0]
    tr = _pick(R, 1024)

    def body(own_ref, recv_ref, w_ref, m_ref, v_ref, g_ref, d_ref, nm_ref, nv_ref):
        x, y, _, _ = _place()
        g = own_ref[...]
        for k in range(N_CHIP):
            g = g + jnp.where(k == 2 * x + y, 0.0, recv_ref[k].astype(F32))
        delta, nm, nv = _adamw_math(w_ref[...], g, m_ref[...], v_ref[...])
        g_ref[...] = g
        d_ref[...] = delta
        nm_ref[...] = nm
        nv_ref[...] = nv

    row = pl.BlockSpec((tr, LANE), lambda i: (i, 0))
    out = jax.ShapeDtypeStruct((R, LANE), F32)
    return pl.pallas_call(body, grid=(R // tr,),
                          in_specs=[row, pl.BlockSpec((N_CHIP, tr, LANE), lambda i: (0, i, 0)), row, row, row],
                          out_specs=(row,) * 4, out_shape=(out,) * 4,
                          compiler_params=_cp(("parallel",)), name="adamw_shard")(own, recv, w, m, v)


def _adamw_replicated(parts, w, m, v):
    K = w.shape[0]

    def body(p_ref, w_ref, m_ref, v_ref, g_ref, d_ref, nm_ref, nv_ref):
        g = p_ref[0]
        for p in range(1, N_DEV):
            g = g + p_ref[p]
        delta, nm, nv = _adamw_math(w_ref[...], g, m_ref[...], v_ref[...])
        g_ref[...] = g
        d_ref[...] = delta
        nm_ref[...] = nm
        nv_ref[...] = nv

    out = jax.ShapeDtypeStruct((K, LANE), F32)
    return pl.pallas_call(body, out_shape=(out,) * 4, compiler_params=_cp(), name="adamw_replicated")(parts, w, m, v)


_SHARDED = (("w_in_a", 1), ("conv_w_a", 2), ("w_in_b", 2), ("w_mem_kv", 1), ("w_out", 1), ("w_up", 2), ("w_down", 1))
_REPLICATED = ("mem_norm", "norm_pre_mix", "norm_post_mix", "norm_pre_mlp", "norm_post_mlp", "a_log_a", "dt_bias_a", "onorm_a")
ROW_ALIGN = 16


def _rows(n_elems, align=ROW_ALIGN):
    r = -(-n_elems // LANE)
    return -(-r // align) * align


def _pack(arrays, align=ROW_ALIGN, total=None, lead=()):
    parts = []
    for a in arrays:
        n = math.prod(a.shape[len(lead):])
        flat = a.reshape(lead + (n,))
        r = _rows(n, align)
        flat = jnp.pad(flat, [(0, 0)] * len(lead) + [(0, r * LANE - n)])
        parts.append(flat.reshape(lead + (r, LANE)))
    used = sum(part.shape[len(lead)] for part in parts)
    if total is not None and used < total:
        parts.append(jnp.zeros(lead + (total - used, LANE), parts[0].dtype))
    return jnp.concatenate(parts, axis=len(lead))


def _unpack(flat, shapes, align=ROW_ALIGN, lead=()):
    outs = []
    r0 = 0
    for shp in shapes:
        n = math.prod(shp)
        r = _rows(n, align)
        part = lax.slice_in_dim(flat, r0, r0 + r, axis=len(lead))
        part = part.reshape(lead + (r * LANE,))
        part = lax.slice_in_dim(part, 0, n, axis=len(lead))
        outs.append(part.reshape(lead + tuple(shp)))
        r0 += r
    return outs


def _to_full(gathered, axis):
    g = jnp.moveaxis(gathered, 0, axis)
    shp = g.shape
    return g.reshape(shp[:axis] + (shp[axis] * shp[axis + 1],) + shp[axis + 2:])


def _to_blocks(full, axis):
    shp = full.shape
    g = full.reshape(shp[:axis] + (N_DEV, shp[axis] // N_DEV) + shp[axis + 1:])
    return jnp.moveaxis(g, axis, 0)


def _widen_in_a(w):
    main = w[:, :4 * SEQ_MIX_WIDTH]
    small = w[:, 4 * SEQ_MIX_WIDTH:4 * SEQ_MIX_WIDTH + 2 * N_LIN_HEADS]
    memq = w[:, 4 * SEQ_MIX_WIDTH + 2 * N_LIN_HEADS:]
    pad = jnp.zeros((w.shape[0], IN_A_PAD - IN_A), w.dtype)
    return jnp.concatenate([main, memq, small, pad], axis=1)


def _narrow_in_a(g):
    main = g[:, :4 * SEQ_MIX_WIDTH]
    memq = g[:, 4 * SEQ_MIX_WIDTH:4 * SEQ_MIX_WIDTH + X_WIDTH]
    small = g[:, SM_COL:SM_COL + 2 * N_LIN_HEADS]
    return jnp.concatenate([main, small, memq], axis=1)


def _row128(v):
    return jnp.pad(v.reshape(1, -1), ((0, 0), (0, LANE - v.shape[-1])))


def _local_step(x, mem, target, p):
    B, S, D = x.shape
    T = B * S
    md = _MXU_DTYPE
    x0 = x.reshape(T, D)
    tgt = target.reshape(T, D)
    memf = mem.reshape(B * N_MEM, D)
    vec = lambda a: a.reshape(1, -1)

    mem_n = _norm_fwd(memf, vec(p["mem_norm"]), out_dtype=md, name="norm_mem")
    w_in = [_widen_in_a(p["w_in_a"][0]), p["w_in_b"][0]]
    memq_col = [4 * SEQ_MIX_WIDTH, 3 * SEQ_MIX_WIDTH]
    alog = _row128(p["a_log_a"][0])
    dtb = _row128(p["dt_bias_a"][0])
    onorm = vec(p["onorm_a"][0])
    conv_w = p["conv_w_a"][0]
    saved = []
    xi = x0
    for i in range(2):
        s = {"x_in": xi}
        h1 = _norm_fwd(xi, vec(p["norm_pre_mix"][i]), out_dtype=md, name=f"norm_pre_mix{i}")
        proj = _mm(h1, w_in[i], out_dtypes=(F32 if i == 0 else md,), name=f"in_proj{i}")
        mem_kv = _mm(mem_n, p["w_mem_kv"][i], out_dtypes=(md,), name=f"mem_kv{i}")
        if i == 0:
            act = _gdn_conv_fwd(proj, conv_w, B, S)
            beta, gc = _gdn_gates_fwd(proj, alog, dtb, B, S)
            u, w, qd, kd, intra = _gdn_prep_fwd(act, beta, gc, B, S)
            o, states = _gdn_scan_fwd(u, w, qd, kd, intra, gc, B, S)
            mix = _gdn_post_fwd(o, proj, onorm, T)
            s.update(act=act, beta=beta, gc=gc, u=u, w=w, qd=qd, kd=kd, intra=intra, o=o, states=states)
        else:
            mix, tot, first = _sb_fwd(proj, B, S)
            s.update(tot=tot, first=first)
        cross = _mem_fwd(proj, memq_col[i], mem_kv, B, S, name=f"mem_fwd{i}")
        cat = jnp.concatenate([mix.astype(md), cross.astype(md)], axis=1)
        y = _mm(cat, p["w_out"][i], name=f"out_proj{i}")
        x_mid = _norm_fwd(y, vec(p["norm_post_mix"][i]), resid=xi, name=f"norm_post_mix{i}")
        h2 = _norm_fwd(x_mid, vec(p["norm_pre_mlp"][i]), out_dtype=md, name=f"norm_pre_mlp{i}")
        a_act, r = _mm(h2, p["w_up"][i], out_dtypes=(md, md), epilogue=_relu2_epilogue, name=f"up_proj{i}")
        y2 = _mm(a_act, p["w_down"][i], name=f"down_proj{i}")
        x_out = _norm_fwd(y2, vec(p["norm_post_mlp"][i]), resid=x_mid, name=f"norm_post_mlp{i}")
        s.update(h1=h1, proj=proj, mem_kv=mem_kv, cat=cat, y=y, x_mid=x_mid, h2=h2, a_act=a_act, r=r, y2=y2)
        saved.append(s)
        xi = x_out

    loss_row, dx = _loss_head(xi, tgt)

    g = {}
    d_mem_n = None
    gn = {k: [None, None] for k in ("norm_pre_mix", "norm_post_mix", "norm_pre_mlp", "norm_post_mlp")}
    g_w_mem_kv, g_w_out, g_w_up, g_w_down = [None, None], [None, None], [None, None], [None, None]
    for i in (1, 0):
        s = saved[i]
        d_y2, gn["norm_post_mlp"][i] = _norm_bwd(dx, s["y2"], vec(p["norm_post_mlp"][i]), name=f"norm_post_mlp_bwd{i}")
        g_w_down[i] = _mm(s["a_act"], d_y2, ta=True, out_dtypes=(_WIRE_DTYPE,), name=f"down_proj_dw{i}")
        d_u = _mm(d_y2, p["w_down"][i], tb=True, out_dtypes=(md,), epilogue=_drelu2_epilogue, extras=(s["r"],),
                  name=f"down_proj_dx{i}")
        g_w_up[i] = _mm(s["h2"], d_u, ta=True, out_dtypes=(_WIRE_DTYPE,), name=f"up_proj_dw{i}")
        d_h2 = _mm(d_u, p["w_up"][i], tb=True, name=f"up_proj_dx{i}")
        dx, gn["norm_pre_mlp"][i] = _norm_bwd(d_h2, s["x_mid"], vec(p["norm_pre_mlp"][i]), resid=dx,
                                              name=f"norm_pre_mlp_bwd{i}")
        d_y, gn["norm_post_mix"][i] = _norm_bwd(dx, s["y"], vec(p["norm_post_mix"][i]), name=f"norm_post_mix_bwd{i}")
        g_w_out[i] = _mm(s["cat"], d_y, ta=True, out_dtypes=(_WIRE_DTYPE,), name=f"out_proj_dw{i}")
        d_cat = _mm(d_y, p["w_out"][i], tb=True, name=f"out_proj_dx{i}")
        d_memq, d_mk, d_mv = _mem_bwd(s["proj"], memq_col[i], s["mem_kv"], d_cat, B, S, name=f"mem_bwd{i}")
        d_mem_kv = jnp.concatenate([d_mk.astype(md), d_mv.astype(md)], axis=1)
        g_w_mem_kv[i] = _mm(mem_n, d_mem_kv, ta=True, out_dtypes=(_WIRE_DTYPE,), name=f"mem_kv_dw{i}")
        d_mn = _mm(d_mem_kv, p["w_mem_kv"][i], tb=True, name=f"mem_kv_dx{i}")
        d_mem_n = d_mn if d_mem_n is None else d_mem_n + d_mn
        if i == 0:
            d_o, d_gate, g["onorm_a"] = _gdn_post_bwd(d_cat, s["o"], s["proj"], onorm, T)
            du, dw, dqd, dkd, da, dgc_s = _gdn_scan_bwd(s["u"], s["w"], s["qd"], s["kd"], s["intra"], s["gc"],
                                                         s["states"], d_o, B, S)
            dq, dk, dv, d_beta, d_gc = _gdn_prep_bwd(s["act"], s["beta"], s["gc"], du, dw, dqd, dkd, da, B, S)
            d_qkv, g["conv_w_a"] = _gdn_conv_bwd(dq, dk, dv, s["proj"], conv_w, B, S)
            d_sm, g["a_log_a"], g["dt_bias_a"] = _gdn_gates_bwd(d_beta, d_gc + dgc_s, s["proj"], alog, dtb, B, S)
            pad = jnp.zeros((T, IN_A_PAD - SM_COL - LANE), md)
            d_proj = jnp.concatenate([d_qkv, d_gate, d_memq, d_sm, pad], axis=1)
        else:
            dq, dk, dv = _sb_bwd(s["proj"], s["tot"], s["first"], d_cat, B, S)
            d_proj = jnp.concatenate([dq, dk, dv, d_memq], axis=1)
        g_w_in = _mm(s["h1"], d_proj, ta=True, out_dtypes=(_WIRE_DTYPE,), name=f"in_proj_dw{i}")
        d_h1 = _mm(d_proj, w_in[i], tb=True, name=f"in_proj_dx{i}")
        dx, gn["norm_pre_mix"][i] = _norm_bwd(d_h1, s["x_in"], vec(p["norm_pre_mix"][i]), resid=dx,
                                              name=f"norm_pre_mix_bwd{i}")
        if i == 0:
            g["w_in_a"] = _narrow_in_a(g_w_in)[None]
        else:
            g["w_in_b"] = g_w_in[None]
    _, g_mem_norm = _norm_bwd(d_mem_n, memf, vec(p["mem_norm"]), name="norm_mem_bwd")
    g["mem_norm"] = g_mem_norm.reshape(-1)
    for k, v in gn.items():
        g[k] = jnp.concatenate(v, axis=0)
    g["w_mem_kv"] = jnp.stack(g_w_mem_kv)
    g["w_out"] = jnp.stack(g_w_out)
    g["w_up"] = jnp.stack(g_w_up)
    g["w_down"] = jnp.stack(g_w_down)
    g["conv_w_a"] = g["conv_w_a"][None]
    g["a_log_a"] = g["a_log_a"][:, :N_LIN_HEADS]
    g["dt_bias_a"] = g["dt_bias_a"][:, :N_LIN_HEADS]
    return loss_row, dx.reshape(B, S, D), g


def kernel(x, mem, mem_norm, norm_pre_mix, norm_post_mix, norm_pre_mlp, norm_post_mlp, w_in_a, conv_w_a, a_log_a, dt_bias_a, onorm_a, w_in_b, w_mem_kv, w_out, w_up, w_down, loss_target, m_mem_norm, m_norm_pre_mix, m_norm_post_mix, m_norm_pre_mlp, m_norm_post_mlp, m_w_in_a, m_conv_w_a, m_a_log_a, m_dt_bias_a, m_onorm_a, m_w_in_b, m_w_mem_kv, m_w_out, m_w_up, m_w_down, v_mem_norm, v_norm_pre_mix, v_norm_post_mix, v_norm_pre_mlp, v_norm_post_mlp, v_w_in_a, v_conv_w_a, v_a_log_a, v_dt_bias_a, v_onorm_a, v_w_in_b, v_w_mem_kv, v_w_out, v_w_up, v_w_down):
    args = dict(locals())
    big_names = [n for n, _ in _SHARDED]
    axes = dict(_SHARDED)
    shard_shapes = [args[n].shape for n in big_names]
    rows_total = -(-sum(_rows(math.prod(s)) for s in shard_shapes) // 1024) * 1024

    exact = ("conv_w_a",) if _WIRE_DTYPE != F32 else ()
    wire = [lax.bitcast_convert_type(args[n], _WIRE_DTYPE) if n in exact else args[n].astype(_WIRE_DTYPE)
            for n in big_names]
    wire_shapes = [a.shape for a in wire]
    wire_rows = -(-sum(_rows(math.prod(s)) for s in wire_shapes) // ROW_ALIGN) * ROW_ALIGN
    gathered = _all_gather(_pack(wire, total=wire_rows))
    p = {}
    for n, b in zip(big_names, _unpack(gathered, wire_shapes, lead=(N_DEV,))):
        b = lax.bitcast_convert_type(b, F32) if n in exact else b.astype(_MXU_DTYPE)
        p[n] = _to_full(b, axes[n])
    for n in _REPLICATED:
        p[n] = args[n]
    w_flat = _pack([args[n] for n in big_names], total=rows_total)

    loss_row, grad_x, g = _local_step(x, mem, loss_target, p)
    loss = lax.psum(loss_row[0, 0], ("x", "y", "c"))

    g_blocks = _pack([_to_blocks(g[n], axes[n]).astype(_WIRE_DTYPE) for n in big_names], total=rows_total, lead=(N_DEV,))
    rep_shapes = [args[n].shape for n in _REPLICATED]
    g_small = _pack([g[n] for n in _REPLICATED], align=8)
    recv_sib = _exchange_sibling(g_blocks)
    part, own = _partial_sum(g_blocks.reshape(N_CHIP, 2, rows_total, LANE), recv_sib)
    recv_big, recv_small = _exchange_chips(part, g_small)

    m_flat = _pack([args["m_" + n] for n in big_names], total=rows_total)
    v_flat = _pack([args["v_" + n] for n in big_names], total=rows_total)
    outs_big = [_unpack(f, shard_shapes) for f in _adamw_shard(own, recv_big, w_flat, m_flat, v_flat)]
    outs_small = [_unpack(f, rep_shapes, align=8) for f in _adamw_replicated(
        recv_small, _pack([args[n] for n in _REPLICATED], align=8),
        _pack([args["m_" + n] for n in _REPLICATED], align=8), _pack([args["v_" + n] for n in _REPLICATED], align=8))]

    order = ["mem_norm", "norm_pre_mix", "norm_post_mix", "norm_pre_mlp", "norm_post_mlp", "w_in_a", "conv_w_a",
             "a_log_a", "dt_bias_a", "onorm_a", "w_in_b", "w_mem_kv", "w_out", "w_up", "w_down"]
    result = [loss, grad_x]
    for kind in range(4):
        for n in order:
            if n in axes:
                result.append(outs_big[kind][big_names.index(n)])
            else:
                result.append(outs_small[kind][_REPLICATED.index(n)])
    return tuple(result)
```

```python
import functools
import math

import jax
import jax.numpy as jnp
from jax import lax
from jax.experimental import pallas as pl
from jax.experimental.pallas import tpu as pltpu

F32 = jnp.float32
_MXU_DTYPE = jnp.bfloat16
_WIRE_DTYPE = jnp.bfloat16
_HI = lax.Precision.HIGH

D_MODEL = 1024
N_DEV = 8
N_MEM = 256
X_WIDTH = 256
N_X_HEADS = 4
X_HEAD_DIM = 64
SEQ_MIX_WIDTH = 768
LIN_HEAD_DIM = 128
N_LIN_HEADS = 6
CONV_WIDTH = 4
CHUNK = 64
SB_HEAD_DIM = 64
SB_BLOCK = 128
D_FF = 4096
EPS = 1e-6
IN_A = 3340
IN_A_PAD = 3584
IN_B = 2560
SM_COL = 3328

ADAM_LR = 0.001
ADAM_B1 = 0.9
ADAM_B2 = 0.999
ADAM_EPS = 1e-08
ADAM_WD = 0.01
ADAM_STEP = 10

LANE = 128
VMEM_LIMIT = 56 * 1024 * 1024


def _cp(sem=None):
    return pltpu.CompilerParams(dimension_semantics=sem, vmem_limit_bytes=VMEM_LIMIT)


def _pick(n, target):
    if n <= target:
        return n
    best = None
    for t in range(LANE, target + 1, LANE):
        if n % t == 0:
            best = t
    assert best is not None, (n, target)
    return best


def _dot(a, b, ca=1, cb=0):
    return lax.dot_general(a.astype(_MXU_DTYPE), b.astype(_MXU_DTYPE), (((ca,), (cb,)), ((), ())),
                           preferred_element_type=F32)


def _bdot(a, b, ca, cb):
    return lax.dot_general(a.astype(_MXU_DTYPE), b.astype(_MXU_DTYPE), (((ca,), (cb,)), ((0,), (0,))),
                           preferred_element_type=F32)


def _bdot_hi(a, b):
    return lax.dot_general(a, b, (((2,), (1,)), ((0,), (0,))), precision=_HI, preferred_element_type=F32)


def _sigmoid(x):
    return 1.0 / (1.0 + jnp.exp(-x))


def _silu(x):
    return x * _sigmoid(x)


def _softplus(x):
    return jnp.maximum(x, 0.0) + jnp.log(1.0 + jnp.exp(-jnp.abs(x)))


def _rms(x, g):
    return x * lax.rsqrt(jnp.mean(x * x, axis=-1, keepdims=True) + EPS) * g


def _norm_fwd(x, g, resid=None, out_dtype=F32, name="norm_fwd"):
    T, D = x.shape
    tm = _pick(T, 512)
    has_resid = resid is not None

    def body(*refs):
        if has_resid:
            x_ref, g_ref, r_ref, o_ref = refs
        else:
            x_ref, g_ref, o_ref = refs
        y = _rms(x_ref[...].astype(F32), g_ref[...])
        if has_resid:
            y = r_ref[...] + y
        o_ref[...] = y.astype(out_dtype)

    row = pl.BlockSpec((tm, D), lambda i: (i, 0))
    in_specs = [row, pl.BlockSpec((1, D), lambda i: (0, 0))] + ([row] if has_resid else [])
    args = (x, g) + ((resid,) if has_resid else ())
    return pl.pallas_call(body, grid=(T // tm,), in_specs=in_specs, out_specs=row,
                          out_shape=jax.ShapeDtypeStruct((T, D), out_dtype),
                          compiler_params=_cp(("parallel",)), name=name)(*args)


def _norm_bwd(dy, x, g, resid=None, name="norm_bwd"):
    T, D = x.shape
    tm = _pick(T, 512)
    has_resid = resid is not None

    def body(*refs):
        if has_resid:
            dy_ref, x_ref, g_ref, r_ref, dx_ref, dg_ref = refs
        else:
            dy_ref, x_ref, g_ref, dx_ref, dg_ref = refs
        _, vjp = jax.vjp(_rms, x_ref[...].astype(F32), g_ref[...])
        dx, dg = vjp(dy_ref[...].astype(F32))
        if has_resid:
            dx = r_ref[...] + dx
        dx_ref[...] = dx

        @pl.when(pl.program_id(0) == 0)
        def _():
            dg_ref[...] = jnp.zeros_like(dg_ref)

        dg_ref[...] += dg

    row = pl.BlockSpec((tm, D), lambda i: (i, 0))
    vec = pl.BlockSpec((1, D), lambda i: (0, 0))
    in_specs = [row, row, vec] + ([row] if has_resid else [])
    args = (dy, x, g) + ((resid,) if has_resid else ())
    return pl.pallas_call(body, grid=(T // tm,), in_specs=in_specs, out_specs=(row, vec),
                          out_shape=(jax.ShapeDtypeStruct((T, D), F32), jax.ShapeDtypeStruct((1, D), F32)),
                          compiler_params=_cp(("arbitrary",)), name=name)(*args)


def _mm(a, b, *, ta=False, tb=False, out_dtypes=(F32,), epilogue=None, extras=(), name="mm",
        tm_t=1024, tn_t=1024, tk_t=1024):
    M, K = (a.shape[1], a.shape[0]) if ta else a.shape
    N = b.shape[0] if tb else b.shape[1]
    assert (b.shape[1] if tb else b.shape[0]) == K, (a.shape, b.shape, ta, tb)
    tm, tn, tk = _pick(M, tm_t), _pick(N, tn_t), _pick(K, tk_t)
    nk = K // tk
    n_extra = len(extras)
    n_out = len(out_dtypes)

    def body(*refs):
        a_ref, b_ref = refs[0], refs[1]
        e_refs = refs[2:2 + n_extra]
        o_refs = refs[2 + n_extra:2 + n_extra + n_out]

        def finish(acc):
            outs = (acc,) if epilogue is None else epilogue(acc, *[e[...] for e in e_refs])
            for o_ref, o in zip(o_refs, outs):
                o_ref[...] = o.astype(o_ref.dtype)

        d = _dot(a_ref[...], b_ref[...], 0 if ta else 1, 1 if tb else 0)
        if nk == 1:
            finish(d)
            return
        acc_ref = refs[-1]
        k = pl.program_id(2)

        @pl.when(k == 0)
        def _():
            acc_ref[...] = d

        @pl.when((k > 0) & (k < nk - 1))
        def _():
            acc_ref[...] += d

        @pl.when(k == nk - 1)
        def _():
            finish(acc_ref[...] + d)

    a_spec = pl.BlockSpec((tk, tm), lambda i, j, k: (k, i)) if ta else pl.BlockSpec((tm, tk), lambda i, j, k: (i, k))
    b_spec = pl.BlockSpec((tn, tk), lambda i, j, k: (j, k)) if tb else pl.BlockSpec((tk, tn), lambda i, j, k: (k, j))
    o_spec = pl.BlockSpec((tm, tn), lambda i, j, k: (i, j))
    outs = pl.pallas_call(
        body, grid=(M // tm, N // tn, nk),
        in_specs=[a_spec, b_spec] + [o_spec] * n_extra,
        out_specs=tuple([o_spec] * n_out),
        out_shape=tuple(jax.ShapeDtypeStruct((M, N), dt) for dt in out_dtypes),
        scratch_shapes=[pltpu.VMEM((tm, tn), F32)] if nk > 1 else [],
        compiler_params=_cp(("parallel", "parallel", "arbitrary")), name=name)(a, b, *extras)
    return outs[0] if n_out == 1 else outs


def _relu2_epilogue(acc):
    r = jnp.maximum(acc, 0.0)
    return r * r, r


def _drelu2_epilogue(acc, r):
    return (acc * (2.0 * r.astype(F32)),)


def _loss_head(x, target, name="loss_head"):
    T, D = x.shape
    tm = _pick(T, 512)

    def body(x_ref, t_ref, l_ref, dx_ref):
        e = x_ref[...] - t_ref[...]
        dx_ref[...] = e * (1.0 / D)

        @pl.when(pl.program_id(0) == 0)
        def _():
            l_ref[...] = jnp.zeros_like(l_ref)

        part = 0.5 * jnp.sum(jnp.mean(e * e, axis=-1, keepdims=True), axis=0, keepdims=True)
        l_ref[...] += jnp.broadcast_to(part, l_ref.shape)

    row = pl.BlockSpec((tm, D), lambda i: (i, 0))
    return pl.pallas_call(body, grid=(T // tm,), in_specs=[row, row],
                          out_specs=(pl.BlockSpec((1, LANE), lambda i: (0, 0)), row),
                          out_shape=(jax.ShapeDtypeStruct((1, LANE), F32), jax.ShapeDtypeStruct((T, D), F32)),
                          compiler_params=_cp(("arbitrary",)), name=name)(x, target)


def _shift_down(x, k, row):
    return jnp.where(row >= k, pltpu.roll(x, k, 0), 0.0)


def _shift_up(x, k, row, n):
    return jnp.where(row < n - k, pltpu.roll(x, n - k, 0), 0.0)


def _conv_taps(x, w, row):
    y = x * w[CONV_WIDTH - 1:CONV_WIDTH, :]
    for i in range(CONV_WIDTH - 1):
        y = y + _shift_down(x, CONV_WIDTH - 1 - i, row) * w[i:i + 1, :]
    return y


def _qkv_act(xc, j):
    s = _silu(xc)
    n = s * lax.rsqrt(jnp.sum(s * s, axis=-1, keepdims=True) + EPS)
    n = n * jnp.where(j < N_LIN_HEADS, LIN_HEAD_DIM ** -0.5, 1.0)
    return jnp.where(j < 2 * N_LIN_HEADS, n, s)


def _gdn_conv_fwd(proj, conv_w, B, S):
    nblk = 3 * N_LIN_HEADS

    def body(p_ref, w_ref, o_ref):
        j = pl.program_id(1)
        x = p_ref[...]
        row = lax.broadcasted_iota(jnp.int32, x.shape, 0)
        o_ref[...] = _qkv_act(_conv_taps(x, w_ref[...], row), j)

    blk = pl.BlockSpec((S, LANE), lambda b, j: (b, j))
    return pl.pallas_call(body, grid=(B, nblk),
                          in_specs=[blk, pl.BlockSpec((CONV_WIDTH, LANE), lambda b, j: (0, j))],
                          out_specs=blk, out_shape=jax.ShapeDtypeStruct((B * S, nblk * LANE), F32),
                          compiler_params=_cp(("parallel", "parallel")), name="gdn_conv_fwd")(proj, conv_w)


def _gdn_conv_bwd(dq, dk, dv, proj, conv_w, B, S):
    nblk = 3 * N_LIN_HEADS
    H = N_LIN_HEADS

    def body(dq_ref, dk_ref, dv_ref, p_ref, w_ref, dp_ref, dw_ref):
        j = pl.program_id(0)
        b = pl.program_id(1)
        x = p_ref[...]
        w = w_ref[...]
        row = lax.broadcasted_iota(jnp.int32, x.shape, 0)
        d_act = jnp.where(j < H, dq_ref[...], jnp.where(j < 2 * H, dk_ref[...], dv_ref[...]))
        _, vjp = jax.vjp(lambda t: _qkv_act(t, j), _conv_taps(x, w, row))
        (d_xc,) = vjp(d_act)
        dx = d_xc * w[CONV_WIDTH - 1:CONV_WIDTH, :]
        for i in range(CONV_WIDTH - 1):
            dx = dx + _shift_up(d_xc, CONV_WIDTH - 1 - i, row, S) * w[i:i + 1, :]
        dp_ref[...] = dx.astype(dp_ref.dtype)

        @pl.when(b == 0)
        def _():
            dw_ref[...] = jnp.zeros_like(dw_ref)

        for i in range(CONV_WIDTH):
            xs = x if i == CONV_WIDTH - 1 else _shift_down(x, CONV_WIDTH - 1 - i, row)
            dw_ref[i:i + 1, :] += jnp.sum(d_xc * xs, axis=0, keepdims=True)

    blk = pl.BlockSpec((S, LANE), lambda j, b: (b, j))
    wblk = pl.BlockSpec((CONV_WIDTH, LANE), lambda j, b: (0, j))
    return pl.pallas_call(
        body, grid=(nblk, B),
        in_specs=[pl.BlockSpec((S, LANE), lambda j, b: (b, jnp.clip(j, 0, H - 1))),
                  pl.BlockSpec((S, LANE), lambda j, b: (b, jnp.clip(j - H, 0, H - 1))),
                  pl.BlockSpec((S, LANE), lambda j, b: (b, jnp.clip(j - 2 * H, 0, H - 1))),
                  blk, wblk],
        out_specs=(blk, wblk),
        out_shape=(jax.ShapeDtypeStruct((B * S, nblk * LANE), _MXU_DTYPE),
                   jax.ShapeDtypeStruct((CONV_WIDTH, nblk * LANE), F32)),
        compiler_params=_cp(("parallel", "arbitrary")), name="gdn_conv_bwd")(dq, dk, dv, proj, conv_w)


def _chunk_cumsum(x, row):
    pos = row % CHUNK
    k = 1
    while k < CHUNK:
        x = x + jnp.where(pos >= k, pltpu.roll(x, k, 0), 0.0)
        k *= 2
    return x


def _chunk_rev_cumsum(x, row, n):
    pos = row % CHUNK
    k = 1
    while k < CHUNK:
        x = x + jnp.where(pos < CHUNK - k, pltpu.roll(x, n - k, 0), 0.0)
        k *= 2
    return x


def _gdn_gates_fwd(proj, a_log, dt_bias, B, S):
    H = N_LIN_HEADS

    def body(sm_ref, al_ref, dt_ref, beta_ref, gc_ref):
        sm = sm_ref[...]
        row = lax.broadcasted_iota(jnp.int32, (S, LANE), 0)
        for h in range(H):
            beta = _sigmoid(sm[:, h:h + 1])
            g = -jnp.exp(al_ref[0:1, h:h + 1]) * _softplus(sm[:, H + h:H + h + 1] + dt_ref[0:1, h:h + 1])
            beta_ref[:, h * LANE:(h + 1) * LANE] = jnp.broadcast_to(beta, (S, LANE))
            gc_ref[:, h * LANE:(h + 1) * LANE] = _chunk_cumsum(jnp.broadcast_to(g, (S, LANE)), row)

    vec = pl.BlockSpec((1, LANE), lambda b: (0, 0))
    wide = pl.BlockSpec((S, H * LANE), lambda b: (b, 0))
    return pl.pallas_call(body, grid=(B,),
                          in_specs=[pl.BlockSpec((S, LANE), lambda b: (b, SM_COL // LANE)), vec, vec],
                          out_specs=(wide, wide),
                          out_shape=(jax.ShapeDtypeStruct((B * S, H * LANE), F32),) * 2,
                          compiler_params=_cp(("parallel",)), name="gdn_gates_fwd")(proj, a_log, dt_bias)


def _gdn_gates_bwd(d_beta, d_gc, proj, a_log, dt_bias, B, S):
    H = N_LIN_HEADS

    def body(db_ref, dgc_ref, sm_ref, al_ref, dt_ref, dsm_ref, dal_ref, ddt_ref):
        sm = sm_ref[...]
        row = lax.broadcasted_iota(jnp.int32, (S, LANE), 0)
        lane = lax.broadcasted_iota(jnp.int32, (1, LANE), 1)
        dsm = jnp.zeros((S, LANE), F32)
        dal = jnp.zeros((1, LANE), F32)
        ddt = jnp.zeros((1, LANE), F32)
        for h in range(H):
            beta = _sigmoid(sm[:, h:h + 1])
            dbeta = jnp.sum(db_ref[:, h * LANE:(h + 1) * LANE], axis=-1, keepdims=True)
            d_bl = dbeta * beta * (1.0 - beta)
            dgc = jnp.sum(dgc_ref[:, h * LANE:(h + 1) * LANE], axis=-1, keepdims=True)
            dg = _chunk_rev_cumsum(jnp.broadcast_to(dgc, (S, LANE)), row, S)[:, 0:1]
            z = sm[:, H + h:H + h + 1] + dt_ref[0:1, h:h + 1]
            a = jnp.exp(al_ref[0:1, h:h + 1])
            g = -a * _softplus(z)
            d_al = dg * (-a) * _sigmoid(z)
            dsm = dsm + jnp.where(lane == h, d_bl, 0.0) + jnp.where(lane == H + h, d_al, 0.0)
            ddt = ddt + jnp.where(lane == h, jnp.sum(d_al, axis=0, keepdims=True), 0.0)
            dal = dal + jnp.where(lane == h, jnp.sum(dg * g, axis=0, keepdims=True), 0.0)
        dsm_ref[...] = dsm.astype(dsm_ref.dtype)

        @pl.when(pl.program_id(0) == 0)
        def _():
            dal_ref[...] = jnp.zeros_like(dal_ref)
            ddt_ref[...] = jnp.zeros_like(ddt_ref)

        dal_ref[...] += dal
        ddt_ref[...] += ddt

    vec = pl.BlockSpec((1, LANE), lambda b: (0, 0))
    wide = pl.BlockSpec((S, H * LANE), lambda b: (b, 0))
    return pl.pallas_call(body, grid=(B,),
                          in_specs=[wide, wide, pl.BlockSpec((S, LANE), lambda b: (b, SM_COL // LANE)), vec, vec],
                          out_specs=(pl.BlockSpec((S, LANE), lambda b: (b, 0)), vec, vec),
                          out_shape=(jax.ShapeDtypeStruct((B * S, LANE), _MXU_DTYPE),
                                     jax.ShapeDtypeStruct((1, LANE), F32), jax.ShapeDtypeStruct((1, LANE), F32)),
                          compiler_params=_cp(("arbitrary",)), name="gdn_gates_bwd")(d_beta, d_gc, proj, a_log, dt_bias)


PREP_ROWS = 1024


@jax.custom_vjp
def _unit_lower_inverse(lower):
    n, C, _ = lower.shape
    ri = lax.broadcasted_iota(jnp.int32, (C, C), 0)
    ci = lax.broadcasted_iota(jnp.int32, (C, C), 1)
    p = -lower
    inv = jnp.where((ri == ci)[None], 1.0, 0.0) + p
    for _ in range(int(math.log2(C)) - 1):
        p = _bdot_hi(p, p)
        inv = inv + _bdot_hi(inv, p)
    return inv


def _unit_lower_inverse_fwd(lower):
    inv = _unit_lower_inverse(lower)
    return inv, inv


def _unit_lower_inverse_bwd(inv, d_inv):
    inv_t = jnp.swapaxes(inv, 1, 2)
    return (-_bdot_hi(_bdot_hi(inv_t, d_inv), inv_t),)


_unit_lower_inverse.defvjp(_unit_lower_inverse_fwd, _unit_lower_inverse_bwd)


def _prep_fn(q, k, v, beta, gc):
    R = q.shape[0]
    n = R // CHUNK
    q3, k3, v3, b3, g3 = [t.reshape(n, CHUNK, LIN_HEAD_DIM) for t in (q, k, v, beta, gc)]
    ri = lax.broadcasted_iota(jnp.int32, (CHUNK, CHUNK), 0)
    ci = lax.broadcasted_iota(jnp.int32, (CHUNK, CHUNK), 1)
    causal = (ri >= ci)[None]
    strict = (ri > ci)[None]
    gcol = g3[:, :, 0:1]
    grow = jnp.swapaxes(g3, 1, 2)[:, 0:1, :]
    decay = jnp.exp(jnp.where(causal, gcol - grow, -1e30))
    kb = k3 * b3
    lower = jnp.where(strict, _bdot(kb, k3, 2, 2) * decay, 0.0)
    inv = _unit_lower_inverse(lower)
    eg = jnp.exp(g3)
    sol = _bdot_hi(inv, jnp.concatenate([v3 * b3, kb * eg], axis=-1))
    u, w = sol[..., :LIN_HEAD_DIM], sol[..., LIN_HEAD_DIM:]
    intra = _bdot(q3, k3, 2, 2) * decay
    q_dec = q3 * eg
    k_dec = k3 * jnp.exp(g3[:, CHUNK - 1:CHUNK, :] - g3)
    return (u.reshape(R, LIN_HEAD_DIM), w.reshape(R, LIN_HEAD_DIM), q_dec.reshape(R, LIN_HEAD_DIM),
            k_dec.reshape(R, LIN_HEAD_DIM), intra.reshape(R, CHUNK))


def _prep_specs(S):
    H = N_LIN_HEADS
    R = min(PREP_ROWS, S)
    nr = S // R

    def col(off):
        return pl.BlockSpec((R, LANE), lambda b, h, r: (b * nr + r, off + h))

    intra = pl.BlockSpec((None, R, CHUNK), lambda b, h, r: (h, b * nr + r, 0))
    return R, nr, col, intra


def _gdn_prep_fwd(act, beta, gc, B, S):
    H = N_LIN_HEADS
    R, nr, col, intra_spec = _prep_specs(S)
    T = B * S

    def body(q_ref, k_ref, v_ref, b_ref, g_ref, u_ref, w_ref, qd_ref, kd_ref, a_ref):
        u, w, qd, kd, a = _prep_fn(q_ref[...], k_ref[...], v_ref[...], b_ref[...], g_ref[...])
        u_ref[...] = u
        w_ref[...] = w
        qd_ref[...] = qd
        kd_ref[...] = kd
        a_ref[...] = a

    wide = jax.ShapeDtypeStruct((T, H * LANE), F32)
    return pl.pallas_call(body, grid=(B, H, nr),
                          in_specs=[col(0), col(H), col(2 * H), col(0), col(0)],
                          out_specs=(col(0), col(0), col(0), col(0), intra_spec),
                          out_shape=(wide, wide, wide, wide, jax.ShapeDtypeStruct((H, T, CHUNK), F32)),
                          compiler_params=_cp(("parallel", "parallel", "parallel")),
                          name="gdn_prep_fwd")(act, act, act, beta, gc)


def _gdn_prep_bwd(act, beta, gc, du, dw, dqd, dkd, da, B, S):
    H = N_LIN_HEADS
    R, nr, col, intra_spec = _prep_specs(S)
    T = B * S

    def body(q_ref, k_ref, v_ref, b_ref, g_ref, du_ref, dw_ref, dqd_ref, dkd_ref, da_ref,
             dq_ref, dk_ref, dv_ref, db_ref, dg_ref):
        _, vjp = jax.vjp(_prep_fn, q_ref[...], k_ref[...], v_ref[...], b_ref[...], g_ref[...])
        dq, dk, dv, db, dg = vjp((du_ref[...], dw_ref[...], dqd_ref[...], dkd_ref[...], da_ref[...]))
        dq_ref[...] = dq
        dk_ref[...] = dk
        dv_ref[...] = dv
        db_ref[...] = db
        dg_ref[...] = dg

    wide = jax.ShapeDtypeStruct((T, H * LANE), F32)
    return pl.pallas_call(body, grid=(B, H, nr),
                          in_specs=[col(0), col(H), col(2 * H), col(0), col(0),
                                    col(0), col(0), col(0), col(0), intra_spec],
                          out_specs=(col(0),) * 5, out_shape=(wide,) * 5,
                          compiler_params=_cp(("parallel", "parallel", "parallel")),
                          name="gdn_prep_bwd")(act, act, act, beta, gc, du, dw, dqd, dkd, da)


def _scan_step(u, w, qd, kd, a, g_last, state):
    v_new = u - _dot(w, state)
    o = _dot(qd, state) + _dot(a, v_new)
    new_state = state * jnp.exp(g_last) + _dot(kd, v_new, 0, 0)
    return o, new_state


SCAN_HEADS = 6
SCAN_ROWS = 512


def _scan_specs(B, S, reverse):
    HP = SCAN_HEADS
    R = min(SCAN_ROWS, S)
    nr = S // R

    def blk(r):
        return nr - 1 - r if reverse else r

    col = pl.BlockSpec((R, HP * LANE), lambda b, h, r: (b * nr + blk(r), h))
    intra = pl.BlockSpec((HP, R, CHUNK), lambda b, h, r: (h, b * nr + blk(r), 0))
    st = pl.BlockSpec((None, HP, R // CHUNK, LIN_HEAD_DIM, LIN_HEAD_DIM), lambda b, h, r: (b, h, blk(r), 0, 0))
    return R, nr, col, intra, st


def _gdn_scan_fwd(u, w, qd, kd, a, gc, B, S):
    H = N_LIN_HEADS
    R, nr, col, intra, st = _scan_specs(B, S, reverse=False)

    def body(u_ref, w_ref, qd_ref, kd_ref, a_ref, g_ref, o_ref, st_ref, carry_ref):
        @pl.when(pl.program_id(2) == 0)
        def _():
            carry_ref[...] = jnp.zeros_like(carry_ref)

        def step(c, states):
            rows = pl.ds(pl.multiple_of(c * CHUNK, CHUNK), CHUNK)
            new_states = []
            for hh, state in enumerate(states):
                cols = slice(hh * LANE, (hh + 1) * LANE)
                st_ref[hh, c] = state.astype(st_ref.dtype)
                o, new_state = _scan_step(u_ref[rows, cols], w_ref[rows, cols], qd_ref[rows, cols], kd_ref[rows, cols],
                                          a_ref[hh, rows, :], g_ref[rows, cols][CHUNK - 1:CHUNK, :], state)
                o_ref[rows, cols] = o
                new_states.append(new_state)
            return tuple(new_states)

        states = lax.fori_loop(0, R // CHUNK, step, tuple(carry_ref[hh] for hh in range(SCAN_HEADS)))
        for hh, state in enumerate(states):
            carry_ref[hh] = state

    return pl.pallas_call(body, grid=(B, H // SCAN_HEADS, nr), in_specs=[col, col, col, col, intra, col],
                          out_specs=(col, st),
                          out_shape=(jax.ShapeDtypeStruct((B * S, H * LANE), F32),
                                     jax.ShapeDtypeStruct((B, H, S // CHUNK, LIN_HEAD_DIM, LIN_HEAD_DIM), _MXU_DTYPE)),
                          scratch_shapes=[pltpu.VMEM((SCAN_HEADS, LIN_HEAD_DIM, LIN_HEAD_DIM), F32)],
                          compiler_params=_cp(("parallel", "parallel", "arbitrary")),
                          name="gdn_scan_fwd")(u, w, qd, kd, a, gc)


def _gdn_scan_bwd(u, w, qd, kd, a, gc, states, do, B, S):
    H = N_LIN_HEADS
    R, nr, col, intra, st = _scan_specs(B, S, reverse=True)
    T = B * S
    n = R // CHUNK

    def body(u_ref, w_ref, qd_ref, kd_ref, a_ref, g_ref, st_ref, do_ref,
             du_ref, dw_ref, dqd_ref, dkd_ref, da_ref, dg_ref, carry_ref):
        last = lax.broadcasted_iota(jnp.int32, (CHUNK, LANE), 0) == CHUNK - 1

        @pl.when(pl.program_id(2) == 0)
        def _():
            carry_ref[...] = jnp.zeros_like(carry_ref)

        def step(i, d_states):
            c = n - 1 - i
            rows = pl.ds(pl.multiple_of(c * CHUNK, CHUNK), CHUNK)
            d_prevs = []
            for hh, d_state in enumerate(d_states):
                cols = slice(hh * LANE, (hh + 1) * LANE)
                _, vjp = jax.vjp(_scan_step, u_ref[rows, cols], w_ref[rows, cols], qd_ref[rows, cols], kd_ref[rows, cols],
                                 a_ref[hh, rows, :], g_ref[rows, cols][CHUNK - 1:CHUNK, :], st_ref[hh, c].astype(F32))
                du, dw, dqd, dkd, da, dgl, d_prev = vjp((do_ref[rows, cols].astype(F32), d_state))
                du_ref[rows, cols] = du
                dw_ref[rows, cols] = dw
                dqd_ref[rows, cols] = dqd
                dkd_ref[rows, cols] = dkd
                da_ref[hh, rows, :] = da
                dg_ref[rows, cols] = jnp.where(last, dgl, 0.0)
                d_prevs.append(d_prev)
            return tuple(d_prevs)

        d_states = lax.fori_loop(0, n, step, tuple(carry_ref[hh] for hh in range(SCAN_HEADS)))
        for hh, d_state in enumerate(d_states):
            carry_ref[hh] = d_state

    wide = jax.ShapeDtypeStruct((T, H * LANE), F32)
    return pl.pallas_call(body, grid=(B, H // SCAN_HEADS, nr), in_specs=[col, col, col, col, intra, col, st, col],
                          out_specs=(col, col, col, col, intra, col),
                          out_shape=(wide, wide, wide, wide, jax.ShapeDtypeStruct((H, T, CHUNK), F32), wide),
                          scratch_shapes=[pltpu.VMEM((SCAN_HEADS, LIN_HEAD_DIM, LIN_HEAD_DIM), F32)],
                          compiler_params=_cp(("parallel", "parallel", "arbitrary")),
                          name="gdn_scan_bwd")(u, w, qd, kd, a, gc, states, do)


GATE_COL = 3 * SEQ_MIX_WIDTH


def _post_fn(o, gate, gain):
    return o * lax.rsqrt(jnp.mean(o * o, axis=-1, keepdims=True) + EPS) * gain * _silu(gate)


def _gdn_post_fwd(o, proj, onorm, T):
    H = N_LIN_HEADS
    tm = _pick(T, 1024)

    def body(o_ref, g_ref, n_ref, y_ref):
        y_ref[...] = _post_fn(o_ref[...], g_ref[...], n_ref[...]).astype(y_ref.dtype)

    col = pl.BlockSpec((tm, LANE), lambda i, h: (i, h))
    return pl.pallas_call(body, grid=(T // tm, H),
                          in_specs=[col, pl.BlockSpec((tm, LANE), lambda i, h: (i, GATE_COL // LANE + h)),
                                    pl.BlockSpec((1, LANE), lambda i, h: (0, 0))],
                          out_specs=col, out_shape=jax.ShapeDtypeStruct((T, H * LANE), _MXU_DTYPE),
                          compiler_params=_cp(("parallel", "parallel")), name="gdn_post_fwd")(o, proj, onorm)


def _gdn_post_bwd(d_cat, o, proj, onorm, T):
    H = N_LIN_HEADS
    tm = _pick(T, 1024)

    def body(dy_ref, o_ref, g_ref, n_ref, do_ref, dg_ref, dn_ref):
        _, vjp = jax.vjp(_post_fn, o_ref[...], g_ref[...], n_ref[...])
        do, dg, dn = vjp(dy_ref[...].astype(F32))
        do_ref[...] = do
        dg_ref[...] = dg.astype(dg_ref.dtype)

        @pl.when((pl.program_id(0) == 0) & (pl.program_id(1) == 0))
        def _():
            dn_ref[...] = jnp.zeros_like(dn_ref)

        dn_ref[...] += dn

    col = pl.BlockSpec((tm, LANE), lambda i, h: (i, h))
    vec = pl.BlockSpec((1, LANE), lambda i, h: (0, 0))
    return pl.pallas_call(body, grid=(T // tm, H),
                          in_specs=[col, col, pl.BlockSpec((tm, LANE), lambda i, h: (i, GATE_COL // LANE + h)), vec],
                          out_specs=(col, col, vec),
                          out_shape=(jax.ShapeDtypeStruct((T, H * LANE), F32),
                                     jax.ShapeDtypeStruct((T, H * LANE), _MXU_DTYPE),
                                     jax.ShapeDtypeStruct((1, LANE), F32)),
                          compiler_params=_cp(("arbitrary", "arbitrary")), name="gdn_post_bwd")(d_cat, o, proj, onorm)


def _log_sigmoid(z):
    return jnp.minimum(z, 0.0) - jnp.log(1.0 + jnp.exp(-jnp.abs(z)))


def _split_dot(x, m):
    hi = x.astype(_MXU_DTYPE)
    lo = x - hi.astype(F32)
    return _dot(hi, m) + _dot(lo, m)


SB_QB = 256
SB_RC = 512


def _sb_scores(q2, k_j, scale, valid):
    z = _dot(q2, k_j, 1, 1) * scale
    lb = _log_sigmoid(z)
    return lb, jnp.where(valid, lb - z, 0.0)


def _sb_consts(QB):
    ri = lax.broadcasted_iota(jnp.int32, (SB_BLOCK, SB_BLOCK), 0)
    ci = lax.broadcasted_iota(jnp.int32, (SB_BLOCK, SB_BLOCK), 1)
    rc = min(SB_RC, 2 * QB)
    assert rc == 2 * QB or QB % rc == 0
    row = lax.broadcasted_iota(jnp.int32, (rc, SB_BLOCK), 0)
    col = lax.broadcasted_iota(jnp.int32, (rc, SB_BLOCK), 1)
    lane = lax.broadcasted_iota(jnp.int32, (1, LANE), 1)
    return {
        "rc": rc,
        "chunks": [(ch * rc, (ch * rc) % QB) for ch in range(2 * QB // rc)],
        "row_minus_col": row % QB - col,
        "after_excl": (ri > ci).astype(_MXU_DTYPE),
        "upto_incl": (ri <= ci).astype(_MXU_DTYPE),
        "upto_excl": (ri < ci).astype(_MXU_DTYPE),
        "lane": lane,
        "head0": lane < SB_HEAD_DIM,
    }


def _sb_stack(x, c):
    return jnp.concatenate([jnp.where(c["head0"], x, 0.0), jnp.where(c["head0"], 0.0, x)], axis=0)


def _sb_unstack(x2, c, QB):
    return jnp.where(c["head0"], x2[:QB], x2[QB:])


SB_DEAD = -110.0


def _sb_first_shape(S, QB):
    return (-(-(S // QB) // 8) * 8, LANE)


def _sb_fwd(proj, B, S):
    W = SEQ_MIX_WIDTH
    P = W // LANE
    QB = min(SB_QB, S)
    KB = SB_BLOCK
    scale = SB_HEAD_DIM ** -0.5
    fshape = _sb_first_shape(S, QB)

    def body(q_ref, k_ref, v_ref, o_ref, tot_ref, first_ref):
        c = _sb_consts(QB)
        frow = lax.broadcasted_iota(jnp.int32, fshape, 0)

        def q_loop(i, first):
            qrows = pl.ds(pl.multiple_of(i * QB, QB), QB)
            q2 = _sb_stack(q_ref[qrows, :].astype(F32), c)
            nkb = (i + 1) * (QB // KB)

            def scores(j):
                krows = pl.ds(pl.multiple_of(j * KB, KB), KB)
                valid = c["row_minus_col"] > j * KB - i * QB
                lb, l1 = _sb_scores(q2, k_ref[krows, :], scale, valid)
                return jnp.where(valid, lb, -1e30), l1

            def k_cond(st):
                return (st[0] < nkb) & (st[1] > 0)

            def k_body(st):
                t, _, acc, r, lbm, l1 = st
                j = nkb - 1 - t
                nxt = scores(jnp.maximum(j - 1, 0))
                krows = pl.ds(pl.multiple_of(j * KB, KB), KB)
                a = jnp.exp(lbm + r + _split_dot(l1, c["after_excl"]))
                r = r + jnp.sum(l1, axis=-1, keepdims=True)
                alive = (jnp.max(r) > SB_DEAD).astype(jnp.int32)
                return (t + 1, alive, acc + _dot(a, v_ref[krows, :]), r) + nxt

            t, _, acc, r, _, _ = lax.while_loop(
                k_cond, k_body, (jnp.int32(0), jnp.int32(1), jnp.zeros((2 * QB, LANE), F32), jnp.zeros((2 * QB, 1), F32))
                + scores(nkb - 1))
            o_ref[qrows, :] = _sb_unstack(acc, c, QB)
            tot_ref[qrows, :] = _sb_unstack(jnp.broadcast_to(r, (2 * QB, LANE)), c, QB)
            return jnp.where(frow == i, (nkb - t).astype(F32), first)

        first_ref[...] = lax.fori_loop(0, S // QB, q_loop, jnp.zeros(fshape, F32))

    def col(off):
        return pl.BlockSpec((S, LANE), lambda b, p: (b, off + p))

    out = jax.ShapeDtypeStruct((B * S, W), F32)
    return pl.pallas_call(body, grid=(B, P), in_specs=[col(0), col(P), col(2 * P)],
                          out_specs=(col(0), col(0), pl.BlockSpec((None, None) + fshape, lambda b, p: (b, p, 0, 0))),
                          out_shape=(out, out, jax.ShapeDtypeStruct((B, P) + fshape, F32)),
                          compiler_params=_cp(("parallel", "parallel")), name="sb_fwd")(proj, proj, proj)


def _sb_bwd(proj, tot, first, d_cat, B, S):
    W = SEQ_MIX_WIDTH
    P = W // LANE
    QB = min(SB_QB, S)
    KB = SB_BLOCK
    scale = SB_HEAD_DIM ** -0.5

    fshape = _sb_first_shape(S, QB)

    def body(q_ref, k_ref, v_ref, tot_ref, first_ref, do_ref, dq_ref, dk_ref, dv_ref, dk_acc, dv_acc):
        c = _sb_consts(QB)
        frow = lax.broadcasted_iota(jnp.int32, fshape, 0)
        dk_acc[...] = jnp.zeros_like(dk_acc)
        dv_acc[...] = jnp.zeros_like(dv_acc)

        def q_loop(i, carry):
            qrows = pl.ds(pl.multiple_of(i * QB, QB), QB)
            q2 = _sb_stack(q_ref[qrows, :].astype(F32), c)
            do2 = _sb_stack(do_ref[qrows, :].astype(F32), c)
            q2_t = q2.T.astype(_MXU_DTYPE)
            do2_t = do2.T.astype(_MXU_DTYPE)
            tot = tot_ref[qrows, :]
            total = jnp.concatenate(
                [jnp.sum(jnp.where(c["lane"] == h * SB_HEAD_DIM, tot, 0.0), axis=-1, keepdims=True) for h in range(2)],
                axis=0)

            nkb = (i + 1) * (QB // KB)
            j0 = jnp.clip(jnp.max(jnp.where(frow == i, first_ref[...], 0.0)).astype(jnp.int32), 0, nkb - 1)

            def scores(j):
                krows = pl.ds(pl.multiple_of(j * KB, KB), KB)
                valid = c["row_minus_col"] > j * KB - i * QB
                lb, l1 = _sb_scores(q2, k_ref[krows, :], scale, valid)
                return jnp.where(valid, lb, -1e30), l1, _dot(do2, v_ref[krows, :], 1, 1)

            def k_loop(j, st):
                dq_acc, p_l1, p_g, lbm, l1, da = st
                nxt = scores(jnp.minimum(j + 1, nkb - 1))
                krows = pl.ds(pl.multiple_of(j * KB, KB), KB)
                tail = total - p_l1 - _split_dot(l1, c["upto_incl"])
                a = jnp.exp(lbm + tail)
                g = da * a
                g_before = p_g + _dot(g, c["upto_excl"])
                sig = jnp.exp(lbm)
                dz = ((g * (1.0 - sig) - g_before * sig) * scale).astype(_MXU_DTYPE)
                dk_acc[j] += _dot(q2_t, dz)
                dv_acc[j] += _dot(do2_t, a)
                return (dq_acc + _dot(dz, k_ref[krows, :]), p_l1 + jnp.sum(l1, axis=-1, keepdims=True),
                        p_g + jnp.sum(g, axis=-1, keepdims=True)) + nxt

            zero_col = jnp.zeros((2 * QB, 1), F32)
            dq2 = lax.fori_loop(j0, nkb, k_loop, (jnp.zeros((2 * QB, LANE), F32), zero_col, zero_col) + scores(j0))[0]
            dq_ref[qrows, :] = _sb_unstack(dq2, c, QB).astype(dq_ref.dtype)
            return carry

        lax.fori_loop(0, S // QB, q_loop, 0)
        for j in range(S // KB):
            dk_ref[j * KB:(j + 1) * KB, :] = dk_acc[j].T.astype(dk_ref.dtype)
            dv_ref[j * KB:(j + 1) * KB, :] = dv_acc[j].T.astype(dv_ref.dtype)

    def col(off):
        return pl.BlockSpec((S, LANE), lambda b, p: (b, off + p))

    out = jax.ShapeDtypeStruct((B * S, W), _MXU_DTYPE)
    return pl.pallas_call(body, grid=(B, P),
                          in_specs=[col(0), col(P), col(2 * P), col(0),
                                    pl.BlockSpec((None, None) + fshape, lambda b, p: (b, p, 0, 0)), col(0)],
                          out_specs=(col(0),) * 3, out_shape=(out,) * 3,
                          scratch_shapes=[pltpu.VMEM((S // KB, LANE, KB), F32), pltpu.VMEM((S // KB, LANE, KB), F32)],
                          compiler_params=_cp(("parallel", "parallel")), name="sb_bwd")(proj, proj, proj, tot, first, d_cat)


def _mem_fn(q, k, v):
    lane = lax.broadcasted_iota(jnp.int32, (1, X_WIDTH), 1)
    out = jnp.zeros(q.shape, F32)
    for h in range(N_X_HEADS):
        hm = (lane // X_HEAD_DIM) == h
        s = _dot(jnp.where(hm, q, 0.0), k, 1, 1) * (X_HEAD_DIM ** -0.5)
        e = jnp.exp(s - lax.stop_gradient(jnp.max(s, axis=-1, keepdims=True)))
        p = e / jnp.sum(e, axis=-1, keepdims=True)
        out = out + jnp.where(hm, _dot(p, v), 0.0)
    return out


def _mem_specs(S, q_col):
    ts = _pick(S, 1024)
    ns = S // ts
    qs = pl.BlockSpec((ts, X_WIDTH), lambda b, i: (b * ns + i, q_col // X_WIDTH))
    ks = pl.BlockSpec((N_MEM, X_WIDTH), lambda b, i: (b, 0))
    vs = pl.BlockSpec((N_MEM, X_WIDTH), lambda b, i: (b, 1))
    os = pl.BlockSpec((ts, X_WIDTH), lambda b, i: (b * ns + i, 0))
    return ts, ns, qs, ks, vs, os


def _mem_fwd(proj, q_col, mem_kv, B, S, name):
    ts, ns, qs, ks, vs, os = _mem_specs(S, q_col)

    def body(q_ref, k_ref, v_ref, o_ref):
        o_ref[...] = _mem_fn(q_ref[...].astype(F32), k_ref[...].astype(F32), v_ref[...].astype(F32))

    return pl.pallas_call(body, grid=(B, ns), in_specs=[qs, ks, vs], out_specs=os,
                          out_shape=jax.ShapeDtypeStruct((B * S, X_WIDTH), F32),
                          compiler_params=_cp(("parallel", "parallel")), name=name)(proj, mem_kv, mem_kv)


def _mem_bwd(proj, q_col, mem_kv, d_cat, B, S, name):
    ts, ns, qs, ks, vs, os = _mem_specs(S, q_col)

    def body(q_ref, k_ref, v_ref, do_ref, dq_ref, dk_ref, dv_ref):
        _, vjp = jax.vjp(_mem_fn, q_ref[...].astype(F32), k_ref[...].astype(F32), v_ref[...].astype(F32))
        dq, dk, dv = vjp(do_ref[...].astype(F32))
        dq_ref[...] = dq.astype(dq_ref.dtype)

        @pl.when(pl.program_id(1) == 0)
        def _():
            dk_ref[...] = jnp.zeros_like(dk_ref)
            dv_ref[...] = jnp.zeros_like(dv_ref)

        dk_ref[...] += dk
        dv_ref[...] += dv

    dos = pl.BlockSpec((ts, X_WIDTH), lambda b, i: (b * ns + i, SEQ_MIX_WIDTH // X_WIDTH))
    dq, dk, dv = pl.pallas_call(
        body, grid=(B, ns), in_specs=[qs, ks, vs, dos],
        out_specs=(os, pl.BlockSpec((N_MEM, X_WIDTH), lambda b, i: (b, 0)), pl.BlockSpec((N_MEM, X_WIDTH), lambda b, i: (b, 0))),
        out_shape=(jax.ShapeDtypeStruct((B * S, X_WIDTH), _MXU_DTYPE),
                   jax.ShapeDtypeStruct((B * N_MEM, X_WIDTH), F32), jax.ShapeDtypeStruct((B * N_MEM, X_WIDTH), F32)),
        compiler_params=_cp(("parallel", "arbitrary")), name=name)(proj, mem_kv, mem_kv, d_cat)
    return dq, dk, dv


def _peers():
    x, y, c = lax.axis_index("x"), lax.axis_index("y"), lax.axis_index("c")
    me = 4 * x + 2 * y + c
    out = []
    for fx, fy, fc in [(0, 0, 1), (1, 0, 0), (0, 1, 0), (1, 1, 0), (1, 0, 1), (0, 1, 1), (1, 1, 1)]:
        px, py, pc = x ^ fx, y ^ fy, c ^ fc
        out.append(((px, py, pc), 4 * px + 2 * py + pc))
    return me, out


ANY = pl.BlockSpec(memory_space=pl.ANY)


def _remote(src, dst, send_sems, recv_sems, k, dev):
    return pltpu.make_async_remote_copy(src_ref=src, dst_ref=dst, send_sem=send_sems.at[k], recv_sem=recv_sems.at[k],
                                        device_id=dev, device_id_type=pl.DeviceIdType.MESH)


def _place():
    x, y, c = lax.axis_index("x"), lax.axis_index("y"), lax.axis_index("c")
    return x, y, c, [(1 - x, y), (x, 1 - y), (1 - x, 1 - y)]


def _all_gather(shard):
    R = shard.shape[0]

    def body(x_ref, o_ref, send_sems, recv_sems, local_sem):
        x, y, c, chips = _place()
        me, sibling = (x, y, c), (x, y, 1 - c)

        def slot(px, py, pc):
            return o_ref.at[4 * px + 2 * py + pc]

        def copy(k, block, to, src=None):
            return _remote(slot(*block) if src is None else src, slot(*block), send_sems, recv_sems, k, to)

        mine = pltpu.make_async_copy(x_ref, slot(*me), local_sem)
        mine.start()
        first = [copy(0, me, sibling, src=x_ref)] + [copy(1 + j, me, (*chip, c), src=x_ref) for j, chip in enumerate(chips)]
        for cp in first:
            cp.start()
        passed = [copy(4 + j, (*chip, c), sibling) for j, chip in enumerate(chips)]
        for j, chip in enumerate(chips):
            copy(1 + j, (*chip, c), me).wait_recv()
            passed[j].start()
        copy(0, sibling, me).wait_recv()
        for j, chip in enumerate(chips):
            copy(4 + j, (*chip, 1 - c), me).wait_recv()
        for cp in first + passed:
            cp.wait_send()
        mine.wait()

    return pl.pallas_call(body, in_specs=[ANY], out_specs=ANY,
                          out_shape=jax.ShapeDtypeStruct((N_DEV, R, LANE), shard.dtype),
                          scratch_shapes=[pltpu.SemaphoreType.DMA((7,)), pltpu.SemaphoreType.DMA((7,)),
                                          pltpu.SemaphoreType.DMA],
                          compiler_params=pltpu.CompilerParams(has_side_effects=True),
                          name="all_gather_weights")(shard)


N_CHIP = 4


def _exchange_sibling(big):
    R = big.shape[1]

    def body(b_ref, o_ref, send_sems, recv_sems):
        x, y, c, _ = _place()
        copies = [_remote(b_ref.at[2 * k + (1 - c)], o_ref.at[k], send_sems, recv_sems, k, (x, y, 1 - c))
                  for k in range(N_CHIP)]
        for cp in copies:
            cp.start()
        for cp in copies:
            cp.wait()

    return pl.pallas_call(body, in_specs=[ANY], out_specs=ANY,
                          out_shape=jax.ShapeDtypeStruct((N_CHIP, R, LANE), big.dtype),
                          scratch_shapes=[pltpu.SemaphoreType.DMA((N_CHIP,)), pltpu.SemaphoreType.DMA((N_CHIP,))],
                          compiler_params=pltpu.CompilerParams(has_side_effects=True),
                          name="exchange_sibling")(big)


def _partial_sum(g4, recv):
    R = g4.shape[2]
    tr = _pick(R, 1024)

    def body(g_ref, r_ref, pw_ref, po_ref):
        x, y, c, _ = _place()
        g = jnp.where(c == 0, g_ref[0], g_ref[1]).astype(F32) + r_ref[...].astype(F32)
        pw_ref[...] = g.astype(pw_ref.dtype)

        @pl.when(pl.program_id(1) == 2 * x + y)
        def _():
            po_ref[...] = g

    return pl.pallas_call(body, grid=(R // tr, N_CHIP),
                          in_specs=[pl.BlockSpec((None, 2, tr, LANE), lambda i, k: (k, 0, i, 0)),
                                    pl.BlockSpec((None, tr, LANE), lambda i, k: (k, i, 0))],
                          out_specs=(pl.BlockSpec((None, tr, LANE), lambda i, k: (k, i, 0)),
                                     pl.BlockSpec((tr, LANE), lambda i, k: (i, 0))),
                          out_shape=(jax.ShapeDtypeStruct((N_CHIP, R, LANE), recv.dtype),
                                     jax.ShapeDtypeStruct((R, LANE), F32)),
                          compiler_params=_cp(("parallel", "arbitrary")), name="partial_sum")(g4, recv)


def _exchange_chips(part, small):
    R = part.shape[1]
    K = small.shape[0]

    def body(p_ref, s_ref, ob_ref, os_ref, send_sems, recv_sems, local_sems):
        x, y, c, chips = _place()
        my_chip = 2 * x + y
        me, peers = _peers()
        own_b = pltpu.make_async_copy(p_ref.at[my_chip], ob_ref.at[my_chip], local_sems.at[0])
        own_s = pltpu.make_async_copy(s_ref, os_ref.at[me], local_sems.at[1])
        own_b.start()
        own_s.start()
        copies = [_remote(p_ref.at[2 * px + py], ob_ref.at[my_chip], send_sems, recv_sems, j, (px, py, c))
                  for j, (px, py) in enumerate(chips)]
        copies += [_remote(s_ref, os_ref.at[me], send_sems, recv_sems, 3 + k, dev) for k, (dev, _) in enumerate(peers)]
        for cp in copies:
            cp.start()
        for j, (px, py) in enumerate(chips):
            _remote(p_ref.at[my_chip], ob_ref.at[2 * px + py], send_sems, recv_sems, j, (px, py, c)).wait_recv()
        for k, (dev, idx) in enumerate(peers):
            _remote(s_ref, os_ref.at[idx], send_sems, recv_sems, 3 + k, dev).wait_recv()
        for cp in copies:
            cp.wait_send()
        own_b.wait()
        own_s.wait()

    return pl.pallas_call(body, in_specs=[ANY, ANY], out_specs=(ANY, ANY),
                          out_shape=(jax.ShapeDtypeStruct((N_CHIP, R, LANE), part.dtype),
                                     jax.ShapeDtypeStruct((N_DEV, K, LANE), small.dtype)),
                          scratch_shapes=[pltpu.SemaphoreType.DMA((10,)), pltpu.SemaphoreType.DMA((10,)),
                                          pltpu.SemaphoreType.DMA((2,))],
                          compiler_params=pltpu.CompilerParams(has_side_effects=True),
                          name="exchange_chips")(part, small)


def _adamw_math(w, g, m, v):
    m = ADAM_B1 * m + (1.0 - ADAM_B1) * g
    v = ADAM_B2 * v + (1.0 - ADAM_B2) * (g * g)
    m_hat = m / (1.0 - ADAM_B1 ** ADAM_STEP)
    v_hat = v / (1.0 - ADAM_B2 ** ADAM_STEP)
    delta = -ADAM_LR * (m_hat / (jnp.sqrt(v_hat) + ADAM_EPS) + ADAM_WD * w)
    return delta, m, v


def _adamw_shard(own, recv, w, m, v):
    R = own.shape[0]
    tr = _pick(R, 1024)

    def body(own_ref, recv_ref, w_ref, m_ref, v_ref, g_ref, d_ref, nm_ref, nv_ref):
        x, y, _, _ = _place()
        g = own_ref[...]
        for k in range(N_CHIP):
            g = g + jnp.where(k == 2 * x + y, 0.0, recv_ref[k].astype(F32))
        delta, nm, nv = _adamw_math(w_ref[...], g, m_ref[...], v_ref[...])
        g_ref[...] = g
        d_ref[...] = delta
        nm_ref[...] = nm
        nv_ref[...] = nv

    row = pl.BlockSpec((tr, LANE), lambda i: (i, 0))
    out = jax.ShapeDtypeStruct((R, LANE), F32)
    return pl.pallas_call(body, grid=(R // tr,),
                          in_specs=[row, pl.BlockSpec((N_CHIP, tr, LANE), lambda i: (0, i, 0)), row, row, row],
                          out_specs=(row,) * 4, out_shape=(out,) * 4,
                          compiler_params=_cp(("parallel",)), name="adamw_shard")(own, recv, w, m, v)


def _adamw_replicated(parts, w, m, v):
    K = w.shape[0]

    def body(p_ref, w_ref, m_ref, v_ref, g_ref, d_ref, nm_ref, nv_ref):
        g = p_ref[0]
        for p in range(1, N_DEV):
            g = g + p_ref[p]
        delta, nm, nv = _adamw_math(w_ref[...], g, m_ref[...], v_ref[...])
        g_ref[...] = g
        d_ref[...] = delta
        nm_ref[...] = nm
        nv_ref[...] = nv

    out = jax.ShapeDtypeStruct((K, LANE), F32)
    return pl.pallas_call(body, out_shape=(out,) * 4, compiler_params=_cp(), name="adamw_replicated")(parts, w, m, v)


_SHARDED = (("w_in_a", 1), ("conv_w_a", 2), ("w_in_b", 2), ("w_mem_kv", 1), ("w_out", 1), ("w_up", 2), ("w_down", 1))
_REPLICATED = ("mem_norm", "norm_pre_mix", "norm_post_mix", "norm_pre_mlp", "norm_post_mlp", "a_log_a", "dt_bias_a", "onorm_a")
ROW_ALIGN = 16


def _rows(n_elems, align=ROW_ALIGN):
    r = -(-n_elems // LANE)
    return -(-r // align) * align


def _pack(arrays, align=ROW_ALIGN, total=None, lead=()):
    parts = []
    for a in arrays:
        n = math.prod(a.shape[len(lead):])
        flat = a.reshape(lead + (n,))
        r = _rows(n, align)
        flat = jnp.pad(flat, [(0, 0)] * len(lead) + [(0, r * LANE - n)])
        parts.append(flat.reshape(lead + (r, LANE)))
    used = sum(part.shape[len(lead)] for part in parts)
    if total is not None and used < total:
        parts.append(jnp.zeros(lead + (total - used, LANE), parts[0].dtype))
    return jnp.concatenate(parts, axis=len(lead))


def _unpack(flat, shapes, align=ROW_ALIGN, lead=()):
    outs = []
    r0 = 0
    for shp in shapes:
        n = math.prod(shp)
        r = _rows(n, align)
        part = lax.slice_in_dim(flat, r0, r0 + r, axis=len(lead))
        part = part.reshape(lead + (r * LANE,))
        part = lax.slice_in_dim(part, 0, n, axis=len(lead))
        outs.append(part.reshape(lead + tuple(shp)))
        r0 += r
    return outs


def _to_full(gathered, axis):
    g = jnp.moveaxis(gathered, 0, axis)
    shp = g.shape
    return g.reshape(shp[:axis] + (shp[axis] * shp[axis + 1],) + shp[axis + 2:])


def _to_blocks(full, axis):
    shp = full.shape
    g = full.reshape(shp[:axis] + (N_DEV, shp[axis] // N_DEV) + shp[axis + 1:])
    return jnp.moveaxis(g, axis, 0)


def _widen_in_a(w):
    main = w[:, :4 * SEQ_MIX_WIDTH]
    small = w[:, 4 * SEQ_MIX_WIDTH:4 * SEQ_MIX_WIDTH + 2 * N_LIN_HEADS]
    memq = w[:, 4 * SEQ_MIX_WIDTH + 2 * N_LIN_HEADS:]
    pad = jnp.zeros((w.shape[0], IN_A_PAD - IN_A), w.dtype)
    return jnp.concatenate([main, memq, small, pad], axis=1)


def _narrow_in_a(g):
    main = g[:, :4 * SEQ_MIX_WIDTH]
    memq = g[:, 4 * SEQ_MIX_WIDTH:4 * SEQ_MIX_WIDTH + X_WIDTH]
    small = g[:, SM_COL:SM_COL + 2 * N_LIN_HEADS]
    return jnp.concatenate([main, small, memq], axis=1)


def _row128(v):
    return jnp.pad(v.reshape(1, -1), ((0, 0), (0, LANE - v.shape[-1])))


def _local_step(x, mem, target, p):
    B, S, D = x.shape
    T = B * S
    md = _MXU_DTYPE
    x0 = x.reshape(T, D)
    tgt = target.reshape(T, D)
    memf = mem.reshape(B * N_MEM, D)
    vec = lambda a: a.reshape(1, -1)

    mem_n = _norm_fwd(memf, vec(p["mem_norm"]), out_dtype=md, name="norm_mem")
    w_in = [_widen_in_a(p["w_in_a"][0]), p["w_in_b"][0]]
    memq_col = [4 * SEQ_MIX_WIDTH, 3 * SEQ_MIX_WIDTH]
    alog = _row128(p["a_log_a"][0])
    dtb = _row128(p["dt_bias_a"][0])
    onorm = vec(p["onorm_a"][0])
    conv_w = p["conv_w_a"][0]
    saved = []
    xi = x0
    for i in range(2):
        s = {"x_in": xi}
        h1 = _norm_fwd(xi, vec(p["norm_pre_mix"][i]), out_dtype=md, name=f"norm_pre_mix{i}")
        proj = _mm(h1, w_in[i], out_dtypes=(F32 if i == 0 else md,), name=f"in_proj{i}")
        mem_kv = _mm(mem_n, p["w_mem_kv"][i], out_dtypes=(md,), name=f"mem_kv{i}")
        if i == 0:
            act = _gdn_conv_fwd(proj, conv_w, B, S)
            beta, gc = _gdn_gates_fwd(proj, alog, dtb, B, S)
            u, w, qd, kd, intra = _gdn_prep_fwd(act, beta, gc, B, S)
            o, states = _gdn_scan_fwd(u, w, qd, kd, intra, gc, B, S)
            mix = _gdn_post_fwd(o, proj, onorm, T)
            s.update(act=act, beta=beta, gc=gc, u=u, w=w, qd=qd, kd=kd, intra=intra, o=o, states=states)
        else:
            mix, tot, first = _sb_fwd(proj, B, S)
            s.update(tot=tot, first=first)
        cross = _mem_fwd(proj, memq_col[i], mem_kv, B, S, name=f"mem_fwd{i}")
        cat = jnp.concatenate([mix.astype(md), cross.astype(md)], axis=1)
        y = _mm(cat, p["w_out"][i], name=f"out_proj{i}")
        x_mid = _norm_fwd(y, vec(p["norm_post_mix"][i]), resid=xi, name=f"norm_post_mix{i}")
        h2 = _norm_fwd(x_mid, vec(p["norm_pre_mlp"][i]), out_dtype=md, name=f"norm_pre_mlp{i}")
        a_act, r = _mm(h2, p["w_up"][i], out_dtypes=(md, md), epilogue=_relu2_epilogue, name=f"up_proj{i}")
        y2 = _mm(a_act, p["w_down"][i], name=f"down_proj{i}")
        x_out = _norm_fwd(y2, vec(p["norm_post_mlp"][i]), resid=x_mid, name=f"norm_post_mlp{i}")
        s.update(h1=h1, proj=proj, mem_kv=mem_kv, cat=cat, y=y, x_mid=x_mid, h2=h2, a_act=a_act, r=r, y2=y2)
        saved.append(s)
        xi = x_out

    loss_row, dx = _loss_head(xi, tgt)

    g = {}
    d_mem_n = None
    gn = {k: [None, None] for k in ("norm_pre_mix", "norm_post_mix", "norm_pre_mlp", "norm_post_mlp")}
    g_w_mem_kv, g_w_out, g_w_up, g_w_down = [None, None], [None, None], [None, None], [None, None]
    for i in (1, 0):
        s = saved[i]
        d_y2, gn["norm_post_mlp"][i] = _norm_bwd(dx, s["y2"], vec(p["norm_post_mlp"][i]), name=f"norm_post_mlp_bwd{i}")
        g_w_down[i] = _mm(s["a_act"], d_y2, ta=True, out_dtypes=(_WIRE_DTYPE,), name=f"down_proj_dw{i}")
        d_u = _mm(d_y2, p["w_down"][i], tb=True, out_dtypes=(md,), epilogue=_drelu2_epilogue, extras=(s["r"],),
                  name=f"down_proj_dx{i}")
        g_w_up[i] = _mm(s["h2"], d_u, ta=True, out_dtypes=(_WIRE_DTYPE,), name=f"up_proj_dw{i}")
        d_h2 = _mm(d_u, p["w_up"][i], tb=True, name=f"up_proj_dx{i}")
        dx, gn["norm_pre_mlp"][i] = _norm_bwd(d_h2, s["x_mid"], vec(p["norm_pre_mlp"][i]), resid=dx,
                                              name=f"norm_pre_mlp_bwd{i}")
        d_y, gn["norm_post_mix"][i] = _norm_bwd(dx, s["y"], vec(p["norm_post_mix"][i]), name=f"norm_post_mix_bwd{i}")
        g_w_out[i] = _mm(s["cat"], d_y, ta=True, out_dtypes=(_WIRE_DTYPE,), name=f"out_proj_dw{i}")
        d_cat = _mm(d_y, p["w_out"][i], tb=True, name=f"out_proj_dx{i}")
        d_memq, d_mk, d_mv = _mem_bwd(s["proj"], memq_col[i], s["mem_kv"], d_cat, B, S, name=f"mem_bwd{i}")
        d_mem_kv = jnp.concatenate([d_mk.astype(md), d_mv.astype(md)], axis=1)
        g_w_mem_kv[i] = _mm(mem_n, d_mem_kv, ta=True, out_dtypes=(_WIRE_DTYPE,), name=f"mem_kv_dw{i}")
        d_mn = _mm(d_mem_kv, p["w_mem_kv"][i], tb=True, name=f"mem_kv_dx{i}")
        d_mem_n = d_mn if d_mem_n is None else d_mem_n + d_mn
        if i == 0:
            d_o, d_gate, g["onorm_a"] = _gdn_post_bwd(d_cat, s["o"], s["proj"], onorm, T)
            du, dw, dqd, dkd, da, dgc_s = _gdn_scan_bwd(s["u"], s["w"], s["qd"], s["kd"], s["intra"], s["gc"],
                                                         s["states"], d_o, B, S)
            dq, dk, dv, d_beta, d_gc = _gdn_prep_bwd(s["act"], s["beta"], s["gc"], du, dw, dqd, dkd, da, B, S)
            d_qkv, g["conv_w_a"] = _gdn_conv_bwd(dq, dk, dv, s["proj"], conv_w, B, S)
            d_sm, g["a_log_a"], g["dt_bias_a"] = _gdn_gates_bwd(d_beta, d_gc + dgc_s, s["proj"], alog, dtb, B, S)
            pad = jnp.zeros((T, IN_A_PAD - SM_COL - LANE), md)
            d_proj = jnp.concatenate([d_qkv, d_gate, d_memq, d_sm, pad], axis=1)
        else:
            dq, dk, dv = _sb_bwd(s["proj"], s["tot"], s["first"], d_cat, B, S)
            d_proj = jnp.concatenate([dq, dk, dv, d_memq], axis=1)
        g_w_in = _mm(s["h1"], d_proj, ta=True, out_dtypes=(_WIRE_DTYPE,), name=f"in_proj_dw{i}")
        d_h1 = _mm(d_proj, w_in[i], tb=True, name=f"in_proj_dx{i}")
        dx, gn["norm_pre_mix"][i] = _norm_bwd(d_h1, s["x_in"], vec(p["norm_pre_mix"][i]), resid=dx,
                                              name=f"norm_pre_mix_bwd{i}")
        if i == 0:
            g["w_in_a"] = _narrow_in_a(g_w_in)[None]
        else:
            g["w_in_b"] = g_w_in[None]
    _, g_mem_norm = _norm_bwd(d_mem_n, memf, vec(p["mem_norm"]), name="norm_mem_bwd")
    g["mem_norm"] = g_mem_norm.reshape(-1)
    for k, v in gn.items():
        g[k] = jnp.concatenate(v, axis=0)
    g["w_mem_kv"] = jnp.stack(g_w_mem_kv)
    g["w_out"] = jnp.stack(g_w_out)
    g["w_up"] = jnp.stack(g_w_up)
    g["w_down"] = jnp.stack(g_w_down)
    g["conv_w_a"] = g["conv_w_a"][None]
    g["a_log_a"] = g["a_log_a"][:, :N_LIN_HEADS]
    g["dt_bias_a"] = g["dt_bias_a"][:, :N_LIN_HEADS]
    return loss_row, dx.reshape(B, S, D), g


def kernel(x, mem, mem_norm, norm_pre_mix, norm_post_mix, norm_pre_mlp, norm_post_mlp, w_in_a, conv_w_a, a_log_a, dt_bias_a, onorm_a, w_in_b, w_mem_kv, w_out, w_up, w_down, loss_target, m_mem_norm, m_norm_pre_mix, m_norm_post_mix, m_norm_pre_mlp, m_norm_post_mlp, m_w_in_a, m_conv_w_a, m_a_log_a, m_dt_bias_a, m_onorm_a, m_w_in_b, m_w_mem_kv, m_w_out, m_w_up, m_w_down, v_mem_norm, v_norm_pre_mix, v_norm_post_mix, v_norm_pre_mlp, v_norm_post_mlp, v_w_in_a, v_conv_w_a, v_a_log_a, v_dt_bias_a, v_onorm_a, v_w_in_b, v_w_mem_kv, v_w_out, v_w_up, v_w_down):
    args = dict(locals())
    big_names = [n for n, _ in _SHARDED]
    axes = dict(_SHARDED)
    shard_shapes = [args[n].shape for n in big_names]
    rows_total = -(-sum(_rows(math.prod(s)) for s in shard_shapes) // 1024) * 1024

    exact = ("conv_w_a",) if _WIRE_DTYPE != F32 else ()
    wire = [lax.bitcast_convert_type(args[n], _WIRE_DTYPE) if n in exact else args[n].astype(_WIRE_DTYPE)
            for n in big_names]
    wire_shapes = [a.shape for a in wire]
    wire_rows = -(-sum(_rows(math.prod(s)) for s in wire_shapes) // ROW_ALIGN) * ROW_ALIGN
    gathered = _all_gather(_pack(wire, total=wire_rows))
    p = {}
    for n, b in zip(big_names, _unpack(gathered, wire_shapes, lead=(N_DEV,))):
        b = lax.bitcast_convert_type(b, F32) if n in exact else b.astype(_MXU_DTYPE)
        p[n] = _to_full(b, axes[n])
    for n in _REPLICATED:
        p[n] = args[n]
    w_flat = _pack([args[n] for n in big_names], total=rows_total)

    loss_row, grad_x, g = _local_step(x, mem, loss_target, p)
    loss = lax.psum(loss_row[0, 0], ("x", "y", "c"))

    g_blocks = _pack([_to_blocks(g[n], axes[n]).astype(_WIRE_DTYPE) for n in big_names], total=rows_total, lead=(N_DEV,))
    rep_shapes = [args[n].shape for n in _REPLICATED]
    g_small = _pack([g[n] for n in _REPLICATED], align=8)
    recv_sib = _exchange_sibling(g_blocks)
    part, own = _partial_sum(g_blocks.reshape(N_CHIP, 2, rows_total, LANE), recv_sib)
    recv_big, recv_small = _exchange_chips(part, g_small)

    m_flat = _pack([args["m_" + n] for n in big_names], total=rows_total)
    v_flat = _pack([args["v_" + n] for n in big_names], total=rows_total)
    outs_big = [_unpack(f, shard_shapes) for f in _adamw_shard(own, recv_big, w_flat, m_flat, v_flat)]
    outs_small = [_unpack(f, rep_shapes, align=8) for f in _adamw_replicated(
        recv_small, _pack([args[n] for n in _REPLICATED], align=8),
        _pack([args["m_" + n] for n in _REPLICATED], align=8), _pack([args["v_" + n] for n in _REPLICATED], align=8))]

    order = ["mem_norm", "norm_pre_mix", "norm_post_mix", "norm_pre_mlp", "norm_post_mlp", "w_in_a", "conv_w_a",
             "a_log_a", "dt_bias_a", "onorm_a", "w_in_b", "w_mem_kv", "w_out", "w_up", "w_down"]
    result = [loss, grad_x]
    for kind in range(4):
        for n in order:
            if n in axes:
                result.append(outs_big[kind][big_names.index(n)])
            else:
                result.append(outs_small[kind][_REPLICATED.index(n)])
    return tuple(result)
```

```python
import functools
import math

import jax
import jax.numpy as jnp
from jax import lax
from jax.experimental import pallas as pl
from jax.experimental.pallas import tpu as pltpu

F32 = jnp.float32
_MXU_DTYPE = jnp.bfloat16
_WIRE_DTYPE = jnp.bfloat16
_HI = lax.Precision.HIGH

D_MODEL = 1024
N_DEV = 8
N_MEM = 256
X_WIDTH = 256
N_X_HEADS = 4
X_HEAD_DIM = 64
SEQ_MIX_WIDTH = 768
LIN_HEAD_DIM = 128
N_LIN_HEADS = 6
CONV_WIDTH = 4
CHUNK = 64
SB_HEAD_DIM = 64
SB_BLOCK = 128
D_FF = 4096
EPS = 1e-6
IN_A = 3340
IN_A_PAD = 3584
IN_B = 2560
SM_COL = 3328

ADAM_LR = 0.001
ADAM_B1 = 0.9
ADAM_B2 = 0.999
ADAM_EPS = 1e-08
ADAM_WD = 0.01
ADAM_STEP = 10

LANE = 128
VMEM_LIMIT = 56 * 1024 * 1024


def _cp(sem=None):
    return pltpu.CompilerParams(dimension_semantics=sem, vmem_limit_bytes=VMEM_LIMIT)


def _pick(n, target):
    if n <= target:
        return n
    best = None
    for t in range(LANE, target + 1, LANE):
        if n % t == 0:
            best = t
    assert best is not None, (n, target)
    return best


def _dot(a, b, ca=1, cb=0):
    return lax.dot_general(a.astype(_MXU_DTYPE), b.astype(_MXU_DTYPE), (((ca,), (cb,)), ((), ())),
                           preferred_element_type=F32)


def _bdot(a, b, ca, cb):
    return lax.dot_general(a.astype(_MXU_DTYPE), b.astype(_MXU_DTYPE), (((ca,), (cb,)), ((0,), (0,))),
                           preferred_element_type=F32)


def _bdot_hi(a, b):
    return lax.dot_general(a, b, (((2,), (1,)), ((0,), (0,))), precision=_HI, preferred_element_type=F32)


def _sigmoid(x):
    return 1.0 / (1.0 + jnp.exp(-x))


def _silu(x):
    return x * _sigmoid(x)


def _softplus(x):
    return jnp.maximum(x, 0.0) + jnp.log(1.0 + jnp.exp(-jnp.abs(x)))


def _rms(x, g):
    return x * lax.rsqrt(jnp.mean(x * x, axis=-1, keepdims=True) + EPS) * g


def _norm_fwd(x, g, resid=None, out_dtype=F32, name="norm_fwd"):
    T, D = x.shape
    tm = _pick(T, 512)
    has_resid = resid is not None

    def body(*refs):
        if has_resid:
            x_ref, g_ref, r_ref, o_ref = refs
        else:
            x_ref, g_ref, o_ref = refs
        y = _rms(x_ref[...].astype(F32), g_ref[...])
        if has_resid:
            y = r_ref[...] + y
        o_ref[...] = y.astype(out_dtype)

    row = pl.BlockSpec((tm, D), lambda i: (i, 0))
    in_specs = [row, pl.BlockSpec((1, D), lambda i: (0, 0))] + ([row] if has_resid else [])
    args = (x, g) + ((resid,) if has_resid else ())
    return pl.pallas_call(body, grid=(T // tm,), in_specs=in_specs, out_specs=row,
                          out_shape=jax.ShapeDtypeStruct((T, D), out_dtype),
                          compiler_params=_cp(("parallel",)), name=name)(*args)


def _norm_bwd(dy, x, g, resid=None, name="norm_bwd"):
    T, D = x.shape
    tm = _pick(T, 512)
    has_resid = resid is not None

    def body(*refs):
        if has_resid:
            dy_ref, x_ref, g_ref, r_ref, dx_ref, dg_ref = refs
        else:
            dy_ref, x_ref, g_ref, dx_ref, dg_ref = refs
        _, vjp = jax.vjp(_rms, x_ref[...].astype(F32), g_ref[...])
        dx, dg = vjp(dy_ref[...].astype(F32))
        if has_resid:
            dx = r_ref[...] + dx
        dx_ref[...] = dx

        @pl.when(pl.program_id(0) == 0)
        def _():
            dg_ref[...] = jnp.zeros_like(dg_ref)

        dg_ref[...] += dg

    row = pl.BlockSpec((tm, D), lambda i: (i, 0))
    vec = pl.BlockSpec((1, D), lambda i: (0, 0))
    in_specs = [row, row, vec] + ([row] if has_resid else [])
    args = (dy, x, g) + ((resid,) if has_resid else ())
    return pl.pallas_call(body, grid=(T // tm,), in_specs=in_specs, out_specs=(row, vec),
                          out_shape=(jax.ShapeDtypeStruct((T, D), F32), jax.ShapeDtypeStruct((1, D), F32)),
                          compiler_params=_cp(("arbitrary",)), name=name)(*args)


def _norm_pair_fwd(y, g_post, resid, g_pre, out_dtype, name):
    T, D = y.shape
    tm = _pick(T, 512)

    def body(y_ref, gp_ref, r_ref, gn_ref, x_ref, h_ref):
        x = r_ref[...] + _rms(y_ref[...], gp_ref[...])
        x_ref[...] = x
        h_ref[...] = _rms(x, gn_ref[...]).astype(out_dtype)

    row = pl.BlockSpec((tm, D), lambda i: (i, 0))
    vec = pl.BlockSpec((1, D), lambda i: (0, 0))
    return pl.pallas_call(body, grid=(T // tm,), in_specs=[row, vec, row, vec], out_specs=(row, row),
                          out_shape=(jax.ShapeDtypeStruct((T, D), F32), jax.ShapeDtypeStruct((T, D), out_dtype)),
                          compiler_params=_cp(("parallel",)), name=name)(y, g_post, resid, g_pre)


def _norm_pair_bwd(d_h, x, g_pre, dx_resid, y, g_post, name):
    T, D = x.shape
    tm = _pick(T, 512)

    def body(dh_ref, x_ref, gn_ref, r_ref, y_ref, gp_ref, dx_ref, dy_ref, dgn_ref, dgp_ref):
        _, vjp_pre = jax.vjp(_rms, x_ref[...], gn_ref[...])
        dx, dgn = vjp_pre(dh_ref[...].astype(F32))
        dx = r_ref[...] + dx
        _, vjp_post = jax.vjp(_rms, y_ref[...], gp_ref[...])
        dy, dgp = vjp_post(dx)
        dx_ref[...] = dx
        dy_ref[...] = dy

        @pl.when(pl.program_id(0) == 0)
        def _():
            dgn_ref[...] = jnp.zeros_like(dgn_ref)
            dgp_ref[...] = jnp.zeros_like(dgp_ref)

        dgn_ref[...] += dgn
        dgp_ref[...] += dgp

    row = pl.BlockSpec((tm, D), lambda i: (i, 0))
    vec = pl.BlockSpec((1, D), lambda i: (0, 0))
    big = jax.ShapeDtypeStruct((T, D), F32)
    small = jax.ShapeDtypeStruct((1, D), F32)
    return pl.pallas_call(body, grid=(T // tm,), in_specs=[row, row, vec, row, row, vec],
                          out_specs=(row, row, vec, vec), out_shape=(big, big, small, small),
                          compiler_params=_cp(("arbitrary",)), name=name)(d_h, x, g_pre, dx_resid, y, g_post)


def _mm(a, b, *, ta=False, tb=False, out_dtypes=(F32,), epilogue=None, extras=(), name="mm",
        tm_t=1024, tn_t=1024, tk_t=1024):
    M, K = (a.shape[1], a.shape[0]) if ta else a.shape
    N = b.shape[0] if tb else b.shape[1]
    assert (b.shape[1] if tb else b.shape[0]) == K, (a.shape, b.shape, ta, tb)
    tm, tn, tk = _pick(M, tm_t), _pick(N, tn_t), _pick(K, tk_t)
    nk = K // tk
    n_extra = len(extras)
    n_out = len(out_dtypes)

    def body(*refs):
        a_ref, b_ref = refs[0], refs[1]
        e_refs = refs[2:2 + n_extra]
        o_refs = refs[2 + n_extra:2 + n_extra + n_out]

        def finish(acc):
            outs = (acc,) if epilogue is None else epilogue(acc, *[e[...] for e in e_refs])
            for o_ref, o in zip(o_refs, outs):
                o_ref[...] = o.astype(o_ref.dtype)

        d = _dot(a_ref[...], b_ref[...], 0 if ta else 1, 1 if tb else 0)
        if nk == 1:
            finish(d)
            return
        acc_ref = refs[-1]
        k = pl.program_id(2)

        @pl.when(k == 0)
        def _():
            acc_ref[...] = d

        @pl.when((k > 0) & (k < nk - 1))
        def _():
            acc_ref[...] += d

        @pl.when(k == nk - 1)
        def _():
            finish(acc_ref[...] + d)

    a_spec = pl.BlockSpec((tk, tm), lambda i, j, k: (k, i)) if ta else pl.BlockSpec((tm, tk), lambda i, j, k: (i, k))
    b_spec = pl.BlockSpec((tn, tk), lambda i, j, k: (j, k)) if tb else pl.BlockSpec((tk, tn), lambda i, j, k: (k, j))
    o_spec = pl.BlockSpec((tm, tn), lambda i, j, k: (i, j))
    outs = pl.pallas_call(
        body, grid=(M // tm, N // tn, nk),
        in_specs=[a_spec, b_spec] + [o_spec] * n_extra,
        out_specs=tuple([o_spec] * n_out),
        out_shape=tuple(jax.ShapeDtypeStruct((M, N), dt) for dt in out_dtypes),
        scratch_shapes=[pltpu.VMEM((tm, tn), F32)] if nk > 1 else [],
        compiler_params=_cp(("parallel", "parallel", "arbitrary")), name=name)(a, b, *extras)
    return outs[0] if n_out == 1 else outs


def _relu2_epilogue(acc):
    r = jnp.maximum(acc, 0.0)
    return r * r, r


def _drelu2_epilogue(acc, r):
    return (acc * (2.0 * r.astype(F32)),)


def _loss_head(x, target, name="loss_head"):
    T, D = x.shape
    tm = _pick(T, 512)

    def body(x_ref, t_ref, l_ref, dx_ref):
        e = x_ref[...] - t_ref[...]
        dx_ref[...] = e * (1.0 / D)

        @pl.when(pl.program_id(0) == 0)
        def _():
            l_ref[...] = jnp.zeros_like(l_ref)

        part = 0.5 * jnp.sum(jnp.mean(e * e, axis=-1, keepdims=True), axis=0, keepdims=True)
        l_ref[...] += jnp.broadcast_to(part, l_ref.shape)

    row = pl.BlockSpec((tm, D), lambda i: (i, 0))
    return pl.pallas_call(body, grid=(T // tm,), in_specs=[row, row],
                          out_specs=(pl.BlockSpec((1, LANE), lambda i: (0, 0)), row),
                          out_shape=(jax.ShapeDtypeStruct((1, LANE), F32), jax.ShapeDtypeStruct((T, D), F32)),
                          compiler_params=_cp(("arbitrary",)), name=name)(x, target)


def _shift_down(x, k, row):
    return jnp.where(row >= k, pltpu.roll(x, k, 0), 0.0)


def _shift_up(x, k, row, n):
    return jnp.where(row < n - k, pltpu.roll(x, n - k, 0), 0.0)


def _conv_taps(x, w, row):
    y = x * w[CONV_WIDTH - 1:CONV_WIDTH, :]
    for i in range(CONV_WIDTH - 1):
        y = y + _shift_down(x, CONV_WIDTH - 1 - i, row) * w[i:i + 1, :]
    return y


def _qkv_act(xc, j):
    s = _silu(xc)
    n = s * lax.rsqrt(jnp.sum(s * s, axis=-1, keepdims=True) + EPS)
    n = n * jnp.where(j < N_LIN_HEADS, LIN_HEAD_DIM ** -0.5, 1.0)
    return jnp.where(j < 2 * N_LIN_HEADS, n, s)


def _gdn_conv_fwd(proj, conv_w, B, S):
    nblk = 3 * N_LIN_HEADS

    def body(p_ref, w_ref, o_ref):
        j = pl.program_id(1)
        x = p_ref[...]
        row = lax.broadcasted_iota(jnp.int32, x.shape, 0)
        o_ref[...] = _qkv_act(_conv_taps(x, w_ref[...], row), j)

    blk = pl.BlockSpec((S, LANE), lambda b, j: (b, j))
    return pl.pallas_call(body, grid=(B, nblk),
                          in_specs=[blk, pl.BlockSpec((CONV_WIDTH, LANE), lambda b, j: (0, j))],
                          out_specs=blk, out_shape=jax.ShapeDtypeStruct((B * S, nblk * LANE), F32),
                          compiler_params=_cp(("parallel", "parallel")), name="gdn_conv_fwd")(proj, conv_w)


def _gdn_conv_bwd(dq, dk, dv, proj, conv_w, B, S):
    nblk = 3 * N_LIN_HEADS
    H = N_LIN_HEADS

    def body(dq_ref, dk_ref, dv_ref, p_ref, w_ref, dp_ref, dw_ref):
        j = pl.program_id(0)
        b = pl.program_id(1)
        x = p_ref[...]
        w = w_ref[...]
        row = lax.broadcasted_iota(jnp.int32, x.shape, 0)
        d_act = jnp.where(j < H, dq_ref[...], jnp.where(j < 2 * H, dk_ref[...], dv_ref[...]))
        _, vjp = jax.vjp(lambda t: _qkv_act(t, j), _conv_taps(x, w, row))
        (d_xc,) = vjp(d_act)
        dx = d_xc * w[CONV_WIDTH - 1:CONV_WIDTH, :]
        for i in range(CONV_WIDTH - 1):
            dx = dx + _shift_up(d_xc, CONV_WIDTH - 1 - i, row, S) * w[i:i + 1, :]
        dp_ref[...] = dx.astype(dp_ref.dtype)

        @pl.when(b == 0)
        def _():
            dw_ref[...] = jnp.zeros_like(dw_ref)

        for i in range(CONV_WIDTH):
            xs = x if i == CONV_WIDTH - 1 else _shift_down(x, CONV_WIDTH - 1 - i, row)
            dw_ref[i:i + 1, :] += jnp.sum(d_xc * xs, axis=0, keepdims=True)

    blk = pl.BlockSpec((S, LANE), lambda j, b: (b, j))
    wblk = pl.BlockSpec((CONV_WIDTH, LANE), lambda j, b: (0, j))
    return pl.pallas_call(
        body, grid=(nblk, B),
        in_specs=[pl.BlockSpec((S, LANE), lambda j, b: (b, jnp.clip(j, 0, H - 1))),
                  pl.BlockSpec((S, LANE), lambda j, b: (b, jnp.clip(j - H, 0, H - 1))),
                  pl.BlockSpec((S, LANE), lambda j, b: (b, jnp.clip(j - 2 * H, 0, H - 1))),
                  blk, wblk],
        out_specs=(blk, wblk),
        out_shape=(jax.ShapeDtypeStruct((B * S, nblk * LANE), _MXU_DTYPE),
                   jax.ShapeDtypeStruct((CONV_WIDTH, nblk * LANE), F32)),
        compiler_params=_cp(("parallel", "arbitrary")), name="gdn_conv_bwd")(dq, dk, dv, proj, conv_w)


def _chunk_cumsum(x, row):
    pos = row % CHUNK
    k = 1
    while k < CHUNK:
        x = x + jnp.where(pos >= k, pltpu.roll(x, k, 0), 0.0)
        k *= 2
    return x


def _chunk_rev_cumsum(x, row, n):
    pos = row % CHUNK
    k = 1
    while k < CHUNK:
        x = x + jnp.where(pos < CHUNK - k, pltpu.roll(x, n - k, 0), 0.0)
        k *= 2
    return x


def _gdn_gates_fwd(proj, a_log, dt_bias, B, S):
    H = N_LIN_HEADS

    def body(sm_ref, al_ref, dt_ref, beta_ref, gc_ref):
        sm = sm_ref[...]
        row = lax.broadcasted_iota(jnp.int32, (S, LANE), 0)
        for h in range(H):
            beta = _sigmoid(sm[:, h:h + 1])
            g = -jnp.exp(al_ref[0:1, h:h + 1]) * _softplus(sm[:, H + h:H + h + 1] + dt_ref[0:1, h:h + 1])
            beta_ref[:, h * LANE:(h + 1) * LANE] = jnp.broadcast_to(beta, (S, LANE))
            gc_ref[:, h * LANE:(h + 1) * LANE] = _chunk_cumsum(jnp.broadcast_to(g, (S, LANE)), row)

    vec = pl.BlockSpec((1, LANE), lambda b: (0, 0))
    wide = pl.BlockSpec((S, H * LANE), lambda b: (b, 0))
    return pl.pallas_call(body, grid=(B,),
                          in_specs=[pl.BlockSpec((S, LANE), lambda b: (b, SM_COL // LANE)), vec, vec],
                          out_specs=(wide, wide),
                          out_shape=(jax.ShapeDtypeStruct((B * S, H * LANE), F32),) * 2,
                          compiler_params=_cp(("parallel",)), name="gdn_gates_fwd")(proj, a_log, dt_bias)


def _gdn_gates_bwd(d_beta, d_gc, proj, a_log, dt_bias, B, S):
    H = N_LIN_HEADS

    def body(db_ref, dgc_ref, sm_ref, al_ref, dt_ref, dsm_ref, dal_ref, ddt_ref):
        sm = sm_ref[...]
        row = lax.broadcasted_iota(jnp.int32, (S, LANE), 0)
        lane = lax.broadcasted_iota(jnp.int32, (1, LANE), 1)
        dsm = jnp.zeros((S, LANE), F32)
        dal = jnp.zeros((1, LANE), F32)
        ddt = jnp.zeros((1, LANE), F32)
        for h in range(H):
            beta = _sigmoid(sm[:, h:h + 1])
            dbeta = jnp.sum(db_ref[:, h * LANE:(h + 1) * LANE], axis=-1, keepdims=True)
            d_bl = dbeta * beta * (1.0 - beta)
            dgc = jnp.sum(dgc_ref[:, h * LANE:(h + 1) * LANE], axis=-1, keepdims=True)
            dg = _chunk_rev_cumsum(jnp.broadcast_to(dgc, (S, LANE)), row, S)[:, 0:1]
            z = sm[:, H + h:H + h + 1] + dt_ref[0:1, h:h + 1]
            a = jnp.exp(al_ref[0:1, h:h + 1])
            g = -a * _softplus(z)
            d_al = dg * (-a) * _sigmoid(z)
            dsm = dsm + jnp.where(lane == h, d_bl, 0.0) + jnp.where(lane == H + h, d_al, 0.0)
            ddt = ddt + jnp.where(lane == h, jnp.sum(d_al, axis=0, keepdims=True), 0.0)
            dal = dal + jnp.where(lane == h, jnp.sum(dg * g, axis=0, keepdims=True), 0.0)
        dsm_ref[...] = dsm.astype(dsm_ref.dtype)

        @pl.when(pl.program_id(0) == 0)
        def _():
            dal_ref[...] = jnp.zeros_like(dal_ref)
            ddt_ref[...] = jnp.zeros_like(ddt_ref)

        dal_ref[...] += dal
        ddt_ref[...] += ddt

    vec = pl.BlockSpec((1, LANE), lambda b: (0, 0))
    wide = pl.BlockSpec((S, H * LANE), lambda b: (b, 0))
    return pl.pallas_call(body, grid=(B,),
                          in_specs=[wide, wide, pl.BlockSpec((S, LANE), lambda b: (b, SM_COL // LANE)), vec, vec],
                          out_specs=(pl.BlockSpec((S, LANE), lambda b: (b, 0)), vec, vec),
                          out_shape=(jax.ShapeDtypeStruct((B * S, LANE), _MXU_DTYPE),
                                     jax.ShapeDtypeStruct((1, LANE), F32), jax.ShapeDtypeStruct((1, LANE), F32)),
                          compiler_params=_cp(("arbitrary",)), name="gdn_gates_bwd")(d_beta, d_gc, proj, a_log, dt_bias)


PREP_ROWS = 1024


@jax.custom_vjp
def _unit_lower_inverse(lower):
    n, C, _ = lower.shape
    ri = lax.broadcasted_iota(jnp.int32, (C, C), 0)
    ci = lax.broadcasted_iota(jnp.int32, (C, C), 1)
    p = -lower
    inv = jnp.where((ri == ci)[None], 1.0, 0.0) + p
    for _ in range(int(math.log2(C)) - 1):
        p = _bdot_hi(p, p)
        inv = inv + _bdot_hi(inv, p)
    return inv


def _unit_lower_inverse_fwd(lower):
    inv = _unit_lower_inverse(lower)
    return inv, inv


def _unit_lower_inverse_bwd(inv, d_inv):
    inv_t = jnp.swapaxes(inv, 1, 2)
    return (-_bdot_hi(_bdot_hi(inv_t, d_inv), inv_t),)


_unit_lower_inverse.defvjp(_unit_lower_inverse_fwd, _unit_lower_inverse_bwd)


def _prep_fn(q, k, v, beta, gc):
    R = q.shape[0]
    n = R // CHUNK
    q3, k3, v3, b3, g3 = [t.reshape(n, CHUNK, LIN_HEAD_DIM) for t in (q, k, v, beta, gc)]
    ri = lax.broadcasted_iota(jnp.int32, (CHUNK, CHUNK), 0)
    ci = lax.broadcasted_iota(jnp.int32, (CHUNK, CHUNK), 1)
    causal = (ri >= ci)[None]
    strict = (ri > ci)[None]
    gcol = g3[:, :, 0:1]
    grow = jnp.swapaxes(g3, 1, 2)[:, 0:1, :]
    decay = jnp.exp(jnp.where(causal, gcol - grow, -1e30))
    kb = k3 * b3
    lower = jnp.where(strict, _bdot(kb, k3, 2, 2) * decay, 0.0)
    inv = _unit_lower_inverse(lower)
    eg = jnp.exp(g3)
    sol = _bdot_hi(inv, jnp.concatenate([v3 * b3, kb * eg], axis=-1))
    u, w = sol[..., :LIN_HEAD_DIM], sol[..., LIN_HEAD_DIM:]
    intra = _bdot(q3, k3, 2, 2) * decay
    q_dec = q3 * eg
    k_dec = k3 * jnp.exp(g3[:, CHUNK - 1:CHUNK, :] - g3)
    return (u.reshape(R, LIN_HEAD_DIM), w.reshape(R, LIN_HEAD_DIM), q_dec.reshape(R, LIN_HEAD_DIM),
            k_dec.reshape(R, LIN_HEAD_DIM), intra.reshape(R, CHUNK))


def _prep_specs(S):
    H = N_LIN_HEADS
    R = min(PREP_ROWS, S)
    nr = S // R

    def col(off):
        return pl.BlockSpec((R, LANE), lambda b, h, r: (b * nr + r, off + h))

    intra = pl.BlockSpec((None, R, CHUNK), lambda b, h, r: (h, b * nr + r, 0))
    return R, nr, col, intra


def _gdn_prep_fwd(act, beta, gc, B, S):
    H = N_LIN_HEADS
    R, nr, col, intra_spec = _prep_specs(S)
    T = B * S

    def body(q_ref, k_ref, v_ref, b_ref, g_ref, u_ref, w_ref, qd_ref, kd_ref, a_ref):
        u, w, qd, kd, a = _prep_fn(q_ref[...], k_ref[...], v_ref[...], b_ref[...], g_ref[...])
        u_ref[...] = u
        w_ref[...] = w
        qd_ref[...] = qd
        kd_ref[...] = kd
        a_ref[...] = a

    wide = jax.ShapeDtypeStruct((T, H * LANE), F32)
    return pl.pallas_call(body, grid=(B, H, nr),
                          in_specs=[col(0), col(H), col(2 * H), col(0), col(0)],
                          out_specs=(col(0), col(0), col(0), col(0), intra_spec),
                          out_shape=(wide, wide, wide, wide, jax.ShapeDtypeStruct((H, T, CHUNK), F32)),
                          compiler_params=_cp(("parallel", "parallel", "parallel")),
                          name="gdn_prep_fwd")(act, act, act, beta, gc)


def _gdn_prep_bwd(act, beta, gc, du, dw, dqd, dkd, da, B, S):
    H = N_LIN_HEADS
    R, nr, col, intra_spec = _prep_specs(S)
    T = B * S

    def body(q_ref, k_ref, v_ref, b_ref, g_ref, du_ref, dw_ref, dqd_ref, dkd_ref, da_ref,
             dq_ref, dk_ref, dv_ref, db_ref, dg_ref):
        _, vjp = jax.vjp(_prep_fn, q_ref[...], k_ref[...], v_ref[...], b_ref[...], g_ref[...])
        dq, dk, dv, db, dg = vjp((du_ref[...], dw_ref[...], dqd_ref[...], dkd_ref[...], da_ref[...]))
        dq_ref[...] = dq
        dk_ref[...] = dk
        dv_ref[...] = dv
        db_ref[...] = db
        dg_ref[...] = dg

    wide = jax.ShapeDtypeStruct((T, H * LANE), F32)
    return pl.pallas_call(body, grid=(B, H, nr),
                          in_specs=[col(0), col(H), col(2 * H), col(0), col(0),
                                    col(0), col(0), col(0), col(0), intra_spec],
                          out_specs=(col(0),) * 5, out_shape=(wide,) * 5,
                          compiler_params=_cp(("parallel", "parallel", "parallel")),
                          name="gdn_prep_bwd")(act, act, act, beta, gc, du, dw, dqd, dkd, da)


def _scan_step(u, w, qd, kd, a, g_last, state):
    v_new = u - _dot(w, state)
    o = _dot(qd, state) + _dot(a, v_new)
    new_state = state * jnp.exp(g_last) + _dot(kd, v_new, 0, 0)
    return o, new_state


SCAN_HEADS = 6
SCAN_ROWS = 512


def _scan_specs(B, S, reverse):
    HP = SCAN_HEADS
    R = min(SCAN_ROWS, S)
    nr = S // R

    def blk(r):
        return nr - 1 - r if reverse else r

    col = pl.BlockSpec((R, HP * LANE), lambda b, h, r: (b * nr + blk(r), h))
    intra = pl.BlockSpec((HP, R, CHUNK), lambda b, h, r: (h, b * nr + blk(r), 0))
    st = pl.BlockSpec((None, HP, R // CHUNK, LIN_HEAD_DIM, LIN_HEAD_DIM), lambda b, h, r: (b, h, blk(r), 0, 0))
    return R, nr, col, intra, st


def _gdn_scan_fwd(u, w, qd, kd, a, gc, B, S):
    H = N_LIN_HEADS
    R, nr, col, intra, st = _scan_specs(B, S, reverse=False)

    def body(u_ref, w_ref, qd_ref, kd_ref, a_ref, g_ref, o_ref, st_ref, carry_ref):
        @pl.when(pl.program_id(2) == 0)
        def _():
            carry_ref[...] = jnp.zeros_like(carry_ref)

        def step(c, states):
            rows = pl.ds(pl.multiple_of(c * CHUNK, CHUNK), CHUNK)
            new_states = []
            for hh, state in enumerate(states):
                cols = slice(hh * LANE, (hh + 1) * LANE)
                st_ref[hh, c] = state.astype(st_ref.dtype)
                o, new_state = _scan_step(u_ref[rows, cols], w_ref[rows, cols], qd_ref[rows, cols], kd_ref[rows, cols],
                                          a_ref[hh, rows, :], g_ref[rows, cols][CHUNK - 1:CHUNK, :], state)
                o_ref[rows, cols] = o
                new_states.append(new_state)
            return tuple(new_states)

        states = lax.fori_loop(0, R // CHUNK, step, tuple(carry_ref[hh] for hh in range(SCAN_HEADS)))
        for hh, state in enumerate(states):
            carry_ref[hh] = state

    return pl.pallas_call(body, grid=(B, H // SCAN_HEADS, nr), in_specs=[col, col, col, col, intra, col],
                          out_specs=(col, st),
                          out_shape=(jax.ShapeDtypeStruct((B * S, H * LANE), F32),
                                     jax.ShapeDtypeStruct((B, H, S // CHUNK, LIN_HEAD_DIM, LIN_HEAD_DIM), _MXU_DTYPE)),
                          scratch_shapes=[pltpu.VMEM((SCAN_HEADS, LIN_HEAD_DIM, LIN_HEAD_DIM), F32)],
                          compiler_params=_cp(("parallel", "parallel", "arbitrary")),
                          name="gdn_scan_fwd")(u, w, qd, kd, a, gc)


def _gdn_scan_bwd(u, w, qd, kd, a, gc, states, do, B, S):
    H = N_LIN_HEADS
    R, nr, col, intra, st = _scan_specs(B, S, reverse=True)
    T = B * S
    n = R // CHUNK

    def body(u_ref, w_ref, qd_ref, kd_ref, a_ref, g_ref, st_ref, do_ref,
             du_ref, dw_ref, dqd_ref, dkd_ref, da_ref, dg_ref, carry_ref):
        last = lax.broadcasted_iota(jnp.int32, (CHUNK, LANE), 0) == CHUNK - 1

        @pl.when(pl.program_id(2) == 0)
        def _():
            carry_ref[...] = jnp.zeros_like(carry_ref)

        def step(i, d_states):
            c = n - 1 - i
            rows = pl.ds(pl.multiple_of(c * CHUNK, CHUNK), CHUNK)
            d_prevs = []
            for hh, d_state in enumerate(d_states):
                cols = slice(hh * LANE, (hh + 1) * LANE)
                _, vjp = jax.vjp(_scan_step, u_ref[rows, cols], w_ref[rows, cols], qd_ref[rows, cols], kd_ref[rows, cols],
                                 a_ref[hh, rows, :], g_ref[rows, cols][CHUNK - 1:CHUNK, :], st_ref[hh, c].astype(F32))
                du, dw, dqd, dkd, da, dgl, d_prev = vjp((do_ref[rows, cols].astype(F32), d_state))
                du_ref[rows, cols] = du
                dw_ref[rows, cols] = dw
                dqd_ref[rows, cols] = dqd
                dkd_ref[rows, cols] = dkd
                da_ref[hh, rows, :] = da
                dg_ref[rows, cols] = jnp.where(last, dgl, 0.0)
                d_prevs.append(d_prev)
            return tuple(d_prevs)

        d_states = lax.fori_loop(0, n, step, tuple(carry_ref[hh] for hh in range(SCAN_HEADS)))
        for hh, d_state in enumerate(d_states):
            carry_ref[hh] = d_state

    wide = jax.ShapeDtypeStruct((T, H * LANE), F32)
    return pl.pallas_call(body, grid=(B, H // SCAN_HEADS, nr), in_specs=[col, col, col, col, intra, col, st, col],
                          out_specs=(col, col, col, col, intra, col),
                          out_shape=(wide, wide, wide, wide, jax.ShapeDtypeStruct((H, T, CHUNK), F32), wide),
                          scratch_shapes=[pltpu.VMEM((SCAN_HEADS, LIN_HEAD_DIM, LIN_HEAD_DIM), F32)],
                          compiler_params=_cp(("parallel", "parallel", "arbitrary")),
                          name="gdn_scan_bwd")(u, w, qd, kd, a, gc, states, do)


GATE_COL = 3 * SEQ_MIX_WIDTH


def _post_fn(o, gate, gain):
    return o * lax.rsqrt(jnp.mean(o * o, axis=-1, keepdims=True) + EPS) * gain * _silu(gate)


def _gdn_post_fwd(o, proj, onorm, T):
    H = N_LIN_HEADS
    tm = _pick(T, 1024)

    def body(o_ref, g_ref, n_ref, y_ref):
        y_ref[...] = _post_fn(o_ref[...], g_ref[...], n_ref[...]).astype(y_ref.dtype)

    col = pl.BlockSpec((tm, LANE), lambda i, h: (i, h))
    return pl.pallas_call(body, grid=(T // tm, H),
                          in_specs=[col, pl.BlockSpec((tm, LANE), lambda i, h: (i, GATE_COL // LANE + h)),
                                    pl.BlockSpec((1, LANE), lambda i, h: (0, 0))],
                          out_specs=col, out_shape=jax.ShapeDtypeStruct((T, H * LANE), _MXU_DTYPE),
                          compiler_params=_cp(("parallel", "parallel")), name="gdn_post_fwd")(o, proj, onorm)


def _gdn_post_bwd(d_cat, o, proj, onorm, T):
    H = N_LIN_HEADS
    tm = _pick(T, 1024)

    def body(dy_ref, o_ref, g_ref, n_ref, do_ref, dg_ref, dn_ref):
        _, vjp = jax.vjp(_post_fn, o_ref[...], g_ref[...], n_ref[...])
        do, dg, dn = vjp(dy_ref[...].astype(F32))
        do_ref[...] = do
        dg_ref[...] = dg.astype(dg_ref.dtype)

        @pl.when((pl.program_id(0) == 0) & (pl.program_id(1) == 0))
        def _():
            dn_ref[...] = jnp.zeros_like(dn_ref)

        dn_ref[...] += dn

    col = pl.BlockSpec((tm, LANE), lambda i, h: (i, h))
    vec = pl.BlockSpec((1, LANE), lambda i, h: (0, 0))
    return pl.pallas_call(body, grid=(T // tm, H),
                          in_specs=[col, col, pl.BlockSpec((tm, LANE), lambda i, h: (i, GATE_COL // LANE + h)), vec],
                          out_specs=(col, col, vec),
                          out_shape=(jax.ShapeDtypeStruct((T, H * LANE), F32),
                                     jax.ShapeDtypeStruct((T, H * LANE), _MXU_DTYPE),
                                     jax.ShapeDtypeStruct((1, LANE), F32)),
                          compiler_params=_cp(("arbitrary", "arbitrary")), name="gdn_post_bwd")(d_cat, o, proj, onorm)


def _log_sigmoid(z):
    return jnp.minimum(z, 0.0) - jnp.log(1.0 + jnp.exp(-jnp.abs(z)))


def _split_dot(x, m):
    hi = x.astype(_MXU_DTYPE)
    lo = x - hi.astype(F32)
    return _dot(hi, m) + _dot(lo, m)


SB_QB = 256
SB_RC = 512


def _sb_scores(q2, k_j, scale, valid):
    z = _dot(q2, k_j, 1, 1) * scale
    lb = _log_sigmoid(z)
    return lb, jnp.where(valid, lb - z, 0.0)


def _sb_consts(QB):
    ri = lax.broadcasted_iota(jnp.int32, (SB_BLOCK, SB_BLOCK), 0)
    ci = lax.broadcasted_iota(jnp.int32, (SB_BLOCK, SB_BLOCK), 1)
    rc = min(SB_RC, 2 * QB)
    assert rc == 2 * QB or QB % rc == 0
    row = lax.broadcasted_iota(jnp.int32, (rc, SB_BLOCK), 0)
    col = lax.broadcasted_iota(jnp.int32, (rc, SB_BLOCK), 1)
    lane = lax.broadcasted_iota(jnp.int32, (1, LANE), 1)
    return {
        "rc": rc,
        "chunks": [(ch * rc, (ch * rc) % QB) for ch in range(2 * QB // rc)],
        "row_minus_col": row % QB - col,
        "after_excl": (ri > ci).astype(_MXU_DTYPE),
        "upto_incl": (ri <= ci).astype(_MXU_DTYPE),
        "upto_excl": (ri < ci).astype(_MXU_DTYPE),
        "lane": lane,
        "head0": lane < SB_HEAD_DIM,
    }


def _sb_stack(x, c):
    return jnp.concatenate([jnp.where(c["head0"], x, 0.0), jnp.where(c["head0"], 0.0, x)], axis=0)


def _sb_unstack(x2, c, QB):
    return jnp.where(c["head0"], x2[:QB], x2[QB:])


SB_DEAD = -110.0


def _sb_first_shape(S, QB):
    return (-(-(S // QB) // 8) * 8, LANE)


def _sb_fwd(proj, B, S):
    W = SEQ_MIX_WIDTH
    P = W // LANE
    QB = min(SB_QB, S)
    KB = SB_BLOCK
    scale = SB_HEAD_DIM ** -0.5
    fshape = _sb_first_shape(S, QB)

    def body(q_ref, k_ref, v_ref, o_ref, tot_ref, first_ref):
        c = _sb_consts(QB)
        frow = lax.broadcasted_iota(jnp.int32, fshape, 0)

        def q_loop(i, first):
            qrows = pl.ds(pl.multiple_of(i * QB, QB), QB)
            q2 = _sb_stack(q_ref[qrows, :].astype(F32), c)
            nkb = (i + 1) * (QB // KB)

            def scores(j):
                krows = pl.ds(pl.multiple_of(j * KB, KB), KB)
                valid = c["row_minus_col"] > j * KB - i * QB
                lb, l1 = _sb_scores(q2, k_ref[krows, :], scale, valid)
                return jnp.where(valid, lb, -1e30), l1

            def k_cond(st):
                return (st[0] < nkb) & (st[1] > 0)

            def k_body(st):
                t, _, acc, r, lbm, l1 = st
                j = nkb - 1 - t
                nxt = scores(jnp.maximum(j - 1, 0))
                krows = pl.ds(pl.multiple_of(j * KB, KB), KB)
                a = jnp.exp(lbm + r + _split_dot(l1, c["after_excl"]))
                r = r + jnp.sum(l1, axis=-1, keepdims=True)
                alive = (jnp.max(r) > SB_DEAD).astype(jnp.int32)
                return (t + 1, alive, acc + _dot(a, v_ref[krows, :]), r) + nxt

            t, _, acc, r, _, _ = lax.while_loop(
                k_cond, k_body, (jnp.int32(0), jnp.int32(1), jnp.zeros((2 * QB, LANE), F32), jnp.zeros((2 * QB, 1), F32))
                + scores(nkb - 1))
            o_ref[qrows, :] = _sb_unstack(acc, c, QB)
            tot_ref[qrows, :] = _sb_unstack(jnp.broadcast_to(r, (2 * QB, LANE)), c, QB)
            return jnp.where(frow == i, (nkb - t).astype(F32), first)

        first_ref[...] = lax.fori_loop(0, S // QB, q_loop, jnp.zeros(fshape, F32))

    def col(off):
        return pl.BlockSpec((S, LANE), lambda b, p: (b, off + p))

    out = jax.ShapeDtypeStruct((B * S, W), F32)
    return pl.pallas_call(body, grid=(B, P), in_specs=[col(0), col(P), col(2 * P)],
                          out_specs=(col(0), col(0), pl.BlockSpec((None, None) + fshape, lambda b, p: (b, p, 0, 0))),
                          out_shape=(out, out, jax.ShapeDtypeStruct((B, P) + fshape, F32)),
                          compiler_params=_cp(("parallel", "parallel")), name="sb_fwd")(proj, proj, proj)


def _sb_bwd(proj, tot, first, d_cat, B, S):
    W = SEQ_MIX_WIDTH
    P = W // LANE
    QB = min(SB_QB, S)
    KB = SB_BLOCK
    scale = SB_HEAD_DIM ** -0.5

    fshape = _sb_first_shape(S, QB)

    def body(q_ref, k_ref, v_ref, tot_ref, first_ref, do_ref, dq_ref, dk_ref, dv_ref, dk_acc, dv_acc):
        c = _sb_consts(QB)
        frow = lax.broadcasted_iota(jnp.int32, fshape, 0)
        dk_acc[...] = jnp.zeros_like(dk_acc)
        dv_acc[...] = jnp.zeros_like(dv_acc)

        def q_loop(i, carry):
            qrows = pl.ds(pl.multiple_of(i * QB, QB), QB)
            q2 = _sb_stack(q_ref[qrows, :].astype(F32), c)
            do2 = _sb_stack(do_ref[qrows, :].astype(F32), c)
            q2_t = q2.T.astype(_MXU_DTYPE)
            do2_t = do2.T.astype(_MXU_DTYPE)
            tot = tot_ref[qrows, :]
            total = jnp.concatenate(
                [jnp.sum(jnp.where(c["lane"] == h * SB_HEAD_DIM, tot, 0.0), axis=-1, keepdims=True) for h in range(2)],
                axis=0)

            nkb = (i + 1) * (QB // KB)
            j0 = jnp.clip(jnp.max(jnp.where(frow == i, first_ref[...], 0.0)).astype(jnp.int32), 0, nkb - 1)

            def scores(j):
                krows = pl.ds(pl.multiple_of(j * KB, KB), KB)
                valid = c["row_minus_col"] > j * KB - i * QB
                lb, l1 = _sb_scores(q2, k_ref[krows, :], scale, valid)
                return jnp.where(valid, lb, -1e30), l1, _dot(do2, v_ref[krows, :], 1, 1)

            def k_loop(j, st):
                dq_acc, p_l1, p_g, lbm, l1, da = st
                nxt = scores(jnp.minimum(j + 1, nkb - 1))
                krows = pl.ds(pl.multiple_of(j * KB, KB), KB)
                tail = total - p_l1 - _split_dot(l1, c["upto_incl"])
                a = jnp.exp(lbm + tail)
                g = da * a
                g_before = p_g + _dot(g, c["upto_excl"])
                sig = jnp.exp(lbm)
                dz = ((g * (1.0 - sig) - g_before * sig) * scale).astype(_MXU_DTYPE)
                dk_acc[j] += _dot(q2_t, dz)
                dv_acc[j] += _dot(do2_t, a)
                return (dq_acc + _dot(dz, k_ref[krows, :]), p_l1 + jnp.sum(l1, axis=-1, keepdims=True),
                        p_g + jnp.sum(g, axis=-1, keepdims=True)) + nxt

            zero_col = jnp.zeros((2 * QB, 1), F32)
            dq2 = lax.fori_loop(j0, nkb, k_loop, (jnp.zeros((2 * QB, LANE), F32), zero_col, zero_col) + scores(j0))[0]
            dq_ref[qrows, :] = _sb_unstack(dq2, c, QB).astype(dq_ref.dtype)
            return carry

        lax.fori_loop(0, S // QB, q_loop, 0)
        for j in range(S // KB):
            dk_ref[j * KB:(j + 1) * KB, :] = dk_acc[j].T.astype(dk_ref.dtype)
            dv_ref[j * KB:(j + 1) * KB, :] = dv_acc[j].T.astype(dv_ref.dtype)

    def col(off):
        return pl.BlockSpec((S, LANE), lambda b, p: (b, off + p))

    out = jax.ShapeDtypeStruct((B * S, W), _MXU_DTYPE)
    return pl.pallas_call(body, grid=(B, P),
                          in_specs=[col(0), col(P), col(2 * P), col(0),
                                    pl.BlockSpec((None, None) + fshape, lambda b, p: (b, p, 0, 0)), col(0)],
                          out_specs=(col(0),) * 3, out_shape=(out,) * 3,
                          scratch_shapes=[pltpu.VMEM((S // KB, LANE, KB), F32), pltpu.VMEM((S // KB, LANE, KB), F32)],
                          compiler_params=_cp(("parallel", "parallel")), name="sb_bwd")(proj, proj, proj, tot, first, d_cat)


def _mem_fn(q, k, v):
    lane = lax.broadcasted_iota(jnp.int32, (1, X_WIDTH), 1)
    out = jnp.zeros(q.shape, F32)
    for h in range(N_X_HEADS):
        hm = (lane // X_HEAD_DIM) == h
        s = _dot(jnp.where(hm, q, 0.0), k, 1, 1) * (X_HEAD_DIM ** -0.5)
        e = jnp.exp(s - lax.stop_gradient(jnp.max(s, axis=-1, keepdims=True)))
        p = e / jnp.sum(e, axis=-1, keepdims=True)
        out = out + jnp.where(hm, _dot(p, v), 0.0)
    return out


def _mem_specs(S, q_col):
    ts = _pick(S, 1024)
    ns = S // ts
    qs = pl.BlockSpec((ts, X_WIDTH), lambda b, i: (b * ns + i, q_col // X_WIDTH))
    ks = pl.BlockSpec((N_MEM, X_WIDTH), lambda b, i: (b, 0))
    vs = pl.BlockSpec((N_MEM, X_WIDTH), lambda b, i: (b, 1))
    os = pl.BlockSpec((ts, X_WIDTH), lambda b, i: (b * ns + i, 0))
    return ts, ns, qs, ks, vs, os


def _mem_fwd(proj, q_col, mem_kv, B, S, name):
    ts, ns, qs, ks, vs, os = _mem_specs(S, q_col)

    def body(q_ref, k_ref, v_ref, o_ref):
        o_ref[...] = _mem_fn(q_ref[...].astype(F32), k_ref[...].astype(F32), v_ref[...].astype(F32))

    return pl.pallas_call(body, grid=(B, ns), in_specs=[qs, ks, vs], out_specs=os,
                          out_shape=jax.ShapeDtypeStruct((B * S, X_WIDTH), F32),
                          compiler_params=_cp(("parallel", "parallel")), name=name)(proj, mem_kv, mem_kv)


def _mem_bwd(proj, q_col, mem_kv, d_cat, B, S, name):
    ts, ns, qs, ks, vs, os = _mem_specs(S, q_col)

    def body(q_ref, k_ref, v_ref, do_ref, dq_ref, dk_ref, dv_ref):
        _, vjp = jax.vjp(_mem_fn, q_ref[...].astype(F32), k_ref[...].astype(F32), v_ref[...].astype(F32))
        dq, dk, dv = vjp(do_ref[...].astype(F32))
        dq_ref[...] = dq.astype(dq_ref.dtype)

        @pl.when(pl.program_id(1) == 0)
        def _():
            dk_ref[...] = jnp.zeros_like(dk_ref)
            dv_ref[...] = jnp.zeros_like(dv_ref)

        dk_ref[...] += dk
        dv_ref[...] += dv

    dos = pl.BlockSpec((ts, X_WIDTH), lambda b, i: (b * ns + i, SEQ_MIX_WIDTH // X_WIDTH))
    dq, dk, dv = pl.pallas_call(
        body, grid=(B, ns), in_specs=[qs, ks, vs, dos],
        out_specs=(os, pl.BlockSpec((N_MEM, X_WIDTH), lambda b, i: (b, 0)), pl.BlockSpec((N_MEM, X_WIDTH), lambda b, i: (b, 0))),
        out_shape=(jax.ShapeDtypeStruct((B * S, X_WIDTH), _MXU_DTYPE),
                   jax.ShapeDtypeStruct((B * N_MEM, X_WIDTH), F32), jax.ShapeDtypeStruct((B * N_MEM, X_WIDTH), F32)),
        compiler_params=_cp(("parallel", "arbitrary")), name=name)(proj, mem_kv, mem_kv, d_cat)
    return dq, dk, dv


def _peers():
    x, y, c = lax.axis_index("x"), lax.axis_index("y"), lax.axis_index("c")
    me = 4 * x + 2 * y + c
    out = []
    for fx, fy, fc in [(0, 0, 1), (1, 0, 0), (0, 1, 0), (1, 1, 0), (1, 0, 1), (0, 1, 1), (1, 1, 1)]:
        px, py, pc = x ^ fx, y ^ fy, c ^ fc
        out.append(((px, py, pc), 4 * px + 2 * py + pc))
    return me, out


ANY = pl.BlockSpec(memory_space=pl.ANY)


def _remote(src, dst, send_sems, recv_sems, k, dev):
    return pltpu.make_async_remote_copy(src_ref=src, dst_ref=dst, send_sem=send_sems.at[k], recv_sem=recv_sems.at[k],
                                        device_id=dev, device_id_type=pl.DeviceIdType.MESH)


def _place():
    x, y, c = lax.axis_index("x"), lax.axis_index("y"), lax.axis_index("c")
    return x, y, c, [(1 - x, y), (x, 1 - y), (1 - x, 1 - y)]


def _all_gather(shard):
    R = shard.shape[0]

    def body(x_ref, o_ref, send_sems, recv_sems, local_sem):
        x, y, c, chips = _place()
        me, sibling = (x, y, c), (x, y, 1 - c)

        def slot(px, py, pc):
            return o_ref.at[4 * px + 2 * py + pc]

        def copy(k, block, to, src=None):
            return _remote(slot(*block) if src is None else src, slot(*block), send_sems, recv_sems, k, to)

        mine = pltpu.make_async_copy(x_ref, slot(*me), local_sem)
        mine.start()
        first = [copy(0, me, sibling, src=x_ref)] + [copy(1 + j, me, (*chip, c), src=x_ref) for j, chip in enumerate(chips)]
        for cp in first:
            cp.start()
        passed = [copy(4 + j, (*chip, c), sibling) for j, chip in enumerate(chips)]
        for j, chip in enumerate(chips):
            copy(1 + j, (*chip, c), me).wait_recv()
            passed[j].start()
        copy(0, sibling, me).wait_recv()
        for j, chip in enumerate(chips):
            copy(4 + j, (*chip, 1 - c), me).wait_recv()
        for cp in first + passed:
            cp.wait_send()
        mine.wait()

    return pl.pallas_call(body, in_specs=[ANY], out_specs=ANY,
                          out_shape=jax.ShapeDtypeStruct((N_DEV, R, LANE), shard.dtype),
                          scratch_shapes=[pltpu.SemaphoreType.DMA((7,)), pltpu.SemaphoreType.DMA((7,)),
                                          pltpu.SemaphoreType.DMA],
                          compiler_params=pltpu.CompilerParams(has_side_effects=True),
                          name="all_gather_weights")(shard)


N_CHIP = 4


def _exchange_sibling(big):
    R = big.shape[1]

    def body(b_ref, o_ref, send_sems, recv_sems):
        x, y, c, _ = _place()
        copies = [_remote(b_ref.at[2 * k + (1 - c)], o_ref.at[k], send_sems, recv_sems, k, (x, y, 1 - c))
                  for k in range(N_CHIP)]
        for cp in copies:
            cp.start()
        for cp in copies:
            cp.wait()

    return pl.pallas_call(body, in_specs=[ANY], out_specs=ANY,
                          out_shape=jax.ShapeDtypeStruct((N_CHIP, R, LANE), big.dtype),
                          scratch_shapes=[pltpu.SemaphoreType.DMA((N_CHIP,)), pltpu.SemaphoreType.DMA((N_CHIP,))],
                          compiler_params=pltpu.CompilerParams(has_side_effects=True),
                          name="exchange_sibling")(big)


def _partial_sum(g4, recv):
    R = g4.shape[2]
    tr = _pick(R, 6400)

    def body(g_ref, r_ref, pw_ref, po_ref):
        x, y, c, _ = _place()
        g = jnp.where(c == 0, g_ref[0], g_ref[1]).astype(F32) + r_ref[...].astype(F32)
        pw_ref[...] = g.astype(pw_ref.dtype)

        @pl.when(pl.program_id(1) == 2 * x + y)
        def _():
            po_ref[...] = g

    return pl.pallas_call(body, grid=(R // tr, N_CHIP),
                          in_specs=[pl.BlockSpec((None, 2, tr, LANE), lambda i, k: (k, 0, i, 0)),
                                    pl.BlockSpec((None, tr, LANE), lambda i, k: (k, i, 0))],
                          out_specs=(pl.BlockSpec((None, tr, LANE), lambda i, k: (k, i, 0)),
                                     pl.BlockSpec((tr, LANE), lambda i, k: (i, 0))),
                          out_shape=(jax.ShapeDtypeStruct((N_CHIP, R, LANE), recv.dtype),
                                     jax.ShapeDtypeStruct((R, LANE), F32)),
                          compiler_params=_cp(("parallel", "arbitrary")), name="partial_sum")(g4, recv)


def _exchange_chips(part, small):
    R = part.shape[1]
    K = small.shape[0]

    def body(p_ref, s_ref, ob_ref, os_ref, send_sems, recv_sems, local_sems):
        x, y, c, chips = _place()
        my_chip = 2 * x + y
        me, peers = _peers()
        own_b = pltpu.make_async_copy(p_ref.at[my_chip], ob_ref.at[my_chip], local_sems.at[0])
        own_s = pltpu.make_async_copy(s_ref, os_ref.at[me], local_sems.at[1])
        own_b.start()
        own_s.start()
        copies = [_remote(p_ref.at[2 * px + py], ob_ref.at[my_chip], send_sems, recv_sems, j, (px, py, c))
                  for j, (px, py) in enumerate(chips)]
        copies += [_remote(s_ref, os_ref.at[me], send_sems, recv_sems, 3 + k, dev) for k, (dev, _) in enumerate(peers)]
        for cp in copies:
            cp.start()
        for j, (px, py) in enumerate(chips):
            _remote(p_ref.at[my_chip], ob_ref.at[2 * px + py], send_sems, recv_sems, j, (px, py, c)).wait_recv()
        for k, (dev, idx) in enumerate(peers):
            _remote(s_ref, os_ref.at[idx], send_sems, recv_sems, 3 + k, dev).wait_recv()
        for cp in copies:
            cp.wait_send()
        own_b.wait()
        own_s.wait()

    return pl.pallas_call(body, in_specs=[ANY, ANY], out_specs=(ANY, ANY),
                          out_shape=(jax.ShapeDtypeStruct((N_CHIP, R, LANE), part.dtype),
                                     jax.ShapeDtypeStruct((N_DEV, K, LANE), small.dtype)),
                          scratch_shapes=[pltpu.SemaphoreType.DMA((10,)), pltpu.SemaphoreType.DMA((10,)),
                                          pltpu.SemaphoreType.DMA((2,))],
                          compiler_params=pltpu.CompilerParams(has_side_effects=True),
                          name="exchange_chips")(part, small)


def _adamw_math(w, g, m, v):
    m = ADAM_B1 * m + (1.0 - ADAM_B1) * g
    v = ADAM_B2 * v + (1.0 - ADAM_B2) * (g * g)
    m_hat = m / (1.0 - ADAM_B1 ** ADAM_STEP)
    v_hat = v / (1.0 - ADAM_B2 ** ADAM_STEP)
    delta = -ADAM_LR * (m_hat / (jnp.sqrt(v_hat) + ADAM_EPS) + ADAM_WD * w)
    return delta, m, v


def _adamw_shard(own, recv, w, m, v):
    R = own.shape[0]
    tr = _pick(R, 1024)

    def body(own_ref, recv_ref, w_ref, m_ref, v_ref, g_ref, d_ref, nm_ref, nv_ref):
        x, y, _, _ = _place()
        g = own_ref[...]
        for k in range(N_CHIP):
            g = g + jnp.where(k == 2 * x + y, 0.0, recv_ref[k].astype(F32))
        delta, nm, nv = _adamw_math(w_ref[...], g, m_ref[...], v_ref[...])
        g_ref[...] = g
        d_ref[...] = delta
        nm_ref[...] = nm
        nv_ref[...] = nv

    row = pl.BlockSpec((tr, LANE), lambda i: (i, 0))
    out = jax.ShapeDtypeStruct((R, LANE), F32)
    return pl.pallas_call(body, grid=(R // tr,),
                          in_specs=[row, pl.BlockSpec((N_CHIP, tr, LANE), lambda i: (0, i, 0)), row, row, row],
                          out_specs=(row,) * 4, out_shape=(out,) * 4,
                          compiler_params=_cp(("parallel",)), name="adamw_shard")(own, recv, w, m, v)


def _adamw_replicated(parts, w, m, v):
    K = w.shape[0]

    def body(p_ref, w_ref, m_ref, v_ref, g_ref, d_ref, nm_ref, nv_ref):
        g = p_ref[0]
        for p in range(1, N_DEV):
            g = g + p_ref[p]
        delta, nm, nv = _adamw_math(w_ref[...], g, m_ref[...], v_ref[...])
        g_ref[...] = g
        d_ref[...] = delta
        nm_ref[...] = nm
        nv_ref[...] = nv

    out = jax.ShapeDtypeStruct((K, LANE), F32)
    return pl.pallas_call(body, out_shape=(out,) * 4, compiler_params=_cp(), name="adamw_replicated")(parts, w, m, v)


_SHARDED = (("w_in_a", 1), ("conv_w_a", 2), ("w_in_b", 2), ("w_mem_kv", 1), ("w_out", 1), ("w_up", 2), ("w_down", 1))
_REPLICATED = ("mem_norm", "norm_pre_mix", "norm_post_mix", "norm_pre_mlp", "norm_post_mlp", "a_log_a", "dt_bias_a", "onorm_a")
ROW_ALIGN = 16


def _rows(n_elems, align=ROW_ALIGN):
    r = -(-n_elems // LANE)
    return -(-r // align) * align


def _pack(arrays, align=ROW_ALIGN, total=None, lead=()):
    parts = []
    for a in arrays:
        n = math.prod(a.shape[len(lead):])
        flat = a.reshape(lead + (n,))
        r = _rows(n, align)
        flat = jnp.pad(flat, [(0, 0)] * len(lead) + [(0, r * LANE - n)])
        parts.append(flat.reshape(lead + (r, LANE)))
    used = sum(part.shape[len(lead)] for part in parts)
    if total is not None and used < total:
        parts.append(jnp.zeros(lead + (total - used, LANE), parts[0].dtype))
    return jnp.concatenate(parts, axis=len(lead))


def _unpack(flat, shapes, align=ROW_ALIGN, lead=()):
    outs = []
    r0 = 0
    for shp in shapes:
        n = math.prod(shp)
        r = _rows(n, align)
        part = lax.slice_in_dim(flat, r0, r0 + r, axis=len(lead))
        part = part.reshape(lead + (r * LANE,))
        part = lax.slice_in_dim(part, 0, n, axis=len(lead))
        outs.append(part.reshape(lead + tuple(shp)))
        r0 += r
    return outs


def _to_full(gathered, axis):
    g = jnp.moveaxis(gathered, 0, axis)
    shp = g.shape
    return g.reshape(shp[:axis] + (shp[axis] * shp[axis + 1],) + shp[axis + 2:])


def _to_blocks(full, axis):
    shp = full.shape
    g = full.reshape(shp[:axis] + (N_DEV, shp[axis] // N_DEV) + shp[axis + 1:])
    return jnp.moveaxis(g, axis, 0)


def _widen_in_a(w):
    main = w[:, :4 * SEQ_MIX_WIDTH]
    small = w[:, 4 * SEQ_MIX_WIDTH:4 * SEQ_MIX_WIDTH + 2 * N_LIN_HEADS]
    memq = w[:, 4 * SEQ_MIX_WIDTH + 2 * N_LIN_HEADS:]
    pad = jnp.zeros((w.shape[0], IN_A_PAD - IN_A), w.dtype)
    return jnp.concatenate([main, memq, small, pad], axis=1)


def _narrow_in_a(g):
    main = g[:, :4 * SEQ_MIX_WIDTH]
    memq = g[:, 4 * SEQ_MIX_WIDTH:4 * SEQ_MIX_WIDTH + X_WIDTH]
    small = g[:, SM_COL:SM_COL + 2 * N_LIN_HEADS]
    return jnp.concatenate([main, small, memq], axis=1)


def _row128(v):
    return jnp.pad(v.reshape(1, -1), ((0, 0), (0, LANE - v.shape[-1])))


def _local_step(x, mem, target, p):
    B, S, D = x.shape
    T = B * S
    md = _MXU_DTYPE
    x0 = x.reshape(T, D)
    tgt = target.reshape(T, D)
    memf = mem.reshape(B * N_MEM, D)
    vec = lambda a: a.reshape(1, -1)

    mem_n = _norm_fwd(memf, vec(p["mem_norm"]), out_dtype=md, name="norm_mem")
    w_in = [_widen_in_a(p["w_in_a"][0]), p["w_in_b"][0]]
    memq_col = [4 * SEQ_MIX_WIDTH, 3 * SEQ_MIX_WIDTH]
    alog = _row128(p["a_log_a"][0])
    dtb = _row128(p["dt_bias_a"][0])
    onorm = vec(p["onorm_a"][0])
    conv_w = p["conv_w_a"][0]
    saved = []
    xi = x0
    h1 = _norm_fwd(xi, vec(p["norm_pre_mix"][0]), out_dtype=md, name="norm_pre_mix0")
    for i in range(2):
        s = {"x_in": xi}
        proj = _mm(h1, w_in[i], out_dtypes=(F32 if i == 0 else md,), name=f"in_proj{i}")
        mem_kv = _mm(mem_n, p["w_mem_kv"][i], out_dtypes=(md,), name=f"mem_kv{i}")
        if i == 0:
            act = _gdn_conv_fwd(proj, conv_w, B, S)
            beta, gc = _gdn_gates_fwd(proj, alog, dtb, B, S)
            u, w, qd, kd, intra = _gdn_prep_fwd(act, beta, gc, B, S)
            o, states = _gdn_scan_fwd(u, w, qd, kd, intra, gc, B, S)
            mix = _gdn_post_fwd(o, proj, onorm, T)
            s.update(act=act, beta=beta, gc=gc, u=u, w=w, qd=qd, kd=kd, intra=intra, o=o, states=states)
        else:
            mix, tot, first = _sb_fwd(proj, B, S)
            s.update(tot=tot, first=first)
        cross = _mem_fwd(proj, memq_col[i], mem_kv, B, S, name=f"mem_fwd{i}")
        cat = jnp.concatenate([mix.astype(md), cross.astype(md)], axis=1)
        y = _mm(cat, p["w_out"][i], name=f"out_proj{i}")
        x_mid, h2 = _norm_pair_fwd(y, vec(p["norm_post_mix"][i]), xi, vec(p["norm_pre_mlp"][i]), md,
                                   name=f"norm_post_mix_pre_mlp{i}")
        a_act, r = _mm(h2, p["w_up"][i], out_dtypes=(md, md), epilogue=_relu2_epilogue, name=f"up_proj{i}")
        y2 = _mm(a_act, p["w_down"][i], name=f"down_proj{i}")
        s.update(h1=h1, proj=proj, mem_kv=mem_kv, cat=cat, y=y, x_mid=x_mid, h2=h2, a_act=a_act, r=r, y2=y2)
        saved.append(s)
        if i == 0:
            xi, h1 = _norm_pair_fwd(y2, vec(p["norm_post_mlp"][0]), x_mid, vec(p["norm_pre_mix"][1]), md,
                                    name="norm_post_mlp0_pre_mix1")
        else:
            xi = _norm_fwd(y2, vec(p["norm_post_mlp"][1]), resid=x_mid, name="norm_post_mlp1")

    loss_row, dx = _loss_head(xi, tgt)

    g = {}
    d_mem_n = None
    gn = {k: [None, None] for k in ("norm_pre_mix", "norm_post_mix", "norm_pre_mlp", "norm_post_mlp")}
    g_w_mem_kv, g_w_out, g_w_up, g_w_down = [None, None], [None, None], [None, None], [None, None]
    d_y2, gn["norm_post_mlp"][1] = _norm_bwd(dx, saved[1]["y2"], vec(p["norm_post_mlp"][1]), name="norm_post_mlp_bwd1")
    for i in (1, 0):
        s = saved[i]
        g_w_down[i] = _mm(s["a_act"], d_y2, ta=True, out_dtypes=(_WIRE_DTYPE,), name=f"down_proj_dw{i}")
        d_u = _mm(d_y2, p["w_down"][i], tb=True, out_dtypes=(md,), epilogue=_drelu2_epilogue, extras=(s["r"],),
                  name=f"down_proj_dx{i}")
        g_w_up[i] = _mm(s["h2"], d_u, ta=True, out_dtypes=(_WIRE_DTYPE,), name=f"up_proj_dw{i}")
        d_h2 = _mm(d_u, p["w_up"][i], tb=True, name=f"up_proj_dx{i}")
        dx, d_y, gn["norm_pre_mlp"][i], gn["norm_post_mix"][i] = _norm_pair_bwd(
            d_h2, s["x_mid"], vec(p["norm_pre_mlp"][i]), dx, s["y"], vec(p["norm_post_mix"][i]),
            name=f"norm_pre_mlp_post_mix_bwd{i}")
        g_w_out[i] = _mm(s["cat"], d_y, ta=True, out_dtypes=(_WIRE_DTYPE,), name=f"out_proj_dw{i}")
        d_cat = _mm(d_y, p["w_out"][i], tb=True, name=f"out_proj_dx{i}")
        d_memq, d_mk, d_mv = _mem_bwd(s["proj"], memq_col[i], s["mem_kv"], d_cat, B, S, name=f"mem_bwd{i}")
        d_mem_kv = jnp.concatenate([d_mk.astype(md), d_mv.astype(md)], axis=1)
        g_w_mem_kv[i] = _mm(mem_n, d_mem_kv, ta=True, out_dtypes=(_WIRE_DTYPE,), name=f"mem_kv_dw{i}")
        d_mn = _mm(d_mem_kv, p["w_mem_kv"][i], tb=True, name=f"mem_kv_dx{i}")
        d_mem_n = d_mn if d_mem_n is None else d_mem_n + d_mn
        if i == 0:
            d_o, d_gate, g["onorm_a"] = _gdn_post_bwd(d_cat, s["o"], s["proj"], onorm, T)
            du, dw, dqd, dkd, da, dgc_s = _gdn_scan_bwd(s["u"], s["w"], s["qd"], s["kd"], s["intra"], s["gc"],
                                                         s["states"], d_o, B, S)
            dq, dk, dv, d_beta, d_gc = _gdn_prep_bwd(s["act"], s["beta"], s["gc"], du, dw, dqd, dkd, da, B, S)
            d_qkv, g["conv_w_a"] = _gdn_conv_bwd(dq, dk, dv, s["proj"], conv_w, B, S)
            d_sm, g["a_log_a"], g["dt_bias_a"] = _gdn_gates_bwd(d_beta, d_gc + dgc_s, s["proj"], alog, dtb, B, S)
            pad = jnp.zeros((T, IN_A_PAD - SM_COL - LANE), md)
            d_proj = jnp.concatenate([d_qkv, d_gate, d_memq, d_sm, pad], axis=1)
        else:
            dq, dk, dv = _sb_bwd(s["proj"], s["tot"], s["first"], d_cat, B, S)
            d_proj = jnp.concatenate([dq, dk, dv, d_memq], axis=1)
        g_w_in = _mm(s["h1"], d_proj, ta=True, out_dtypes=(_WIRE_DTYPE,), name=f"in_proj_dw{i}")
        d_h1 = _mm(d_proj, w_in[i], tb=True, name=f"in_proj_dx{i}")
        if i == 0:
            dx, gn["norm_pre_mix"][0] = _norm_bwd(d_h1, s["x_in"], vec(p["norm_pre_mix"][0]), resid=dx,
                                                  name="norm_pre_mix_bwd0")
            g["w_in_a"] = _narrow_in_a(g_w_in)[None]
        else:
            dx, d_y2, gn["norm_pre_mix"][1], gn["norm_post_mlp"][0] = _norm_pair_bwd(
                d_h1, s["x_in"], vec(p["norm_pre_mix"][1]), dx, saved[0]["y2"], vec(p["norm_post_mlp"][0]),
                name="norm_pre_mix1_post_mlp0_bwd")
            g["w_in_b"] = g_w_in[None]
    _, g_mem_norm = _norm_bwd(d_mem_n, memf, vec(p["mem_norm"]), name="norm_mem_bwd")
    g["mem_norm"] = g_mem_norm.reshape(-1)
    for k, v in gn.items():
        g[k] = jnp.concatenate(v, axis=0)
    g["w_mem_kv"] = jnp.stack(g_w_mem_kv)
    g["w_out"] = jnp.stack(g_w_out)
    g["w_up"] = jnp.stack(g_w_up)
    g["w_down"] = jnp.stack(g_w_down)
    g["conv_w_a"] = g["conv_w_a"][None]
    g["a_log_a"] = g["a_log_a"][:, :N_LIN_HEADS]
    g["dt_bias_a"] = g["dt_bias_a"][:, :N_LIN_HEADS]
    return loss_row, dx.reshape(B, S, D), g


def kernel(x, mem, mem_norm, norm_pre_mix, norm_post_mix, norm_pre_mlp, norm_post_mlp, w_in_a, conv_w_a, a_log_a, dt_bias_a, onorm_a, w_in_b, w_mem_kv, w_out, w_up, w_down, loss_target, m_mem_norm, m_norm_pre_mix, m_norm_post_mix, m_norm_pre_mlp, m_norm_post_mlp, m_w_in_a, m_conv_w_a, m_a_log_a, m_dt_bias_a, m_onorm_a, m_w_in_b, m_w_mem_kv, m_w_out, m_w_up, m_w_down, v_mem_norm, v_norm_pre_mix, v_norm_post_mix, v_norm_pre_mlp, v_norm_post_mlp, v_w_in_a, v_conv_w_a, v_a_log_a, v_dt_bias_a, v_onorm_a, v_w_in_b, v_w_mem_kv, v_w_out, v_w_up, v_w_down):
    args = dict(locals())
    big_names = [n for n, _ in _SHARDED]
    axes = dict(_SHARDED)
    shard_shapes = [args[n].shape for n in big_names]
    rows_total = -(-sum(_rows(math.prod(s)) for s in shard_shapes) // 1024) * 1024

    exact = ("conv_w_a",) if _WIRE_DTYPE != F32 else ()
    wire = [lax.bitcast_convert_type(args[n], _WIRE_DTYPE) if n in exact else args[n].astype(_WIRE_DTYPE)
            for n in big_names]
    wire_shapes = [a.shape for a in wire]
    wire_rows = -(-sum(_rows(math.prod(s)) for s in wire_shapes) // ROW_ALIGN) * ROW_ALIGN
    gathered = _all_gather(_pack(wire, total=wire_rows))
    p = {}
    for n, b in zip(big_names, _unpack(gathered, wire_shapes, lead=(N_DEV,))):
        b = lax.bitcast_convert_type(b, F32) if n in exact else b.astype(_MXU_DTYPE)
        p[n] = _to_full(b, axes[n])
    for n in _REPLICATED:
        p[n] = args[n]
    w_flat = _pack([args[n] for n in big_names], total=rows_total)

    loss_row, grad_x, g = _local_step(x, mem, loss_target, p)
    loss = lax.psum(loss_row[0, 0], ("x", "y", "c"))

    g_blocks = _pack([_to_blocks(g[n], axes[n]).astype(_WIRE_DTYPE) for n in big_names], total=rows_total, lead=(N_DEV,))
    rep_shapes = [args[n].shape for n in _REPLICATED]
    g_small = _pack([g[n] for n in _REPLICATED], align=8)
    recv_sib = _exchange_sibling(g_blocks)
    part, own = _partial_sum(g_blocks.reshape(N_CHIP, 2, rows_total, LANE), recv_sib)
    recv_big, recv_small = _exchange_chips(part, g_small)

    m_flat = _pack([args["m_" + n] for n in big_names], total=rows_total)
    v_flat = _pack([args["v_" + n] for n in big_names], total=rows_total)
    outs_big = [_unpack(f, shard_shapes) for f in _adamw_shard(own, recv_big, w_flat, m_flat, v_flat)]
    outs_small = [_unpack(f, rep_shapes, align=8) for f in _adamw_replicated(
        recv_small, _pack([args[n] for n in _REPLICATED], align=8),
        _pack([args["m_" + n] for n in _REPLICATED], align=8), _pack([args["v_" + n] for n in _REPLICATED], align=8))]

    order = ["mem_norm", "norm_pre_mix", "norm_post_mix", "norm_pre_mlp", "norm_post_mlp", "w_in_a", "conv_w_a",
             "a_log_a", "dt_bias_a", "onorm_a", "w_in_b", "w_mem_kv", "w_out", "w_up", "w_down"]
    result = [loss, grad_x]
    for kind in range(4):
        for n in order:
            if n in axes:
                result.append(outs_big[kind][big_names.index(n)])
            else:
                result.append(outs_small[kind][_REPLICATED.index(n)])
    return tuple(result)
```

```python
import functools
import math

import jax
import jax.numpy as jnp
from jax import lax
from jax.experimental import pallas as pl
from jax.experimental.pallas import tpu as pltpu

F32 = jnp.float32
_MXU_DTYPE = jnp.bfloat16
_WIRE_DTYPE = jnp.bfloat16
_HI = lax.Precision.HIGH

D_MODEL = 1024
N_DEV = 8
N_MEM = 256
X_WIDTH = 256
N_X_HEADS = 4
X_HEAD_DIM = 64
SEQ_MIX_WIDTH = 768
LIN_HEAD_DIM = 128
N_LIN_HEADS = 6
CONV_WIDTH = 4
CHUNK = 64
SB_HEAD_DIM = 64
SB_BLOCK = 128
D_FF = 4096
EPS = 1e-6
IN_A = 3340
IN_A_PAD = 3584
IN_B = 2560
SM_COL = 3328

ADAM_LR = 0.001
ADAM_B1 = 0.9
ADAM_B2 = 0.999
ADAM_EPS = 1e-08
ADAM_WD = 0.01
ADAM_STEP = 10

LANE = 128
VMEM_LIMIT = 56 * 1024 * 1024


def _cp(sem=None):
    return pltpu.CompilerParams(dimension_semantics=sem, vmem_limit_bytes=VMEM_LIMIT)


def _pick(n, target):
    if n <= target:
        return n
    best = None
    for t in range(LANE, target + 1, LANE):
        if n % t == 0:
            best = t
    assert best is not None, (n, target)
    return best


def _dot(a, b, ca=1, cb=0):
    return lax.dot_general(a.astype(_MXU_DTYPE), b.astype(_MXU_DTYPE), (((ca,), (cb,)), ((), ())),
                           preferred_element_type=F32)


def _bdot(a, b, ca, cb):
    return lax.dot_general(a.astype(_MXU_DTYPE), b.astype(_MXU_DTYPE), (((ca,), (cb,)), ((0,), (0,))),
                           preferred_element_type=F32)


def _bdot_hi(a, b):
    return lax.dot_general(a, b, (((2,), (1,)), ((0,), (0,))), precision=_HI, preferred_element_type=F32)


def _sigmoid(x):
    return 1.0 / (1.0 + jnp.exp(-x))


def _silu(x):
    return x * _sigmoid(x)


def _softplus(x):
    return jnp.maximum(x, 0.0) + jnp.log(1.0 + jnp.exp(-jnp.abs(x)))


def _rms(x, g):
    return x * lax.rsqrt(jnp.mean(x * x, axis=-1, keepdims=True) + EPS) * g


def _norm_fwd(x, g, resid=None, out_dtype=F32, name="norm_fwd"):
    T, D = x.shape
    tm = _pick(T, 512)
    has_resid = resid is not None

    def body(*refs):
        if has_resid:
            x_ref, g_ref, r_ref, o_ref = refs
        else:
            x_ref, g_ref, o_ref = refs
        y = _rms(x_ref[...].astype(F32), g_ref[...])
        if has_resid:
            y = r_ref[...] + y
        o_ref[...] = y.astype(out_dtype)

    row = pl.BlockSpec((tm, D), lambda i: (i, 0))
    in_specs = [row, pl.BlockSpec((1, D), lambda i: (0, 0))] + ([row] if has_resid else [])
    args = (x, g) + ((resid,) if has_resid else ())
    return pl.pallas_call(body, grid=(T // tm,), in_specs=in_specs, out_specs=row,
                          out_shape=jax.ShapeDtypeStruct((T, D), out_dtype),
                          compiler_params=_cp(("parallel",)), name=name)(*args)


def _norm_bwd(dy, x, g, resid=None, name="norm_bwd"):
    T, D = x.shape
    tm = _pick(T, 512)
    has_resid = resid is not None

    def body(*refs):
        if has_resid:
            dy_ref, x_ref, g_ref, r_ref, dx_ref, dg_ref = refs
        else:
            dy_ref, x_ref, g_ref, dx_ref, dg_ref = refs
        _, vjp = jax.vjp(_rms, x_ref[...].astype(F32), g_ref[...])
        dx, dg = vjp(dy_ref[...].astype(F32))
        if has_resid:
            dx = r_ref[...] + dx
        dx_ref[...] = dx

        @pl.when(pl.program_id(0) == 0)
        def _():
            dg_ref[...] = jnp.zeros_like(dg_ref)

        dg_ref[...] += dg

    row = pl.BlockSpec((tm, D), lambda i: (i, 0))
    vec = pl.BlockSpec((1, D), lambda i: (0, 0))
    in_specs = [row, row, vec] + ([row] if has_resid else [])
    args = (dy, x, g) + ((resid,) if has_resid else ())
    return pl.pallas_call(body, grid=(T // tm,), in_specs=in_specs, out_specs=(row, vec),
                          out_shape=(jax.ShapeDtypeStruct((T, D), F32), jax.ShapeDtypeStruct((1, D), F32)),
                          compiler_params=_cp(("arbitrary",)), name=name)(*args)


def _norm_pair_fwd(y, g_post, resid, g_pre, out_dtype, name):
    T, D = y.shape
    tm = _pick(T, 512)

    def body(y_ref, gp_ref, r_ref, gn_ref, x_ref, h_ref):
        x = r_ref[...] + _rms(y_ref[...], gp_ref[...])
        x_ref[...] = x
        h_ref[...] = _rms(x, gn_ref[...]).astype(out_dtype)

    row = pl.BlockSpec((tm, D), lambda i: (i, 0))
    vec = pl.BlockSpec((1, D), lambda i: (0, 0))
    return pl.pallas_call(body, grid=(T // tm,), in_specs=[row, vec, row, vec], out_specs=(row, row),
                          out_shape=(jax.ShapeDtypeStruct((T, D), F32), jax.ShapeDtypeStruct((T, D), out_dtype)),
                          compiler_params=_cp(("parallel",)), name=name)(y, g_post, resid, g_pre)


def _norm_pair_bwd(d_h, x, g_pre, dx_resid, y, g_post, name):
    T, D = x.shape
    tm = _pick(T, 512)

    def body(dh_ref, x_ref, gn_ref, r_ref, y_ref, gp_ref, dx_ref, dy_ref, dgn_ref, dgp_ref):
        _, vjp_pre = jax.vjp(_rms, x_ref[...], gn_ref[...])
        dx, dgn = vjp_pre(dh_ref[...].astype(F32))
        dx = r_ref[...] + dx
        _, vjp_post = jax.vjp(_rms, y_ref[...], gp_ref[...])
        dy, dgp = vjp_post(dx)
        dx_ref[...] = dx
        dy_ref[...] = dy

        @pl.when(pl.program_id(0) == 0)
        def _():
            dgn_ref[...] = jnp.zeros_like(dgn_ref)
            dgp_ref[...] = jnp.zeros_like(dgp_ref)

        dgn_ref[...] += dgn
        dgp_ref[...] += dgp

    row = pl.BlockSpec((tm, D), lambda i: (i, 0))
    vec = pl.BlockSpec((1, D), lambda i: (0, 0))
    big = jax.ShapeDtypeStruct((T, D), F32)
    small = jax.ShapeDtypeStruct((1, D), F32)
    return pl.pallas_call(body, grid=(T // tm,), in_specs=[row, row, vec, row, row, vec],
                          out_specs=(row, row, vec, vec), out_shape=(big, big, small, small),
                          compiler_params=_cp(("arbitrary",)), name=name)(d_h, x, g_pre, dx_resid, y, g_post)


def _mm(a, b, *, ta=False, tb=False, out_dtypes=(F32,), epilogue=None, extras=(), name="mm",
        tm_t=1024, tn_t=1024, tk_t=1024):
    M, K = (a.shape[1], a.shape[0]) if ta else a.shape
    N = b.shape[0] if tb else b.shape[1]
    assert (b.shape[1] if tb else b.shape[0]) == K, (a.shape, b.shape, ta, tb)
    tm, tn, tk = _pick(M, tm_t), _pick(N, tn_t), _pick(K, tk_t)
    nk = K // tk
    n_extra = len(extras)
    n_out = len(out_dtypes)

    def body(*refs):
        a_ref, b_ref = refs[0], refs[1]
        e_refs = refs[2:2 + n_extra]
        o_refs = refs[2 + n_extra:2 + n_extra + n_out]

        def finish(acc):
            outs = (acc,) if epilogue is None else epilogue(acc, *[e[...] for e in e_refs])
            for o_ref, o in zip(o_refs, outs):
                o_ref[...] = o.astype(o_ref.dtype)

        d = _dot(a_ref[...], b_ref[...], 0 if ta else 1, 1 if tb else 0)
        if nk == 1:
            finish(d)
            return
        acc_ref = refs[-1]
        k = pl.program_id(2)

        @pl.when(k == 0)
        def _():
            acc_ref[...] = d

        @pl.when((k > 0) & (k < nk - 1))
        def _():
            acc_ref[...] += d

        @pl.when(k == nk - 1)
        def _():
            finish(acc_ref[...] + d)

    a_spec = pl.BlockSpec((tk, tm), lambda i, j, k: (k, i)) if ta else pl.BlockSpec((tm, tk), lambda i, j, k: (i, k))
    b_spec = pl.BlockSpec((tn, tk), lambda i, j, k: (j, k)) if tb else pl.BlockSpec((tk, tn), lambda i, j, k: (k, j))
    o_spec = pl.BlockSpec((tm, tn), lambda i, j, k: (i, j))
    outs = pl.pallas_call(
        body, grid=(M // tm, N // tn, nk),
        in_specs=[a_spec, b_spec] + [o_spec] * n_extra,
        out_specs=tuple([o_spec] * n_out),
        out_shape=tuple(jax.ShapeDtypeStruct((M, N), dt) for dt in out_dtypes),
        scratch_shapes=[pltpu.VMEM((tm, tn), F32)] if nk > 1 else [],
        compiler_params=_cp(("parallel", "parallel", "arbitrary")), name=name)(a, b, *extras)
    return outs[0] if n_out == 1 else outs


def _relu2_epilogue(acc):
    r = jnp.maximum(acc, 0.0)
    return r * r, r


def _drelu2_epilogue(acc, r):
    return (acc * (2.0 * r.astype(F32)),)


def _loss_head(x, target, name="loss_head"):
    T, D = x.shape
    tm = _pick(T, 512)

    def body(x_ref, t_ref, l_ref, dx_ref):
        e = x_ref[...] - t_ref[...]
        dx_ref[...] = e * (1.0 / D)

        @pl.when(pl.program_id(0) == 0)
        def _():
            l_ref[...] = jnp.zeros_like(l_ref)

        part = 0.5 * jnp.sum(jnp.mean(e * e, axis=-1, keepdims=True), axis=0, keepdims=True)
        l_ref[...] += jnp.broadcast_to(part, l_ref.shape)

    row = pl.BlockSpec((tm, D), lambda i: (i, 0))
    return pl.pallas_call(body, grid=(T // tm,), in_specs=[row, row],
                          out_specs=(pl.BlockSpec((1, LANE), lambda i: (0, 0)), row),
                          out_shape=(jax.ShapeDtypeStruct((1, LANE), F32), jax.ShapeDtypeStruct((T, D), F32)),
                          compiler_params=_cp(("arbitrary",)), name=name)(x, target)


def _shift_down(x, k, row):
    return jnp.where(row >= k, pltpu.roll(x, k, 0), 0.0)


def _shift_up(x, k, row, n):
    return jnp.where(row < n - k, pltpu.roll(x, n - k, 0), 0.0)


def _conv_taps(x, w, row):
    y = x * w[CONV_WIDTH - 1:CONV_WIDTH, :]
    for i in range(CONV_WIDTH - 1):
        y = y + _shift_down(x, CONV_WIDTH - 1 - i, row) * w[i:i + 1, :]
    return y


def _qkv_act(xc, j):
    s = _silu(xc)
    n = s * lax.rsqrt(jnp.sum(s * s, axis=-1, keepdims=True) + EPS)
    n = n * jnp.where(j < N_LIN_HEADS, LIN_HEAD_DIM ** -0.5, 1.0)
    return jnp.where(j < 2 * N_LIN_HEADS, n, s)


def _gdn_conv_fwd(proj, conv_w, B, S):
    nblk = 3 * N_LIN_HEADS

    def body(p_ref, w_ref, o_ref):
        j = pl.program_id(1)
        x = p_ref[...]
        row = lax.broadcasted_iota(jnp.int32, x.shape, 0)
        o_ref[...] = _qkv_act(_conv_taps(x, w_ref[...], row), j)

    blk = pl.BlockSpec((S, LANE), lambda b, j: (b, j))
    return pl.pallas_call(body, grid=(B, nblk),
                          in_specs=[blk, pl.BlockSpec((CONV_WIDTH, LANE), lambda b, j: (0, j))],
                          out_specs=blk, out_shape=jax.ShapeDtypeStruct((B * S, nblk * LANE), F32),
                          compiler_params=_cp(("parallel", "parallel")), name="gdn_conv_fwd")(proj, conv_w)


def _gdn_conv_bwd(dq, dk, dv, proj, conv_w, B, S):
    nblk = 3 * N_LIN_HEADS
    H = N_LIN_HEADS

    def body(dq_ref, dk_ref, dv_ref, p_ref, w_ref, dp_ref, dw_ref):
        j = pl.program_id(0)
        b = pl.program_id(1)
        x = p_ref[...]
        w = w_ref[...]
        row = lax.broadcasted_iota(jnp.int32, x.shape, 0)
        d_act = jnp.where(j < H, dq_ref[...], jnp.where(j < 2 * H, dk_ref[...], dv_ref[...]))
        _, vjp = jax.vjp(lambda t: _qkv_act(t, j), _conv_taps(x, w, row))
        (d_xc,) = vjp(d_act)
        dx = d_xc * w[CONV_WIDTH - 1:CONV_WIDTH, :]
        for i in range(CONV_WIDTH - 1):
            dx = dx + _shift_up(d_xc, CONV_WIDTH - 1 - i, row, S) * w[i:i + 1, :]
        dp_ref[...] = dx.astype(dp_ref.dtype)

        @pl.when(b == 0)
        def _():
            dw_ref[...] = jnp.zeros_like(dw_ref)

        for i in range(CONV_WIDTH):
            xs = x if i == CONV_WIDTH - 1 else _shift_down(x, CONV_WIDTH - 1 - i, row)
            dw_ref[i:i + 1, :] += jnp.sum(d_xc * xs, axis=0, keepdims=True)

    blk = pl.BlockSpec((S, LANE), lambda j, b: (b, j))
    wblk = pl.BlockSpec((CONV_WIDTH, LANE), lambda j, b: (0, j))
    return pl.pallas_call(
        body, grid=(nblk, B),
        in_specs=[pl.BlockSpec((S, LANE), lambda j, b: (b, jnp.clip(j, 0, H - 1))),
                  pl.BlockSpec((S, LANE), lambda j, b: (b, jnp.clip(j - H, 0, H - 1))),
                  pl.BlockSpec((S, LANE), lambda j, b: (b, jnp.clip(j - 2 * H, 0, H - 1))),
                  blk, wblk],
        out_specs=(blk, wblk),
        out_shape=(jax.ShapeDtypeStruct((B * S, nblk * LANE), _MXU_DTYPE),
                   jax.ShapeDtypeStruct((CONV_WIDTH, nblk * LANE), F32)),
        compiler_params=_cp(("parallel", "arbitrary")), name="gdn_conv_bwd")(dq, dk, dv, proj, conv_w)


def _chunk_cumsum(x, row):
    pos = row % CHUNK
    k = 1
    while k < CHUNK:
        x = x + jnp.where(pos >= k, pltpu.roll(x, k, 0), 0.0)
        k *= 2
    return x


def _chunk_rev_cumsum(x, row, n):
    pos = row % CHUNK
    k = 1
    while k < CHUNK:
        x = x + jnp.where(pos < CHUNK - k, pltpu.roll(x, n - k, 0), 0.0)
        k *= 2
    return x


def _gdn_gates_fwd(proj, a_log, dt_bias, B, S):
    H = N_LIN_HEADS

    def body(sm_ref, al_ref, dt_ref, beta_ref, gc_ref):
        sm = sm_ref[...]
        row = lax.broadcasted_iota(jnp.int32, (S, LANE), 0)
        for h in range(H):
            beta = _sigmoid(sm[:, h:h + 1])
            g = -jnp.exp(al_ref[0:1, h:h + 1]) * _softplus(sm[:, H + h:H + h + 1] + dt_ref[0:1, h:h + 1])
            beta_ref[:, h * LANE:(h + 1) * LANE] = jnp.broadcast_to(beta, (S, LANE))
            gc_ref[:, h * LANE:(h + 1) * LANE] = _chunk_cumsum(jnp.broadcast_to(g, (S, LANE)), row)

    vec = pl.BlockSpec((1, LANE), lambda b: (0, 0))
    wide = pl.BlockSpec((S, H * LANE), lambda b: (b, 0))
    return pl.pallas_call(body, grid=(B,),
                          in_specs=[pl.BlockSpec((S, LANE), lambda b: (b, SM_COL // LANE)), vec, vec],
                          out_specs=(wide, wide),
                          out_shape=(jax.ShapeDtypeStruct((B * S, H * LANE), F32),) * 2,
                          compiler_params=_cp(("parallel",)), name="gdn_gates_fwd")(proj, a_log, dt_bias)


def _gdn_gates_bwd(d_beta, d_gc, proj, a_log, dt_bias, B, S):
    H = N_LIN_HEADS

    def body(db_ref, dgc_ref, sm_ref, al_ref, dt_ref, dsm_ref, dal_ref, ddt_ref):
        sm = sm_ref[...]
        row = lax.broadcasted_iota(jnp.int32, (S, LANE), 0)
        lane = lax.broadcasted_iota(jnp.int32, (1, LANE), 1)
        dsm = jnp.zeros((S, LANE), F32)
        dal = jnp.zeros((1, LANE), F32)
        ddt = jnp.zeros((1, LANE), F32)
        for h in range(H):
            beta = _sigmoid(sm[:, h:h + 1])
            dbeta = jnp.sum(db_ref[:, h * LANE:(h + 1) * LANE], axis=-1, keepdims=True)
            d_bl = dbeta * beta * (1.0 - beta)
            dgc = jnp.sum(dgc_ref[:, h * LANE:(h + 1) * LANE], axis=-1, keepdims=True)
            dg = _chunk_rev_cumsum(jnp.broadcast_to(dgc, (S, LANE)), row, S)[:, 0:1]
            z = sm[:, H + h:H + h + 1] + dt_ref[0:1, h:h + 1]
            a = jnp.exp(al_ref[0:1, h:h + 1])
            g = -a * _softplus(z)
            d_al = dg * (-a) * _sigmoid(z)
            dsm = dsm + jnp.where(lane == h, d_bl, 0.0) + jnp.where(lane == H + h, d_al, 0.0)
            ddt = ddt + jnp.where(lane == h, jnp.sum(d_al, axis=0, keepdims=True), 0.0)
            dal = dal + jnp.where(lane == h, jnp.sum(dg * g, axis=0, keepdims=True), 0.0)
        dsm_ref[...] = dsm.astype(dsm_ref.dtype)

        @pl.when(pl.program_id(0) == 0)
        def _():
            dal_ref[...] = jnp.zeros_like(dal_ref)
            ddt_ref[...] = jnp.zeros_like(ddt_ref)

        dal_ref[...] += dal
        ddt_ref[...] += ddt

    vec = pl.BlockSpec((1, LANE), lambda b: (0, 0))
    wide = pl.BlockSpec((S, H * LANE), lambda b: (b, 0))
    return pl.pallas_call(body, grid=(B,),
                          in_specs=[wide, wide, pl.BlockSpec((S, LANE), lambda b: (b, SM_COL // LANE)), vec, vec],
                          out_specs=(pl.BlockSpec((S, LANE), lambda b: (b, 0)), vec, vec),
                          out_shape=(jax.ShapeDtypeStruct((B * S, LANE), _MXU_DTYPE),
                                     jax.ShapeDtypeStruct((1, LANE), F32), jax.ShapeDtypeStruct((1, LANE), F32)),
                          compiler_params=_cp(("arbitrary",)), name="gdn_gates_bwd")(d_beta, d_gc, proj, a_log, dt_bias)


PREP_ROWS = 1024


@jax.custom_vjp
def _unit_lower_inverse(lower):
    n, C, _ = lower.shape
    ri = lax.broadcasted_iota(jnp.int32, (C, C), 0)
    ci = lax.broadcasted_iota(jnp.int32, (C, C), 1)
    p = -lower
    inv = jnp.where((ri == ci)[None], 1.0, 0.0) + p
    for _ in range(int(math.log2(C)) - 1):
        p = _bdot_hi(p, p)
        inv = inv + _bdot_hi(inv, p)
    return inv


def _unit_lower_inverse_fwd(lower):
    inv = _unit_lower_inverse(lower)
    return inv, inv


def _unit_lower_inverse_bwd(inv, d_inv):
    inv_t = jnp.swapaxes(inv, 1, 2)
    return (-_bdot_hi(_bdot_hi(inv_t, d_inv), inv_t),)


_unit_lower_inverse.defvjp(_unit_lower_inverse_fwd, _unit_lower_inverse_bwd)


def _prep_fn(q, k, v, beta, gc):
    R = q.shape[0]
    n = R // CHUNK
    q3, k3, v3, b3, g3 = [t.reshape(n, CHUNK, LIN_HEAD_DIM) for t in (q, k, v, beta, gc)]
    ri = lax.broadcasted_iota(jnp.int32, (CHUNK, CHUNK), 0)
    ci = lax.broadcasted_iota(jnp.int32, (CHUNK, CHUNK), 1)
    causal = (ri >= ci)[None]
    strict = (ri > ci)[None]
    gcol = g3[:, :, 0:1]
    grow = jnp.swapaxes(g3, 1, 2)[:, 0:1, :]
    decay = jnp.exp(jnp.where(causal, gcol - grow, -1e30))
    kb = k3 * b3
    lower = jnp.where(strict, _bdot(kb, k3, 2, 2) * decay, 0.0)
    inv = _unit_lower_inverse(lower)
    eg = jnp.exp(g3)
    sol = _bdot_hi(inv, jnp.concatenate([v3 * b3, kb * eg], axis=-1))
    u, w = sol[..., :LIN_HEAD_DIM], sol[..., LIN_HEAD_DIM:]
    intra = _bdot(q3, k3, 2, 2) * decay
    q_dec = q3 * eg
    k_dec = k3 * jnp.exp(g3[:, CHUNK - 1:CHUNK, :] - g3)
    return (u.reshape(R, LIN_HEAD_DIM), w.reshape(R, LIN_HEAD_DIM), q_dec.reshape(R, LIN_HEAD_DIM),
            k_dec.reshape(R, LIN_HEAD_DIM), intra.reshape(R, CHUNK))


def _prep_specs(S):
    H = N_LIN_HEADS
    R = min(PREP_ROWS, S)
    nr = S // R

    def col(off):
        return pl.BlockSpec((R, LANE), lambda b, h, r: (b * nr + r, off + h))

    intra = pl.BlockSpec((None, R, CHUNK), lambda b, h, r: (h, b * nr + r, 0))
    return R, nr, col, intra


def _gdn_prep_fwd(act, beta, gc, B, S):
    H = N_LIN_HEADS
    R, nr, col, intra_spec = _prep_specs(S)
    T = B * S

    def body(q_ref, k_ref, v_ref, b_ref, g_ref, u_ref, w_ref, qd_ref, kd_ref, a_ref):
        u, w, qd, kd, a = _prep_fn(q_ref[...], k_ref[...], v_ref[...], b_ref[...], g_ref[...])
        u_ref[...] = u
        w_ref[...] = w
        qd_ref[...] = qd
        kd_ref[...] = kd
        a_ref[...] = a

    wide = jax.ShapeDtypeStruct((T, H * LANE), F32)
    return pl.pallas_call(body, grid=(B, H, nr),
                          in_specs=[col(0), col(H), col(2 * H), col(0), col(0)],
                          out_specs=(col(0), col(0), col(0), col(0), intra_spec),
                          out_shape=(wide, wide, wide, wide, jax.ShapeDtypeStruct((H, T, CHUNK), F32)),
                          compiler_params=_cp(("parallel", "parallel", "parallel")),
                          name="gdn_prep_fwd")(act, act, act, beta, gc)


def _gdn_prep_bwd(act, beta, gc, du, dw, dqd, dkd, da, B, S):
    H = N_LIN_HEADS
    R, nr, col, intra_spec = _prep_specs(S)
    T = B * S

    def body(q_ref, k_ref, v_ref, b_ref, g_ref, du_ref, dw_ref, dqd_ref, dkd_ref, da_ref,
             dq_ref, dk_ref, dv_ref, db_ref, dg_ref):
        _, vjp = jax.vjp(_prep_fn, q_ref[...], k_ref[...], v_ref[...], b_ref[...], g_ref[...])
        dq, dk, dv, db, dg = vjp((du_ref[...], dw_ref[...], dqd_ref[...], dkd_ref[...], da_ref[...]))
        dq_ref[...] = dq
        dk_ref[...] = dk
        dv_ref[...] = dv
        db_ref[...] = db
        dg_ref[...] = dg

    wide = jax.ShapeDtypeStruct((T, H * LANE), F32)
    return pl.pallas_call(body, grid=(B, H, nr),
                          in_specs=[col(0), col(H), col(2 * H), col(0), col(0),
                                    col(0), col(0), col(0), col(0), intra_spec],
                          out_specs=(col(0),) * 5, out_shape=(wide,) * 5,
                          compiler_params=_cp(("parallel", "parallel", "parallel")),
                          name="gdn_prep_bwd")(act, act, act, beta, gc, du, dw, dqd, dkd, da)


def _scan_step(u, w, qd, kd, a, g_last, state):
    v_new = u - _dot(w, state)
    o = _dot(qd, state) + _dot(a, v_new)
    new_state = state * jnp.exp(g_last) + _dot(kd, v_new, 0, 0)
    return o, new_state


SCAN_HEADS = 6
SCAN_ROWS = 512


def _scan_specs(B, S, reverse):
    HP = SCAN_HEADS
    R = min(SCAN_ROWS, S)
    nr = S // R

    def blk(r):
        return nr - 1 - r if reverse else r

    col = pl.BlockSpec((R, HP * LANE), lambda b, h, r: (b * nr + blk(r), h))
    intra = pl.BlockSpec((HP, R, CHUNK), lambda b, h, r: (h, b * nr + blk(r), 0))
    st = pl.BlockSpec((None, HP, R // CHUNK, LIN_HEAD_DIM, LIN_HEAD_DIM), lambda b, h, r: (b, h, blk(r), 0, 0))
    return R, nr, col, intra, st


def _gdn_scan_fwd(u, w, qd, kd, a, gc, B, S):
    H = N_LIN_HEADS
    R, nr, col, intra, st = _scan_specs(B, S, reverse=False)

    def body(u_ref, w_ref, qd_ref, kd_ref, a_ref, g_ref, o_ref, st_ref, carry_ref):
        @pl.when(pl.program_id(2) == 0)
        def _():
            carry_ref[...] = jnp.zeros_like(carry_ref)

        def step(c, states):
            rows = pl.ds(pl.multiple_of(c * CHUNK, CHUNK), CHUNK)
            new_states = []
            for hh, state in enumerate(states):
                cols = slice(hh * LANE, (hh + 1) * LANE)
                st_ref[hh, c] = state.astype(st_ref.dtype)
                o, new_state = _scan_step(u_ref[rows, cols], w_ref[rows, cols], qd_ref[rows, cols], kd_ref[rows, cols],
                                          a_ref[hh, rows, :], g_ref[rows, cols][CHUNK - 1:CHUNK, :], state)
                o_ref[rows, cols] = o
                new_states.append(new_state)
            return tuple(new_states)

        states = lax.fori_loop(0, R // CHUNK, step, tuple(carry_ref[hh] for hh in range(SCAN_HEADS)))
        for hh, state in enumerate(states):
            carry_ref[hh] = state

    return pl.pallas_call(body, grid=(B, H // SCAN_HEADS, nr), in_specs=[col, col, col, col, intra, col],
                          out_specs=(col, st),
                          out_shape=(jax.ShapeDtypeStruct((B * S, H * LANE), F32),
                                     jax.ShapeDtypeStruct((B, H, S // CHUNK, LIN_HEAD_DIM, LIN_HEAD_DIM), _MXU_DTYPE)),
                          scratch_shapes=[pltpu.VMEM((SCAN_HEADS, LIN_HEAD_DIM, LIN_HEAD_DIM), F32)],
                          compiler_params=_cp(("parallel", "parallel", "arbitrary")),
                          name="gdn_scan_fwd")(u, w, qd, kd, a, gc)


def _gdn_scan_bwd(u, w, qd, kd, a, gc, states, do, B, S):
    H = N_LIN_HEADS
    R, nr, col, intra, st = _scan_specs(B, S, reverse=True)
    T = B * S
    n = R // CHUNK

    def body(u_ref, w_ref, qd_ref, kd_ref, a_ref, g_ref, st_ref, do_ref,
             du_ref, dw_ref, dqd_ref, dkd_ref, da_ref, dg_ref, carry_ref):
        last = lax.broadcasted_iota(jnp.int32, (CHUNK, LANE), 0) == CHUNK - 1

        @pl.when(pl.program_id(2) == 0)
        def _():
            carry_ref[...] = jnp.zeros_like(carry_ref)

        def step(i, d_states):
            c = n - 1 - i
            rows = pl.ds(pl.multiple_of(c * CHUNK, CHUNK), CHUNK)
            d_prevs = []
            for hh, d_state in enumerate(d_states):
                cols = slice(hh * LANE, (hh + 1) * LANE)
                _, vjp = jax.vjp(_scan_step, u_ref[rows, cols], w_ref[rows, cols], qd_ref[rows, cols], kd_ref[rows, cols],
                                 a_ref[hh, rows, :], g_ref[rows, cols][CHUNK - 1:CHUNK, :], st_ref[hh, c].astype(F32))
                du, dw, dqd, dkd, da, dgl, d_prev = vjp((do_ref[rows, cols].astype(F32), d_state))
                du_ref[rows, cols] = du
                dw_ref[rows, cols] = dw
                dqd_ref[rows, cols] = dqd
                dkd_ref[rows, cols] = dkd
                da_ref[hh, rows, :] = da
                dg_ref[rows, cols] = jnp.where(last, dgl, 0.0)
                d_prevs.append(d_prev)
            return tuple(d_prevs)

        d_states = lax.fori_loop(0, n, step, tuple(carry_ref[hh] for hh in range(SCAN_HEADS)))
        for hh, d_state in enumerate(d_states):
            carry_ref[hh] = d_state

    wide = jax.ShapeDtypeStruct((T, H * LANE), F32)
    return pl.pallas_call(body, grid=(B, H // SCAN_HEADS, nr), in_specs=[col, col, col, col, intra, col, st, col],
                          out_specs=(col, col, col, col, intra, col),
                          out_shape=(wide, wide, wide, wide, jax.ShapeDtypeStruct((H, T, CHUNK), F32), wide),
                          scratch_shapes=[pltpu.VMEM((SCAN_HEADS, LIN_HEAD_DIM, LIN_HEAD_DIM), F32)],
                          compiler_params=_cp(("parallel", "parallel", "arbitrary")),
                          name="gdn_scan_bwd")(u, w, qd, kd, a, gc, states, do)


GATE_COL = 3 * SEQ_MIX_WIDTH


def _post_fn(o, gate, gain):
    return o * lax.rsqrt(jnp.mean(o * o, axis=-1, keepdims=True) + EPS) * gain * _silu(gate)


def _gdn_post_fwd(o, proj, onorm, T):
    H = N_LIN_HEADS
    tm = _pick(T, 1024)

    def body(o_ref, g_ref, n_ref, y_ref):
        y_ref[...] = _post_fn(o_ref[...], g_ref[...], n_ref[...]).astype(y_ref.dtype)

    col = pl.BlockSpec((tm, LANE), lambda i, h: (i, h))
    return pl.pallas_call(body, grid=(T // tm, H),
                          in_specs=[col, pl.BlockSpec((tm, LANE), lambda i, h: (i, GATE_COL // LANE + h)),
                                    pl.BlockSpec((1, LANE), lambda i, h: (0, 0))],
                          out_specs=col, out_shape=jax.ShapeDtypeStruct((T, H * LANE), _MXU_DTYPE),
                          compiler_params=_cp(("parallel", "parallel")), name="gdn_post_fwd")(o, proj, onorm)


def _gdn_post_bwd(d_cat, o, proj, onorm, T):
    H = N_LIN_HEADS
    tm = _pick(T, 1024)

    def body(dy_ref, o_ref, g_ref, n_ref, do_ref, dg_ref, dn_ref):
        _, vjp = jax.vjp(_post_fn, o_ref[...], g_ref[...], n_ref[...])
        do, dg, dn = vjp(dy_ref[...].astype(F32))
        do_ref[...] = do
        dg_ref[...] = dg.astype(dg_ref.dtype)

        @pl.when((pl.program_id(0) == 0) & (pl.program_id(1) == 0))
        def _():
            dn_ref[...] = jnp.zeros_like(dn_ref)

        dn_ref[...] += dn

    col = pl.BlockSpec((tm, LANE), lambda i, h: (i, h))
    vec = pl.BlockSpec((1, LANE), lambda i, h: (0, 0))
    return pl.pallas_call(body, grid=(T // tm, H),
                          in_specs=[col, col, pl.BlockSpec((tm, LANE), lambda i, h: (i, GATE_COL // LANE + h)), vec],
                          out_specs=(col, col, vec),
                          out_shape=(jax.ShapeDtypeStruct((T, H * LANE), F32),
                                     jax.ShapeDtypeStruct((T, H * LANE), _MXU_DTYPE),
                                     jax.ShapeDtypeStruct((1, LANE), F32)),
                          compiler_params=_cp(("arbitrary", "arbitrary")), name="gdn_post_bwd")(d_cat, o, proj, onorm)


def _log_sigmoid(z):
    return jnp.minimum(z, 0.0) - jnp.log(1.0 + jnp.exp(-jnp.abs(z)))


def _split_dot(x, m):
    hi = x.astype(_MXU_DTYPE)
    lo = x - hi.astype(F32)
    return _dot(hi, m) + _dot(lo, m)


SB_QB = 256
SB_KB = 256
SB_RC = 512


def _sb_scores(q2, k_j, scale, valid):
    z = _dot(q2, k_j, 1, 1) * scale
    lb = _log_sigmoid(z)
    return lb, jnp.where(valid, lb - z, 0.0)


def _sb_consts(QB, KB):
    ri = lax.broadcasted_iota(jnp.int32, (KB, KB), 0)
    ci = lax.broadcasted_iota(jnp.int32, (KB, KB), 1)
    rc = min(SB_RC, 2 * QB)
    assert rc == 2 * QB or QB % rc == 0
    row = lax.broadcasted_iota(jnp.int32, (rc, KB), 0)
    col = lax.broadcasted_iota(jnp.int32, (rc, KB), 1)
    lane = lax.broadcasted_iota(jnp.int32, (1, LANE), 1)
    return {
        "rc": rc,
        "chunks": [(ch * rc, (ch * rc) % QB) for ch in range(2 * QB // rc)],
        "row_minus_col": row % QB - col,
        "after_excl": (ri > ci).astype(_MXU_DTYPE),
        "upto_incl": (ri <= ci).astype(_MXU_DTYPE),
        "upto_excl": (ri < ci).astype(_MXU_DTYPE),
        "lane": lane,
        "head0": lane < SB_HEAD_DIM,
    }


def _sb_stack(x, c):
    return jnp.concatenate([jnp.where(c["head0"], x, 0.0), jnp.where(c["head0"], 0.0, x)], axis=0)


def _sb_unstack(x2, c, QB):
    return jnp.where(c["head0"], x2[:QB], x2[QB:])


SB_DEAD = -110.0


def _sb_first_shape(S, QB):
    return (-(-(S // QB) // 8) * 8, LANE)


def _sb_fwd(proj, B, S):
    W = SEQ_MIX_WIDTH
    P = W // LANE
    QB = min(SB_QB, S)
    KB = min(SB_KB, QB)
    scale = SB_HEAD_DIM ** -0.5
    fshape = _sb_first_shape(S, QB)

    def body(q_ref, k_ref, v_ref, o_ref, tot_ref, first_ref):
        c = _sb_consts(QB, KB)
        frow = lax.broadcasted_iota(jnp.int32, fshape, 0)

        def q_loop(i, first):
            qrows = pl.ds(pl.multiple_of(i * QB, QB), QB)
            q2 = _sb_stack(q_ref[qrows, :].astype(F32), c)
            nkb = (i + 1) * (QB // KB)

            def scores(j):
                krows = pl.ds(pl.multiple_of(j * KB, KB), KB)
                valid = c["row_minus_col"] > j * KB - i * QB
                lb, l1 = _sb_scores(q2, k_ref[krows, :], scale, valid)
                return jnp.where(valid, lb, -1e30), l1

            def k_cond(st):
                return (st[0] < nkb) & (st[1] > 0)

            def k_body(st):
                t, _, acc, r, lbm, l1 = st
                j = nkb - 1 - t
                nxt = scores(jnp.maximum(j - 1, 0))
                krows = pl.ds(pl.multiple_of(j * KB, KB), KB)
                a = jnp.exp(lbm + r + _split_dot(l1, c["after_excl"]))
                r = r + jnp.sum(l1, axis=-1, keepdims=True)
                alive = (jnp.max(r) > SB_DEAD).astype(jnp.int32)
                return (t + 1, alive, acc + _dot(a, v_ref[krows, :]), r) + nxt

            t, _, acc, r, _, _ = lax.while_loop(
                k_cond, k_body, (jnp.int32(0), jnp.int32(1), jnp.zeros((2 * QB, LANE), F32), jnp.zeros((2 * QB, 1), F32))
                + scores(nkb - 1))
            o_ref[qrows, :] = _sb_unstack(acc, c, QB)
            tot_ref[qrows, :] = _sb_unstack(jnp.broadcast_to(r, (2 * QB, LANE)), c, QB)
            return jnp.where(frow == i, (nkb - t).astype(F32), first)

        first_ref[...] = lax.fori_loop(0, S // QB, q_loop, jnp.zeros(fshape, F32))

    def col(off):
        return pl.BlockSpec((S, LANE), lambda b, p: (b, off + p))

    out = jax.ShapeDtypeStruct((B * S, W), F32)
    return pl.pallas_call(body, grid=(B, P), in_specs=[col(0), col(P), col(2 * P)],
                          out_specs=(col(0), col(0), pl.BlockSpec((None, None) + fshape, lambda b, p: (b, p, 0, 0))),
                          out_shape=(out, out, jax.ShapeDtypeStruct((B, P) + fshape, F32)),
                          compiler_params=_cp(("parallel", "parallel")), name="sb_fwd")(proj, proj, proj)


def _sb_bwd(proj, tot, first, d_cat, B, S):
    W = SEQ_MIX_WIDTH
    P = W // LANE
    QB = min(SB_QB, S)
    KB = min(SB_KB, QB)
    scale = SB_HEAD_DIM ** -0.5

    fshape = _sb_first_shape(S, QB)

    def body(q_ref, k_ref, v_ref, tot_ref, first_ref, do_ref, dq_ref, dk_ref, dv_ref, dk_acc, dv_acc):
        c = _sb_consts(QB, KB)
        frow = lax.broadcasted_iota(jnp.int32, fshape, 0)
        dk_acc[...] = jnp.zeros_like(dk_acc)
        dv_acc[...] = jnp.zeros_like(dv_acc)

        def q_loop(i, carry):
            qrows = pl.ds(pl.multiple_of(i * QB, QB), QB)
            q2 = _sb_stack(q_ref[qrows, :].astype(F32), c)
            do2 = _sb_stack(do_ref[qrows, :].astype(F32), c)
            q2_t = q2.T.astype(_MXU_DTYPE)
            do2_t = do2.T.astype(_MXU_DTYPE)
            tot = tot_ref[qrows, :]
            total = jnp.concatenate(
                [jnp.sum(jnp.where(c["lane"] == h * SB_HEAD_DIM, tot, 0.0), axis=-1, keepdims=True) for h in range(2)],
                axis=0)

            nkb = (i + 1) * (QB // KB)
            j0 = jnp.clip(jnp.max(jnp.where(frow == i, first_ref[...], 0.0)).astype(jnp.int32), 0, nkb - 1)

            def scores(j):
                krows = pl.ds(pl.multiple_of(j * KB, KB), KB)
                valid = c["row_minus_col"] > j * KB - i * QB
                lb, l1 = _sb_scores(q2, k_ref[krows, :], scale, valid)
                return jnp.where(valid, lb, -1e30), l1, _dot(do2, v_ref[krows, :], 1, 1)

            def k_loop(j, st):
                dq_acc, p_l1, p_g, lbm, l1, da = st
                nxt = scores(jnp.minimum(j + 1, nkb - 1))
                krows = pl.ds(pl.multiple_of(j * KB, KB), KB)
                tail = total - p_l1 - _split_dot(l1, c["upto_incl"])
                a = jnp.exp(lbm + tail)
                g = da * a
                g_before = p_g + _dot(g, c["upto_excl"])
                sig = jnp.exp(lbm)
                dz = ((g * (1.0 - sig) - g_before * sig) * scale).astype(_MXU_DTYPE)
                dk_acc[j] += _dot(q2_t, dz)
                dv_acc[j] += _dot(do2_t, a)
                return (dq_acc + _dot(dz, k_ref[krows, :]), p_l1 + jnp.sum(l1, axis=-1, keepdims=True),
                        p_g + jnp.sum(g, axis=-1, keepdims=True)) + nxt

            zero_col = jnp.zeros((2 * QB, 1), F32)
            dq2 = lax.fori_loop(j0, nkb, k_loop, (jnp.zeros((2 * QB, LANE), F32), zero_col, zero_col) + scores(j0))[0]
            dq_ref[qrows, :] = _sb_unstack(dq2, c, QB).astype(dq_ref.dtype)
            return carry

        lax.fori_loop(0, S // QB, q_loop, 0)
        for j in range(S // KB):
            dk_ref[j * KB:(j + 1) * KB, :] = dk_acc[j].T.astype(dk_ref.dtype)
            dv_ref[j * KB:(j + 1) * KB, :] = dv_acc[j].T.astype(dv_ref.dtype)

    def col(off):
        return pl.BlockSpec((S, LANE), lambda b, p: (b, off + p))

    out = jax.ShapeDtypeStruct((B * S, W), _MXU_DTYPE)
    return pl.pallas_call(body, grid=(B, P),
                          in_specs=[col(0), col(P), col(2 * P), col(0),
                                    pl.BlockSpec((None, None) + fshape, lambda b, p: (b, p, 0, 0)), col(0)],
                          out_specs=(col(0),) * 3, out_shape=(out,) * 3,
                          scratch_shapes=[pltpu.VMEM((S // KB, LANE, KB), F32), pltpu.VMEM((S // KB, LANE, KB), F32)],
                          compiler_params=_cp(("parallel", "parallel")), name="sb_bwd")(proj, proj, proj, tot, first, d_cat)


def _mem_fn(q, k, v):
    lane = lax.broadcasted_iota(jnp.int32, (1, X_WIDTH), 1)
    out = jnp.zeros(q.shape, F32)
    for h in range(N_X_HEADS):
        hm = (lane // X_HEAD_DIM) == h
        s = _dot(jnp.where(hm, q, 0.0), k, 1, 1) * (X_HEAD_DIM ** -0.5)
        e = jnp.exp(s - lax.stop_gradient(jnp.max(s, axis=-1, keepdims=True)))
        p = e / jnp.sum(e, axis=-1, keepdims=True)
        out = out + jnp.where(hm, _dot(p, v), 0.0)
    return out


def _mem_specs(S, q_col):
    ts = _pick(S, 1024)
    ns = S // ts
    qs = pl.BlockSpec((ts, X_WIDTH), lambda b, i: (b * ns + i, q_col // X_WIDTH))
    ks = pl.BlockSpec((N_MEM, X_WIDTH), lambda b, i: (b, 0))
    vs = pl.BlockSpec((N_MEM, X_WIDTH), lambda b, i: (b, 1))
    os = pl.BlockSpec((ts, X_WIDTH), lambda b, i: (b * ns + i, 0))
    return ts, ns, qs, ks, vs, os


def _mem_fwd(proj, q_col, mem_kv, B, S, name):
    ts, ns, qs, ks, vs, os = _mem_specs(S, q_col)

    def body(q_ref, k_ref, v_ref, o_ref):
        o_ref[...] = _mem_fn(q_ref[...].astype(F32), k_ref[...].astype(F32), v_ref[...].astype(F32))

    return pl.pallas_call(body, grid=(B, ns), in_specs=[qs, ks, vs], out_specs=os,
                          out_shape=jax.ShapeDtypeStruct((B * S, X_WIDTH), F32),
                          compiler_params=_cp(("parallel", "parallel")), name=name)(proj, mem_kv, mem_kv)


def _mem_bwd(proj, q_col, mem_kv, d_cat, B, S, name):
    ts, ns, qs, ks, vs, os = _mem_specs(S, q_col)

    def body(q_ref, k_ref, v_ref, do_ref, dq_ref, dk_ref, dv_ref):
        _, vjp = jax.vjp(_mem_fn, q_ref[...].astype(F32), k_ref[...].astype(F32), v_ref[...].astype(F32))
        dq, dk, dv = vjp(do_ref[...].astype(F32))
        dq_ref[...] = dq.astype(dq_ref.dtype)

        @pl.when(pl.program_id(1) == 0)
        def _():
            dk_ref[...] = jnp.zeros_like(dk_ref)
            dv_ref[...] = jnp.zeros_like(dv_ref)

        dk_ref[...] += dk
        dv_ref[...] += dv

    dos = pl.BlockSpec((ts, X_WIDTH), lambda b, i: (b * ns + i, SEQ_MIX_WIDTH // X_WIDTH))
    dq, dk, dv = pl.pallas_call(
        body, grid=(B, ns), in_specs=[qs, ks, vs, dos],
        out_specs=(os, pl.BlockSpec((N_MEM, X_WIDTH), lambda b, i: (b, 0)), pl.BlockSpec((N_MEM, X_WIDTH), lambda b, i: (b, 0))),
        out_shape=(jax.ShapeDtypeStruct((B * S, X_WIDTH), _MXU_DTYPE),
                   jax.ShapeDtypeStruct((B * N_MEM, X_WIDTH), F32), jax.ShapeDtypeStruct((B * N_MEM, X_WIDTH), F32)),
        compiler_params=_cp(("parallel", "arbitrary")), name=name)(proj, mem_kv, mem_kv, d_cat)
    return dq, dk, dv


def _peers():
    x, y, c = lax.axis_index("x"), lax.axis_index("y"), lax.axis_index("c")
    me = 4 * x + 2 * y + c
    out = []
    for fx, fy, fc in [(0, 0, 1), (1, 0, 0), (0, 1, 0), (1, 1, 0), (1, 0, 1), (0, 1, 1), (1, 1, 1)]:
        px, py, pc = x ^ fx, y ^ fy, c ^ fc
        out.append(((px, py, pc), 4 * px + 2 * py + pc))
    return me, out


ANY = pl.BlockSpec(memory_space=pl.ANY)


def _remote(src, dst, send_sems, recv_sems, k, dev):
    return pltpu.make_async_remote_copy(src_ref=src, dst_ref=dst, send_sem=send_sems.at[k], recv_sem=recv_sems.at[k],
                                        device_id=dev, device_id_type=pl.DeviceIdType.MESH)


def _place():
    x, y, c = lax.axis_index("x"), lax.axis_index("y"), lax.axis_index("c")
    return x, y, c, [(1 - x, y), (x, 1 - y), (1 - x, 1 - y)]


def _all_gather(shard):
    R = shard.shape[0]

    def body(x_ref, o_ref, send_sems, recv_sems, local_sem):
        x, y, c, chips = _place()
        me, sibling = (x, y, c), (x, y, 1 - c)

        def slot(px, py, pc):
            return o_ref.at[4 * px + 2 * py + pc]

        def copy(k, block, to, src=None):
            return _remote(slot(*block) if src is None else src, slot(*block), send_sems, recv_sems, k, to)

        mine = pltpu.make_async_copy(x_ref, slot(*me), local_sem)
        mine.start()
        first = [copy(0, me, sibling, src=x_ref)] + [copy(1 + j, me, (*chip, c), src=x_ref) for j, chip in enumerate(chips)]
        for cp in first:
            cp.start()
        passed = [copy(4 + j, (*chip, c), sibling) for j, chip in enumerate(chips)]
        for j, chip in enumerate(chips):
            copy(1 + j, (*chip, c), me).wait_recv()
            passed[j].start()
        copy(0, sibling, me).wait_recv()
        for j, chip in enumerate(chips):
            copy(4 + j, (*chip, 1 - c), me).wait_recv()
        for cp in first + passed:
            cp.wait_send()
        mine.wait()

    return pl.pallas_call(body, in_specs=[ANY], out_specs=ANY,
                          out_shape=jax.ShapeDtypeStruct((N_DEV, R, LANE), shard.dtype),
                          scratch_shapes=[pltpu.SemaphoreType.DMA((7,)), pltpu.SemaphoreType.DMA((7,)),
                                          pltpu.SemaphoreType.DMA],
                          compiler_params=pltpu.CompilerParams(has_side_effects=True),
                          name="all_gather_weights")(shard)


N_CHIP = 4


def _exchange_sibling(big):
    R = big.shape[1]

    def body(b_ref, o_ref, send_sems, recv_sems):
        x, y, c, _ = _place()
        copies = [_remote(b_ref.at[2 * k + (1 - c)], o_ref.at[k], send_sems, recv_sems, k, (x, y, 1 - c))
                  for k in range(N_CHIP)]
        for cp in copies:
            cp.start()
        for cp in copies:
            cp.wait()

    return pl.pallas_call(body, in_specs=[ANY], out_specs=ANY,
                          out_shape=jax.ShapeDtypeStruct((N_CHIP, R, LANE), big.dtype),
                          scratch_shapes=[pltpu.SemaphoreType.DMA((N_CHIP,)), pltpu.SemaphoreType.DMA((N_CHIP,))],
                          compiler_params=pltpu.CompilerParams(has_side_effects=True),
                          name="exchange_sibling")(big)


def _partial_sum(g4, recv):
    R = g4.shape[2]
    tr = _pick(R, 6400)

    def body(g_ref, r_ref, pw_ref, po_ref):
        x, y, c, _ = _place()
        g = jnp.where(c == 0, g_ref[0], g_ref[1]).astype(F32) + r_ref[...].astype(F32)
        pw_ref[...] = g.astype(pw_ref.dtype)

        @pl.when(pl.program_id(1) == 2 * x + y)
        def _():
            po_ref[...] = g

    return pl.pallas_call(body, grid=(R // tr, N_CHIP),
                          in_specs=[pl.BlockSpec((None, 2, tr, LANE), lambda i, k: (k, 0, i, 0)),
                                    pl.BlockSpec((None, tr, LANE), lambda i, k: (k, i, 0))],
                          out_specs=(pl.BlockSpec((None, tr, LANE), lambda i, k: (k, i, 0)),
                                     pl.BlockSpec((tr, LANE), lambda i, k: (i, 0))),
                          out_shape=(jax.ShapeDtypeStruct((N_CHIP, R, LANE), recv.dtype),
                                     jax.ShapeDtypeStruct((R, LANE), F32)),
                          compiler_params=_cp(("parallel", "arbitrary")), name="partial_sum")(g4, recv)


def _exchange_chips(part, small):
    R = part.shape[1]
    K = small.shape[0]

    def body(p_ref, s_ref, ob_ref, os_ref, send_sems, recv_sems, local_sems):
        x, y, c, chips = _place()
        my_chip = 2 * x + y
        me, peers = _peers()
        own_b = pltpu.make_async_copy(p_ref.at[my_chip], ob_ref.at[my_chip], local_sems.at[0])
        own_s = pltpu.make_async_copy(s_ref, os_ref.at[me], local_sems.at[1])
        own_b.start()
        own_s.start()
        copies = [_remote(p_ref.at[2 * px + py], ob_ref.at[my_chip], send_sems, recv_sems, j, (px, py, c))
                  for j, (px, py) in enumerate(chips)]
        copies += [_remote(s_ref, os_ref.at[me], send_sems, recv_sems, 3 + k, dev) for k, (dev, _) in enumerate(peers)]
        for cp in copies:
            cp.start()
        for j, (px, py) in enumerate(chips):
            _remote(p_ref.at[my_chip], ob_ref.at[2 * px + py], send_sems, recv_sems, j, (px, py, c)).wait_recv()
        for k, (dev, idx) in enumerate(peers):
            _remote(s_ref, os_ref.at[idx], send_sems, recv_sems, 3 + k, dev).wait_recv()
        for cp in copies:
            cp.wait_send()
        own_b.wait()
        own_s.wait()

    return pl.pallas_call(body, in_specs=[ANY, ANY], out_specs=(ANY, ANY),
                          out_shape=(jax.ShapeDtypeStruct((N_CHIP, R, LANE), part.dtype),
                                     jax.ShapeDtypeStruct((N_DEV, K, LANE), small.dtype)),
                          scratch_shapes=[pltpu.SemaphoreType.DMA((10,)), pltpu.SemaphoreType.DMA((10,)),
                                          pltpu.SemaphoreType.DMA((2,))],
                          compiler_params=pltpu.CompilerParams(has_side_effects=True),
                          name="exchange_chips")(part, small)


def _adamw_math(w, g, m, v):
    m = ADAM_B1 * m + (1.0 - ADAM_B1) * g
    v = ADAM_B2 * v + (1.0 - ADAM_B2) * (g * g)
    m_hat = m / (1.0 - ADAM_B1 ** ADAM_STEP)
    v_hat = v / (1.0 - ADAM_B2 ** ADAM_STEP)
    delta = -ADAM_LR * (m_hat / (jnp.sqrt(v_hat) + ADAM_EPS) + ADAM_WD * w)
    return delta, m, v


def _adamw_shard(own, recv, w, m, v):
    R = own.shape[0]
    tr = _pick(R, 1024)

    def body(own_ref, recv_ref, w_ref, m_ref, v_ref, g_ref, d_ref, nm_ref, nv_ref):
        x, y, _, _ = _place()
        g = own_ref[...]
        for k in range(N_CHIP):
            g = g + jnp.where(k == 2 * x + y, 0.0, recv_ref[k].astype(F32))
        delta, nm, nv = _adamw_math(w_ref[...], g, m_ref[...], v_ref[...])
        g_ref[...] = g
        d_ref[...] = delta
        nm_ref[...] = nm
        nv_ref[...] = nv

    row = pl.BlockSpec((tr, LANE), lambda i: (i, 0))
    out = jax.ShapeDtypeStruct((R, LANE), F32)
    return pl.pallas_call(body, grid=(R // tr,),
                          in_specs=[row, pl.BlockSpec((N_CHIP, tr, LANE), lambda i: (0, i, 0)), row, row, row],
                          out_specs=(row,) * 4, out_shape=(out,) * 4,
                          compiler_params=_cp(("parallel",)), name="adamw_shard")(own, recv, w, m, v)


def _adamw_replicated(parts, w, m, v):
    K = w.shape[0]

    def body(p_ref, w_ref, m_ref, v_ref, g_ref, d_ref, nm_ref, nv_ref):
        g = p_ref[0]
        for p in range(1, N_DEV):
            g = g + p_ref[p]
        delta, nm, nv = _adamw_math(w_ref[...], g, m_ref[...], v_ref[...])
        g_ref[...] = g
        d_ref[...] = delta
        nm_ref[...] = nm
        nv_ref[...] = nv

    out = jax.ShapeDtypeStruct((K, LANE), F32)
    return pl.pallas_call(body, out_shape=(out,) * 4, compiler_params=_cp(), name="adamw_replicated")(parts, w, m, v)


_SHARDED = (("w_in_a", 1), ("conv_w_a", 2), ("w_in_b", 2), ("w_mem_kv", 1), ("w_out", 1), ("w_up", 2), ("w_down", 1))
_REPLICATED = ("mem_norm", "norm_pre_mix", "norm_post_mix", "norm_pre_mlp", "norm_post_mlp", "a_log_a", "dt_bias_a", "onorm_a")
ROW_ALIGN = 16


def _rows(n_elems, align=ROW_ALIGN):
    r = -(-n_elems // LANE)
    return -(-r // align) * align


def _pack(arrays, align=ROW_ALIGN, total=None, lead=()):
    parts = []
    for a in arrays:
        n = math.prod(a.shape[len(lead):])
        flat = a.reshape(lead + (n,))
        r = _rows(n, align)
        flat = jnp.pad(flat, [(0, 0)] * len(lead) + [(0, r * LANE - n)])
        parts.append(flat.reshape(lead + (r, LANE)))
    used = sum(part.shape[len(lead)] for part in parts)
    if total is not None and used < total:
        parts.append(jnp.zeros(lead + (total - used, LANE), parts[0].dtype))
    return jnp.concatenate(parts, axis=len(lead))


def _unpack(flat, shapes, align=ROW_ALIGN, lead=()):
    outs = []
    r0 = 0
    for shp in shapes:
        n = math.prod(shp)
        r = _rows(n, align)
        part = lax.slice_in_dim(flat, r0, r0 + r, axis=len(lead))
        part = part.reshape(lead + (r * LANE,))
        part = lax.slice_in_dim(part, 0, n, axis=len(lead))
        outs.append(part.reshape(lead + tuple(shp)))
        r0 += r
    return outs


def _to_full(gathered, axis):
    g = jnp.moveaxis(gathered, 0, axis)
    shp = g.shape
    return g.reshape(shp[:axis] + (shp[axis] * shp[axis + 1],) + shp[axis + 2:])


def _to_blocks(full, axis):
    shp = full.shape
    g = full.reshape(shp[:axis] + (N_DEV, shp[axis] // N_DEV) + shp[axis + 1:])
    return jnp.moveaxis(g, axis, 0)


def _widen_in_a(w):
    main = w[:, :4 * SEQ_MIX_WIDTH]
    small = w[:, 4 * SEQ_MIX_WIDTH:4 * SEQ_MIX_WIDTH + 2 * N_LIN_HEADS]
    memq = w[:, 4 * SEQ_MIX_WIDTH + 2 * N_LIN_HEADS:]
    pad = jnp.zeros((w.shape[0], IN_A_PAD - IN_A), w.dtype)
    return jnp.concatenate([main, memq, small, pad], axis=1)


def _narrow_in_a(g):
    main = g[:, :4 * SEQ_MIX_WIDTH]
    memq = g[:, 4 * SEQ_MIX_WIDTH:4 * SEQ_MIX_WIDTH + X_WIDTH]
    small = g[:, SM_COL:SM_COL + 2 * N_LIN_HEADS]
    return jnp.concatenate([main, small, memq], axis=1)


def _row128(v):
    return jnp.pad(v.reshape(1, -1), ((0, 0), (0, LANE - v.shape[-1])))


def _local_step(x, mem, target, p):
    B, S, D = x.shape
    T = B * S
    md = _MXU_DTYPE
    x0 = x.reshape(T, D)
    tgt = target.reshape(T, D)
    memf = mem.reshape(B * N_MEM, D)
    vec = lambda a: a.reshape(1, -1)

    mem_n = _norm_fwd(memf, vec(p["mem_norm"]), out_dtype=md, name="norm_mem")
    w_in = [_widen_in_a(p["w_in_a"][0]), p["w_in_b"][0]]
    memq_col = [4 * SEQ_MIX_WIDTH, 3 * SEQ_MIX_WIDTH]
    alog = _row128(p["a_log_a"][0])
    dtb = _row128(p["dt_bias_a"][0])
    onorm = vec(p["onorm_a"][0])
    conv_w = p["conv_w_a"][0]
    saved = []
    xi = x0
    h1 = _norm_fwd(xi, vec(p["norm_pre_mix"][0]), out_dtype=md, name="norm_pre_mix0")
    for i in range(2):
        s = {"x_in": xi}
        proj = _mm(h1, w_in[i], out_dtypes=(F32 if i == 0 else md,), name=f"in_proj{i}")
        mem_kv = _mm(mem_n, p["w_mem_kv"][i], out_dtypes=(md,), name=f"mem_kv{i}")
        if i == 0:
            act = _gdn_conv_fwd(proj, conv_w, B, S)
            beta, gc = _gdn_gates_fwd(proj, alog, dtb, B, S)
            u, w, qd, kd, intra = _gdn_prep_fwd(act, beta, gc, B, S)
            o, states = _gdn_scan_fwd(u, w, qd, kd, intra, gc, B, S)
            mix = _gdn_post_fwd(o, proj, onorm, T)
            s.update(act=act, beta=beta, gc=gc, u=u, w=w, qd=qd, kd=kd, intra=intra, o=o, states=states)
        else:
            mix, tot, first = _sb_fwd(proj, B, S)
            s.update(tot=tot, first=first)
        cross = _mem_fwd(proj, memq_col[i], mem_kv, B, S, name=f"mem_fwd{i}")
        cat = jnp.concatenate([mix.astype(md), cross.astype(md)], axis=1)
        y = _mm(cat, p["w_out"][i], name=f"out_proj{i}")
        x_mid, h2 = _norm_pair_fwd(y, vec(p["norm_post_mix"][i]), xi, vec(p["norm_pre_mlp"][i]), md,
                                   name=f"norm_post_mix_pre_mlp{i}")
        a_act, r = _mm(h2, p["w_up"][i], out_dtypes=(md, md), epilogue=_relu2_epilogue, name=f"up_proj{i}")
        y2 = _mm(a_act, p["w_down"][i], name=f"down_proj{i}")
        s.update(h1=h1, proj=proj, mem_kv=mem_kv, cat=cat, y=y, x_mid=x_mid, h2=h2, a_act=a_act, r=r, y2=y2)
        saved.append(s)
        if i == 0:
            xi, h1 = _norm_pair_fwd(y2, vec(p["norm_post_mlp"][0]), x_mid, vec(p["norm_pre_mix"][1]), md,
                                    name="norm_post_mlp0_pre_mix1")
        else:
            xi = _norm_fwd(y2, vec(p["norm_post_mlp"][1]), resid=x_mid, name="norm_post_mlp1")

    loss_row, dx = _loss_head(xi, tgt)

    g = {}
    d_mem_n = None
    gn = {k: [None, None] for k in ("norm_pre_mix", "norm_post_mix", "norm_pre_mlp", "norm_post_mlp")}
    g_w_mem_kv, g_w_out, g_w_up, g_w_down = [None, None], [None, None], [None, None], [None, None]
    d_y2, gn["norm_post_mlp"][1] = _norm_bwd(dx, saved[1]["y2"], vec(p["norm_post_mlp"][1]), name="norm_post_mlp_bwd1")
    for i in (1, 0):
        s = saved[i]
        g_w_down[i] = _mm(s["a_act"], d_y2, ta=True, out_dtypes=(_WIRE_DTYPE,), name=f"down_proj_dw{i}")
        d_u = _mm(d_y2, p["w_down"][i], tb=True, out_dtypes=(md,), epilogue=_drelu2_epilogue, extras=(s["r"],),
                  name=f"down_proj_dx{i}")
        g_w_up[i] = _mm(s["h2"], d_u, ta=True, out_dtypes=(_WIRE_DTYPE,), name=f"up_proj_dw{i}")
        d_h2 = _mm(d_u, p["w_up"][i], tb=True, name=f"up_proj_dx{i}")
        dx, d_y, gn["norm_pre_mlp"][i], gn["norm_post_mix"][i] = _norm_pair_bwd(
            d_h2, s["x_mid"], vec(p["norm_pre_mlp"][i]), dx, s["y"], vec(p["norm_post_mix"][i]),
            name=f"norm_pre_mlp_post_mix_bwd{i}")
        g_w_out[i] = _mm(s["cat"], d_y, ta=True, out_dtypes=(_WIRE_DTYPE,), name=f"out_proj_dw{i}")
        d_cat = _mm(d_y, p["w_out"][i], tb=True, name=f"out_proj_dx{i}")
        d_memq, d_mk, d_mv = _mem_bwd(s["proj"], memq_col[i], s["mem_kv"], d_cat, B, S, name=f"mem_bwd{i}")
        d_mem_kv = jnp.concatenate([d_mk.astype(md), d_mv.astype(md)], axis=1)
        g_w_mem_kv[i] = _mm(mem_n, d_mem_kv, ta=True, out_dtypes=(_WIRE_DTYPE,), name=f"mem_kv_dw{i}")
        d_mn = _mm(d_mem_kv, p["w_mem_kv"][i], tb=True, name=f"mem_kv_dx{i}")
        d_mem_n = d_mn if d_mem_n is None else d_mem_n + d_mn
        if i == 0:
            d_o, d_gate, g["onorm_a"] = _gdn_post_bwd(d_cat, s["o"], s["proj"], onorm, T)
            du, dw, dqd, dkd, da, dgc_s = _gdn_scan_bwd(s["u"], s["w"], s["qd"], s["kd"], s["intra"], s["gc"],
                                                         s["states"], d_o, B, S)
            dq, dk, dv, d_beta, d_gc = _gdn_prep_bwd(s["act"], s["beta"], s["gc"], du, dw, dqd, dkd, da, B, S)
            d_qkv, g["conv_w_a"] = _gdn_conv_bwd(dq, dk, dv, s["proj"], conv_w, B, S)
            d_sm, g["a_log_a"], g["dt_bias_a"] = _gdn_gates_bwd(d_beta, d_gc + dgc_s, s["proj"], alog, dtb, B, S)
            pad = jnp.zeros((T, IN_A_PAD - SM_COL - LANE), md)
            d_proj = jnp.concatenate([d_qkv, d_gate, d_memq, d_sm, pad], axis=1)
        else:
            dq, dk, dv = _sb_bwd(s["proj"], s["tot"], s["first"], d_cat, B, S)
            d_proj = jnp.concatenate([dq, dk, dv, d_memq], axis=1)
        g_w_in = _mm(s["h1"], d_proj, ta=True, out_dtypes=(_WIRE_DTYPE,), name=f"in_proj_dw{i}")
        d_h1 = _mm(d_proj, w_in[i], tb=True, name=f"in_proj_dx{i}")
        if i == 0:
            dx, gn["norm_pre_mix"][0] = _norm_bwd(d_h1, s["x_in"], vec(p["norm_pre_mix"][0]), resid=dx,
                                                  name="norm_pre_mix_bwd0")
            g["w_in_a"] = _narrow_in_a(g_w_in)[None]
        else:
            dx, d_y2, gn["norm_pre_mix"][1], gn["norm_post_mlp"][0] = _norm_pair_bwd(
                d_h1, s["x_in"], vec(p["norm_pre_mix"][1]), dx, saved[0]["y2"], vec(p["norm_post_mlp"][0]),
                name="norm_pre_mix1_post_mlp0_bwd")
            g["w_in_b"] = g_w_in[None]
    _, g_mem_norm = _norm_bwd(d_mem_n, memf, vec(p["mem_norm"]), name="norm_mem_bwd")
    g["mem_norm"] = g_mem_norm.reshape(-1)
    for k, v in gn.items():
        g[k] = jnp.concatenate(v, axis=0)
    g["w_mem_kv"] = jnp.stack(g_w_mem_kv)
    g["w_out"] = jnp.stack(g_w_out)
    g["w_up"] = jnp.stack(g_w_up)
    g["w_down"] = jnp.stack(g_w_down)
    g["conv_w_a"] = g["conv_w_a"][None]
    g["a_log_a"] = g["a_log_a"][:, :N_LIN_HEADS]
    g["dt_bias_a"] = g["dt_bias_a"][:, :N_LIN_HEADS]
    return loss_row, dx.reshape(B, S, D), g


def kernel(x, mem, mem_norm, norm_pre_mix, norm_post_mix, norm_pre_mlp, norm_post_mlp, w_in_a, conv_w_a, a_log_a, dt_bias_a, onorm_a, w_in_b, w_mem_kv, w_out, w_up, w_down, loss_target, m_mem_norm, m_norm_pre_mix, m_norm_post_mix, m_norm_pre_mlp, m_norm_post_mlp, m_w_in_a, m_conv_w_a, m_a_log_a, m_dt_bias_a, m_onorm_a, m_w_in_b, m_w_mem_kv, m_w_out, m_w_up, m_w_down, v_mem_norm, v_norm_pre_mix, v_norm_post_mix, v_norm_pre_mlp, v_norm_post_mlp, v_w_in_a, v_conv_w_a, v_a_log_a, v_dt_bias_a, v_onorm_a, v_w_in_b, v_w_mem_kv, v_w_out, v_w_up, v_w_down):
    args = dict(locals())
    big_names = [n for n, _ in _SHARDED]
    axes = dict(_SHARDED)
    shard_shapes = [args[n].shape for n in big_names]
    rows_total = -(-sum(_rows(math.prod(s)) for s in shard_shapes) // 1024) * 1024

    exact = ("conv_w_a",) if _WIRE_DTYPE != F32 else ()
    wire = [lax.bitcast_convert_type(args[n], _WIRE_DTYPE) if n in exact else args[n].astype(_WIRE_DTYPE)
            for n in big_names]
    wire_shapes = [a.shape for a in wire]
    wire_rows = -(-sum(_rows(math.prod(s)) for s in wire_shapes) // ROW_ALIGN) * ROW_ALIGN
    gathered = _all_gather(_pack(wire, total=wire_rows))
    p = {}
    for n, b in zip(big_names, _unpack(gathered, wire_shapes, lead=(N_DEV,))):
        b = lax.bitcast_convert_type(b, F32) if n in exact else b.astype(_MXU_DTYPE)
        p[n] = _to_full(b, axes[n])
    for n in _REPLICATED:
        p[n] = args[n]
    w_flat = _pack([args[n] for n in big_names], total=rows_total)

    loss_row, grad_x, g = _local_step(x, mem, loss_target, p)
    loss = lax.psum(loss_row[0, 0], ("x", "y", "c"))

    g_blocks = _pack([_to_blocks(g[n], axes[n]).astype(_WIRE_DTYPE) for n in big_names], total=rows_total, lead=(N_DEV,))
    rep_shapes = [args[n].shape for n in _REPLICATED]
    g_small = _pack([g[n] for n in _REPLICATED], align=8)
    recv_sib = _exchange_sibling(g_blocks)
    part, own = _partial_sum(g_blocks.reshape(N_CHIP, 2, rows_total, LANE), recv_sib)
    recv_big, recv_small = _exchange_chips(part, g_small)

    m_flat = _pack([args["m_" + n] for n in big_names], total=rows_total)
    v_flat = _pack([args["v_" + n] for n in big_names], total=rows_total)
    outs_big = [_unpack(f, shard_shapes) for f in _adamw_shard(own, recv_big, w_flat, m_flat, v_flat)]
    outs_small = [_unpack(f, rep_shapes, align=8) for f in _adamw_replicated(
        recv_small, _pack([args[n] for n in _REPLICATED], align=8),
        _pack([args["m_" + n] for n in _REPLICATED], align=8), _pack([args["v_" + n] for n in _REPLICATED], align=8))]

    order = ["mem_norm", "norm_pre_mix", "norm_post_mix", "norm_pre_mlp", "norm_post_mlp", "w_in_a", "conv_w_a",
             "a_log_a", "dt_bias_a", "onorm_a", "w_in_b", "w_mem_kv", "w_out", "w_up", "w_down"]
    result = [loss, grad_x]
    for kind in range(4):
        for n in order:
            if n in axes:
                result.append(outs_big[kind][big_names.index(n)])
            else:
                result.append(outs_small[kind][_REPLICATED.index(n)])
    return tuple(result)
```

```python
import math

import jax
import jax.numpy as jnp
from jax import lax
from jax.experimental import pallas as pl
from jax.experimental.pallas import tpu as pltpu

F32 = jnp.float32
_MXU_DTYPE = jnp.bfloat16
_WIRE_DTYPE = jnp.bfloat16
_HI = lax.Precision.HIGH

D_MODEL = 1024
N_DEV = 8
N_MEM = 256
X_WIDTH = 256
N_X_HEADS = 4
X_HEAD_DIM = 64
SEQ_MIX_WIDTH = 768
LIN_HEAD_DIM = 128
N_LIN_HEADS = 6
CONV_WIDTH = 4
CHUNK = 64
SB_HEAD_DIM = 64
D_FF = 4096
EPS = 1e-6
IN_A = 3340
IN_A_PAD = 3584
IN_B = 2560
SM_COL = 3328

ADAM_LR = 0.001
ADAM_B1 = 0.9
ADAM_B2 = 0.999
ADAM_EPS = 1e-08
ADAM_WD = 0.01
ADAM_STEP = 10

LANE = 128
VMEM_LIMIT = 56 * 1024 * 1024


def _cp(sem=None):
    return pltpu.CompilerParams(dimension_semantics=sem, vmem_limit_bytes=VMEM_LIMIT)


def _pick(n, target):
    if n <= target:
        return n
    best = None
    for t in range(LANE, target + 1, LANE):
        if n % t == 0:
            best = t
    assert best is not None, (n, target)
    return best


def _dot(a, b, ca=1, cb=0):
    return lax.dot_general(a.astype(_MXU_DTYPE), b.astype(_MXU_DTYPE), (((ca,), (cb,)), ((), ())),
                           preferred_element_type=F32)


def _bdot(a, b, ca, cb):
    return lax.dot_general(a.astype(_MXU_DTYPE), b.astype(_MXU_DTYPE), (((ca,), (cb,)), ((0,), (0,))),
                           preferred_element_type=F32)


def _bdot_hi(a, b):
    return lax.dot_general(a, b, (((2,), (1,)), ((0,), (0,))), precision=_HI, preferred_element_type=F32)


def _sigmoid(x):
    return 1.0 / (1.0 + jnp.exp(-x))


def _silu(x):
    return x * _sigmoid(x)


def _softplus(x):
    return jnp.maximum(x, 0.0) + jnp.log(1.0 + jnp.exp(-jnp.abs(x)))


def _rms(x, g):
    return x * lax.rsqrt(jnp.mean(x * x, axis=-1, keepdims=True) + EPS) * g


def _norm_fwd(x, g, resid=None, out_dtype=F32, name="norm_fwd"):
    T, D = x.shape
    tm = _pick(T, 512)
    has_resid = resid is not None

    def body(*refs):
        if has_resid:
            x_ref, g_ref, r_ref, o_ref = refs
        else:
            x_ref, g_ref, o_ref = refs
        y = _rms(x_ref[...].astype(F32), g_ref[...])
        if has_resid:
            y = r_ref[...] + y
        o_ref[...] = y.astype(out_dtype)

    row = pl.BlockSpec((tm, D), lambda i: (i, 0))
    in_specs = [row, pl.BlockSpec((1, D), lambda i: (0, 0))] + ([row] if has_resid else [])
    args = (x, g) + ((resid,) if has_resid else ())
    return pl.pallas_call(body, grid=(T // tm,), in_specs=in_specs, out_specs=row,
                          out_shape=jax.ShapeDtypeStruct((T, D), out_dtype),
                          compiler_params=_cp(("parallel",)), name=name)(*args)


def _norm_bwd(dy, x, g, resid=None, name="norm_bwd"):
    T, D = x.shape
    tm = _pick(T, 512)
    has_resid = resid is not None

    def body(*refs):
        if has_resid:
            dy_ref, x_ref, g_ref, r_ref, dx_ref, dg_ref = refs
        else:
            dy_ref, x_ref, g_ref, dx_ref, dg_ref = refs
        _, vjp = jax.vjp(_rms, x_ref[...].astype(F32), g_ref[...])
        dx, dg = vjp(dy_ref[...].astype(F32))
        if has_resid:
            dx = r_ref[...] + dx
        dx_ref[...] = dx

        @pl.when(pl.program_id(0) == 0)
        def _():
            dg_ref[...] = jnp.zeros_like(dg_ref)

        dg_ref[...] += dg

    row = pl.BlockSpec((tm, D), lambda i: (i, 0))
    vec = pl.BlockSpec((1, D), lambda i: (0, 0))
    in_specs = [row, row, vec] + ([row] if has_resid else [])
    args = (dy, x, g) + ((resid,) if has_resid else ())
    return pl.pallas_call(body, grid=(T // tm,), in_specs=in_specs, out_specs=(row, vec),
                          out_shape=(jax.ShapeDtypeStruct((T, D), F32), jax.ShapeDtypeStruct((1, D), F32)),
                          compiler_params=_cp(("arbitrary",)), name=name)(*args)


def _norm_pair_fwd(y, g_post, resid, g_pre, out_dtype, name):
    T, D = y.shape
    tm = _pick(T, 512)

    def body(y_ref, gp_ref, r_ref, gn_ref, x_ref, h_ref):
        x = r_ref[...] + _rms(y_ref[...], gp_ref[...])
        x_ref[...] = x
        h_ref[...] = _rms(x, gn_ref[...]).astype(out_dtype)

    row = pl.BlockSpec((tm, D), lambda i: (i, 0))
    vec = pl.BlockSpec((1, D), lambda i: (0, 0))
    return pl.pallas_call(body, grid=(T // tm,), in_specs=[row, vec, row, vec], out_specs=(row, row),
                          out_shape=(jax.ShapeDtypeStruct((T, D), F32), jax.ShapeDtypeStruct((T, D), out_dtype)),
                          compiler_params=_cp(("parallel",)), name=name)(y, g_post, resid, g_pre)


def _norm_pair_bwd(d_h, x, g_pre, dx_resid, y, g_post, name):
    T, D = x.shape
    tm = _pick(T, 512)

    def body(dh_ref, x_ref, gn_ref, r_ref, y_ref, gp_ref, dx_ref, dy_ref, dgn_ref, dgp_ref):
        _, vjp_pre = jax.vjp(_rms, x_ref[...], gn_ref[...])
        dx, dgn = vjp_pre(dh_ref[...].astype(F32))
        dx = r_ref[...] + dx
        _, vjp_post = jax.vjp(_rms, y_ref[...], gp_ref[...])
        dy, dgp = vjp_post(dx)
        dx_ref[...] = dx
        dy_ref[...] = dy

        @pl.when(pl.program_id(0) == 0)
        def _():
            dgn_ref[...] = jnp.zeros_like(dgn_ref)
            dgp_ref[...] = jnp.zeros_like(dgp_ref)

        dgn_ref[...] += dgn
        dgp_ref[...] += dgp

    row = pl.BlockSpec((tm, D), lambda i: (i, 0))
    vec = pl.BlockSpec((1, D), lambda i: (0, 0))
    big = jax.ShapeDtypeStruct((T, D), F32)
    small = jax.ShapeDtypeStruct((1, D), F32)
    return pl.pallas_call(body, grid=(T // tm,), in_specs=[row, row, vec, row, row, vec],
                          out_specs=(row, row, vec, vec), out_shape=(big, big, small, small),
                          compiler_params=_cp(("arbitrary",)), name=name)(d_h, x, g_pre, dx_resid, y, g_post)


def _mm(a, b, *, ta=False, tb=False, out_dtypes=(F32,), epilogue=None, extras=(), name="mm",
        tm_t=1024, tn_t=1024, tk_t=1024):
    M, K = (a.shape[1], a.shape[0]) if ta else a.shape
    N = b.shape[0] if tb else b.shape[1]
    assert (b.shape[1] if tb else b.shape[0]) == K, (a.shape, b.shape, ta, tb)
    tm, tn, tk = _pick(M, tm_t), _pick(N, tn_t), _pick(K, tk_t)
    nk = K // tk
    n_extra = len(extras)
    n_out = len(out_dtypes)

    def body(*refs):
        a_ref, b_ref = refs[0], refs[1]
        e_refs = refs[2:2 + n_extra]
        o_refs = refs[2 + n_extra:2 + n_extra + n_out]

        def finish(acc):
            outs = (acc,) if epilogue is None else epilogue(acc, *[e[...] for e in e_refs])
            for o_ref, o in zip(o_refs, outs):
                o_ref[...] = o.astype(o_ref.dtype)

        d = _dot(a_ref[...], b_ref[...], 0 if ta else 1, 1 if tb else 0)
        if nk == 1:
            finish(d)
            return
        acc_ref = refs[-1]
        k = pl.program_id(2)

        @pl.when(k == 0)
        def _():
            acc_ref[...] = d

        @pl.when((k > 0) & (k < nk - 1))
        def _():
            acc_ref[...] += d

        @pl.when(k == nk - 1)
        def _():
            finish(acc_ref[...] + d)

    a_spec = pl.BlockSpec((tk, tm), lambda i, j, k: (k, i)) if ta else pl.BlockSpec((tm, tk), lambda i, j, k: (i, k))
    b_spec = pl.BlockSpec((tn, tk), lambda i, j, k: (j, k)) if tb else pl.BlockSpec((tk, tn), lambda i, j, k: (k, j))
    o_spec = pl.BlockSpec((tm, tn), lambda i, j, k: (i, j))
    outs = pl.pallas_call(
        body, grid=(M // tm, N // tn, nk),
        in_specs=[a_spec, b_spec] + [o_spec] * n_extra,
        out_specs=tuple([o_spec] * n_out),
        out_shape=tuple(jax.ShapeDtypeStruct((M, N), dt) for dt in out_dtypes),
        scratch_shapes=[pltpu.VMEM((tm, tn), F32)] if nk > 1 else [],
        compiler_params=_cp(("parallel", "parallel", "arbitrary")), name=name)(a, b, *extras)
    return outs[0] if n_out == 1 else outs


def _relu2_epilogue(acc):
    r = jnp.maximum(acc, 0.0)
    return (r * r,)


def _drelu2_epilogue(acc, a):
    return (acc * (2.0 * jnp.sqrt(a.astype(F32))),)


def _loss_head(x, target, name="loss_head"):
    T, D = x.shape
    tm = _pick(T, 512)

    def body(x_ref, t_ref, l_ref, dx_ref):
        e = x_ref[...] - t_ref[...]
        dx_ref[...] = e * (1.0 / D)

        @pl.when(pl.program_id(0) == 0)
        def _():
            l_ref[...] = jnp.zeros_like(l_ref)

        part = 0.5 * jnp.sum(jnp.mean(e * e, axis=-1, keepdims=True), axis=0, keepdims=True)
        l_ref[...] += jnp.broadcast_to(part, l_ref.shape)

    row = pl.BlockSpec((tm, D), lambda i: (i, 0))
    return pl.pallas_call(body, grid=(T // tm,), in_specs=[row, row],
                          out_specs=(pl.BlockSpec((1, LANE), lambda i: (0, 0)), row),
                          out_shape=(jax.ShapeDtypeStruct((1, LANE), F32), jax.ShapeDtypeStruct((T, D), F32)),
                          compiler_params=_cp(("arbitrary",)), name=name)(x, target)


SUBLANES = 8


def _shift_down(x, k, row):
    r = pltpu.roll(x, k, 0)
    top = jnp.where(row[:SUBLANES] >= k, r[:SUBLANES], 0.0)
    return jnp.concatenate([top, r[SUBLANES:]], axis=0)


def _shift_up(x, k, row, n):
    r = pltpu.roll(x, n - k, 0)
    bottom = jnp.where(row[n - SUBLANES:] < n - k, r[n - SUBLANES:], 0.0)
    return jnp.concatenate([r[:n - SUBLANES], bottom], axis=0)


def _conv_taps(x, w, row):
    y = x * w[CONV_WIDTH - 1:CONV_WIDTH, :]
    for i in range(CONV_WIDTH - 1):
        y = y + _shift_down(x, CONV_WIDTH - 1 - i, row) * w[i:i + 1, :]
    return y


def _qkv_act(xc, j):
    s = _silu(xc)
    n = s * lax.rsqrt(jnp.sum(s * s, axis=-1, keepdims=True) + EPS)
    n = n * jnp.where(j < N_LIN_HEADS, LIN_HEAD_DIM ** -0.5, 1.0)
    return jnp.where(j < 2 * N_LIN_HEADS, n, s)


def _gdn_conv_fwd(proj, conv_w, B, S):
    nblk = 3 * N_LIN_HEADS

    def body(p_ref, w_ref, o_ref):
        j = pl.program_id(1)
        x = p_ref[...]
        row = lax.broadcasted_iota(jnp.int32, x.shape, 0)
        o_ref[...] = _qkv_act(_conv_taps(x, w_ref[...], row), j)

    blk = pl.BlockSpec((S, LANE), lambda b, j: (b, j))
    return pl.pallas_call(body, grid=(B, nblk),
                          in_specs=[blk, pl.BlockSpec((CONV_WIDTH, LANE), lambda b, j: (0, j))],
                          out_specs=blk, out_shape=jax.ShapeDtypeStruct((B * S, nblk * LANE), F32),
                          compiler_params=_cp(("parallel", "parallel")), name="gdn_conv_fwd")(proj, conv_w)


def _gdn_conv_bwd(dq, dk, dv, proj, conv_w, B, S):
    nblk = 3 * N_LIN_HEADS
    H = N_LIN_HEADS

    def body(dq_ref, dk_ref, dv_ref, p_ref, w_ref, dp_ref, dw_ref):
        j = pl.program_id(0)
        b = pl.program_id(1)
        x = p_ref[...]
        w = w_ref[...]
        row = lax.broadcasted_iota(jnp.int32, x.shape, 0)
        d_act = jnp.where(j < H, dq_ref[...], jnp.where(j < 2 * H, dk_ref[...], dv_ref[...]))
        _, vjp = jax.vjp(lambda t: _qkv_act(t, j), _conv_taps(x, w, row))
        (d_xc,) = vjp(d_act)
        dx = d_xc * w[CONV_WIDTH - 1:CONV_WIDTH, :]
        for i in range(CONV_WIDTH - 1):
            dx = dx + _shift_up(d_xc, CONV_WIDTH - 1 - i, row, S) * w[i:i + 1, :]
        dp_ref[...] = dx.astype(dp_ref.dtype)

        @pl.when(b == 0)
        def _():
            dw_ref[...] = jnp.zeros_like(dw_ref)

        for i in range(CONV_WIDTH):
            xs = x if i == CONV_WIDTH - 1 else _shift_down(x, CONV_WIDTH - 1 - i, row)
            dw_ref[i:i + 1, :] += jnp.sum(d_xc * xs, axis=0, keepdims=True)

    blk = pl.BlockSpec((S, LANE), lambda j, b: (b, j))
    wblk = pl.BlockSpec((CONV_WIDTH, LANE), lambda j, b: (0, j))
    return pl.pallas_call(
        body, grid=(nblk, B),
        in_specs=[pl.BlockSpec((S, LANE), lambda j, b: (b, jnp.clip(j, 0, H - 1))),
                  pl.BlockSpec((S, LANE), lambda j, b: (b, jnp.clip(j - H, 0, H - 1))),
                  pl.BlockSpec((S, LANE), lambda j, b: (b, jnp.clip(j - 2 * H, 0, H - 1))),
                  blk, wblk],
        out_specs=(blk, wblk),
        out_shape=(jax.ShapeDtypeStruct((B * S, nblk * LANE), _MXU_DTYPE),
                   jax.ShapeDtypeStruct((CONV_WIDTH, nblk * LANE), F32)),
        compiler_params=_cp(("parallel", "arbitrary")), name="gdn_conv_bwd")(dq, dk, dv, proj, conv_w)


def _chunk_cumsum(x, row):
    pos = row % CHUNK
    k = 1
    while k < CHUNK:
        x = x + jnp.where(pos >= k, pltpu.roll(x, k, 0), 0.0)
        k *= 2
    return x


def _chunk_rev_cumsum(x, row, n):
    pos = row % CHUNK
    k = 1
    while k < CHUNK:
        x = x + jnp.where(pos < CHUNK - k, pltpu.roll(x, n - k, 0), 0.0)
        k *= 2
    return x


def _gdn_gates_fwd(proj, a_log, dt_bias, B, S):
    H = N_LIN_HEADS

    def body(sm_ref, al_ref, dt_ref, beta_ref, gc_ref):
        sm = sm_ref[...]
        row = lax.broadcasted_iota(jnp.int32, (S, LANE), 0)
        for h in range(H):
            beta = _sigmoid(sm[:, h:h + 1])
            g = -jnp.exp(al_ref[0:1, h:h + 1]) * _softplus(sm[:, H + h:H + h + 1] + dt_ref[0:1, h:h + 1])
            beta_ref[:, h * LANE:(h + 1) * LANE] = jnp.broadcast_to(beta, (S, LANE))
            gc_ref[:, h * LANE:(h + 1) * LANE] = _chunk_cumsum(jnp.broadcast_to(g, (S, LANE)), row)

    vec = pl.BlockSpec((1, LANE), lambda b: (0, 0))
    wide = pl.BlockSpec((S, H * LANE), lambda b: (b, 0))
    return pl.pallas_call(body, grid=(B,),
                          in_specs=[pl.BlockSpec((S, LANE), lambda b: (b, SM_COL // LANE)), vec, vec],
                          out_specs=(wide, wide),
                          out_shape=(jax.ShapeDtypeStruct((B * S, H * LANE), F32),) * 2,
                          compiler_params=_cp(("parallel",)), name="gdn_gates_fwd")(proj, a_log, dt_bias)


def _gdn_gates_bwd(d_beta, d_gc, proj, a_log, dt_bias, B, S):
    H = N_LIN_HEADS

    def body(db_ref, dgc_ref, sm_ref, al_ref, dt_ref, dsm_ref, dal_ref, ddt_ref):
        sm = sm_ref[...]
        row = lax.broadcasted_iota(jnp.int32, (S, LANE), 0)
        lane = lax.broadcasted_iota(jnp.int32, (1, LANE), 1)
        dsm = jnp.zeros((S, LANE), F32)
        dal = jnp.zeros((1, LANE), F32)
        ddt = jnp.zeros((1, LANE), F32)
        for h in range(H):
            beta = _sigmoid(sm[:, h:h + 1])
            dbeta = jnp.sum(db_ref[:, h * LANE:(h + 1) * LANE], axis=-1, keepdims=True)
            d_bl = dbeta * beta * (1.0 - beta)
            dgc = jnp.sum(dgc_ref[:, h * LANE:(h + 1) * LANE], axis=-1, keepdims=True)
            dg = _chunk_rev_cumsum(jnp.broadcast_to(dgc, (S, LANE)), row, S)[:, 0:1]
            z = sm[:, H + h:H + h + 1] + dt_ref[0:1, h:h + 1]
            a = jnp.exp(al_ref[0:1, h:h + 1])
            g = -a * _softplus(z)
            d_al = dg * (-a) * _sigmoid(z)
            dsm = dsm + jnp.where(lane == h, d_bl, 0.0) + jnp.where(lane == H + h, d_al, 0.0)
            ddt = ddt + jnp.where(lane == h, jnp.sum(d_al, axis=0, keepdims=True), 0.0)
            dal = dal + jnp.where(lane == h, jnp.sum(dg * g, axis=0, keepdims=True), 0.0)
        dsm_ref[...] = dsm.astype(dsm_ref.dtype)

        @pl.when(pl.program_id(0) == 0)
        def _():
            dal_ref[...] = jnp.zeros_like(dal_ref)
            ddt_ref[...] = jnp.zeros_like(ddt_ref)

        dal_ref[...] += dal
        ddt_ref[...] += ddt

    vec = pl.BlockSpec((1, LANE), lambda b: (0, 0))
    wide = pl.BlockSpec((S, H * LANE), lambda b: (b, 0))
    return pl.pallas_call(body, grid=(B,),
                          in_specs=[wide, wide, pl.BlockSpec((S, LANE), lambda b: (b, SM_COL // LANE)), vec, vec],
                          out_specs=(pl.BlockSpec((S, LANE), lambda b: (b, 0)), vec, vec),
                          out_shape=(jax.ShapeDtypeStruct((B * S, LANE), _MXU_DTYPE),
                                     jax.ShapeDtypeStruct((1, LANE), F32), jax.ShapeDtypeStruct((1, LANE), F32)),
                          compiler_params=_cp(("arbitrary",)), name="gdn_gates_bwd")(d_beta, d_gc, proj, a_log, dt_bias)


PREP_ROWS = 1024


@jax.custom_vjp
def _unit_lower_inverse(lower):
    n, C, _ = lower.shape
    ri = lax.broadcasted_iota(jnp.int32, (C, C), 0)
    ci = lax.broadcasted_iota(jnp.int32, (C, C), 1)
    p = -lower
    inv = jnp.where((ri == ci)[None], 1.0, 0.0) + p
    for _ in range(int(math.log2(C)) - 1):
        p = _bdot_hi(p, p)
        inv = inv + _bdot_hi(inv, p)
    return inv


def _unit_lower_inverse_fwd(lower):
    inv = _unit_lower_inverse(lower)
    return inv, inv


def _unit_lower_inverse_bwd(inv, d_inv):
    inv_t = jnp.swapaxes(inv, 1, 2)
    return (-_bdot_hi(_bdot_hi(inv_t, d_inv), inv_t),)


_unit_lower_inverse.defvjp(_unit_lower_inverse_fwd, _unit_lower_inverse_bwd)


def _prep_fn(q, k, v, beta, gc):
    R = q.shape[0]
    n = R // CHUNK
    q3, k3, v3, b3, g3 = [t.reshape(n, CHUNK, LIN_HEAD_DIM) for t in (q, k, v, beta, gc)]
    ri = lax.broadcasted_iota(jnp.int32, (CHUNK, CHUNK), 0)
    ci = lax.broadcasted_iota(jnp.int32, (CHUNK, CHUNK), 1)
    causal = (ri >= ci)[None]
    strict = (ri > ci)[None]
    gcol = g3[:, :, 0:1]
    grow = jnp.swapaxes(g3, 1, 2)[:, 0:1, :]
    decay = jnp.exp(jnp.where(causal, gcol - grow, -1e30))
    kb = k3 * b3
    lower = jnp.where(strict, _bdot(kb, k3, 2, 2) * decay, 0.0)
    inv = _unit_lower_inverse(lower)
    eg = jnp.exp(g3)
    sol = _bdot_hi(inv, jnp.concatenate([v3 * b3, kb * eg], axis=-1))
    u, w = sol[..., :LIN_HEAD_DIM], sol[..., LIN_HEAD_DIM:]
    intra = _bdot(q3, k3, 2, 2) * decay
    q_dec = q3 * eg
    k_dec = k3 * jnp.exp(g3[:, CHUNK - 1:CHUNK, :] - g3)
    return (u.reshape(R, LIN_HEAD_DIM), w.reshape(R, LIN_HEAD_DIM), q_dec.reshape(R, LIN_HEAD_DIM),
            k_dec.reshape(R, LIN_HEAD_DIM), intra.reshape(R, CHUNK))


def _prep_specs(S):
    H = N_LIN_HEADS
    R = min(PREP_ROWS, S)
    nr = S // R

    def col(off):
        return pl.BlockSpec((R, LANE), lambda b, h, r: (b * nr + r, off + h))

    intra = pl.BlockSpec((None, R, CHUNK), lambda b, h, r: (h, b * nr + r, 0))
    return R, nr, col, intra


def _gdn_prep_fwd(act, beta, gc, B, S):
    H = N_LIN_HEADS
    R, nr, col, intra_spec = _prep_specs(S)
    T = B * S

    def body(q_ref, k_ref, v_ref, b_ref, g_ref, u_ref, w_ref, qd_ref, kd_ref, a_ref):
        u, w, qd, kd, a = _prep_fn(q_ref[...], k_ref[...], v_ref[...], b_ref[...], g_ref[...])
        u_ref[...] = u
        w_ref[...] = w
        qd_ref[...] = qd
        kd_ref[...] = kd
        a_ref[...] = a

    wide = jax.ShapeDtypeStruct((T, H * LANE), F32)
    return pl.pallas_call(body, grid=(B, H, nr),
                          in_specs=[col(0), col(H), col(2 * H), col(0), col(0)],
                          out_specs=(col(0), col(0), col(0), col(0), intra_spec),
                          out_shape=(wide, wide, wide, wide, jax.ShapeDtypeStruct((H, T, CHUNK), F32)),
                          compiler_params=_cp(("parallel", "parallel", "parallel")),
                          name="gdn_prep_fwd")(act, act, act, beta, gc)


def _gdn_prep_bwd(act, beta, gc, du, dw, dqd, dkd, da, B, S):
    H = N_LIN_HEADS
    R, nr, col, intra_spec = _prep_specs(S)
    T = B * S

    def body(q_ref, k_ref, v_ref, b_ref, g_ref, du_ref, dw_ref, dqd_ref, dkd_ref, da_ref,
             dq_ref, dk_ref, dv_ref, db_ref, dg_ref):
        _, vjp = jax.vjp(_prep_fn, q_ref[...], k_ref[...], v_ref[...], b_ref[...], g_ref[...])
        dq, dk, dv, db, dg = vjp((du_ref[...], dw_ref[...], dqd_ref[...], dkd_ref[...], da_ref[...]))
        dq_ref[...] = dq
        dk_ref[...] = dk
        dv_ref[...] = dv
        db_ref[...] = db
        dg_ref[...] = dg

    wide = jax.ShapeDtypeStruct((T, H * LANE), F32)
    return pl.pallas_call(body, grid=(B, H, nr),
                          in_specs=[col(0), col(H), col(2 * H), col(0), col(0),
                                    col(0), col(0), col(0), col(0), intra_spec],
                          out_specs=(col(0),) * 5, out_shape=(wide,) * 5,
                          compiler_params=_cp(("parallel", "parallel", "parallel")),
                          name="gdn_prep_bwd")(act, act, act, beta, gc, du, dw, dqd, dkd, da)


def _scan_step(u, w, qd, kd, a, g_last, state):
    v_new = u - _dot(w, state)
    o = _dot(qd, state) + _dot(a, v_new)
    new_state = state * jnp.exp(g_last) + _dot(kd, v_new, 0, 0)
    return o, new_state


SCAN_HEADS = 6
SCAN_ROWS = 512


def _scan_specs(B, S, reverse):
    HP = SCAN_HEADS
    R = min(SCAN_ROWS, S)
    nr = S // R

    def blk(r):
        return nr - 1 - r if reverse else r

    col = pl.BlockSpec((R, HP * LANE), lambda b, h, r: (b * nr + blk(r), h))
    intra = pl.BlockSpec((HP, R, CHUNK), lambda b, h, r: (h, b * nr + blk(r), 0))
    st = pl.BlockSpec((None, HP, R // CHUNK, LIN_HEAD_DIM, LIN_HEAD_DIM), lambda b, h, r: (b, h, blk(r), 0, 0))
    return R, nr, col, intra, st


def _gdn_scan_fwd(u, w, qd, kd, a, gc, B, S):
    H = N_LIN_HEADS
    R, nr, col, intra, st = _scan_specs(B, S, reverse=False)

    def body(u_ref, w_ref, qd_ref, kd_ref, a_ref, g_ref, o_ref, st_ref, carry_ref):
        @pl.when(pl.program_id(2) == 0)
        def _():
            carry_ref[...] = jnp.zeros_like(carry_ref)

        def step(c, states):
            rows = pl.ds(pl.multiple_of(c * CHUNK, CHUNK), CHUNK)
            new_states = []
            for hh, state in enumerate(states):
                cols = slice(hh * LANE, (hh + 1) * LANE)
                st_ref[hh, c] = state.astype(st_ref.dtype)
                o, new_state = _scan_step(u_ref[rows, cols], w_ref[rows, cols], qd_ref[rows, cols], kd_ref[rows, cols],
                                          a_ref[hh, rows, :], g_ref[rows, cols][CHUNK - 1:CHUNK, :], state)
                o_ref[rows, cols] = o
                new_states.append(new_state)
            return tuple(new_states)

        states = lax.fori_loop(0, R // CHUNK, step, tuple(carry_ref[hh] for hh in range(SCAN_HEADS)))
        for hh, state in enumerate(states):
            carry_ref[hh] = state

    return pl.pallas_call(body, grid=(B, H // SCAN_HEADS, nr), in_specs=[col, col, col, col, intra, col],
                          out_specs=(col, st),
                          out_shape=(jax.ShapeDtypeStruct((B * S, H * LANE), F32),
                                     jax.ShapeDtypeStruct((B, H, S // CHUNK, LIN_HEAD_DIM, LIN_HEAD_DIM), _MXU_DTYPE)),
                          scratch_shapes=[pltpu.VMEM((SCAN_HEADS, LIN_HEAD_DIM, LIN_HEAD_DIM), F32)],
                          compiler_params=_cp(("parallel", "parallel", "arbitrary")),
                          name="gdn_scan_fwd")(u, w, qd, kd, a, gc)


def _gdn_scan_bwd(u, w, qd, kd, a, gc, states, do, B, S):
    H = N_LIN_HEADS
    R, nr, col, intra, st = _scan_specs(B, S, reverse=True)
    T = B * S
    n = R // CHUNK

    def body(u_ref, w_ref, qd_ref, kd_ref, a_ref, g_ref, st_ref, do_ref,
             du_ref, dw_ref, dqd_ref, dkd_ref, da_ref, dg_ref, carry_ref):
        last = lax.broadcasted_iota(jnp.int32, (CHUNK, LANE), 0) == CHUNK - 1

        @pl.when(pl.program_id(2) == 0)
        def _():
            carry_ref[...] = jnp.zeros_like(carry_ref)

        def step(i, d_states):
            c = n - 1 - i
            rows = pl.ds(pl.multiple_of(c * CHUNK, CHUNK), CHUNK)
            d_prevs = []
            for hh, d_state in enumerate(d_states):
                cols = slice(hh * LANE, (hh + 1) * LANE)
                _, vjp = jax.vjp(_scan_step, u_ref[rows, cols], w_ref[rows, cols], qd_ref[rows, cols], kd_ref[rows, cols],
                                 a_ref[hh, rows, :], g_ref[rows, cols][CHUNK - 1:CHUNK, :], st_ref[hh, c].astype(F32))
                du, dw, dqd, dkd, da, dgl, d_prev = vjp((do_ref[rows, cols].astype(F32), d_state))
                du_ref[rows, cols] = du
                dw_ref[rows, cols] = dw
                dqd_ref[rows, cols] = dqd
                dkd_ref[rows, cols] = dkd
                da_ref[hh, rows, :] = da
                dg_ref[rows, cols] = jnp.where(last, dgl, 0.0)
                d_prevs.append(d_prev)
            return tuple(d_prevs)

        d_states = lax.fori_loop(0, n, step, tuple(carry_ref[hh] for hh in range(SCAN_HEADS)))
        for hh, d_state in enumerate(d_states):
            carry_ref[hh] = d_state

    wide = jax.ShapeDtypeStruct((T, H * LANE), F32)
    return pl.pallas_call(body, grid=(B, H // SCAN_HEADS, nr), in_specs=[col, col, col, col, intra, col, st, col],
                          out_specs=(col, col, col, col, intra, col),
                          out_shape=(wide, wide, wide, wide, jax.ShapeDtypeStruct((H, T, CHUNK), F32), wide),
                          scratch_shapes=[pltpu.VMEM((SCAN_HEADS, LIN_HEAD_DIM, LIN_HEAD_DIM), F32)],
                          compiler_params=_cp(("parallel", "parallel", "arbitrary")),
                          name="gdn_scan_bwd")(u, w, qd, kd, a, gc, states, do)


GATE_COL = 3 * SEQ_MIX_WIDTH


def _post_fn(o, gate, gain):
    return o * lax.rsqrt(jnp.mean(o * o, axis=-1, keepdims=True) + EPS) * gain * _silu(gate)


def _gdn_post_fwd(o, proj, onorm, T):
    H = N_LIN_HEADS
    tm = _pick(T, 1024)

    def body(o_ref, g_ref, n_ref, y_ref):
        y_ref[...] = _post_fn(o_ref[...], g_ref[...], n_ref[...]).astype(y_ref.dtype)

    col = pl.BlockSpec((tm, LANE), lambda i, h: (i, h))
    return pl.pallas_call(body, grid=(T // tm, H),
                          in_specs=[col, pl.BlockSpec((tm, LANE), lambda i, h: (i, GATE_COL // LANE + h)),
                                    pl.BlockSpec((1, LANE), lambda i, h: (0, 0))],
                          out_specs=col, out_shape=jax.ShapeDtypeStruct((T, H * LANE), _MXU_DTYPE),
                          compiler_params=_cp(("parallel", "parallel")), name="gdn_post_fwd")(o, proj, onorm)


def _gdn_post_bwd(d_cat, o, proj, onorm, T):
    H = N_LIN_HEADS
    tm = _pick(T, 1024)

    def body(dy_ref, o_ref, g_ref, n_ref, do_ref, dg_ref, dn_ref):
        _, vjp = jax.vjp(_post_fn, o_ref[...], g_ref[...], n_ref[...])
        do, dg, dn = vjp(dy_ref[...].astype(F32))
        do_ref[...] = do
        dg_ref[...] = dg.astype(dg_ref.dtype)

        @pl.when((pl.program_id(0) == 0) & (pl.program_id(1) == 0))
        def _():
            dn_ref[...] = jnp.zeros_like(dn_ref)

        dn_ref[...] += dn

    col = pl.BlockSpec((tm, LANE), lambda i, h: (i, h))
    vec = pl.BlockSpec((1, LANE), lambda i, h: (0, 0))
    return pl.pallas_call(body, grid=(T // tm, H),
                          in_specs=[col, col, pl.BlockSpec((tm, LANE), lambda i, h: (i, GATE_COL // LANE + h)), vec],
                          out_specs=(col, col, vec),
                          out_shape=(jax.ShapeDtypeStruct((T, H * LANE), F32),
                                     jax.ShapeDtypeStruct((T, H * LANE), _MXU_DTYPE),
                                     jax.ShapeDtypeStruct((1, LANE), F32)),
                          compiler_params=_cp(("arbitrary", "arbitrary")), name="gdn_post_bwd")(d_cat, o, proj, onorm)


def _log_sigmoid(z):
    return jnp.minimum(z, 0.0) - jnp.log(1.0 + jnp.exp(-jnp.abs(z)))


def _split_dot(x, m):
    hi = x.astype(_MXU_DTYPE)
    lo = x - hi.astype(F32)
    return _dot(hi, m) + _dot(lo, m)


SB_QB = 256
SB_KB = 256


def _sb_scores(q2, k_j, scale, valid):
    z = _dot(q2, k_j, 1, 1) * scale
    lb = _log_sigmoid(z)
    return lb, jnp.where(valid, lb - z, 0.0)


def _sb_consts(QB, KB):
    ri = lax.broadcasted_iota(jnp.int32, (KB, KB), 0)
    ci = lax.broadcasted_iota(jnp.int32, (KB, KB), 1)
    row = lax.broadcasted_iota(jnp.int32, (2 * QB, KB), 0)
    col = lax.broadcasted_iota(jnp.int32, (2 * QB, KB), 1)
    lane = lax.broadcasted_iota(jnp.int32, (1, LANE), 1)
    return {
        "row_minus_col": row % QB - col,
        "after_excl": (ri > ci).astype(_MXU_DTYPE),
        "upto_incl": (ri <= ci).astype(_MXU_DTYPE),
        "upto_excl": (ri < ci).astype(_MXU_DTYPE),
        "lane": lane,
        "head0": lane < SB_HEAD_DIM,
    }


def _sb_stack(x, c):
    return jnp.concatenate([jnp.where(c["head0"], x, 0.0), jnp.where(c["head0"], 0.0, x)], axis=0)


def _sb_unstack(x2, c, QB):
    return jnp.where(c["head0"], x2[:QB], x2[QB:])


SB_DEAD = -110.0


def _sb_first_shape(S, QB):
    return (-(-(S // QB) // 8) * 8, LANE)


def _sb_fwd(proj, B, S):
    W = SEQ_MIX_WIDTH
    P = W // LANE
    QB = min(SB_QB, S)
    KB = min(SB_KB, QB)
    scale = SB_HEAD_DIM ** -0.5
    fshape = _sb_first_shape(S, QB)

    def body(q_ref, k_ref, v_ref, o_ref, tot_ref, first_ref):
        c = _sb_consts(QB, KB)
        frow = lax.broadcasted_iota(jnp.int32, fshape, 0)

        def q_loop(i, first):
            qrows = pl.ds(pl.multiple_of(i * QB, QB), QB)
            q2 = _sb_stack(q_ref[qrows, :].astype(F32), c)
            nkb = (i + 1) * (QB // KB)

            def scores(j):
                krows = pl.ds(pl.multiple_of(j * KB, KB), KB)
                valid = c["row_minus_col"] > j * KB - i * QB
                lb, l1 = _sb_scores(q2, k_ref[krows, :], scale, valid)
                return jnp.where(valid, lb, -1e30), l1

            def k_cond(st):
                return (st[0] < nkb) & (st[1] > 0)

            def k_body(st):
                t, _, acc, r, lbm, l1 = st
                j = nkb - 1 - t
                nxt = scores(jnp.maximum(j - 1, 0))
                krows = pl.ds(pl.multiple_of(j * KB, KB), KB)
                a = jnp.exp(lbm + r + _split_dot(l1, c["after_excl"]))
                r = r + jnp.sum(l1, axis=-1, keepdims=True)
                alive = (jnp.max(r) > SB_DEAD).astype(jnp.int32)
                return (t + 1, alive, acc + _dot(a, v_ref[krows, :]), r) + nxt

            t, _, acc, r, _, _ = lax.while_loop(
                k_cond, k_body, (jnp.int32(0), jnp.int32(1), jnp.zeros((2 * QB, LANE), F32), jnp.zeros((2 * QB, 1), F32))
                + scores(nkb - 1))
            o_ref[qrows, :] = _sb_unstack(acc, c, QB)
            tot_ref[qrows, :] = _sb_unstack(jnp.broadcast_to(r, (2 * QB, LANE)), c, QB)
            return jnp.where(frow == i, (nkb - t).astype(F32), first)

        first_ref[...] = lax.fori_loop(0, S // QB, q_loop, jnp.zeros(fshape, F32))

    def col(off):
        return pl.BlockSpec((S, LANE), lambda b, p: (b, off + p))

    out = jax.ShapeDtypeStruct((B * S, W), F32)
    return pl.pallas_call(body, grid=(B, P), in_specs=[col(0), col(P), col(2 * P)],
                          out_specs=(col(0), col(0), pl.BlockSpec((None, None) + fshape, lambda b, p: (b, p, 0, 0))),
                          out_shape=(out, out, jax.ShapeDtypeStruct((B, P) + fshape, F32)),
                          compiler_params=_cp(("parallel", "parallel")), name="sb_fwd")(proj, proj, proj)


def _sb_bwd(proj, tot, first, d_cat, B, S):
    W = SEQ_MIX_WIDTH
    P = W // LANE
    QB = min(SB_QB, S)
    KB = min(SB_KB, QB)
    scale = SB_HEAD_DIM ** -0.5

    fshape = _sb_first_shape(S, QB)

    def body(q_ref, k_ref, v_ref, tot_ref, first_ref, do_ref, dq_ref, dk_ref, dv_ref, dk_acc, dv_acc):
        c = _sb_consts(QB, KB)
        frow = lax.broadcasted_iota(jnp.int32, fshape, 0)
        dk_acc[...] = jnp.zeros_like(dk_acc)
        dv_acc[...] = jnp.zeros_like(dv_acc)

        def q_loop(i, carry):
            qrows = pl.ds(pl.multiple_of(i * QB, QB), QB)
            q2 = _sb_stack(q_ref[qrows, :].astype(F32), c)
            do2 = _sb_stack(do_ref[qrows, :].astype(F32), c)
            q2_t = q2.T.astype(_MXU_DTYPE)
            do2_t = do2.T.astype(_MXU_DTYPE)
            tot = tot_ref[qrows, :]
            total = jnp.concatenate(
                [jnp.sum(jnp.where(c["lane"] == h * SB_HEAD_DIM, tot, 0.0), axis=-1, keepdims=True) for h in range(2)],
                axis=0)

            nkb = (i + 1) * (QB // KB)
            j0 = jnp.clip(jnp.max(jnp.where(frow == i, first_ref[...], 0.0)).astype(jnp.int32), 0, nkb - 1)

            def scores(j):
                krows = pl.ds(pl.multiple_of(j * KB, KB), KB)
                valid = c["row_minus_col"] > j * KB - i * QB
                lb, l1 = _sb_scores(q2, k_ref[krows, :], scale, valid)
                return jnp.where(valid, lb, -1e30), l1, _dot(do2, v_ref[krows, :], 1, 1)

            def k_loop(j, st):
                dq_acc, p_l1, p_g, lbm, l1, da = st
                nxt = scores(jnp.minimum(j + 1, nkb - 1))
                krows = pl.ds(pl.multiple_of(j * KB, KB), KB)
                tail = total - p_l1 - _split_dot(l1, c["upto_incl"])
                a = jnp.exp(lbm + tail)
                g = da * a
                g_before = p_g + _dot(g, c["upto_excl"])
                sig = jnp.exp(lbm)
                dz = ((g * (1.0 - sig) - g_before * sig) * scale).astype(_MXU_DTYPE)
                dk_acc[j] += _dot(q2_t, dz)
                dv_acc[j] += _dot(do2_t, a)
                return (dq_acc + _dot(dz, k_ref[krows, :]), p_l1 + jnp.sum(l1, axis=-1, keepdims=True),
                        p_g + jnp.sum(g, axis=-1, keepdims=True)) + nxt

            zero_col = jnp.zeros((2 * QB, 1), F32)
            dq2 = lax.fori_loop(j0, nkb, k_loop, (jnp.zeros((2 * QB, LANE), F32), zero_col, zero_col) + scores(j0))[0]
            dq_ref[qrows, :] = _sb_unstack(dq2, c, QB).astype(dq_ref.dtype)
            return carry

        lax.fori_loop(0, S // QB, q_loop, 0)
        for j in range(S // KB):
            dk_ref[j * KB:(j + 1) * KB, :] = dk_acc[j].T.astype(dk_ref.dtype)
            dv_ref[j * KB:(j + 1) * KB, :] = dv_acc[j].T.astype(dv_ref.dtype)

    def col(off):
        return pl.BlockSpec((S, LANE), lambda b, p: (b, off + p))

    out = jax.ShapeDtypeStruct((B * S, W), _MXU_DTYPE)
    return pl.pallas_call(body, grid=(B, P),
                          in_specs=[col(0), col(P), col(2 * P), col(0),
                                    pl.BlockSpec((None, None) + fshape, lambda b, p: (b, p, 0, 0)), col(0)],
                          out_specs=(col(0),) * 3, out_shape=(out,) * 3,
                          scratch_shapes=[pltpu.VMEM((S // KB, LANE, KB), F32), pltpu.VMEM((S // KB, LANE, KB), F32)],
                          compiler_params=_cp(("parallel", "parallel")), name="sb_bwd")(proj, proj, proj, tot, first, d_cat)


def _mem_fn(q, k, v):
    lane = lax.broadcasted_iota(jnp.int32, (1, X_WIDTH), 1)
    out = jnp.zeros(q.shape, F32)
    for h in range(N_X_HEADS):
        hm = (lane // X_HEAD_DIM) == h
        s = _dot(jnp.where(hm, q, 0.0), k, 1, 1) * (X_HEAD_DIM ** -0.5)
        e = jnp.exp(s - lax.stop_gradient(jnp.max(s, axis=-1, keepdims=True)))
        p = e / jnp.sum(e, axis=-1, keepdims=True)
        out = out + jnp.where(hm, _dot(p, v), 0.0)
    return out


def _mem_specs(S, q_col):
    ts = _pick(S, 1024)
    ns = S // ts
    qs = pl.BlockSpec((ts, X_WIDTH), lambda b, i: (b * ns + i, q_col // X_WIDTH))
    ks = pl.BlockSpec((N_MEM, X_WIDTH), lambda b, i: (b, 0))
    vs = pl.BlockSpec((N_MEM, X_WIDTH), lambda b, i: (b, 1))
    os = pl.BlockSpec((ts, X_WIDTH), lambda b, i: (b * ns + i, 0))
    return ts, ns, qs, ks, vs, os


def _mem_fwd(proj, q_col, mem_kv, B, S, name):
    ts, ns, qs, ks, vs, os = _mem_specs(S, q_col)

    def body(q_ref, k_ref, v_ref, o_ref):
        o_ref[...] = _mem_fn(q_ref[...].astype(F32), k_ref[...].astype(F32), v_ref[...].astype(F32))

    return pl.pallas_call(body, grid=(B, ns), in_specs=[qs, ks, vs], out_specs=os,
                          out_shape=jax.ShapeDtypeStruct((B * S, X_WIDTH), F32),
                          compiler_params=_cp(("parallel", "parallel")), name=name)(proj, mem_kv, mem_kv)


def _mem_bwd(proj, q_col, mem_kv, d_cat, B, S, name):
    ts, ns, qs, ks, vs, os = _mem_specs(S, q_col)

    def body(q_ref, k_ref, v_ref, do_ref, dq_ref, dk_ref, dv_ref):
        _, vjp = jax.vjp(_mem_fn, q_ref[...].astype(F32), k_ref[...].astype(F32), v_ref[...].astype(F32))
        dq, dk, dv = vjp(do_ref[...].astype(F32))
        dq_ref[...] = dq.astype(dq_ref.dtype)

        @pl.when(pl.program_id(1) == 0)
        def _():
            dk_ref[...] = jnp.zeros_like(dk_ref)
            dv_ref[...] = jnp.zeros_like(dv_ref)

        dk_ref[...] += dk
        dv_ref[...] += dv

    dos = pl.BlockSpec((ts, X_WIDTH), lambda b, i: (b * ns + i, SEQ_MIX_WIDTH // X_WIDTH))
    dq, dk, dv = pl.pallas_call(
        body, grid=(B, ns), in_specs=[qs, ks, vs, dos],
        out_specs=(os, pl.BlockSpec((N_MEM, X_WIDTH), lambda b, i: (b, 0)), pl.BlockSpec((N_MEM, X_WIDTH), lambda b, i: (b, 0))),
        out_shape=(jax.ShapeDtypeStruct((B * S, X_WIDTH), _MXU_DTYPE),
                   jax.ShapeDtypeStruct((B * N_MEM, X_WIDTH), F32), jax.ShapeDtypeStruct((B * N_MEM, X_WIDTH), F32)),
        compiler_params=_cp(("parallel", "arbitrary")), name=name)(proj, mem_kv, mem_kv, d_cat)
    return dq, dk, dv


def _peers():
    x, y, c = lax.axis_index("x"), lax.axis_index("y"), lax.axis_index("c")
    me = 4 * x + 2 * y + c
    out = []
    for fx, fy, fc in [(0, 0, 1), (1, 0, 0), (0, 1, 0), (1, 1, 0), (1, 0, 1), (0, 1, 1), (1, 1, 1)]:
        px, py, pc = x ^ fx, y ^ fy, c ^ fc
        out.append(((px, py, pc), 4 * px + 2 * py + pc))
    return me, out


ANY = pl.BlockSpec(memory_space=pl.ANY)


def _remote(src, dst, send_sems, recv_sems, k, dev):
    return pltpu.make_async_remote_copy(src_ref=src, dst_ref=dst, send_sem=send_sems.at[k], recv_sem=recv_sems.at[k],
                                        device_id=dev, device_id_type=pl.DeviceIdType.MESH)


def _place():
    x, y, c = lax.axis_index("x"), lax.axis_index("y"), lax.axis_index("c")
    return x, y, c, [(1 - x, y), (x, 1 - y), (1 - x, 1 - y)]


def _all_gather(shard):
    R = shard.shape[0]

    def body(x_ref, o_ref, send_sems, recv_sems, local_sem):
        x, y, c, chips = _place()
        me, sibling = (x, y, c), (x, y, 1 - c)

        def slot(px, py, pc):
            return o_ref.at[4 * px + 2 * py + pc]

        def copy(k, block, to, src=None):
            return _remote(slot(*block) if src is None else src, slot(*block), send_sems, recv_sems, k, to)

        mine = pltpu.make_async_copy(x_ref, slot(*me), local_sem)
        mine.start()
        first = [copy(0, me, sibling, src=x_ref)] + [copy(1 + j, me, (*chip, c), src=x_ref) for j, chip in enumerate(chips)]
        for cp in first:
            cp.start()
        passed = [copy(4 + j, (*chip, c), sibling) for j, chip in enumerate(chips)]
        for j, chip in enumerate(chips):
            copy(1 + j, (*chip, c), me).wait_recv()
            passed[j].start()
        copy(0, sibling, me).wait_recv()
        for j, chip in enumerate(chips):
            copy(4 + j, (*chip, 1 - c), me).wait_recv()
        for cp in first + passed:
            cp.wait_send()
        mine.wait()

    return pl.pallas_call(body, in_specs=[ANY], out_specs=ANY,
                          out_shape=jax.ShapeDtypeStruct((N_DEV, R, LANE), shard.dtype),
                          scratch_shapes=[pltpu.SemaphoreType.DMA((7,)), pltpu.SemaphoreType.DMA((7,)),
                                          pltpu.SemaphoreType.DMA],
                          compiler_params=pltpu.CompilerParams(has_side_effects=True),
                          name="all_gather_weights")(shard)


N_CHIP = 4


def _exchange_sibling(big):
    R = big.shape[1]

    def body(b_ref, o_ref, send_sems, recv_sems):
        x, y, c, _ = _place()
        copies = [_remote(b_ref.at[2 * k + (1 - c)], o_ref.at[k], send_sems, recv_sems, k, (x, y, 1 - c))
                  for k in range(N_CHIP)]
        for cp in copies:
            cp.start()
        for cp in copies:
            cp.wait()

    return pl.pallas_call(body, in_specs=[ANY], out_specs=ANY,
                          out_shape=jax.ShapeDtypeStruct((N_CHIP, R, LANE), big.dtype),
                          scratch_shapes=[pltpu.SemaphoreType.DMA((N_CHIP,)), pltpu.SemaphoreType.DMA((N_CHIP,))],
                          compiler_params=pltpu.CompilerParams(has_side_effects=True),
                          name="exchange_sibling")(big)


def _partial_sum(g4, recv):
    R = g4.shape[2]
    tr = _pick(R, 6400)

    def body(g_ref, r_ref, pw_ref, po_ref):
        x, y, c, _ = _place()
        g = jnp.where(c == 0, g_ref[0], g_ref[1]).astype(F32) + r_ref[...].astype(F32)
        pw_ref[...] = g.astype(pw_ref.dtype)

        @pl.when(pl.program_id(1) == 2 * x + y)
        def _():
            po_ref[...] = g

    return pl.pallas_call(body, grid=(R // tr, N_CHIP),
                          in_specs=[pl.BlockSpec((None, 2, tr, LANE), lambda i, k: (k, 0, i, 0)),
                                    pl.BlockSpec((None, tr, LANE), lambda i, k: (k, i, 0))],
                          out_specs=(pl.BlockSpec((None, tr, LANE), lambda i, k: (k, i, 0)),
                                     pl.BlockSpec((tr, LANE), lambda i, k: (i, 0))),
                          out_shape=(jax.ShapeDtypeStruct((N_CHIP, R, LANE), recv.dtype),
                                     jax.ShapeDtypeStruct((R, LANE), F32)),
                          compiler_params=_cp(("parallel", "arbitrary")), name="partial_sum")(g4, recv)


def _exchange_chips(part, small):
    R = part.shape[1]
    K = small.shape[0]

    def body(p_ref, s_ref, ob_ref, os_ref, send_sems, recv_sems, local_sems):
        x, y, c, chips = _place()
        my_chip = 2 * x + y
        me, peers = _peers()
        own_b = pltpu.make_async_copy(p_ref.at[my_chip], ob_ref.at[my_chip], local_sems.at[0])
        own_s = pltpu.make_async_copy(s_ref, os_ref.at[me], local_sems.at[1])
        own_b.start()
        own_s.start()
        copies = [_remote(p_ref.at[2 * px + py], ob_ref.at[my_chip], send_sems, recv_sems, j, (px, py, c))
                  for j, (px, py) in enumerate(chips)]
        copies += [_remote(s_ref, os_ref.at[me], send_sems, recv_sems, 3 + k, dev) for k, (dev, _) in enumerate(peers)]
        for cp in copies:
            cp.start()
        for j, (px, py) in enumerate(chips):
            _remote(p_ref.at[my_chip], ob_ref.at[2 * px + py], send_sems, recv_sems, j, (px, py, c)).wait_recv()
        for k, (dev, idx) in enumerate(peers):
            _remote(s_ref, os_ref.at[idx], send_sems, recv_sems, 3 + k, dev).wait_recv()
        for cp in copies:
            cp.wait_send()
        own_b.wait()
        own_s.wait()

    return pl.pallas_call(body, in_specs=[ANY, ANY], out_specs=(ANY, ANY),
                          out_shape=(jax.ShapeDtypeStruct((N_CHIP, R, LANE), part.dtype),
                                     jax.ShapeDtypeStruct((N_DEV, K, LANE), small.dtype)),
                          scratch_shapes=[pltpu.SemaphoreType.DMA((10,)), pltpu.SemaphoreType.DMA((10,)),
                                          pltpu.SemaphoreType.DMA((2,))],
                          compiler_params=pltpu.CompilerParams(has_side_effects=True),
                          name="exchange_chips")(part, small)


def _adamw_math(w, g, m, v):
    m = ADAM_B1 * m + (1.0 - ADAM_B1) * g
    v = ADAM_B2 * v + (1.0 - ADAM_B2) * (g * g)
    m_hat = m / (1.0 - ADAM_B1 ** ADAM_STEP)
    v_hat = v / (1.0 - ADAM_B2 ** ADAM_STEP)
    delta = -ADAM_LR * (m_hat / (jnp.sqrt(v_hat) + ADAM_EPS) + ADAM_WD * w)
    return delta, m, v


def _adamw_shard(own, recv, w, m, v):
    R = own.shape[0]
    tr = _pick(R, 1024)

    def body(own_ref, recv_ref, w_ref, m_ref, v_ref, g_ref, d_ref, nm_ref, nv_ref):
        x, y, _, _ = _place()
        g = own_ref[...]
        for k in range(N_CHIP):
            g = g + jnp.where(k == 2 * x + y, 0.0, recv_ref[k].astype(F32))
        delta, nm, nv = _adamw_math(w_ref[...], g, m_ref[...], v_ref[...])
        g_ref[...] = g
        d_ref[...] = delta
        nm_ref[...] = nm
        nv_ref[...] = nv

    row = pl.BlockSpec((tr, LANE), lambda i: (i, 0))
    out = jax.ShapeDtypeStruct((R, LANE), F32)
    return pl.pallas_call(body, grid=(R // tr,),
                          in_specs=[row, pl.BlockSpec((N_CHIP, tr, LANE), lambda i: (0, i, 0)), row, row, row],
                          out_specs=(row,) * 4, out_shape=(out,) * 4,
                          compiler_params=_cp(("parallel",)), name="adamw_shard")(own, recv, w, m, v)


def _adamw_replicated(parts, w, m, v):
    K = w.shape[0]

    def body(p_ref, w_ref, m_ref, v_ref, g_ref, d_ref, nm_ref, nv_ref):
        g = p_ref[0]
        for p in range(1, N_DEV):
            g = g + p_ref[p]
        delta, nm, nv = _adamw_math(w_ref[...], g, m_ref[...], v_ref[...])
        g_ref[...] = g
        d_ref[...] = delta
        nm_ref[...] = nm
        nv_ref[...] = nv

    out = jax.ShapeDtypeStruct((K, LANE), F32)
    return pl.pallas_call(body, out_shape=(out,) * 4, compiler_params=_cp(), name="adamw_replicated")(parts, w, m, v)


_SHARDED = (("w_in_a", 1), ("conv_w_a", 2), ("w_in_b", 2), ("w_mem_kv", 1), ("w_out", 1), ("w_up", 2), ("w_down", 1))
_REPLICATED = ("mem_norm", "norm_pre_mix", "norm_post_mix", "norm_pre_mlp", "norm_post_mlp", "a_log_a", "dt_bias_a", "onorm_a")
ROW_ALIGN = 16


def _rows(n_elems, align=ROW_ALIGN):
    r = -(-n_elems // LANE)
    return -(-r // align) * align


def _pack(arrays, align=ROW_ALIGN, total=None, lead=()):
    parts = []
    for a in arrays:
        n = math.prod(a.shape[len(lead):])
        flat = a.reshape(lead + (n,))
        r = _rows(n, align)
        flat = jnp.pad(flat, [(0, 0)] * len(lead) + [(0, r * LANE - n)])
        parts.append(flat.reshape(lead + (r, LANE)))
    used = sum(part.shape[len(lead)] for part in parts)
    if total is not None and used < total:
        parts.append(jnp.zeros(lead + (total - used, LANE), parts[0].dtype))
    return jnp.concatenate(parts, axis=len(lead))


def _unpack(flat, shapes, align=ROW_ALIGN, lead=()):
    outs = []
    r0 = 0
    for shp in shapes:
        n = math.prod(shp)
        r = _rows(n, align)
        part = lax.slice_in_dim(flat, r0, r0 + r, axis=len(lead))
        part = part.reshape(lead + (r * LANE,))
        part = lax.slice_in_dim(part, 0, n, axis=len(lead))
        outs.append(part.reshape(lead + tuple(shp)))
        r0 += r
    return outs


def _to_full(gathered, axis):
    g = jnp.moveaxis(gathered, 0, axis)
    shp = g.shape
    return g.reshape(shp[:axis] + (shp[axis] * shp[axis + 1],) + shp[axis + 2:])


def _to_blocks(full, axis):
    shp = full.shape
    g = full.reshape(shp[:axis] + (N_DEV, shp[axis] // N_DEV) + shp[axis + 1:])
    return jnp.moveaxis(g, axis, 0)


def _widen_in_a(w):
    main = w[:, :4 * SEQ_MIX_WIDTH]
    small = w[:, 4 * SEQ_MIX_WIDTH:4 * SEQ_MIX_WIDTH + 2 * N_LIN_HEADS]
    memq = w[:, 4 * SEQ_MIX_WIDTH + 2 * N_LIN_HEADS:]
    pad = jnp.zeros((w.shape[0], IN_A_PAD - IN_A), w.dtype)
    return jnp.concatenate([main, memq, small, pad], axis=1)


def _narrow_in_a(g):
    main = g[:, :4 * SEQ_MIX_WIDTH]
    memq = g[:, 4 * SEQ_MIX_WIDTH:4 * SEQ_MIX_WIDTH + X_WIDTH]
    small = g[:, SM_COL:SM_COL + 2 * N_LIN_HEADS]
    return jnp.concatenate([main, small, memq], axis=1)


def _row128(v):
    return jnp.pad(v.reshape(1, -1), ((0, 0), (0, LANE - v.shape[-1])))


def _local_step(x, mem, target, p):
    B, S, D = x.shape
    T = B * S
    md = _MXU_DTYPE
    x0 = x.reshape(T, D)
    tgt = target.reshape(T, D)
    memf = mem.reshape(B * N_MEM, D)
    vec = lambda a: a.reshape(1, -1)

    mem_n = _norm_fwd(memf, vec(p["mem_norm"]), out_dtype=md, name="norm_mem")
    w_in = [_widen_in_a(p["w_in_a"][0]), p["w_in_b"][0]]
    memq_col = [4 * SEQ_MIX_WIDTH, 3 * SEQ_MIX_WIDTH]
    alog = _row128(p["a_log_a"][0])
    dtb = _row128(p["dt_bias_a"][0])
    onorm = vec(p["onorm_a"][0])
    conv_w = p["conv_w_a"][0]
    saved = []
    xi = x0
    h1 = _norm_fwd(xi, vec(p["norm_pre_mix"][0]), out_dtype=md, name="norm_pre_mix0")
    for i in range(2):
        s = {"x_in": xi}
        proj = _mm(h1, w_in[i], out_dtypes=(F32 if i == 0 else md,), name=f"in_proj{i}")
        mem_kv = _mm(mem_n, p["w_mem_kv"][i], out_dtypes=(md,), name=f"mem_kv{i}")
        if i == 0:
            act = _gdn_conv_fwd(proj, conv_w, B, S)
            beta, gc = _gdn_gates_fwd(proj, alog, dtb, B, S)
            u, w, qd, kd, intra = _gdn_prep_fwd(act, beta, gc, B, S)
            o, states = _gdn_scan_fwd(u, w, qd, kd, intra, gc, B, S)
            mix = _gdn_post_fwd(o, proj, onorm, T)
            s.update(act=act, beta=beta, gc=gc, u=u, w=w, qd=qd, kd=kd, intra=intra, o=o, states=states)
        else:
            mix, tot, first = _sb_fwd(proj, B, S)
            s.update(tot=tot, first=first)
        cross = _mem_fwd(proj, memq_col[i], mem_kv, B, S, name=f"mem_fwd{i}")
        cat = jnp.concatenate([mix.astype(md), cross.astype(md)], axis=1)
        y = _mm(cat, p["w_out"][i], name=f"out_proj{i}")
        x_mid, h2 = _norm_pair_fwd(y, vec(p["norm_post_mix"][i]), xi, vec(p["norm_pre_mlp"][i]), md,
                                   name=f"norm_post_mix_pre_mlp{i}")
        a_act = _mm(h2, p["w_up"][i], out_dtypes=(md,), epilogue=_relu2_epilogue, name=f"up_proj{i}")
        y2 = _mm(a_act, p["w_down"][i], name=f"down_proj{i}")
        s.update(h1=h1, proj=proj, mem_kv=mem_kv, cat=cat, y=y, x_mid=x_mid, h2=h2, a_act=a_act, y2=y2)
        saved.append(s)
        if i == 0:
            xi, h1 = _norm_pair_fwd(y2, vec(p["norm_post_mlp"][0]), x_mid, vec(p["norm_pre_mix"][1]), md,
                                    name="norm_post_mlp0_pre_mix1")
        else:
            xi = _norm_fwd(y2, vec(p["norm_post_mlp"][1]), resid=x_mid, name="norm_post_mlp1")

    loss_row, dx = _loss_head(xi, tgt)

    g = {}
    d_mem_n = None
    gn = {k: [None, None] for k in ("norm_pre_mix", "norm_post_mix", "norm_pre_mlp", "norm_post_mlp")}
    g_w_mem_kv, g_w_out, g_w_up, g_w_down = [None, None], [None, None], [None, None], [None, None]
    d_y2, gn["norm_post_mlp"][1] = _norm_bwd(dx, saved[1]["y2"], vec(p["norm_post_mlp"][1]), name="norm_post_mlp_bwd1")
    for i in (1, 0):
        s = saved[i]
        g_w_down[i] = _mm(s["a_act"], d_y2, ta=True, out_dtypes=(_WIRE_DTYPE,), name=f"down_proj_dw{i}")
        d_u = _mm(d_y2, p["w_down"][i], tb=True, out_dtypes=(md,), epilogue=_drelu2_epilogue, extras=(s["a_act"],),
                  name=f"down_proj_dx{i}")
        g_w_up[i] = _mm(s["h2"], d_u, ta=True, out_dtypes=(_WIRE_DTYPE,), name=f"up_proj_dw{i}")
        d_h2 = _mm(d_u, p["w_up"][i], tb=True, name=f"up_proj_dx{i}")
        dx, d_y, gn["norm_pre_mlp"][i], gn["norm_post_mix"][i] = _norm_pair_bwd(
            d_h2, s["x_mid"], vec(p["norm_pre_mlp"][i]), dx, s["y"], vec(p["norm_post_mix"][i]),
            name=f"norm_pre_mlp_post_mix_bwd{i}")
        g_w_out[i] = _mm(s["cat"], d_y, ta=True, out_dtypes=(_WIRE_DTYPE,), name=f"out_proj_dw{i}")
        d_cat = _mm(d_y, p["w_out"][i], tb=True, name=f"out_proj_dx{i}")
        d_memq, d_mk, d_mv = _mem_bwd(s["proj"], memq_col[i], s["mem_kv"], d_cat, B, S, name=f"mem_bwd{i}")
        d_mem_kv = jnp.concatenate([d_mk.astype(md), d_mv.astype(md)], axis=1)
        g_w_mem_kv[i] = _mm(mem_n, d_mem_kv, ta=True, out_dtypes=(_WIRE_DTYPE,), name=f"mem_kv_dw{i}")
        d_mn = _mm(d_mem_kv, p["w_mem_kv"][i], tb=True, name=f"mem_kv_dx{i}")
        d_mem_n = d_mn if d_mem_n is None else d_mem_n + d_mn
        if i == 0:
            d_o, d_gate, g["onorm_a"] = _gdn_post_bwd(d_cat, s["o"], s["proj"], onorm, T)
            du, dw, dqd, dkd, da, dgc_s = _gdn_scan_bwd(s["u"], s["w"], s["qd"], s["kd"], s["intra"], s["gc"],
                                                         s["states"], d_o, B, S)
            dq, dk, dv, d_beta, d_gc = _gdn_prep_bwd(s["act"], s["beta"], s["gc"], du, dw, dqd, dkd, da, B, S)
            d_qkv, g["conv_w_a"] = _gdn_conv_bwd(dq, dk, dv, s["proj"], conv_w, B, S)
            d_sm, g["a_log_a"], g["dt_bias_a"] = _gdn_gates_bwd(d_beta, d_gc + dgc_s, s["proj"], alog, dtb, B, S)
            pad = jnp.zeros((T, IN_A_PAD - SM_COL - LANE), md)
            d_proj = jnp.concatenate([d_qkv, d_gate, d_memq, d_sm, pad], axis=1)
        else:
            dq, dk, dv = _sb_bwd(s["proj"], s["tot"], s["first"], d_cat, B, S)
            d_proj = jnp.concatenate([dq, dk, dv, d_memq], axis=1)
        g_w_in = _mm(s["h1"], d_proj, ta=True, out_dtypes=(_WIRE_DTYPE,), name=f"in_proj_dw{i}")
        d_h1 = _mm(d_proj, w_in[i], tb=True, name=f"in_proj_dx{i}")
        if i == 0:
            dx, gn["norm_pre_mix"][0] = _norm_bwd(d_h1, s["x_in"], vec(p["norm_pre_mix"][0]), resid=dx,
                                                  name="norm_pre_mix_bwd0")
            g["w_in_a"] = _narrow_in_a(g_w_in)[None]
        else:
            dx, d_y2, gn["norm_pre_mix"][1], gn["norm_post_mlp"][0] = _norm_pair_bwd(
                d_h1, s["x_in"], vec(p["norm_pre_mix"][1]), dx, saved[0]["y2"], vec(p["norm_post_mlp"][0]),
                name="norm_pre_mix1_post_mlp0_bwd")
            g["w_in_b"] = g_w_in[None]
    _, g_mem_norm = _norm_bwd(d_mem_n, memf, vec(p["mem_norm"]), name="norm_mem_bwd")
    g["mem_norm"] = g_mem_norm.reshape(-1)
    for k, v in gn.items():
        g[k] = jnp.concatenate(v, axis=0)
    g["w_mem_kv"] = jnp.stack(g_w_mem_kv)
    g["w_out"] = jnp.stack(g_w_out)
    g["w_up"] = jnp.stack(g_w_up)
    g["w_down"] = jnp.stack(g_w_down)
    g["conv_w_a"] = g["conv_w_a"][None]
    g["a_log_a"] = g["a_log_a"][:, :N_LIN_HEADS]
    g["dt_bias_a"] = g["dt_bias_a"][:, :N_LIN_HEADS]
    return loss_row, dx.reshape(B, S, D), g


def kernel(x, mem, mem_norm, norm_pre_mix, norm_post_mix, norm_pre_mlp, norm_post_mlp, w_in_a, conv_w_a, a_log_a, dt_bias_a, onorm_a, w_in_b, w_mem_kv, w_out, w_up, w_down, loss_target, m_mem_norm, m_norm_pre_mix, m_norm_post_mix, m_norm_pre_mlp, m_norm_post_mlp, m_w_in_a, m_conv_w_a, m_a_log_a, m_dt_bias_a, m_onorm_a, m_w_in_b, m_w_mem_kv, m_w_out, m_w_up, m_w_down, v_mem_norm, v_norm_pre_mix, v_norm_post_mix, v_norm_pre_mlp, v_norm_post_mlp, v_w_in_a, v_conv_w_a, v_a_log_a, v_dt_bias_a, v_onorm_a, v_w_in_b, v_w_mem_kv, v_w_out, v_w_up, v_w_down):
    args = dict(locals())
    big_names = [n for n, _ in _SHARDED]
    axes = dict(_SHARDED)
    shard_shapes = [args[n].shape for n in big_names]
    rows_total = -(-sum(_rows(math.prod(s)) for s in shard_shapes) // 1024) * 1024

    exact = ("conv_w_a",) if _WIRE_DTYPE != F32 else ()
    wire = [lax.bitcast_convert_type(args[n], _WIRE_DTYPE) if n in exact else args[n].astype(_WIRE_DTYPE)
            for n in big_names]
    wire_shapes = [a.shape for a in wire]
    wire_rows = -(-sum(_rows(math.prod(s)) for s in wire_shapes) // ROW_ALIGN) * ROW_ALIGN
    gathered = _all_gather(_pack(wire, total=wire_rows))
    p = {}
    for n, b in zip(big_names, _unpack(gathered, wire_shapes, lead=(N_DEV,))):
        b = lax.bitcast_convert_type(b, F32) if n in exact else b.astype(_MXU_DTYPE)
        p[n] = _to_full(b, axes[n])
    for n in _REPLICATED:
        p[n] = args[n]
    w_flat = _pack([args[n] for n in big_names], total=rows_total)

    loss_row, grad_x, g = _local_step(x, mem, loss_target, p)
    loss = lax.psum(loss_row[0, 0], ("x", "y", "c"))

    g_blocks = _pack([_to_blocks(g[n], axes[n]).astype(_WIRE_DTYPE) for n in big_names], total=rows_total, lead=(N_DEV,))
    rep_shapes = [args[n].shape for n in _REPLICATED]
    g_small = _pack([g[n] for n in _REPLICATED], align=8)
    recv_sib = _exchange_sibling(g_blocks)
    part, own = _partial_sum(g_blocks.reshape(N_CHIP, 2, rows_total, LANE), recv_sib)
    recv_big, recv_small = _exchange_chips(part, g_small)

    m_flat = _pack([args["m_" + n] for n in big_names], total=rows_total)
    v_flat = _pack([args["v_" + n] for n in big_names], total=rows_total)
    outs_big = [_unpack(f, shard_shapes) for f in _adamw_shard(own, recv_big, w_flat, m_flat, v_flat)]
    outs_small = [_unpack(f, rep_shapes, align=8) for f in _adamw_replicated(
        recv_small, _pack([args[n] for n in _REPLICATED], align=8),
        _pack([args["m_" + n] for n in _REPLICATED], align=8), _pack([args["v_" + n] for n in _REPLICATED], align=8))]

    order = ["mem_norm", "norm_pre_mix", "norm_post_mix", "norm_pre_mlp", "norm_post_mlp", "w_in_a", "conv_w_a",
             "a_log_a", "dt_bias_a", "onorm_a", "w_in_b", "w_mem_kv", "w_out", "w_up", "w_down"]
    result = [loss, grad_x]
    for kind in range(4):
        for n in order:
            if n in axes:
                result.append(outs_big[kind][big_names.index(n)])
            else:
                result.append(outs_small[kind][_REPLICATED.index(n)])
    return tuple(result)
```

```python
import math

import jax
import jax.numpy as jnp
from jax import lax
from jax.experimental import pallas as pl
from jax.experimental.pallas import tpu as pltpu

F32 = jnp.float32
_MXU_DTYPE = jnp.bfloat16
_WIRE_DTYPE = jnp.bfloat16
_HI = lax.Precision.HIGH

D_MODEL = 1024
N_DEV = 8
N_MEM = 256
X_WIDTH = 256
N_X_HEADS = 4
X_HEAD_DIM = 64
SEQ_MIX_WIDTH = 768
LIN_HEAD_DIM = 128
N_LIN_HEADS = 6
CONV_WIDTH = 4
CHUNK = 64
SB_HEAD_DIM = 64
D_FF = 4096
EPS = 1e-6
IN_A = 3340
IN_A_PAD = 3584
IN_B = 2560
SM_COL = 3328

ADAM_LR = 0.001
ADAM_B1 = 0.9
ADAM_B2 = 0.999
ADAM_EPS = 1e-08
ADAM_WD = 0.01
ADAM_STEP = 10

LANE = 128
VMEM_LIMIT = 56 * 1024 * 1024


def _cp(sem=None):
    return pltpu.CompilerParams(dimension_semantics=sem, vmem_limit_bytes=VMEM_LIMIT)


def _pick(n, target):
    if n <= target:
        return n
    best = None
    for t in range(LANE, target + 1, LANE):
        if n % t == 0:
            best = t
    assert best is not None, (n, target)
    return best


def _dot(a, b, ca=1, cb=0):
    return lax.dot_general(a.astype(_MXU_DTYPE), b.astype(_MXU_DTYPE), (((ca,), (cb,)), ((), ())),
                           preferred_element_type=F32)


def _bdot(a, b, ca, cb):
    return lax.dot_general(a.astype(_MXU_DTYPE), b.astype(_MXU_DTYPE), (((ca,), (cb,)), ((0,), (0,))),
                           preferred_element_type=F32)


def _bdot_hi(a, b):
    return lax.dot_general(a, b, (((2,), (1,)), ((0,), (0,))), precision=_HI, preferred_element_type=F32)


def _sigmoid(x):
    return 1.0 / (1.0 + jnp.exp(-x))


def _silu(x):
    return x * _sigmoid(x)


def _softplus(x):
    return jnp.maximum(x, 0.0) + jnp.log(1.0 + jnp.exp(-jnp.abs(x)))


def _rms(x, g):
    return x * lax.rsqrt(jnp.mean(x * x, axis=-1, keepdims=True) + EPS) * g


def _norm_fwd(x, g, resid=None, out_dtype=F32, name="norm_fwd"):
    T, D = x.shape
    tm = _pick(T, 512)
    has_resid = resid is not None

    def body(*refs):
        if has_resid:
            x_ref, g_ref, r_ref, o_ref = refs
        else:
            x_ref, g_ref, o_ref = refs
        y = _rms(x_ref[...].astype(F32), g_ref[...])
        if has_resid:
            y = r_ref[...] + y
        o_ref[...] = y.astype(out_dtype)

    row = pl.BlockSpec((tm, D), lambda i: (i, 0))
    in_specs = [row, pl.BlockSpec((1, D), lambda i: (0, 0))] + ([row] if has_resid else [])
    args = (x, g) + ((resid,) if has_resid else ())
    return pl.pallas_call(body, grid=(T // tm,), in_specs=in_specs, out_specs=row,
                          out_shape=jax.ShapeDtypeStruct((T, D), out_dtype),
                          compiler_params=_cp(("parallel",)), name=name)(*args)


def _norm_bwd(dy, x, g, resid=None, name="norm_bwd"):
    T, D = x.shape
    tm = _pick(T, 512)
    has_resid = resid is not None

    def body(*refs):
        if has_resid:
            dy_ref, x_ref, g_ref, r_ref, dx_ref, dg_ref = refs
        else:
            dy_ref, x_ref, g_ref, dx_ref, dg_ref = refs
        _, vjp = jax.vjp(_rms, x_ref[...].astype(F32), g_ref[...])
        dx, dg = vjp(dy_ref[...].astype(F32))
        if has_resid:
            dx = r_ref[...] + dx
        dx_ref[...] = dx

        @pl.when(pl.program_id(0) == 0)
        def _():
            dg_ref[...] = jnp.zeros_like(dg_ref)

        dg_ref[...] += dg

    row = pl.BlockSpec((tm, D), lambda i: (i, 0))
    vec = pl.BlockSpec((1, D), lambda i: (0, 0))
    in_specs = [row, row, vec] + ([row] if has_resid else [])
    args = (dy, x, g) + ((resid,) if has_resid else ())
    return pl.pallas_call(body, grid=(T // tm,), in_specs=in_specs, out_specs=(row, vec),
                          out_shape=(jax.ShapeDtypeStruct((T, D), F32), jax.ShapeDtypeStruct((1, D), F32)),
                          compiler_params=_cp(("arbitrary",)), name=name)(*args)


def _norm_pair_fwd(y, g_post, resid, g_pre, out_dtype, name):
    T, D = y.shape
    tm = _pick(T, 512)

    def body(y_ref, gp_ref, r_ref, gn_ref, x_ref, h_ref):
        x = r_ref[...] + _rms(y_ref[...], gp_ref[...])
        x_ref[...] = x
        h_ref[...] = _rms(x, gn_ref[...]).astype(out_dtype)

    row = pl.BlockSpec((tm, D), lambda i: (i, 0))
    vec = pl.BlockSpec((1, D), lambda i: (0, 0))
    return pl.pallas_call(body, grid=(T // tm,), in_specs=[row, vec, row, vec], out_specs=(row, row),
                          out_shape=(jax.ShapeDtypeStruct((T, D), F32), jax.ShapeDtypeStruct((T, D), out_dtype)),
                          compiler_params=_cp(("parallel",)), name=name)(y, g_post, resid, g_pre)


def _norm_pair_bwd(d_h, x, g_pre, dx_resid, y, g_post, name):
    T, D = x.shape
    tm = _pick(T, 512)

    def body(dh_ref, x_ref, gn_ref, r_ref, y_ref, gp_ref, dx_ref, dy_ref, dgn_ref, dgp_ref):
        _, vjp_pre = jax.vjp(_rms, x_ref[...], gn_ref[...])
        dx, dgn = vjp_pre(dh_ref[...].astype(F32))
        dx = r_ref[...] + dx
        _, vjp_post = jax.vjp(_rms, y_ref[...], gp_ref[...])
        dy, dgp = vjp_post(dx)
        dx_ref[...] = dx
        dy_ref[...] = dy

        @pl.when(pl.program_id(0) == 0)
        def _():
            dgn_ref[...] = jnp.zeros_like(dgn_ref)
            dgp_ref[...] = jnp.zeros_like(dgp_ref)

        dgn_ref[...] += dgn
        dgp_ref[...] += dgp

    row = pl.BlockSpec((tm, D), lambda i: (i, 0))
    vec = pl.BlockSpec((1, D), lambda i: (0, 0))
    big = jax.ShapeDtypeStruct((T, D), F32)
    small = jax.ShapeDtypeStruct((1, D), F32)
    return pl.pallas_call(body, grid=(T // tm,), in_specs=[row, row, vec, row, row, vec],
                          out_specs=(row, row, vec, vec), out_shape=(big, big, small, small),
                          compiler_params=_cp(("arbitrary",)), name=name)(d_h, x, g_pre, dx_resid, y, g_post)


def _mm(a, b, *, ta=False, tb=False, out_dtypes=(F32,), epilogue=None, extras=(), name="mm",
        tm_t=1024, tn_t=1024, tk_t=1024):
    M, K = (a.shape[1], a.shape[0]) if ta else a.shape
    N = b.shape[0] if tb else b.shape[1]
    assert (b.shape[1] if tb else b.shape[0]) == K, (a.shape, b.shape, ta, tb)
    tm, tn, tk = _pick(M, tm_t), _pick(N, tn_t), _pick(K, tk_t)
    nk = K // tk
    n_extra = len(extras)
    n_out = len(out_dtypes)

    def body(*refs):
        a_ref, b_ref = refs[0], refs[1]
        e_refs = refs[2:2 + n_extra]
        o_refs = refs[2 + n_extra:2 + n_extra + n_out]

        def finish(acc):
            outs = (acc,) if epilogue is None else epilogue(acc, *[e[...] for e in e_refs])
            for o_ref, o in zip(o_refs, outs):
                o_ref[...] = o.astype(o_ref.dtype)

        d = _dot(a_ref[...], b_ref[...], 0 if ta else 1, 1 if tb else 0)
        if nk == 1:
            finish(d)
            return
        acc_ref = refs[-1]
        k = pl.program_id(2)

        @pl.when(k == 0)
        def _():
            acc_ref[...] = d

        @pl.when((k > 0) & (k < nk - 1))
        def _():
            acc_ref[...] += d

        @pl.when(k == nk - 1)
        def _():
            finish(acc_ref[...] + d)

    a_spec = pl.BlockSpec((tk, tm), lambda i, j, k: (k, i)) if ta else pl.BlockSpec((tm, tk), lambda i, j, k: (i, k))
    b_spec = pl.BlockSpec((tn, tk), lambda i, j, k: (j, k)) if tb else pl.BlockSpec((tk, tn), lambda i, j, k: (k, j))
    o_spec = pl.BlockSpec((tm, tn), lambda i, j, k: (i, j))
    outs = pl.pallas_call(
        body, grid=(M // tm, N // tn, nk),
        in_specs=[a_spec, b_spec] + [o_spec] * n_extra,
        out_specs=tuple([o_spec] * n_out),
        out_shape=tuple(jax.ShapeDtypeStruct((M, N), dt) for dt in out_dtypes),
        scratch_shapes=[pltpu.VMEM((tm, tn), F32)] if nk > 1 else [],
        compiler_params=_cp(("parallel", "parallel", "arbitrary")), name=name)(a, b, *extras)
    return outs[0] if n_out == 1 else outs


def _relu2_epilogue(acc):
    r = jnp.maximum(acc, 0.0)
    return (r * r,)


def _drelu2_epilogue(acc, a):
    return (acc * (2.0 * jnp.sqrt(a.astype(F32))),)


def _loss_head(x, target, name="loss_head"):
    T, D = x.shape
    tm = _pick(T, 512)

    def body(x_ref, t_ref, l_ref, dx_ref):
        e = x_ref[...] - t_ref[...]
        dx_ref[...] = e * (1.0 / D)

        @pl.when(pl.program_id(0) == 0)
        def _():
            l_ref[...] = jnp.zeros_like(l_ref)

        part = 0.5 * jnp.sum(jnp.mean(e * e, axis=-1, keepdims=True), axis=0, keepdims=True)
        l_ref[...] += jnp.broadcast_to(part, l_ref.shape)

    row = pl.BlockSpec((tm, D), lambda i: (i, 0))
    return pl.pallas_call(body, grid=(T // tm,), in_specs=[row, row],
                          out_specs=(pl.BlockSpec((1, LANE), lambda i: (0, 0)), row),
                          out_shape=(jax.ShapeDtypeStruct((1, LANE), F32), jax.ShapeDtypeStruct((T, D), F32)),
                          compiler_params=_cp(("arbitrary",)), name=name)(x, target)


def _shift_down(x, k, row):
    return jnp.where(row >= k, pltpu.roll(x, k, 0), 0.0)


def _shift_up(x, k, row, n):
    return jnp.where(row < n - k, pltpu.roll(x, n - k, 0), 0.0)


def _conv_taps(x, w, row):
    y = x * w[CONV_WIDTH - 1:CONV_WIDTH, :]
    for i in range(CONV_WIDTH - 1):
        y = y + _shift_down(x, CONV_WIDTH - 1 - i, row) * w[i:i + 1, :]
    return y


def _qkv_act(xc, j):
    s = _silu(xc)
    n = s * lax.rsqrt(jnp.sum(s * s, axis=-1, keepdims=True) + EPS)
    n = n * jnp.where(j < N_LIN_HEADS, LIN_HEAD_DIM ** -0.5, 1.0)
    return jnp.where(j < 2 * N_LIN_HEADS, n, s)


def _gdn_conv_fwd(proj, conv_w, B, S):
    nblk = 3 * N_LIN_HEADS

    def body(p_ref, w_ref, o_ref):
        j = pl.program_id(1)
        x = p_ref[...]
        row = lax.broadcasted_iota(jnp.int32, x.shape, 0)
        o_ref[...] = _qkv_act(_conv_taps(x, w_ref[...], row), j)

    blk = pl.BlockSpec((S, LANE), lambda b, j: (b, j))
    return pl.pallas_call(body, grid=(B, nblk),
                          in_specs=[blk, pl.BlockSpec((CONV_WIDTH, LANE), lambda b, j: (0, j))],
                          out_specs=blk, out_shape=jax.ShapeDtypeStruct((B * S, nblk * LANE), F32),
                          compiler_params=_cp(("parallel", "parallel")), name="gdn_conv_fwd")(proj, conv_w)


def _gdn_conv_bwd(dq, dk, dv, proj, conv_w, B, S):
    nblk = 3 * N_LIN_HEADS
    H = N_LIN_HEADS

    def body(dq_ref, dk_ref, dv_ref, p_ref, w_ref, dp_ref, dw_ref):
        j = pl.program_id(0)
        b = pl.program_id(1)
        x = p_ref[...]
        w = w_ref[...]
        row = lax.broadcasted_iota(jnp.int32, x.shape, 0)
        d_act = jnp.where(j < H, dq_ref[...], jnp.where(j < 2 * H, dk_ref[...], dv_ref[...]))
        _, vjp = jax.vjp(lambda t: _qkv_act(t, j), _conv_taps(x, w, row))
        (d_xc,) = vjp(d_act)
        dx = d_xc * w[CONV_WIDTH - 1:CONV_WIDTH, :]
        for i in range(CONV_WIDTH - 1):
            dx = dx + _shift_up(d_xc, CONV_WIDTH - 1 - i, row, S) * w[i:i + 1, :]
        dp_ref[...] = dx.astype(dp_ref.dtype)

        @pl.when(b == 0)
        def _():
            dw_ref[...] = jnp.zeros_like(dw_ref)

        for i in range(CONV_WIDTH):
            xs = x if i == CONV_WIDTH - 1 else _shift_down(x, CONV_WIDTH - 1 - i, row)
            dw_ref[i:i + 1, :] += jnp.sum(d_xc * xs, axis=0, keepdims=True)

    blk = pl.BlockSpec((S, LANE), lambda j, b: (b, j))
    wblk = pl.BlockSpec((CONV_WIDTH, LANE), lambda j, b: (0, j))
    return pl.pallas_call(
        body, grid=(nblk, B),
        in_specs=[pl.BlockSpec((S, LANE), lambda j, b: (b, jnp.clip(j, 0, H - 1))),
                  pl.BlockSpec((S, LANE), lambda j, b: (b, jnp.clip(j - H, 0, H - 1))),
                  pl.BlockSpec((S, LANE), lambda j, b: (b, jnp.clip(j - 2 * H, 0, H - 1))),
                  blk, wblk],
        out_specs=(blk, wblk),
        out_shape=(jax.ShapeDtypeStruct((B * S, nblk * LANE), _MXU_DTYPE),
                   jax.ShapeDtypeStruct((CONV_WIDTH, nblk * LANE), F32)),
        compiler_params=_cp(("parallel", "arbitrary")), name="gdn_conv_bwd")(dq, dk, dv, proj, conv_w)


def _chunk_cumsum(x, row):
    pos = row % CHUNK
    k = 1
    while k < CHUNK:
        x = x + jnp.where(pos >= k, pltpu.roll(x, k, 0), 0.0)
        k *= 2
    return x


def _chunk_rev_cumsum(x, row, n):
    pos = row % CHUNK
    k = 1
    while k < CHUNK:
        x = x + jnp.where(pos < CHUNK - k, pltpu.roll(x, n - k, 0), 0.0)
        k *= 2
    return x


def _gdn_gates_fwd(proj, a_log, dt_bias, B, S):
    H = N_LIN_HEADS

    def body(sm_ref, al_ref, dt_ref, beta_ref, gc_ref):
        sm = sm_ref[...]
        row = lax.broadcasted_iota(jnp.int32, (S, LANE), 0)
        for h in range(H):
            beta = _sigmoid(sm[:, h:h + 1])
            g = -jnp.exp(al_ref[0:1, h:h + 1]) * _softplus(sm[:, H + h:H + h + 1] + dt_ref[0:1, h:h + 1])
            beta_ref[:, h * LANE:(h + 1) * LANE] = jnp.broadcast_to(beta, (S, LANE))
            gc_ref[:, h * LANE:(h + 1) * LANE] = _chunk_cumsum(jnp.broadcast_to(g, (S, LANE)), row)

    vec = pl.BlockSpec((1, LANE), lambda b: (0, 0))
    wide = pl.BlockSpec((S, H * LANE), lambda b: (b, 0))
    return pl.pallas_call(body, grid=(B,),
                          in_specs=[pl.BlockSpec((S, LANE), lambda b: (b, SM_COL // LANE)), vec, vec],
                          out_specs=(wide, wide),
                          out_shape=(jax.ShapeDtypeStruct((B * S, H * LANE), F32),) * 2,
                          compiler_params=_cp(("parallel",)), name="gdn_gates_fwd")(proj, a_log, dt_bias)


def _gdn_gates_bwd(d_beta, d_gc, proj, a_log, dt_bias, B, S):
    H = N_LIN_HEADS

    def body(db_ref, dgc_ref, sm_ref, al_ref, dt_ref, dsm_ref, dal_ref, ddt_ref):
        sm = sm_ref[...]
        row = lax.broadcasted_iota(jnp.int32, (S, LANE), 0)
        lane = lax.broadcasted_iota(jnp.int32, (1, LANE), 1)
        dsm = jnp.zeros((S, LANE), F32)
        dal = jnp.zeros((1, LANE), F32)
        ddt = jnp.zeros((1, LANE), F32)
        for h in range(H):
            beta = _sigmoid(sm[:, h:h + 1])
            dbeta = jnp.sum(db_ref[:, h * LANE:(h + 1) * LANE], axis=-1, keepdims=True)
            d_bl = dbeta * beta * (1.0 - beta)
            dgc = jnp.sum(dgc_ref[:, h * LANE:(h + 1) * LANE], axis=-1, keepdims=True)
            dg = _chunk_rev_cumsum(jnp.broadcast_to(dgc, (S, LANE)), row, S)[:, 0:1]
            z = sm[:, H + h:H + h + 1] + dt_ref[0:1, h:h + 1]
            a = jnp.exp(al_ref[0:1, h:h + 1])
            g = -a * _softplus(z)
            d_al = dg * (-a) * _sigmoid(z)
            dsm = dsm + jnp.where(lane == h, d_bl, 0.0) + jnp.where(lane == H + h, d_al, 0.0)
            ddt = ddt + jnp.where(lane == h, jnp.sum(d_al, axis=0, keepdims=True), 0.0)
            dal = dal + jnp.where(lane == h, jnp.sum(dg * g, axis=0, keepdims=True), 0.0)
        dsm_ref[...] = dsm.astype(dsm_ref.dtype)

        @pl.when(pl.program_id(0) == 0)
        def _():
            dal_ref[...] = jnp.zeros_like(dal_ref)
            ddt_ref[...] = jnp.zeros_like(ddt_ref)

        dal_ref[...] += dal
        ddt_ref[...] += ddt

    vec = pl.BlockSpec((1, LANE), lambda b: (0, 0))
    wide = pl.BlockSpec((S, H * LANE), lambda b: (b, 0))
    return pl.pallas_call(body, grid=(B,),
                          in_specs=[wide, wide, pl.BlockSpec((S, LANE), lambda b: (b, SM_COL // LANE)), vec, vec],
                          out_specs=(pl.BlockSpec((S, LANE), lambda b: (b, 0)), vec, vec),
                          out_shape=(jax.ShapeDtypeStruct((B * S, LANE), _MXU_DTYPE),
                                     jax.ShapeDtypeStruct((1, LANE), F32), jax.ShapeDtypeStruct((1, LANE), F32)),
                          compiler_params=_cp(("arbitrary",)), name="gdn_gates_bwd")(d_beta, d_gc, proj, a_log, dt_bias)


PREP_ROWS = 1024


@jax.custom_vjp
def _unit_lower_inverse(lower):
    n, C, _ = lower.shape
    ri = lax.broadcasted_iota(jnp.int32, (C, C), 0)
    ci = lax.broadcasted_iota(jnp.int32, (C, C), 1)
    p = -lower
    inv = jnp.where((ri == ci)[None], 1.0, 0.0) + p
    for _ in range(int(math.log2(C)) - 1):
        p = _bdot_hi(p, p)
        inv = inv + _bdot_hi(inv, p)
    return inv


def _unit_lower_inverse_fwd(lower):
    inv = _unit_lower_inverse(lower)
    return inv, inv


def _unit_lower_inverse_bwd(inv, d_inv):
    inv_t = jnp.swapaxes(inv, 1, 2)
    return (-_bdot_hi(_bdot_hi(inv_t, d_inv), inv_t),)


_unit_lower_inverse.defvjp(_unit_lower_inverse_fwd, _unit_lower_inverse_bwd)


def _prep_fn(q, k, v, beta, gc):
    R = q.shape[0]
    n = R // CHUNK
    q3, k3, v3, b3, g3 = [t.reshape(n, CHUNK, LIN_HEAD_DIM) for t in (q, k, v, beta, gc)]
    ri = lax.broadcasted_iota(jnp.int32, (CHUNK, CHUNK), 0)
    ci = lax.broadcasted_iota(jnp.int32, (CHUNK, CHUNK), 1)
    causal = (ri >= ci)[None]
    strict = (ri > ci)[None]
    gcol = g3[:, :, 0:1]
    grow = jnp.swapaxes(g3, 1, 2)[:, 0:1, :]
    decay = jnp.exp(jnp.where(causal, gcol - grow, -1e30))
    kb = k3 * b3
    lower = jnp.where(strict, _bdot(kb, k3, 2, 2) * decay, 0.0)
    inv = _unit_lower_inverse(lower)
    eg = jnp.exp(g3)
    sol = _bdot_hi(inv, jnp.concatenate([v3 * b3, kb * eg], axis=-1))
    u, w = sol[..., :LIN_HEAD_DIM], sol[..., LIN_HEAD_DIM:]
    intra = _bdot(q3, k3, 2, 2) * decay
    q_dec = q3 * eg
    k_dec = k3 * jnp.exp(g3[:, CHUNK - 1:CHUNK, :] - g3)
    return (u.reshape(R, LIN_HEAD_DIM), w.reshape(R, LIN_HEAD_DIM), q_dec.reshape(R, LIN_HEAD_DIM),
            k_dec.reshape(R, LIN_HEAD_DIM), intra.reshape(R, CHUNK))


def _prep_specs(S):
    H = N_LIN_HEADS
    R = min(PREP_ROWS, S)
    nr = S // R

    def col(off):
        return pl.BlockSpec((R, LANE), lambda b, h, r: (b * nr + r, off + h))

    intra = pl.BlockSpec((None, R, CHUNK), lambda b, h, r: (h, b * nr + r, 0))
    return R, nr, col, intra


def _gdn_prep_fwd(act, beta, gc, B, S):
    H = N_LIN_HEADS
    R, nr, col, intra_spec = _prep_specs(S)
    T = B * S

    def body(q_ref, k_ref, v_ref, b_ref, g_ref, u_ref, w_ref, qd_ref, kd_ref, a_ref):
        u, w, qd, kd, a = _prep_fn(q_ref[...], k_ref[...], v_ref[...], b_ref[...], g_ref[...])
        u_ref[...] = u
        w_ref[...] = w
        qd_ref[...] = qd
        kd_ref[...] = kd
        a_ref[...] = a

    wide = jax.ShapeDtypeStruct((T, H * LANE), F32)
    return pl.pallas_call(body, grid=(B, H, nr),
                          in_specs=[col(0), col(H), col(2 * H), col(0), col(0)],
                          out_specs=(col(0), col(0), col(0), col(0), intra_spec),
                          out_shape=(wide, wide, wide, wide, jax.ShapeDtypeStruct((H, T, CHUNK), F32)),
                          compiler_params=_cp(("parallel", "parallel", "parallel")),
                          name="gdn_prep_fwd")(act, act, act, beta, gc)


def _gdn_prep_bwd(act, beta, gc, du, dw, dqd, dkd, da, B, S):
    H = N_LIN_HEADS
    R, nr, col, intra_spec = _prep_specs(S)
    T = B * S

    def body(q_ref, k_ref, v_ref, b_ref, g_ref, du_ref, dw_ref, dqd_ref, dkd_ref, da_ref,
             dq_ref, dk_ref, dv_ref, db_ref, dg_ref):
        _, vjp = jax.vjp(_prep_fn, q_ref[...], k_ref[...], v_ref[...], b_ref[...], g_ref[...])
        dq, dk, dv, db, dg = vjp((du_ref[...], dw_ref[...], dqd_ref[...], dkd_ref[...], da_ref[...]))
        dq_ref[...] = dq
        dk_ref[...] = dk
        dv_ref[...] = dv
        db_ref[...] = db
        dg_ref[...] = dg

    wide = jax.ShapeDtypeStruct((T, H * LANE), F32)
    return pl.pallas_call(body, grid=(B, H, nr),
                          in_specs=[col(0), col(H), col(2 * H), col(0), col(0),
                                    col(0), col(0), col(0), col(0), intra_spec],
                          out_specs=(col(0),) * 5, out_shape=(wide,) * 5,
                          compiler_params=_cp(("parallel", "parallel", "parallel")),
                          name="gdn_prep_bwd")(act, act, act, beta, gc, du, dw, dqd, dkd, da)


def _scan_step(u, w, qd, kd, a, g_last, state):
    v_new = u - _dot(w, state)
    o = _dot(qd, state) + _dot(a, v_new)
    new_state = state * jnp.exp(g_last) + _dot(kd, v_new, 0, 0)
    return o, new_state


SCAN_HEADS = 6
SCAN_ROWS = 512


def _scan_specs(B, S, reverse):
    HP = SCAN_HEADS
    R = min(SCAN_ROWS, S)
    nr = S // R

    def blk(r):
        return nr - 1 - r if reverse else r

    col = pl.BlockSpec((R, HP * LANE), lambda b, h, r: (b * nr + blk(r), h))
    intra = pl.BlockSpec((HP, R, CHUNK), lambda b, h, r: (h, b * nr + blk(r), 0))
    st = pl.BlockSpec((None, HP, R // CHUNK, LIN_HEAD_DIM, LIN_HEAD_DIM), lambda b, h, r: (b, h, blk(r), 0, 0))
    return R, nr, col, intra, st


def _gdn_scan_fwd(u, w, qd, kd, a, gc, B, S):
    H = N_LIN_HEADS
    R, nr, col, intra, st = _scan_specs(B, S, reverse=False)

    def body(u_ref, w_ref, qd_ref, kd_ref, a_ref, g_ref, o_ref, st_ref, carry_ref):
        @pl.when(pl.program_id(2) == 0)
        def _():
            carry_ref[...] = jnp.zeros_like(carry_ref)

        def step(c, states):
            rows = pl.ds(pl.multiple_of(c * CHUNK, CHUNK), CHUNK)
            new_states = []
            for hh, state in enumerate(states):
                cols = slice(hh * LANE, (hh + 1) * LANE)
                st_ref[hh, c] = state.astype(st_ref.dtype)
                o, new_state = _scan_step(u_ref[rows, cols], w_ref[rows, cols], qd_ref[rows, cols], kd_ref[rows, cols],
                                          a_ref[hh, rows, :], g_ref[rows, cols][CHUNK - 1:CHUNK, :], state)
                o_ref[rows, cols] = o
                new_states.append(new_state)
            return tuple(new_states)

        states = lax.fori_loop(0, R // CHUNK, step, tuple(carry_ref[hh] for hh in range(SCAN_HEADS)))
        for hh, state in enumerate(states):
            carry_ref[hh] = state

    return pl.pallas_call(body, grid=(B, H // SCAN_HEADS, nr), in_specs=[col, col, col, col, intra, col],
                          out_specs=(col, st),
                          out_shape=(jax.ShapeDtypeStruct((B * S, H * LANE), F32),
                                     jax.ShapeDtypeStruct((B, H, S // CHUNK, LIN_HEAD_DIM, LIN_HEAD_DIM), _MXU_DTYPE)),
                          scratch_shapes=[pltpu.VMEM((SCAN_HEADS, LIN_HEAD_DIM, LIN_HEAD_DIM), F32)],
                          compiler_params=_cp(("parallel", "parallel", "arbitrary")),
                          name="gdn_scan_fwd")(u, w, qd, kd, a, gc)


def _gdn_scan_bwd(u, w, qd, kd, a, gc, states, do, B, S):
    H = N_LIN_HEADS
    R, nr, col, intra, st = _scan_specs(B, S, reverse=True)
    T = B * S
    n = R // CHUNK

    def body(u_ref, w_ref, qd_ref, kd_ref, a_ref, g_ref, st_ref, do_ref,
             du_ref, dw_ref, dqd_ref, dkd_ref, da_ref, dg_ref, carry_ref):
        last = lax.broadcasted_iota(jnp.int32, (CHUNK, LANE), 0) == CHUNK - 1

        @pl.when(pl.program_id(2) == 0)
        def _():
            carry_ref[...] = jnp.zeros_like(carry_ref)

        def step(i, d_states):
            c = n - 1 - i
            rows = pl.ds(pl.multiple_of(c * CHUNK, CHUNK), CHUNK)
            d_prevs = []
            for hh, d_state in enumerate(d_states):
                cols = slice(hh * LANE, (hh + 1) * LANE)
                _, vjp = jax.vjp(_scan_step, u_ref[rows, cols], w_ref[rows, cols], qd_ref[rows, cols], kd_ref[rows, cols],
                                 a_ref[hh, rows, :], g_ref[rows, cols][CHUNK - 1:CHUNK, :], st_ref[hh, c].astype(F32))
                du, dw, dqd, dkd, da, dgl, d_prev = vjp((do_ref[rows, cols].astype(F32), d_state))
                du_ref[rows, cols] = du
                dw_ref[rows, cols] = dw
                dqd_ref[rows, cols] = dqd
                dkd_ref[rows, cols] = dkd
                da_ref[hh, rows, :] = da
                dg_ref[rows, cols] = jnp.where(last, dgl, 0.0)
                d_prevs.append(d_prev)
            return tuple(d_prevs)

        d_states = lax.fori_loop(0, n, step, tuple(carry_ref[hh] for hh in range(SCAN_HEADS)))
        for hh, d_state in enumerate(d_states):
            carry_ref[hh] = d_state

    wide = jax.ShapeDtypeStruct((T, H * LANE), F32)
    return pl.pallas_call(body, grid=(B, H // SCAN_HEADS, nr), in_specs=[col, col, col, col, intra, col, st, col],
                          out_specs=(col, col, col, col, intra, col),
                          out_shape=(wide, wide, wide, wide, jax.ShapeDtypeStruct((H, T, CHUNK), F32), wide),
                          scratch_shapes=[pltpu.VMEM((SCAN_HEADS, LIN_HEAD_DIM, LIN_HEAD_DIM), F32)],
                          compiler_params=_cp(("parallel", "parallel", "arbitrary")),
                          name="gdn_scan_bwd")(u, w, qd, kd, a, gc, states, do)


GATE_COL = 3 * SEQ_MIX_WIDTH


def _post_fn(o, gate, gain):
    return o * lax.rsqrt(jnp.mean(o * o, axis=-1, keepdims=True) + EPS) * gain * _silu(gate)


def _gdn_post_fwd(o, proj, onorm, T):
    H = N_LIN_HEADS
    tm = _pick(T, 1024)

    def body(o_ref, g_ref, n_ref, y_ref):
        y_ref[...] = _post_fn(o_ref[...], g_ref[...], n_ref[...]).astype(y_ref.dtype)

    col = pl.BlockSpec((tm, LANE), lambda i, h: (i, h))
    return pl.pallas_call(body, grid=(T // tm, H),
                          in_specs=[col, pl.BlockSpec((tm, LANE), lambda i, h: (i, GATE_COL // LANE + h)),
                                    pl.BlockSpec((1, LANE), lambda i, h: (0, 0))],
                          out_specs=col, out_shape=jax.ShapeDtypeStruct((T, H * LANE), _MXU_DTYPE),
                          compiler_params=_cp(("parallel", "parallel")), name="gdn_post_fwd")(o, proj, onorm)


def _gdn_post_bwd(d_cat, o, proj, onorm, T):
    H = N_LIN_HEADS
    tm = _pick(T, 1024)

    def body(dy_ref, o_ref, g_ref, n_ref, do_ref, dg_ref, dn_ref):
        _, vjp = jax.vjp(_post_fn, o_ref[...], g_ref[...], n_ref[...])
        do, dg, dn = vjp(dy_ref[...].astype(F32))
        do_ref[...] = do
        dg_ref[...] = dg.astype(dg_ref.dtype)

        @pl.when((pl.program_id(0) == 0) & (pl.program_id(1) == 0))
        def _():
            dn_ref[...] = jnp.zeros_like(dn_ref)

        dn_ref[...] += dn

    col = pl.BlockSpec((tm, LANE), lambda i, h: (i, h))
    vec = pl.BlockSpec((1, LANE), lambda i, h: (0, 0))
    return pl.pallas_call(body, grid=(T // tm, H),
                          in_specs=[col, col, pl.BlockSpec((tm, LANE), lambda i, h: (i, GATE_COL // LANE + h)), vec],
                          out_specs=(col, col, vec),
                          out_shape=(jax.ShapeDtypeStruct((T, H * LANE), F32),
                                     jax.ShapeDtypeStruct((T, H * LANE), _MXU_DTYPE),
                                     jax.ShapeDtypeStruct((1, LANE), F32)),
                          compiler_params=_cp(("arbitrary", "arbitrary")), name="gdn_post_bwd")(d_cat, o, proj, onorm)


def _log_sigmoid(z):
    return jnp.minimum(z, 0.0) - jnp.log(1.0 + jnp.exp(-jnp.abs(z)))


def _split_dot(x, m):
    hi = x.astype(_MXU_DTYPE)
    lo = x - hi.astype(F32)
    return _dot(hi, m) + _dot(lo, m)


SB_QB = 256
SB_KB = 256


def _sb_scores(q2, k_j, scale, valid):
    z = _dot(q2, k_j, 1, 1) * scale
    lb = _log_sigmoid(z)
    return lb, jnp.where(valid, lb - z, 0.0)


def _sb_consts(QB, KB):
    ri = lax.broadcasted_iota(jnp.int32, (KB, KB), 0)
    ci = lax.broadcasted_iota(jnp.int32, (KB, KB), 1)
    row = lax.broadcasted_iota(jnp.int32, (2 * QB, KB), 0)
    col = lax.broadcasted_iota(jnp.int32, (2 * QB, KB), 1)
    lane = lax.broadcasted_iota(jnp.int32, (1, LANE), 1)
    return {
        "row_minus_col": row % QB - col,
        "after_excl": (ri > ci).astype(_MXU_DTYPE),
        "upto_incl": (ri <= ci).astype(_MXU_DTYPE),
        "upto_excl": (ri < ci).astype(_MXU_DTYPE),
        "lane": lane,
        "head0": lane < SB_HEAD_DIM,
    }


def _sb_stack(x, c):
    return jnp.concatenate([jnp.where(c["head0"], x, 0.0), jnp.where(c["head0"], 0.0, x)], axis=0)


def _sb_unstack(x2, c, QB):
    return jnp.where(c["head0"], x2[:QB], x2[QB:])


SB_DEAD = -110.0


def _sb_first_shape(S, QB):
    return (-(-(S // QB) // 8) * 8, LANE)


def _sb_fwd(proj, B, S):
    W = SEQ_MIX_WIDTH
    P = W // LANE
    QB = min(SB_QB, S)
    KB = min(SB_KB, QB)
    scale = SB_HEAD_DIM ** -0.5
    fshape = _sb_first_shape(S, QB)

    def body(q_ref, k_ref, v_ref, o_ref, tot_ref, first_ref):
        c = _sb_consts(QB, KB)
        frow = lax.broadcasted_iota(jnp.int32, fshape, 0)

        def q_loop(i, first):
            qrows = pl.ds(pl.multiple_of(i * QB, QB), QB)
            q2 = _sb_stack(q_ref[qrows, :].astype(F32), c)
            nkb = (i + 1) * (QB // KB)

            def scores(j):
                krows = pl.ds(pl.multiple_of(j * KB, KB), KB)
                valid = c["row_minus_col"] > j * KB - i * QB
                lb, l1 = _sb_scores(q2, k_ref[krows, :], scale, valid)
                return jnp.where(valid, lb, -1e30), l1

            def k_cond(st):
                return (st[0] < nkb) & (st[1] > 0)

            def k_body(st):
                t, _, acc, r, lbm, l1 = st
                j = nkb - 1 - t
                nxt = scores(jnp.maximum(j - 1, 0))
                krows = pl.ds(pl.multiple_of(j * KB, KB), KB)
                a = jnp.exp(lbm + r + _split_dot(l1, c["after_excl"]))
                r = r + jnp.sum(l1, axis=-1, keepdims=True)
                alive = (jnp.max(r) > SB_DEAD).astype(jnp.int32)
                return (t + 1, alive, acc + _dot(a, v_ref[krows, :]), r) + nxt

            t, _, acc, r, _, _ = lax.while_loop(
                k_cond, k_body, (jnp.int32(0), jnp.int32(1), jnp.zeros((2 * QB, LANE), F32), jnp.zeros((2 * QB, 1), F32))
                + scores(nkb - 1))
            o_ref[qrows, :] = _sb_unstack(acc, c, QB)
            tot_ref[qrows, :] = _sb_unstack(jnp.broadcast_to(r, (2 * QB, LANE)), c, QB)
            return jnp.where(frow == i, (nkb - t).astype(F32), first)

        first_ref[...] = lax.fori_loop(0, S // QB, q_loop, jnp.zeros(fshape, F32))

    def col(off):
        return pl.BlockSpec((S, LANE), lambda b, p: (b, off + p))

    out = jax.ShapeDtypeStruct((B * S, W), F32)
    return pl.pallas_call(body, grid=(B, P), in_specs=[col(0), col(P), col(2 * P)],
                          out_specs=(col(0), col(0), pl.BlockSpec((None, None) + fshape, lambda b, p: (b, p, 0, 0))),
                          out_shape=(out, out, jax.ShapeDtypeStruct((B, P) + fshape, F32)),
                          compiler_params=_cp(("parallel", "parallel")), name="sb_fwd")(proj, proj, proj)


def _sb_bwd(proj, tot, first, d_cat, B, S):
    W = SEQ_MIX_WIDTH
    P = W // LANE
    QB = min(SB_QB, S)
    KB = min(SB_KB, QB)
    scale = SB_HEAD_DIM ** -0.5

    fshape = _sb_first_shape(S, QB)

    def body(q_ref, k_ref, v_ref, tot_ref, first_ref, do_ref, dq_ref, dk_ref, dv_ref, dk_acc, dv_acc):
        c = _sb_consts(QB, KB)
        frow = lax.broadcasted_iota(jnp.int32, fshape, 0)
        dk_acc[...] = jnp.zeros_like(dk_acc)
        dv_acc[...] = jnp.zeros_like(dv_acc)

        def q_loop(i, carry):
            qrows = pl.ds(pl.multiple_of(i * QB, QB), QB)
            q2 = _sb_stack(q_ref[qrows, :].astype(F32), c)
            do2 = _sb_stack(do_ref[qrows, :].astype(F32), c)
            q2_t = q2.T.astype(_MXU_DTYPE)
            do2_t = do2.T.astype(_MXU_DTYPE)
            tot = tot_ref[qrows, :]
            total = jnp.concatenate(
                [jnp.sum(jnp.where(c["lane"] == h * SB_HEAD_DIM, tot, 0.0), axis=-1, keepdims=True) for h in range(2)],
                axis=0)

            nkb = (i + 1) * (QB // KB)
            j0 = jnp.clip(jnp.max(jnp.where(frow == i, first_ref[...], 0.0)).astype(jnp.int32), 0, nkb - 1)

            def scores(j):
                krows = pl.ds(pl.multiple_of(j * KB, KB), KB)
                valid = c["row_minus_col"] > j * KB - i * QB
                lb, l1 = _sb_scores(q2, k_ref[krows, :], scale, valid)
                return jnp.where(valid, lb, -1e30), l1, _dot(do2, v_ref[krows, :], 1, 1)

            def k_loop(j, st):
                dq_acc, p_l1, p_g, lbm, l1, da = st
                nxt = scores(jnp.minimum(j + 1, nkb - 1))
                krows = pl.ds(pl.multiple_of(j * KB, KB), KB)
                tail = total - p_l1 - _split_dot(l1, c["upto_incl"])
                a = jnp.exp(lbm + tail)
                g = da * a
                g_before = p_g + _dot(g, c["upto_excl"])
                sig = jnp.exp(lbm)
                dz = ((g * (1.0 - sig) - g_before * sig) * scale).astype(_MXU_DTYPE)
                dk_acc[j] += _dot(q2_t, dz)
                dv_acc[j] += _dot(do2_t, a)
                return (dq_acc + _dot(dz, k_ref[krows, :]), p_l1 + jnp.sum(l1, axis=-1, keepdims=True),
                        p_g + jnp.sum(g, axis=-1, keepdims=True)) + nxt

            zero_col = jnp.zeros((2 * QB, 1), F32)
            dq2 = lax.fori_loop(j0, nkb, k_loop, (jnp.zeros((2 * QB, LANE), F32), zero_col, zero_col) + scores(j0))[0]
            dq_ref[qrows, :] = _sb_unstack(dq2, c, QB).astype(dq_ref.dtype)
            return carry

        lax.fori_loop(0, S // QB, q_loop, 0)
        for j in range(S // KB):
            dk_ref[j * KB:(j + 1) * KB, :] = dk_acc[j].T.astype(dk_ref.dtype)
            dv_ref[j * KB:(j + 1) * KB, :] = dv_acc[j].T.astype(dv_ref.dtype)

    def col(off):
        return pl.BlockSpec((S, LANE), lambda b, p: (b, off + p))

    out = jax.ShapeDtypeStruct((B * S, W), _MXU_DTYPE)
    return pl.pallas_call(body, grid=(B, P),
                          in_specs=[col(0), col(P), col(2 * P), col(0),
                                    pl.BlockSpec((None, None) + fshape, lambda b, p: (b, p, 0, 0)), col(0)],
                          out_specs=(col(0),) * 3, out_shape=(out,) * 3,
                          scratch_shapes=[pltpu.VMEM((S // KB, LANE, KB), F32), pltpu.VMEM((S // KB, LANE, KB), F32)],
                          compiler_params=_cp(("parallel", "parallel")), name="sb_bwd")(proj, proj, proj, tot, first, d_cat)


def _mem_fn(q, k, v):
    lane = lax.broadcasted_iota(jnp.int32, (1, X_WIDTH), 1)
    out = jnp.zeros(q.shape, F32)
    for h in range(N_X_HEADS):
        hm = (lane // X_HEAD_DIM) == h
        s = _dot(jnp.where(hm, q, 0.0), k, 1, 1) * (X_HEAD_DIM ** -0.5)
        e = jnp.exp(s - lax.stop_gradient(jnp.max(s, axis=-1, keepdims=True)))
        p = e / jnp.sum(e, axis=-1, keepdims=True)
        out = out + jnp.where(hm, _dot(p, v), 0.0)
    return out


def _mem_specs(S, q_col):
    ts = _pick(S, 1024)
    ns = S // ts
    qs = pl.BlockSpec((ts, X_WIDTH), lambda b, i: (b * ns + i, q_col // X_WIDTH))
    ks = pl.BlockSpec((N_MEM, X_WIDTH), lambda b, i: (b, 0))
    vs = pl.BlockSpec((N_MEM, X_WIDTH), lambda b, i: (b, 1))
    os = pl.BlockSpec((ts, X_WIDTH), lambda b, i: (b * ns + i, 0))
    return ts, ns, qs, ks, vs, os


def _mem_fwd(proj, q_col, mem_kv, B, S, name):
    ts, ns, qs, ks, vs, os = _mem_specs(S, q_col)

    def body(q_ref, k_ref, v_ref, o_ref):
        o_ref[...] = _mem_fn(q_ref[...].astype(F32), k_ref[...].astype(F32), v_ref[...].astype(F32))

    return pl.pallas_call(body, grid=(B, ns), in_specs=[qs, ks, vs], out_specs=os,
                          out_shape=jax.ShapeDtypeStruct((B * S, X_WIDTH), F32),
                          compiler_params=_cp(("parallel", "parallel")), name=name)(proj, mem_kv, mem_kv)


def _mem_bwd(proj, q_col, mem_kv, d_cat, B, S, name):
    ts, ns, qs, ks, vs, os = _mem_specs(S, q_col)

    def body(q_ref, k_ref, v_ref, do_ref, dq_ref, dk_ref, dv_ref):
        _, vjp = jax.vjp(_mem_fn, q_ref[...].astype(F32), k_ref[...].astype(F32), v_ref[...].astype(F32))
        dq, dk, dv = vjp(do_ref[...].astype(F32))
        dq_ref[...] = dq.astype(dq_ref.dtype)

        @pl.when(pl.program_id(1) == 0)
        def _():
            dk_ref[...] = jnp.zeros_like(dk_ref)
            dv_ref[...] = jnp.zeros_like(dv_ref)

        dk_ref[...] += dk
        dv_ref[...] += dv

    dos = pl.BlockSpec((ts, X_WIDTH), lambda b, i: (b * ns + i, SEQ_MIX_WIDTH // X_WIDTH))
    dq, dk, dv = pl.pallas_call(
        body, grid=(B, ns), in_specs=[qs, ks, vs, dos],
        out_specs=(os, pl.BlockSpec((N_MEM, X_WIDTH), lambda b, i: (b, 0)), pl.BlockSpec((N_MEM, X_WIDTH), lambda b, i: (b, 0))),
        out_shape=(jax.ShapeDtypeStruct((B * S, X_WIDTH), _MXU_DTYPE),
                   jax.ShapeDtypeStruct((B * N_MEM, X_WIDTH), F32), jax.ShapeDtypeStruct((B * N_MEM, X_WIDTH), F32)),
        compiler_params=_cp(("parallel", "arbitrary")), name=name)(proj, mem_kv, mem_kv, d_cat)
    return dq, dk, dv


def _peers():
    x, y, c = lax.axis_index("x"), lax.axis_index("y"), lax.axis_index("c")
    me = 4 * x + 2 * y + c
    out = []
    for fx, fy, fc in [(0, 0, 1), (1, 0, 0), (0, 1, 0), (1, 1, 0), (1, 0, 1), (0, 1, 1), (1, 1, 1)]:
        px, py, pc = x ^ fx, y ^ fy, c ^ fc
        out.append(((px, py, pc), 4 * px + 2 * py + pc))
    return me, out


ANY = pl.BlockSpec(memory_space=pl.ANY)


def _remote(src, dst, send_sems, recv_sems, k, dev):
    return pltpu.make_async_remote_copy(src_ref=src, dst_ref=dst, send_sem=send_sems.at[k], recv_sem=recv_sems.at[k],
                                        device_id=dev, device_id_type=pl.DeviceIdType.MESH)


def _place():
    x, y, c = lax.axis_index("x"), lax.axis_index("y"), lax.axis_index("c")
    return x, y, c, [(1 - x, y), (x, 1 - y), (1 - x, 1 - y)]


def _all_gather(shard):
    R = shard.shape[0]

    def body(x_ref, o_ref, send_sems, recv_sems, local_sem):
        x, y, c, chips = _place()
        me, sibling = (x, y, c), (x, y, 1 - c)

        def slot(px, py, pc):
            return o_ref.at[4 * px + 2 * py + pc]

        def copy(k, block, to, src=None):
            return _remote(slot(*block) if src is None else src, slot(*block), send_sems, recv_sems, k, to)

        mine = pltpu.make_async_copy(x_ref, slot(*me), local_sem)
        mine.start()
        first = [copy(0, me, sibling, src=x_ref)] + [copy(1 + j, me, (*chip, c), src=x_ref) for j, chip in enumerate(chips)]
        for cp in first:
            cp.start()
        passed = [copy(4 + j, (*chip, c), sibling) for j, chip in enumerate(chips)]
        for j, chip in enumerate(chips):
            copy(1 + j, (*chip, c), me).wait_recv()
            passed[j].start()
        copy(0, sibling, me).wait_recv()
        for j, chip in enumerate(chips):
            copy(4 + j, (*chip, 1 - c), me).wait_recv()
        for cp in first + passed:
            cp.wait_send()
        mine.wait()

    return pl.pallas_call(body, in_specs=[ANY], out_specs=ANY,
                          out_shape=jax.ShapeDtypeStruct((N_DEV, R, LANE), shard.dtype),
                          scratch_shapes=[pltpu.SemaphoreType.DMA((7,)), pltpu.SemaphoreType.DMA((7,)),
                                          pltpu.SemaphoreType.DMA],
                          compiler_params=pltpu.CompilerParams(has_side_effects=True),
                          name="all_gather_weights")(shard)


N_CHIP = 4


def _exchange_sibling(big):
    R = big.shape[1]

    def body(b_ref, o_ref, send_sems, recv_sems):
        x, y, c, _ = _place()
        copies = [_remote(b_ref.at[2 * k + (1 - c)], o_ref.at[k], send_sems, recv_sems, k, (x, y, 1 - c))
                  for k in range(N_CHIP)]
        for cp in copies:
            cp.start()
        for cp in copies:
            cp.wait()

    return pl.pallas_call(body, in_specs=[ANY], out_specs=ANY,
                          out_shape=jax.ShapeDtypeStruct((N_CHIP, R, LANE), big.dtype),
                          scratch_shapes=[pltpu.SemaphoreType.DMA((N_CHIP,)), pltpu.SemaphoreType.DMA((N_CHIP,))],
                          compiler_params=pltpu.CompilerParams(has_side_effects=True),
                          name="exchange_sibling")(big)


def _partial_sum(g4, recv):
    R = g4.shape[2]
    tr = _pick(R, 6400)

    def body(g_ref, r_ref, pw_ref, po_ref):
        x, y, c, _ = _place()
        g = jnp.where(c == 0, g_ref[0], g_ref[1]).astype(F32) + r_ref[...].astype(F32)
        pw_ref[...] = g.astype(pw_ref.dtype)

        @pl.when(pl.program_id(1) == 2 * x + y)
        def _():
            po_ref[...] = g

    return pl.pallas_call(body, grid=(R // tr, N_CHIP),
                          in_specs=[pl.BlockSpec((None, 2, tr, LANE), lambda i, k: (k, 0, i, 0)),
                                    pl.BlockSpec((None, tr, LANE), lambda i, k: (k, i, 0))],
                          out_specs=(pl.BlockSpec((None, tr, LANE), lambda i, k: (k, i, 0)),
                                     pl.BlockSpec((tr, LANE), lambda i, k: (i, 0))),
                          out_shape=(jax.ShapeDtypeStruct((N_CHIP, R, LANE), recv.dtype),
                                     jax.ShapeDtypeStruct((R, LANE), F32)),
                          compiler_params=_cp(("parallel", "arbitrary")), name="partial_sum")(g4, recv)


def _exchange_chips(part, small):
    R = part.shape[1]
    K = small.shape[0]

    def body(p_ref, s_ref, ob_ref, os_ref, send_sems, recv_sems, local_sems):
        x, y, c, chips = _place()
        my_chip = 2 * x + y
        me, peers = _peers()
        own_b = pltpu.make_async_copy(p_ref.at[my_chip], ob_ref.at[my_chip], local_sems.at[0])
        own_s = pltpu.make_async_copy(s_ref, os_ref.at[me], local_sems.at[1])
        own_b.start()
        own_s.start()
        copies = [_remote(p_ref.at[2 * px + py], ob_ref.at[my_chip], send_sems, recv_sems, j, (px, py, c))
                  for j, (px, py) in enumerate(chips)]
        copies += [_remote(s_ref, os_ref.at[me], send_sems, recv_sems, 3 + k, dev) for k, (dev, _) in enumerate(peers)]
        for cp in copies:
            cp.start()
        for j, (px, py) in enumerate(chips):
            _remote(p_ref.at[my_chip], ob_ref.at[2 * px + py], send_sems, recv_sems, j, (px, py, c)).wait_recv()
        for k, (dev, idx) in enumerate(peers):
            _remote(s_ref, os_ref.at[idx], send_sems, recv_sems, 3 + k, dev).wait_recv()
        for cp in copies:
            cp.wait_send()
        own_b.wait()
        own_s.wait()

    return pl.pallas_call(body, in_specs=[ANY, ANY], out_specs=(ANY, ANY),
                          out_shape=(jax.ShapeDtypeStruct((N_CHIP, R, LANE), part.dtype),
                                     jax.ShapeDtypeStruct((N_DEV, K, LANE), small.dtype)),
                          scratch_shapes=[pltpu.SemaphoreType.DMA((10,)), pltpu.SemaphoreType.DMA((10,)),
                                          pltpu.SemaphoreType.DMA((2,))],
                          compiler_params=pltpu.CompilerParams(has_side_effects=True),
                          name="exchange_chips")(part, small)


def _adamw_math(w, g, m, v):
    m = ADAM_B1 * m + (1.0 - ADAM_B1) * g
    v = ADAM_B2 * v + (1.0 - ADAM_B2) * (g * g)
    m_hat = m / (1.0 - ADAM_B1 ** ADAM_STEP)
    v_hat = v / (1.0 - ADAM_B2 ** ADAM_STEP)
    delta = -ADAM_LR * (m_hat / (jnp.sqrt(v_hat) + ADAM_EPS) + ADAM_WD * w)
    return delta, m, v


def _adamw_shard(own, recv, w, m, v):
    R = own.shape[0]
    tr = _pick(R, 1024)

    def body(own_ref, recv_ref, w_ref, m_ref, v_ref, g_ref, d_ref, nm_ref, nv_ref):
        x, y, _, _ = _place()
        g = own_ref[...]
        for k in range(N_CHIP):
            g = g + jnp.where(k == 2 * x + y, 0.0, recv_ref[k].astype(F32))
        delta, nm, nv = _adamw_math(w_ref[...], g, m_ref[...], v_ref[...])
        g_ref[...] = g
        d_ref[...] = delta
        nm_ref[...] = nm
        nv_ref[...] = nv

    row = pl.BlockSpec((tr, LANE), lambda i: (i, 0))
    out = jax.ShapeDtypeStruct((R, LANE), F32)
    return pl.pallas_call(body, grid=(R // tr,),
                          in_specs=[row, pl.BlockSpec((N_CHIP, tr, LANE), lambda i: (0, i, 0)), row, row, row],
                          out_specs=(row,) * 4, out_shape=(out,) * 4,
                          compiler_params=_cp(("parallel",)), name="adamw_shard")(own, recv, w, m, v)


def _adamw_replicated(parts, w, m, v):
    K = w.shape[0]

    def body(p_ref, w_ref, m_ref, v_ref, g_ref, d_ref, nm_ref, nv_ref):
        g = p_ref[0]
        for p in range(1, N_DEV):
            g = g + p_ref[p]
        delta, nm, nv = _adamw_math(w_ref[...], g, m_ref[...], v_ref[...])
        g_ref[...] = g
        d_ref[...] = delta
        nm_ref[...] = nm
        nv_ref[...] = nv

    out = jax.ShapeDtypeStruct((K, LANE), F32)
    return pl.pallas_call(body, out_shape=(out,) * 4, compiler_params=_cp(), name="adamw_replicated")(parts, w, m, v)


_SHARDED = (("w_in_a", 1), ("conv_w_a", 2), ("w_in_b", 2), ("w_mem_kv", 1), ("w_out", 1), ("w_up", 2), ("w_down", 1))
_REPLICATED = ("mem_norm", "norm_pre_mix", "norm_post_mix", "norm_pre_mlp", "norm_post_mlp", "a_log_a", "dt_bias_a", "onorm_a")
ROW_ALIGN = 16


def _rows(n_elems, align=ROW_ALIGN):
    r = -(-n_elems // LANE)
    return -(-r // align) * align


def _pack(arrays, align=ROW_ALIGN, total=None, lead=()):
    parts = []
    for a in arrays:
        n = math.prod(a.shape[len(lead):])
        flat = a.reshape(lead + (n,))
        r = _rows(n, align)
        flat = jnp.pad(flat, [(0, 0)] * len(lead) + [(0, r * LANE - n)])
        parts.append(flat.reshape(lead + (r, LANE)))
    used = sum(part.shape[len(lead)] for part in parts)
    if total is not None and used < total:
        parts.append(jnp.zeros(lead + (total - used, LANE), parts[0].dtype))
    return jnp.concatenate(parts, axis=len(lead))


def _unpack(flat, shapes, align=ROW_ALIGN, lead=()):
    outs = []
    r0 = 0
    for shp in shapes:
        n = math.prod(shp)
        r = _rows(n, align)
        part = lax.slice_in_dim(flat, r0, r0 + r, axis=len(lead))
        part = part.reshape(lead + (r * LANE,))
        part = lax.slice_in_dim(part, 0, n, axis=len(lead))
        outs.append(part.reshape(lead + tuple(shp)))
        r0 += r
    return outs


def _to_full(gathered, axis):
    g = jnp.moveaxis(gathered, 0, axis)
    shp = g.shape
    return g.reshape(shp[:axis] + (shp[axis] * shp[axis + 1],) + shp[axis + 2:])


def _to_blocks(full, axis):
    shp = full.shape
    g = full.reshape(shp[:axis] + (N_DEV, shp[axis] // N_DEV) + shp[axis + 1:])
    return jnp.moveaxis(g, axis, 0)


def _widen_in_a(w):
    main = w[:, :4 * SEQ_MIX_WIDTH]
    small = w[:, 4 * SEQ_MIX_WIDTH:4 * SEQ_MIX_WIDTH + 2 * N_LIN_HEADS]
    memq = w[:, 4 * SEQ_MIX_WIDTH + 2 * N_LIN_HEADS:]
    pad = jnp.zeros((w.shape[0], IN_A_PAD - IN_A), w.dtype)
    return jnp.concatenate([main, memq, small, pad], axis=1)


def _narrow_in_a(g):
    main = g[:, :4 * SEQ_MIX_WIDTH]
    memq = g[:, 4 * SEQ_MIX_WIDTH:4 * SEQ_MIX_WIDTH + X_WIDTH]
    small = g[:, SM_COL:SM_COL + 2 * N_LIN_HEADS]
    return jnp.concatenate([main, small, memq], axis=1)


def _row128(v):
    return jnp.pad(v.reshape(1, -1), ((0, 0), (0, LANE - v.shape[-1])))


def _local_step(x, mem, target, p):
    B, S, D = x.shape
    T = B * S
    md = _MXU_DTYPE
    x0 = x.reshape(T, D)
    tgt = target.reshape(T, D)
    memf = mem.reshape(B * N_MEM, D)
    vec = lambda a: a.reshape(1, -1)

    mem_n = _norm_fwd(memf, vec(p["mem_norm"]), out_dtype=md, name="norm_mem")
    w_in = [_widen_in_a(p["w_in_a"][0]), p["w_in_b"][0]]
    memq_col = [4 * SEQ_MIX_WIDTH, 3 * SEQ_MIX_WIDTH]
    alog = _row128(p["a_log_a"][0])
    dtb = _row128(p["dt_bias_a"][0])
    onorm = vec(p["onorm_a"][0])
    conv_w = p["conv_w_a"][0]
    saved = []
    xi = x0
    h1 = _norm_fwd(xi, vec(p["norm_pre_mix"][0]), out_dtype=md, name="norm_pre_mix0")
    for i in range(2):
        s = {"x_in": xi}
        proj = _mm(h1, w_in[i], out_dtypes=(F32 if i == 0 else md,), name=f"in_proj{i}")
        mem_kv = _mm(mem_n, p["w_mem_kv"][i], out_dtypes=(md,), name=f"mem_kv{i}")
        if i == 0:
            act = _gdn_conv_fwd(proj, conv_w, B, S)
            beta, gc = _gdn_gates_fwd(proj, alog, dtb, B, S)
            u, w, qd, kd, intra = _gdn_prep_fwd(act, beta, gc, B, S)
            o, states = _gdn_scan_fwd(u, w, qd, kd, intra, gc, B, S)
            mix = _gdn_post_fwd(o, proj, onorm, T)
            s.update(act=act, beta=beta, gc=gc, u=u, w=w, qd=qd, kd=kd, intra=intra, o=o, states=states)
        else:
            mix, tot, first = _sb_fwd(proj, B, S)
            s.update(tot=tot, first=first)
        cross = _mem_fwd(proj, memq_col[i], mem_kv, B, S, name=f"mem_fwd{i}")
        cat = jnp.concatenate([mix.astype(md), cross.astype(md)], axis=1)
        y = _mm(cat, p["w_out"][i], name=f"out_proj{i}")
        x_mid, h2 = _norm_pair_fwd(y, vec(p["norm_post_mix"][i]), xi, vec(p["norm_pre_mlp"][i]), md,
                                   name=f"norm_post_mix_pre_mlp{i}")
        a_act = _mm(h2, p["w_up"][i], out_dtypes=(md,), epilogue=_relu2_epilogue, name=f"up_proj{i}")
        y2 = _mm(a_act, p["w_down"][i], name=f"down_proj{i}")
        s.update(h1=h1, proj=proj, mem_kv=mem_kv, cat=cat, y=y, x_mid=x_mid, h2=h2, a_act=a_act, y2=y2)
        saved.append(s)
        if i == 0:
            xi, h1 = _norm_pair_fwd(y2, vec(p["norm_post_mlp"][0]), x_mid, vec(p["norm_pre_mix"][1]), md,
                                    name="norm_post_mlp0_pre_mix1")
        else:
            xi = _norm_fwd(y2, vec(p["norm_post_mlp"][1]), resid=x_mid, name="norm_post_mlp1")

    loss_row, dx = _loss_head(xi, tgt)

    g = {}
    d_mem_n = None
    gn = {k: [None, None] for k in ("norm_pre_mix", "norm_post_mix", "norm_pre_mlp", "norm_post_mlp")}
    g_w_mem_kv, g_w_out, g_w_up, g_w_down = [None, None], [None, None], [None, None], [None, None]
    d_y2, gn["norm_post_mlp"][1] = _norm_bwd(dx, saved[1]["y2"], vec(p["norm_post_mlp"][1]), name="norm_post_mlp_bwd1")
    for i in (1, 0):
        s = saved[i]
        g_w_down[i] = _mm(s["a_act"], d_y2, ta=True, out_dtypes=(_WIRE_DTYPE,), name=f"down_proj_dw{i}")
        d_u = _mm(d_y2, p["w_down"][i], tb=True, out_dtypes=(md,), epilogue=_drelu2_epilogue, extras=(s["a_act"],),
                  name=f"down_proj_dx{i}")
        g_w_up[i] = _mm(s["h2"], d_u, ta=True, out_dtypes=(_WIRE_DTYPE,), name=f"up_proj_dw{i}")
        d_h2 = _mm(d_u, p["w_up"][i], tb=True, name=f"up_proj_dx{i}")
        dx, d_y, gn["norm_pre_mlp"][i], gn["norm_post_mix"][i] = _norm_pair_bwd(
            d_h2, s["x_mid"], vec(p["norm_pre_mlp"][i]), dx, s["y"], vec(p["norm_post_mix"][i]),
            name=f"norm_pre_mlp_post_mix_bwd{i}")
        g_w_out[i] = _mm(s["cat"], d_y, ta=True, out_dtypes=(_WIRE_DTYPE,), name=f"out_proj_dw{i}")
        d_cat = _mm(d_y, p["w_out"][i], tb=True, name=f"out_proj_dx{i}")
        d_memq, d_mk, d_mv = _mem_bwd(s["proj"], memq_col[i], s["mem_kv"], d_cat, B, S, name=f"mem_bwd{i}")
        d_mem_kv = jnp.concatenate([d_mk.astype(md), d_mv.astype(md)], axis=1)
        g_w_mem_kv[i] = _mm(mem_n, d_mem_kv, ta=True, out_dtypes=(_WIRE_DTYPE,), name=f"mem_kv_dw{i}")
        d_mn = _mm(d_mem_kv, p["w_mem_kv"][i], tb=True, name=f"mem_kv_dx{i}")
        d_mem_n = d_mn if d_mem_n is None else d_mem_n + d_mn
        if i == 0:
            d_o, d_gate, g["onorm_a"] = _gdn_post_bwd(d_cat, s["o"], s["proj"], onorm, T)
            du, dw, dqd, dkd, da, dgc_s = _gdn_scan_bwd(s["u"], s["w"], s["qd"], s["kd"], s["intra"], s["gc"],
                                                         s["states"], d_o, B, S)
            dq, dk, dv, d_beta, d_gc = _gdn_prep_bwd(s["act"], s["beta"], s["gc"], du, dw, dqd, dkd, da, B, S)
            d_qkv, g["conv_w_a"] = _gdn_conv_bwd(dq, dk, dv, s["proj"], conv_w, B, S)
            d_sm, g["a_log_a"], g["dt_bias_a"] = _gdn_gates_bwd(d_beta, d_gc + dgc_s, s["proj"], alog, dtb, B, S)
            pad = jnp.zeros((T, IN_A_PAD - SM_COL - LANE), md)
            d_proj = jnp.concatenate([d_qkv, d_gate, d_memq, d_sm, pad], axis=1)
        else:
            dq, dk, dv = _sb_bwd(s["proj"], s["tot"], s["first"], d_cat, B, S)
            d_proj = jnp.concatenate([dq, dk, dv, d_memq], axis=1)
        g_w_in = _mm(s["h1"], d_proj, ta=True, out_dtypes=(_WIRE_DTYPE,), name=f"in_proj_dw{i}")
        d_h1 = _mm(d_proj, w_in[i], tb=True, name=f"in_proj_dx{i}")
        if i == 0:
            dx, gn["norm_pre_mix"][0] = _norm_bwd(d_h1, s["x_in"], vec(p["norm_pre_mix"][0]), resid=dx,
                                                  name="norm_pre_mix_bwd0")
            g["w_in_a"] = _narrow_in_a(g_w_in)[None]
        else:
            dx, d_y2, gn["norm_pre_mix"][1], gn["norm_post_mlp"][0] = _norm_pair_bwd(
                d_h1, s["x_in"], vec(p["norm_pre_mix"][1]), dx, saved[0]["y2"], vec(p["norm_post_mlp"][0]),
                name="norm_pre_mix1_post_mlp0_bwd")
            g["w_in_b"] = g_w_in[None]
    _, g_mem_norm = _norm_bwd(d_mem_n, memf, vec(p["mem_norm"]), name="norm_mem_bwd")
    g["mem_norm"] = g_mem_norm.reshape(-1)
    for k, v in gn.items():
        g[k] = jnp.concatenate(v, axis=0)
    g["w_mem_kv"] = jnp.stack(g_w_mem_kv)
    g["w_out"] = jnp.stack(g_w_out)
    g["w_up"] = jnp.stack(g_w_up)
    g["w_down"] = jnp.stack(g_w_down)
    g["conv_w_a"] = g["conv_w_a"][None]
    g["a_log_a"] = g["a_log_a"][:, :N_LIN_HEADS]
    g["dt_bias_a"] = g["dt_bias_a"][:, :N_LIN_HEADS]
    return loss_row, dx.reshape(B, S, D), g


def kernel(x, mem, mem_norm, norm_pre_mix, norm_post_mix, norm_pre_mlp, norm_post_mlp, w_in_a, conv_w_a, a_log_a, dt_bias_a, onorm_a, w_in_b, w_mem_kv, w_out, w_up, w_down, loss_target, m_mem_norm, m_norm_pre_mix, m_norm_post_mix, m_norm_pre_mlp, m_norm_post_mlp, m_w_in_a, m_conv_w_a, m_a_log_a, m_dt_bias_a, m_onorm_a, m_w_in_b, m_w_mem_kv, m_w_out, m_w_up, m_w_down, v_mem_norm, v_norm_pre_mix, v_norm_post_mix, v_norm_pre_mlp, v_norm_post_mlp, v_w_in_a, v_conv_w_a, v_a_log_a, v_dt_bias_a, v_onorm_a, v_w_in_b, v_w_mem_kv, v_w_out, v_w_up, v_w_down):
    args = dict(locals())
    big_names = [n for n, _ in _SHARDED]
    axes = dict(_SHARDED)
    shard_shapes = [args[n].shape for n in big_names]
    rows_total = -(-sum(_rows(math.prod(s)) for s in shard_shapes) // 1024) * 1024

    exact = ("conv_w_a",) if _WIRE_DTYPE != F32 else ()
    wire = [lax.bitcast_convert_type(args[n], _WIRE_DTYPE) if n in exact else args[n].astype(_WIRE_DTYPE)
            for n in big_names]
    wire_shapes = [a.shape for a in wire]
    wire_rows = -(-sum(_rows(math.prod(s)) for s in wire_shapes) // ROW_ALIGN) * ROW_ALIGN
    gathered = _all_gather(_pack(wire, total=wire_rows))
    p = {}
    for n, b in zip(big_names, _unpack(gathered, wire_shapes, lead=(N_DEV,))):
        b = lax.bitcast_convert_type(b, F32) if n in exact else b.astype(_MXU_DTYPE)
        p[n] = _to_full(b, axes[n])
    for n in _REPLICATED:
        p[n] = args[n]
    w_flat = _pack([args[n] for n in big_names], total=rows_total)

    loss_row, grad_x, g = _local_step(x, mem, loss_target, p)

    g_blocks = _pack([_to_blocks(g[n], axes[n]).astype(_WIRE_DTYPE) for n in big_names], total=rows_total, lead=(N_DEV,))
    rep_shapes = [args[n].shape for n in _REPLICATED] + [loss_row.shape]
    g_small = _pack([g[n] for n in _REPLICATED] + [loss_row], align=8)
    recv_sib = _exchange_sibling(g_blocks)
    part, own = _partial_sum(g_blocks.reshape(N_CHIP, 2, rows_total, LANE), recv_sib)
    recv_big, recv_small = _exchange_chips(part, g_small)

    m_flat = _pack([args["m_" + n] for n in big_names], total=rows_total)
    v_flat = _pack([args["v_" + n] for n in big_names], total=rows_total)
    outs_big = [_unpack(f, shard_shapes) for f in _adamw_shard(own, recv_big, w_flat, m_flat, v_flat)]
    no_state = [jnp.zeros_like(loss_row)]
    outs_small = [_unpack(f, rep_shapes, align=8) for f in _adamw_replicated(
        recv_small, _pack([args[n] for n in _REPLICATED] + no_state, align=8),
        _pack([args["m_" + n] for n in _REPLICATED] + no_state, align=8),
        _pack([args["v_" + n] for n in _REPLICATED] + no_state, align=8))]
    loss = outs_small[0][-1][0, 0]

    order = ["mem_norm", "norm_pre_mix", "norm_post_mix", "norm_pre_mlp", "norm_post_mlp", "w_in_a", "conv_w_a",
             "a_log_a", "dt_bias_a", "onorm_a", "w_in_b", "w_mem_kv", "w_out", "w_up", "w_down"]
    result = [loss, grad_x]
    for kind in range(4):
        for n in order:
            if n in axes:
                result.append(outs_big[kind][big_names.index(n)])
            else:
                result.append(outs_small[kind][_REPLICATED.index(n)])
    return tuple(result)
```

```python
import math

import jax
import jax.numpy as jnp
from jax import lax
from jax.experimental import pallas as pl
from jax.experimental.pallas import tpu as pltpu

F32 = jnp.float32
_MXU_DTYPE = jnp.bfloat16
_WIRE_DTYPE = jnp.bfloat16
_HI = lax.Precision.HIGH

D_MODEL = 1024
N_DEV = 8
N_MEM = 256
X_WIDTH = 256
N_X_HEADS = 4
X_HEAD_DIM = 64
SEQ_MIX_WIDTH = 768
LIN_HEAD_DIM = 128
N_LIN_HEADS = 6
CONV_WIDTH = 4
CHUNK = 64
SB_HEAD_DIM = 64
D_FF = 4096
EPS = 1e-6
IN_A = 3340
IN_A_PAD = 3584
IN_B = 2560
SM_COL = 3328

ADAM_LR = 0.001
ADAM_B1 = 0.9
ADAM_B2 = 0.999
ADAM_EPS = 1e-08
ADAM_WD = 0.01
ADAM_STEP = 10

LANE = 128
VMEM_LIMIT = 56 * 1024 * 1024


def _cp(sem=None):
    return pltpu.CompilerParams(dimension_semantics=sem, vmem_limit_bytes=VMEM_LIMIT)


def _pick(n, target):
    if n <= target:
        return n
    best = None
    for t in range(LANE, target + 1, LANE):
        if n % t == 0:
            best = t
    assert best is not None, (n, target)
    return best


def _dot(a, b, ca=1, cb=0):
    return lax.dot_general(a.astype(_MXU_DTYPE), b.astype(_MXU_DTYPE), (((ca,), (cb,)), ((), ())),
                           preferred_element_type=F32)


def _bdot(a, b, ca, cb):
    return lax.dot_general(a.astype(_MXU_DTYPE), b.astype(_MXU_DTYPE), (((ca,), (cb,)), ((0,), (0,))),
                           preferred_element_type=F32)


def _bdot_hi(a, b):
    return lax.dot_general(a, b, (((2,), (1,)), ((0,), (0,))), precision=_HI, preferred_element_type=F32)


def _sigmoid(x):
    return 1.0 / (1.0 + jnp.exp(-x))


def _silu(x):
    return x * _sigmoid(x)


def _softplus(x):
    return jnp.maximum(x, 0.0) + jnp.log(1.0 + jnp.exp(-jnp.abs(x)))


def _rms(x, g):
    return x * lax.rsqrt(jnp.mean(x * x, axis=-1, keepdims=True) + EPS) * g


def _norm_fwd(x, g, resid=None, out_dtype=F32, name="norm_fwd"):
    T, D = x.shape
    tm = _pick(T, 512)
    has_resid = resid is not None

    def body(*refs):
        if has_resid:
            x_ref, g_ref, r_ref, o_ref = refs
        else:
            x_ref, g_ref, o_ref = refs
        y = _rms(x_ref[...].astype(F32), g_ref[...])
        if has_resid:
            y = r_ref[...] + y
        o_ref[...] = y.astype(out_dtype)

    row = pl.BlockSpec((tm, D), lambda i: (i, 0))
    in_specs = [row, pl.BlockSpec((1, D), lambda i: (0, 0))] + ([row] if has_resid else [])
    args = (x, g) + ((resid,) if has_resid else ())
    return pl.pallas_call(body, grid=(T // tm,), in_specs=in_specs, out_specs=row,
                          out_shape=jax.ShapeDtypeStruct((T, D), out_dtype),
                          compiler_params=_cp(("parallel",)), name=name)(*args)


def _norm_bwd(dy, x, g, resid=None, name="norm_bwd"):
    T, D = x.shape
    tm = _pick(T, 512)
    has_resid = resid is not None

    def body(*refs):
        if has_resid:
            dy_ref, x_ref, g_ref, r_ref, dx_ref, dg_ref = refs
        else:
            dy_ref, x_ref, g_ref, dx_ref, dg_ref = refs
        _, vjp = jax.vjp(_rms, x_ref[...].astype(F32), g_ref[...])
        dx, dg = vjp(dy_ref[...].astype(F32))
        if has_resid:
            dx = r_ref[...] + dx
        dx_ref[...] = dx

        @pl.when(pl.program_id(0) == 0)
        def _():
            dg_ref[...] = jnp.zeros_like(dg_ref)

        dg_ref[...] += dg

    row = pl.BlockSpec((tm, D), lambda i: (i, 0))
    vec = pl.BlockSpec((1, D), lambda i: (0, 0))
    in_specs = [row, row, vec] + ([row] if has_resid else [])
    args = (dy, x, g) + ((resid,) if has_resid else ())
    return pl.pallas_call(body, grid=(T // tm,), in_specs=in_specs, out_specs=(row, vec),
                          out_shape=(jax.ShapeDtypeStruct((T, D), F32), jax.ShapeDtypeStruct((1, D), F32)),
                          compiler_params=_cp(("arbitrary",)), name=name)(*args)


def _norm_pair_fwd(y, g_post, resid, g_pre, out_dtype, name):
    T, D = y.shape
    tm = _pick(T, 512)

    def body(y_ref, gp_ref, r_ref, gn_ref, x_ref, h_ref):
        x = r_ref[...] + _rms(y_ref[...], gp_ref[...])
        x_ref[...] = x
        h_ref[...] = _rms(x, gn_ref[...]).astype(out_dtype)

    row = pl.BlockSpec((tm, D), lambda i: (i, 0))
    vec = pl.BlockSpec((1, D), lambda i: (0, 0))
    return pl.pallas_call(body, grid=(T // tm,), in_specs=[row, vec, row, vec], out_specs=(row, row),
                          out_shape=(jax.ShapeDtypeStruct((T, D), F32), jax.ShapeDtypeStruct((T, D), out_dtype)),
                          compiler_params=_cp(("parallel",)), name=name)(y, g_post, resid, g_pre)


def _norm_pair_bwd(d_h, x, g_pre, dx_resid, y, g_post, name):
    T, D = x.shape
    tm = _pick(T, 512)

    def body(dh_ref, x_ref, gn_ref, r_ref, y_ref, gp_ref, dx_ref, dy_ref, dgn_ref, dgp_ref):
        _, vjp_pre = jax.vjp(_rms, x_ref[...], gn_ref[...])
        dx, dgn = vjp_pre(dh_ref[...].astype(F32))
        dx = r_ref[...] + dx
        _, vjp_post = jax.vjp(_rms, y_ref[...], gp_ref[...])
        dy, dgp = vjp_post(dx)
        dx_ref[...] = dx
        dy_ref[...] = dy

        @pl.when(pl.program_id(0) == 0)
        def _():
            dgn_ref[...] = jnp.zeros_like(dgn_ref)
            dgp_ref[...] = jnp.zeros_like(dgp_ref)

        dgn_ref[...] += dgn
        dgp_ref[...] += dgp

    row = pl.BlockSpec((tm, D), lambda i: (i, 0))
    vec = pl.BlockSpec((1, D), lambda i: (0, 0))
    big = jax.ShapeDtypeStruct((T, D), F32)
    small = jax.ShapeDtypeStruct((1, D), F32)
    return pl.pallas_call(body, grid=(T // tm,), in_specs=[row, row, vec, row, row, vec],
                          out_specs=(row, row, vec, vec), out_shape=(big, big, small, small),
                          compiler_params=_cp(("arbitrary",)), name=name)(d_h, x, g_pre, dx_resid, y, g_post)


def _mm(a, b, *, ta=False, tb=False, out_dtypes=(F32,), epilogue=None, extras=(), name="mm",
        tm_t=1024, tn_t=1024, tk_t=1024):
    M, K = (a.shape[1], a.shape[0]) if ta else a.shape
    N = b.shape[0] if tb else b.shape[1]
    assert (b.shape[1] if tb else b.shape[0]) == K, (a.shape, b.shape, ta, tb)
    tm, tn, tk = _pick(M, tm_t), _pick(N, tn_t), _pick(K, tk_t)
    nk = K // tk
    n_extra = len(extras)
    n_out = len(out_dtypes)

    def body(*refs):
        a_ref, b_ref = refs[0], refs[1]
        e_refs = refs[2:2 + n_extra]
        o_refs = refs[2 + n_extra:2 + n_extra + n_out]

        def finish(acc):
            outs = (acc,) if epilogue is None else epilogue(acc, *[e[...] for e in e_refs])
            for o_ref, o in zip(o_refs, outs):
                o_ref[...] = o.astype(o_ref.dtype)

        d = _dot(a_ref[...], b_ref[...], 0 if ta else 1, 1 if tb else 0)
        if nk == 1:
            finish(d)
            return
        acc_ref = refs[-1]
        k = pl.program_id(2)

        @pl.when(k == 0)
        def _():
            acc_ref[...] = d

        @pl.when((k > 0) & (k < nk - 1))
        def _():
            acc_ref[...] += d

        @pl.when(k == nk - 1)
        def _():
            finish(acc_ref[...] + d)

    a_spec = pl.BlockSpec((tk, tm), lambda i, j, k: (k, i)) if ta else pl.BlockSpec((tm, tk), lambda i, j, k: (i, k))
    b_spec = pl.BlockSpec((tn, tk), lambda i, j, k: (j, k)) if tb else pl.BlockSpec((tk, tn), lambda i, j, k: (k, j))
    o_spec = pl.BlockSpec((tm, tn), lambda i, j, k: (i, j))
    outs = pl.pallas_call(
        body, grid=(M // tm, N // tn, nk),
        in_specs=[a_spec, b_spec] + [o_spec] * n_extra,
        out_specs=tuple([o_spec] * n_out),
        out_shape=tuple(jax.ShapeDtypeStruct((M, N), dt) for dt in out_dtypes),
        scratch_shapes=[pltpu.VMEM((tm, tn), F32)] if nk > 1 else [],
        compiler_params=_cp(("parallel", "parallel", "arbitrary")), name=name)(a, b, *extras)
    return outs[0] if n_out == 1 else outs


def _relu2_epilogue(acc):
    r = jnp.maximum(acc, 0.0)
    return r * r, r


def _drelu2_epilogue(acc, r):
    return (acc * (2.0 * r.astype(F32)),)


def _loss_head(x, target, name="loss_head"):
    T, D = x.shape
    tm = _pick(T, 512)

    def body(x_ref, t_ref, l_ref, dx_ref):
        e = x_ref[...] - t_ref[...]
        dx_ref[...] = e * (1.0 / D)

        @pl.when(pl.program_id(0) == 0)
        def _():
            l_ref[...] = jnp.zeros_like(l_ref)

        part = 0.5 * jnp.sum(jnp.mean(e * e, axis=-1, keepdims=True), axis=0, keepdims=True)
        l_ref[...] += jnp.broadcast_to(part, l_ref.shape)

    row = pl.BlockSpec((tm, D), lambda i: (i, 0))
    return pl.pallas_call(body, grid=(T // tm,), in_specs=[row, row],
                          out_specs=(pl.BlockSpec((1, LANE), lambda i: (0, 0)), row),
                          out_shape=(jax.ShapeDtypeStruct((1, LANE), F32), jax.ShapeDtypeStruct((T, D), F32)),
                          compiler_params=_cp(("arbitrary",)), name=name)(x, target)


def _shift_down(x, k, row):
    return jnp.where(row >= k, pltpu.roll(x, k, 0), 0.0)


def _shift_up(x, k, row, n):
    return jnp.where(row < n - k, pltpu.roll(x, n - k, 0), 0.0)


def _conv_taps(x, w, row):
    y = x * w[CONV_WIDTH - 1:CONV_WIDTH, :]
    for i in range(CONV_WIDTH - 1):
        y = y + _shift_down(x, CONV_WIDTH - 1 - i, row) * w[i:i + 1, :]
    return y


def _qkv_act(xc, j):
    s = _silu(xc)
    n = s * lax.rsqrt(jnp.sum(s * s, axis=-1, keepdims=True) + EPS)
    n = n * jnp.where(j < N_LIN_HEADS, LIN_HEAD_DIM ** -0.5, 1.0)
    return jnp.where(j < 2 * N_LIN_HEADS, n, s)


def _gdn_conv_fwd(proj, conv_w, B, S):
    nblk = 3 * N_LIN_HEADS

    def body(p_ref, w_ref, o_ref):
        j = pl.program_id(1)
        x = p_ref[...]
        row = lax.broadcasted_iota(jnp.int32, x.shape, 0)
        o_ref[...] = _qkv_act(_conv_taps(x, w_ref[...], row), j)

    blk = pl.BlockSpec((S, LANE), lambda b, j: (b, j))
    return pl.pallas_call(body, grid=(B, nblk),
                          in_specs=[blk, pl.BlockSpec((CONV_WIDTH, LANE), lambda b, j: (0, j))],
                          out_specs=blk, out_shape=jax.ShapeDtypeStruct((B * S, nblk * LANE), F32),
                          compiler_params=_cp(("parallel", "parallel")), name="gdn_conv_fwd")(proj, conv_w)


def _gdn_conv_bwd(dq, dk, dv, proj, conv_w, B, S):
    nblk = 3 * N_LIN_HEADS
    H = N_LIN_HEADS

    def body(dq_ref, dk_ref, dv_ref, p_ref, w_ref, dp_ref, dw_ref):
        j = pl.program_id(0)
        b = pl.program_id(1)
        x = p_ref[...]
        w = w_ref[...]
        row = lax.broadcasted_iota(jnp.int32, x.shape, 0)
        d_act = jnp.where(j < H, dq_ref[...], jnp.where(j < 2 * H, dk_ref[...], dv_ref[...]))
        _, vjp = jax.vjp(lambda t: _qkv_act(t, j), _conv_taps(x, w, row))
        (d_xc,) = vjp(d_act)
        dx = d_xc * w[CONV_WIDTH - 1:CONV_WIDTH, :]
        for i in range(CONV_WIDTH - 1):
            dx = dx + _shift_up(d_xc, CONV_WIDTH - 1 - i, row, S) * w[i:i + 1, :]
        dp_ref[...] = dx.astype(dp_ref.dtype)

        @pl.when(b == 0)
        def _():
            dw_ref[...] = jnp.zeros_like(dw_ref)

        for i in range(CONV_WIDTH):
            xs = x if i == CONV_WIDTH - 1 else _shift_down(x, CONV_WIDTH - 1 - i, row)
            dw_ref[i:i + 1, :] += jnp.sum(d_xc * xs, axis=0, keepdims=True)

    blk = pl.BlockSpec((S, LANE), lambda j, b: (b, j))
    wblk = pl.BlockSpec((CONV_WIDTH, LANE), lambda j, b: (0, j))
    return pl.pallas_call(
        body, grid=(nblk, B),
        in_specs=[pl.BlockSpec((S, LANE), lambda j, b: (b, jnp.clip(j, 0, H - 1))),
                  pl.BlockSpec((S, LANE), lambda j, b: (b, jnp.clip(j - H, 0, H - 1))),
                  pl.BlockSpec((S, LANE), lambda j, b: (b, jnp.clip(j - 2 * H, 0, H - 1))),
                  blk, wblk],
        out_specs=(blk, wblk),
        out_shape=(jax.ShapeDtypeStruct((B * S, nblk * LANE), _MXU_DTYPE),
                   jax.ShapeDtypeStruct((CONV_WIDTH, nblk * LANE), F32)),
        compiler_params=_cp(("parallel", "arbitrary")), name="gdn_conv_bwd")(dq, dk, dv, proj, conv_w)


def _chunk_cumsum(x, row):
    pos = row % CHUNK
    k = 1
    while k < CHUNK:
        x = x + jnp.where(pos >= k, pltpu.roll(x, k, 0), 0.0)
        k *= 2
    return x


def _chunk_rev_cumsum(x, row, n):
    pos = row % CHUNK
    k = 1
    while k < CHUNK:
        x = x + jnp.where(pos < CHUNK - k, pltpu.roll(x, n - k, 0), 0.0)
        k *= 2
    return x


def _gdn_gates_fwd(proj, a_log, dt_bias, B, S):
    H = N_LIN_HEADS

    def body(sm_ref, al_ref, dt_ref, beta_ref, gc_ref):
        sm = sm_ref[...]
        row = lax.broadcasted_iota(jnp.int32, (S, LANE), 0)
        for h in range(H):
            beta = _sigmoid(sm[:, h:h + 1])
            g = -jnp.exp(al_ref[0:1, h:h + 1]) * _softplus(sm[:, H + h:H + h + 1] + dt_ref[0:1, h:h + 1])
            beta_ref[:, h * LANE:(h + 1) * LANE] = jnp.broadcast_to(beta, (S, LANE))
            gc_ref[:, h * LANE:(h + 1) * LANE] = _chunk_cumsum(jnp.broadcast_to(g, (S, LANE)), row)

    vec = pl.BlockSpec((1, LANE), lambda b: (0, 0))
    wide = pl.BlockSpec((S, H * LANE), lambda b: (b, 0))
    return pl.pallas_call(body, grid=(B,),
                          in_specs=[pl.BlockSpec((S, LANE), lambda b: (b, SM_COL // LANE)), vec, vec],
                          out_specs=(wide, wide),
                          out_shape=(jax.ShapeDtypeStruct((B * S, H * LANE), F32),) * 2,
                          compiler_params=_cp(("parallel",)), name="gdn_gates_fwd")(proj, a_log, dt_bias)


def _gdn_gates_bwd(d_beta, d_gc, proj, a_log, dt_bias, B, S):
    H = N_LIN_HEADS

    def body(db_ref, dgc_ref, sm_ref, al_ref, dt_ref, dsm_ref, dal_ref, ddt_ref):
        sm = sm_ref[...]
        row = lax.broadcasted_iota(jnp.int32, (S, LANE), 0)
        lane = lax.broadcasted_iota(jnp.int32, (1, LANE), 1)
        dsm = jnp.zeros((S, LANE), F32)
        dal = jnp.zeros((1, LANE), F32)
        ddt = jnp.zeros((1, LANE), F32)
        for h in range(H):
            beta = _sigmoid(sm[:, h:h + 1])
            dbeta = jnp.sum(db_ref[:, h * LANE:(h + 1) * LANE], axis=-1, keepdims=True)
            d_bl = dbeta * beta * (1.0 - beta)
            dgc = jnp.sum(dgc_ref[:, h * LANE:(h + 1) * LANE], axis=-1, keepdims=True)
            dg = _chunk_rev_cumsum(jnp.broadcast_to(dgc, (S, LANE)), row, S)[:, 0:1]
            z = sm[:, H + h:H + h + 1] + dt_ref[0:1, h:h + 1]
            a = jnp.exp(al_ref[0:1, h:h + 1])
            g = -a * _softplus(z)
            d_al = dg * (-a) * _sigmoid(z)
            dsm = dsm + jnp.where(lane == h, d_bl, 0.0) + jnp.where(lane == H + h, d_al, 0.0)
            ddt = ddt + jnp.where(lane == h, jnp.sum(d_al, axis=0, keepdims=True), 0.0)
            dal = dal + jnp.where(lane == h, jnp.sum(dg * g, axis=0, keepdims=True), 0.0)
        dsm_ref[...] = dsm.astype(dsm_ref.dtype)

        @pl.when(pl.program_id(0) == 0)
        def _():
            dal_ref[...] = jnp.zeros_like(dal_ref)
            ddt_ref[...] = jnp.zeros_like(ddt_ref)

        dal_ref[...] += dal
        ddt_ref[...] += ddt

    vec = pl.BlockSpec((1, LANE), lambda b: (0, 0))
    wide = pl.BlockSpec((S, H * LANE), lambda b: (b, 0))
    return pl.pallas_call(body, grid=(B,),
                          in_specs=[wide, wide, pl.BlockSpec((S, LANE), lambda b: (b, SM_COL // LANE)), vec, vec],
                          out_specs=(pl.BlockSpec((S, LANE), lambda b: (b, 0)), vec, vec),
                          out_shape=(jax.ShapeDtypeStruct((B * S, LANE), _MXU_DTYPE),
                                     jax.ShapeDtypeStruct((1, LANE), F32), jax.ShapeDtypeStruct((1, LANE), F32)),
                          compiler_params=_cp(("arbitrary",)), name="gdn_gates_bwd")(d_beta, d_gc, proj, a_log, dt_bias)


PREP_ROWS = 1024


@jax.custom_vjp
def _unit_lower_inverse(lower):
    n, C, _ = lower.shape
    ri = lax.broadcasted_iota(jnp.int32, (C, C), 0)
    ci = lax.broadcasted_iota(jnp.int32, (C, C), 1)
    p = -lower
    inv = jnp.where((ri == ci)[None], 1.0, 0.0) + p
    for _ in range(int(math.log2(C)) - 1):
        p = _bdot_hi(p, p)
        inv = inv + _bdot_hi(inv, p)
    return inv


def _unit_lower_inverse_fwd(lower):
    inv = _unit_lower_inverse(lower)
    return inv, inv


def _unit_lower_inverse_bwd(inv, d_inv):
    inv_t = jnp.swapaxes(inv, 1, 2)
    return (-_bdot_hi(_bdot_hi(inv_t, d_inv), inv_t),)


_unit_lower_inverse.defvjp(_unit_lower_inverse_fwd, _unit_lower_inverse_bwd)


def _prep_fn(q, k, v, beta, gc):
    R = q.shape[0]
    n = R // CHUNK
    q3, k3, v3, b3, g3 = [t.reshape(n, CHUNK, LIN_HEAD_DIM) for t in (q, k, v, beta, gc)]
    ri = lax.broadcasted_iota(jnp.int32, (CHUNK, CHUNK), 0)
    ci = lax.broadcasted_iota(jnp.int32, (CHUNK, CHUNK), 1)
    causal = (ri >= ci)[None]
    strict = (ri > ci)[None]
    gcol = g3[:, :, 0:1]
    grow = jnp.swapaxes(g3, 1, 2)[:, 0:1, :]
    decay = jnp.exp(jnp.where(causal, gcol - grow, -1e30))
    kb = k3 * b3
    lower = jnp.where(strict, _bdot(kb, k3, 2, 2) * decay, 0.0)
    inv = _unit_lower_inverse(lower)
    eg = jnp.exp(g3)
    sol = _bdot_hi(inv, jnp.concatenate([v3 * b3, kb * eg], axis=-1))
    u, w = sol[..., :LIN_HEAD_DIM], sol[..., LIN_HEAD_DIM:]
    intra = _bdot(q3, k3, 2, 2) * decay
    q_dec = q3 * eg
    k_dec = k3 * jnp.exp(g3[:, CHUNK - 1:CHUNK, :] - g3)
    return (u.reshape(R, LIN_HEAD_DIM), w.reshape(R, LIN_HEAD_DIM), q_dec.reshape(R, LIN_HEAD_DIM),
            k_dec.reshape(R, LIN_HEAD_DIM), intra.reshape(R, CHUNK))


def _prep_specs(S):
    H = N_LIN_HEADS
    R = min(PREP_ROWS, S)
    nr = S // R

    def col(off):
        return pl.BlockSpec((R, LANE), lambda b, h, r: (b * nr + r, off + h))

    intra = pl.BlockSpec((None, R, CHUNK), lambda b, h, r: (h, b * nr + r, 0))
    return R, nr, col, intra


def _gdn_prep_fwd(act, beta, gc, B, S):
    H = N_LIN_HEADS
    R, nr, col, intra_spec = _prep_specs(S)
    T = B * S

    def body(q_ref, k_ref, v_ref, b_ref, g_ref, u_ref, w_ref, qd_ref, kd_ref, a_ref):
        u, w, qd, kd, a = _prep_fn(q_ref[...], k_ref[...], v_ref[...], b_ref[...], g_ref[...])
        u_ref[...] = u
        w_ref[...] = w
        qd_ref[...] = qd
        kd_ref[...] = kd
        a_ref[...] = a

    wide = jax.ShapeDtypeStruct((T, H * LANE), F32)
    return pl.pallas_call(body, grid=(B, H, nr),
                          in_specs=[col(0), col(H), col(2 * H), col(0), col(0)],
                          out_specs=(col(0), col(0), col(0), col(0), intra_spec),
                          out_shape=(wide, wide, wide, wide, jax.ShapeDtypeStruct((H, T, CHUNK), F32)),
                          compiler_params=_cp(("parallel", "parallel", "parallel")),
                          name="gdn_prep_fwd")(act, act, act, beta, gc)


def _gdn_prep_bwd(act, beta, gc, du, dw, dqd, dkd, da, B, S):
    H = N_LIN_HEADS
    R, nr, col, intra_spec = _prep_specs(S)
    T = B * S

    def body(q_ref, k_ref, v_ref, b_ref, g_ref, du_ref, dw_ref, dqd_ref, dkd_ref, da_ref,
             dq_ref, dk_ref, dv_ref, db_ref, dg_ref):
        _, vjp = jax.vjp(_prep_fn, q_ref[...], k_ref[...], v_ref[...], b_ref[...], g_ref[...])
        dq, dk, dv, db, dg = vjp((du_ref[...], dw_ref[...], dqd_ref[...], dkd_ref[...], da_ref[...]))
        dq_ref[...] = dq
        dk_ref[...] = dk
        dv_ref[...] = dv
        db_ref[...] = db
        dg_ref[...] = dg

    wide = jax.ShapeDtypeStruct((T, H * LANE), F32)
    return pl.pallas_call(body, grid=(B, H, nr),
                          in_specs=[col(0), col(H), col(2 * H), col(0), col(0),
                                    col(0), col(0), col(0), col(0), intra_spec],
                          out_specs=(col(0),) * 5, out_shape=(wide,) * 5,
                          compiler_params=_cp(("parallel", "parallel", "parallel")),
                          name="gdn_prep_bwd")(act, act, act, beta, gc, du, dw, dqd, dkd, da)


def _scan_step(u, w, qd, kd, a, g_last, state):
    v_new = u - _dot(w, state)
    o = _dot(qd, state) + _dot(a, v_new)
    new_state = state * jnp.exp(g_last) + _dot(kd, v_new, 0, 0)
    return o, new_state


SCAN_HEADS = 6
SCAN_ROWS = 512


def _scan_specs(B, S, reverse):
    HP = SCAN_HEADS
    R = min(SCAN_ROWS, S)
    nr = S // R

    def blk(r):
        return nr - 1 - r if reverse else r

    col = pl.BlockSpec((R, HP * LANE), lambda b, h, r: (b * nr + blk(r), h))
    intra = pl.BlockSpec((HP, R, CHUNK), lambda b, h, r: (h, b * nr + blk(r), 0))
    st = pl.BlockSpec((None, HP, R // CHUNK, LIN_HEAD_DIM, LIN_HEAD_DIM), lambda b, h, r: (b, h, blk(r), 0, 0))
    return R, nr, col, intra, st


def _gdn_scan_fwd(u, w, qd, kd, a, gc, B, S):
    H = N_LIN_HEADS
    R, nr, col, intra, st = _scan_specs(B, S, reverse=False)

    def body(u_ref, w_ref, qd_ref, kd_ref, a_ref, g_ref, o_ref, st_ref, carry_ref):
        @pl.when(pl.program_id(2) == 0)
        def _():
            carry_ref[...] = jnp.zeros_like(carry_ref)

        def step(c, states):
            rows = pl.ds(pl.multiple_of(c * CHUNK, CHUNK), CHUNK)
            new_states = []
            for hh, state in enumerate(states):
                cols = slice(hh * LANE, (hh + 1) * LANE)
                st_ref[hh, c] = state.astype(st_ref.dtype)
                o, new_state = _scan_step(u_ref[rows, cols], w_ref[rows, cols], qd_ref[rows, cols], kd_ref[rows, cols],
                                          a_ref[hh, rows, :], g_ref[rows, cols][CHUNK - 1:CHUNK, :], state)
                o_ref[rows, cols] = o
                new_states.append(new_state)
            return tuple(new_states)

        states = lax.fori_loop(0, R // CHUNK, step, tuple(carry_ref[hh] for hh in range(SCAN_HEADS)))
        for hh, state in enumerate(states):
            carry_ref[hh] = state

    return pl.pallas_call(body, grid=(B, H // SCAN_HEADS, nr), in_specs=[col, col, col, col, intra, col],
                          out_specs=(col, st),
                          out_shape=(jax.ShapeDtypeStruct((B * S, H * LANE), F32),
                                     jax.ShapeDtypeStruct((B, H, S // CHUNK, LIN_HEAD_DIM, LIN_HEAD_DIM), _MXU_DTYPE)),
                          scratch_shapes=[pltpu.VMEM((SCAN_HEADS, LIN_HEAD_DIM, LIN_HEAD_DIM), F32)],
                          compiler_params=_cp(("parallel", "parallel", "arbitrary")),
                          name="gdn_scan_fwd")(u, w, qd, kd, a, gc)


def _gdn_scan_bwd(u, w, qd, kd, a, gc, states, do, B, S):
    H = N_LIN_HEADS
    R, nr, col, intra, st = _scan_specs(B, S, reverse=True)
    T = B * S
    n = R // CHUNK

    def body(u_ref, w_ref, qd_ref, kd_ref, a_ref, g_ref, st_ref, do_ref,
             du_ref, dw_ref, dqd_ref, dkd_ref, da_ref, dg_ref, carry_ref):
        last = lax.broadcasted_iota(jnp.int32, (CHUNK, LANE), 0) == CHUNK - 1

        @pl.when(pl.program_id(2) == 0)
        def _():
            carry_ref[...] = jnp.zeros_like(carry_ref)

        def step(i, d_states):
            c = n - 1 - i
            rows = pl.ds(pl.multiple_of(c * CHUNK, CHUNK), CHUNK)
            d_prevs = []
            for hh, d_state in enumerate(d_states):
                cols = slice(hh * LANE, (hh + 1) * LANE)
                _, vjp = jax.vjp(_scan_step, u_ref[rows, cols], w_ref[rows, cols], qd_ref[rows, cols], kd_ref[rows, cols],
                                 a_ref[hh, rows, :], g_ref[rows, cols][CHUNK - 1:CHUNK, :], st_ref[hh, c].astype(F32))
                du, dw, dqd, dkd, da, dgl, d_prev = vjp((do_ref[rows, cols].astype(F32), d_state))
                du_ref[rows, cols] = du
                dw_ref[rows, cols] = dw
                dqd_ref[rows, cols] = dqd
                dkd_ref[rows, cols] = dkd
                da_ref[hh, rows, :] = da
                dg_ref[rows, cols] = jnp.where(last, dgl, 0.0)
                d_prevs.append(d_prev)
            return tuple(d_prevs)

        d_states = lax.fori_loop(0, n, step, tuple(carry_ref[hh] for hh in range(SCAN_HEADS)))
        for hh, d_state in enumerate(d_states):
            carry_ref[hh] = d_state

    wide = jax.ShapeDtypeStruct((T, H * LANE), F32)
    return pl.pallas_call(body, grid=(B, H // SCAN_HEADS, nr), in_specs=[col, col, col, col, intra, col, st, col],
                          out_specs=(col, col, col, col, intra, col),
                          out_shape=(wide, wide, wide, wide, jax.ShapeDtypeStruct((H, T, CHUNK), F32), wide),
                          scratch_shapes=[pltpu.VMEM((SCAN_HEADS, LIN_HEAD_DIM, LIN_HEAD_DIM), F32)],
                          compiler_params=_cp(("parallel", "parallel", "arbitrary")),
                          name="gdn_scan_bwd")(u, w, qd, kd, a, gc, states, do)


GATE_COL = 3 * SEQ_MIX_WIDTH


def _post_fn(o, gate, gain):
    return o * lax.rsqrt(jnp.mean(o * o, axis=-1, keepdims=True) + EPS) * gain * _silu(gate)


def _gdn_post_fwd(o, proj, onorm, T):
    H = N_LIN_HEADS
    tm = _pick(T, 1024)

    def body(o_ref, g_ref, n_ref, y_ref):
        y_ref[...] = _post_fn(o_ref[...], g_ref[...], n_ref[...]).astype(y_ref.dtype)

    col = pl.BlockSpec((tm, LANE), lambda i, h: (i, h))
    return pl.pallas_call(body, grid=(T // tm, H),
                          in_specs=[col, pl.BlockSpec((tm, LANE), lambda i, h: (i, GATE_COL // LANE + h)),
                                    pl.BlockSpec((1, LANE), lambda i, h: (0, 0))],
                          out_specs=col, out_shape=jax.ShapeDtypeStruct((T, H * LANE), _MXU_DTYPE),
                          compiler_params=_cp(("parallel", "parallel")), name="gdn_post_fwd")(o, proj, onorm)


def _gdn_post_bwd(d_cat, o, proj, onorm, T):
    H = N_LIN_HEADS
    tm = _pick(T, 1024)

    def body(dy_ref, o_ref, g_ref, n_ref, do_ref, dg_ref, dn_ref):
        _, vjp = jax.vjp(_post_fn, o_ref[...], g_ref[...], n_ref[...])
        do, dg, dn = vjp(dy_ref[...].astype(F32))
        do_ref[...] = do
        dg_ref[...] = dg.astype(dg_ref.dtype)

        @pl.when((pl.program_id(0) == 0) & (pl.program_id(1) == 0))
        def _():
            dn_ref[...] = jnp.zeros_like(dn_ref)

        dn_ref[...] += dn

    col = pl.BlockSpec((tm, LANE), lambda i, h: (i, h))
    vec = pl.BlockSpec((1, LANE), lambda i, h: (0, 0))
    return pl.pallas_call(body, grid=(T // tm, H),
                          in_specs=[col, col, pl.BlockSpec((tm, LANE), lambda i, h: (i, GATE_COL // LANE + h)), vec],
                          out_specs=(col, col, vec),
                          out_shape=(jax.ShapeDtypeStruct((T, H * LANE), F32),
                                     jax.ShapeDtypeStruct((T, H * LANE), _MXU_DTYPE),
                                     jax.ShapeDtypeStruct((1, LANE), F32)),
                          compiler_params=_cp(("arbitrary", "arbitrary")), name="gdn_post_bwd")(d_cat, o, proj, onorm)


def _log_sigmoid(z):
    return jnp.minimum(z, 0.0) - jnp.log(1.0 + jnp.exp(-jnp.abs(z)))


def _split_dot(x, m):
    hi = x.astype(_MXU_DTYPE)
    lo = x - hi.astype(F32)
    return _dot(hi, m) + _dot(lo, m)


SB_QB = 256
SB_KB = 256


def _sb_scores(q2, k_j, scale, valid):
    z = _dot(q2, k_j, 1, 1) * scale
    lb = _log_sigmoid(z)
    return lb, jnp.where(valid, lb - z, 0.0)


def _sb_consts(QB, KB):
    ri = lax.broadcasted_iota(jnp.int32, (KB, KB), 0)
    ci = lax.broadcasted_iota(jnp.int32, (KB, KB), 1)
    row = lax.broadcasted_iota(jnp.int32, (2 * QB, KB), 0)
    col = lax.broadcasted_iota(jnp.int32, (2 * QB, KB), 1)
    lane = lax.broadcasted_iota(jnp.int32, (1, LANE), 1)
    return {
        "row_minus_col": row % QB - col,
        "after_excl": (ri > ci).astype(_MXU_DTYPE),
        "upto_incl": (ri <= ci).astype(_MXU_DTYPE),
        "upto_excl": (ri < ci).astype(_MXU_DTYPE),
        "lane": lane,
        "head0": lane < SB_HEAD_DIM,
    }


def _sb_stack(x, c):
    return jnp.concatenate([jnp.where(c["head0"], x, 0.0), jnp.where(c["head0"], 0.0, x)], axis=0)


def _sb_unstack(x2, c, QB):
    return jnp.where(c["head0"], x2[:QB], x2[QB:])


SB_DEAD = -110.0


def _sb_first_shape(S, QB):
    return (-(-(S // QB) // 8) * 8, LANE)


def _sb_fwd(proj, B, S):
    W = SEQ_MIX_WIDTH
    P = W // LANE
    QB = min(SB_QB, S)
    KB = min(SB_KB, QB)
    scale = SB_HEAD_DIM ** -0.5
    fshape = _sb_first_shape(S, QB)

    def body(q_ref, k_ref, v_ref, o_ref, tot_ref, first_ref):
        c = _sb_consts(QB, KB)
        frow = lax.broadcasted_iota(jnp.int32, fshape, 0)

        def q_loop(i, first):
            qrows = pl.ds(pl.multiple_of(i * QB, QB), QB)
            q2 = _sb_stack(q_ref[qrows, :].astype(F32), c)
            nkb = (i + 1) * (QB // KB)

            def scores(j):
                krows = pl.ds(pl.multiple_of(j * KB, KB), KB)
                valid = c["row_minus_col"] > j * KB - i * QB
                lb, l1 = _sb_scores(q2, k_ref[krows, :], scale, valid)
                return jnp.where(valid, lb, -1e30), l1

            def k_cond(st):
                return (st[0] < nkb) & (st[1] > 0)

            def k_body(st):
                t, _, acc, r, lbm, l1 = st
                j = nkb - 1 - t
                nxt = scores(jnp.maximum(j - 1, 0))
                krows = pl.ds(pl.multiple_of(j * KB, KB), KB)
                a = jnp.exp(lbm + r + _split_dot(l1, c["after_excl"]))
                r = r + jnp.sum(l1, axis=-1, keepdims=True)
                alive = (jnp.max(r) > SB_DEAD).astype(jnp.int32)
                return (t + 1, alive, acc + _dot(a, v_ref[krows, :]), r) + nxt

            t, _, acc, r, _, _ = lax.while_loop(
                k_cond, k_body, (jnp.int32(0), jnp.int32(1), jnp.zeros((2 * QB, LANE), F32), jnp.zeros((2 * QB, 1), F32))
                + scores(nkb - 1))
            o_ref[qrows, :] = _sb_unstack(acc, c, QB).astype(o_ref.dtype)
            tot_ref[qrows, :] = _sb_unstack(jnp.broadcast_to(r, (2 * QB, LANE)), c, QB)
            return jnp.where(frow == i, (nkb - t).astype(F32), first)

        first_ref[...] = lax.fori_loop(0, S // QB, q_loop, jnp.zeros(fshape, F32))

    def col(off):
        return pl.BlockSpec((S, LANE), lambda b, p: (b, off + p))

    out = jax.ShapeDtypeStruct((B * S, W), F32)
    return pl.pallas_call(body, grid=(B, P), in_specs=[col(0), col(P), col(2 * P)],
                          out_specs=(col(0), col(0), pl.BlockSpec((None, None) + fshape, lambda b, p: (b, p, 0, 0))),
                          out_shape=(jax.ShapeDtypeStruct((B * S, W), _MXU_DTYPE), out,
                                     jax.ShapeDtypeStruct((B, P) + fshape, F32)),
                          compiler_params=_cp(("parallel", "parallel")), name="sb_fwd")(proj, proj, proj)


def _sb_bwd(proj, tot, first, d_cat, B, S):
    W = SEQ_MIX_WIDTH
    P = W // LANE
    QB = min(SB_QB, S)
    KB = min(SB_KB, QB)
    scale = SB_HEAD_DIM ** -0.5

    fshape = _sb_first_shape(S, QB)

    def body(q_ref, k_ref, v_ref, tot_ref, first_ref, do_ref, dq_ref, dk_ref, dv_ref, dk_acc, dv_acc):
        c = _sb_consts(QB, KB)
        frow = lax.broadcasted_iota(jnp.int32, fshape, 0)
        dk_acc[...] = jnp.zeros_like(dk_acc)
        dv_acc[...] = jnp.zeros_like(dv_acc)

        def q_loop(i, carry):
            qrows = pl.ds(pl.multiple_of(i * QB, QB), QB)
            q2 = _sb_stack(q_ref[qrows, :].astype(F32), c)
            do2 = _sb_stack(do_ref[qrows, :].astype(F32), c)
            q2_t = q2.T.astype(_MXU_DTYPE)
            do2_t = do2.T.astype(_MXU_DTYPE)
            tot = tot_ref[qrows, :]
            total = jnp.concatenate(
                [jnp.sum(jnp.where(c["lane"] == h * SB_HEAD_DIM, tot, 0.0), axis=-1, keepdims=True) for h in range(2)],
                axis=0)

            nkb = (i + 1) * (QB // KB)
            j0 = jnp.clip(jnp.max(jnp.where(frow == i, first_ref[...], 0.0)).astype(jnp.int32), 0, nkb - 1)

            def scores(j):
                krows = pl.ds(pl.multiple_of(j * KB, KB), KB)
                valid = c["row_minus_col"] > j * KB - i * QB
                lb, l1 = _sb_scores(q2, k_ref[krows, :], scale, valid)
                return jnp.where(valid, lb, -1e30), l1, _dot(do2, v_ref[krows, :], 1, 1)

            def k_loop(j, st):
                dq_acc, p_l1, p_g, lbm, l1, da = st
                nxt = scores(jnp.minimum(j + 1, nkb - 1))
                krows = pl.ds(pl.multiple_of(j * KB, KB), KB)
                tail = total - p_l1 - _split_dot(l1, c["upto_incl"])
                a = jnp.exp(lbm + tail)
                g = da * a
                g_before = p_g + _dot(g, c["upto_excl"])
                sig = jnp.exp(lbm)
                dz = ((g * (1.0 - sig) - g_before * sig) * scale).astype(_MXU_DTYPE)
                dk_acc[j] += _dot(q2_t, dz)
                dv_acc[j] += _dot(do2_t, a)
                return (dq_acc + _dot(dz, k_ref[krows, :]), p_l1 + jnp.sum(l1, axis=-1, keepdims=True),
                        p_g + jnp.sum(g, axis=-1, keepdims=True)) + nxt

            zero_col = jnp.zeros((2 * QB, 1), F32)
            dq2 = lax.fori_loop(j0, nkb, k_loop, (jnp.zeros((2 * QB, LANE), F32), zero_col, zero_col) + scores(j0))[0]
            dq_ref[qrows, :] = _sb_unstack(dq2, c, QB).astype(dq_ref.dtype)
            return carry

        lax.fori_loop(0, S // QB, q_loop, 0)
        for j in range(S // KB):
            dk_ref[j * KB:(j + 1) * KB, :] = dk_acc[j].T.astype(dk_ref.dtype)
            dv_ref[j * KB:(j + 1) * KB, :] = dv_acc[j].T.astype(dv_ref.dtype)

    def col(off):
        return pl.BlockSpec((S, LANE), lambda b, p: (b, off + p))

    out = jax.ShapeDtypeStruct((B * S, W), _MXU_DTYPE)
    return pl.pallas_call(body, grid=(B, P),
                          in_specs=[col(0), col(P), col(2 * P), col(0),
                                    pl.BlockSpec((None, None) + fshape, lambda b, p: (b, p, 0, 0)), col(0)],
                          out_specs=(col(0),) * 3, out_shape=(out,) * 3,
                          scratch_shapes=[pltpu.VMEM((S // KB, LANE, KB), F32), pltpu.VMEM((S // KB, LANE, KB), F32)],
                          compiler_params=_cp(("parallel", "parallel")), name="sb_bwd")(proj, proj, proj, tot, first, d_cat)


def _mem_fn(q, k, v):
    lane = lax.broadcasted_iota(jnp.int32, (1, X_WIDTH), 1)
    out = jnp.zeros(q.shape, F32)
    for h in range(N_X_HEADS):
        hm = (lane // X_HEAD_DIM) == h
        s = _dot(jnp.where(hm, q, 0.0), k, 1, 1) * (X_HEAD_DIM ** -0.5)
        e = jnp.exp(s - lax.stop_gradient(jnp.max(s, axis=-1, keepdims=True)))
        p = e / jnp.sum(e, axis=-1, keepdims=True)
        out = out + jnp.where(hm, _dot(p, v), 0.0)
    return out


def _mem_specs(S, q_col):
    ts = _pick(S, 1024)
    ns = S // ts
    qs = pl.BlockSpec((ts, X_WIDTH), lambda b, i: (b * ns + i, q_col // X_WIDTH))
    ks = pl.BlockSpec((N_MEM, X_WIDTH), lambda b, i: (b, 0))
    vs = pl.BlockSpec((N_MEM, X_WIDTH), lambda b, i: (b, 1))
    os = pl.BlockSpec((ts, X_WIDTH), lambda b, i: (b * ns + i, 0))
    return ts, ns, qs, ks, vs, os


def _mem_fwd(proj, q_col, mem_kv, B, S, name):
    ts, ns, qs, ks, vs, os = _mem_specs(S, q_col)

    def body(q_ref, k_ref, v_ref, o_ref):
        o_ref[...] = _mem_fn(q_ref[...].astype(F32), k_ref[...].astype(F32), v_ref[...].astype(F32)).astype(o_ref.dtype)

    return pl.pallas_call(body, grid=(B, ns), in_specs=[qs, ks, vs], out_specs=os,
                          out_shape=jax.ShapeDtypeStruct((B * S, X_WIDTH), _MXU_DTYPE),
                          compiler_params=_cp(("parallel", "parallel")), name=name)(proj, mem_kv, mem_kv)


def _mem_bwd(proj, q_col, mem_kv, d_cat, B, S, name):
    ts, ns, qs, ks, vs, os = _mem_specs(S, q_col)

    def body(q_ref, k_ref, v_ref, do_ref, dq_ref, dk_ref, dv_ref):
        _, vjp = jax.vjp(_mem_fn, q_ref[...].astype(F32), k_ref[...].astype(F32), v_ref[...].astype(F32))
        dq, dk, dv = vjp(do_ref[...].astype(F32))
        dq_ref[...] = dq.astype(dq_ref.dtype)

        @pl.when(pl.program_id(1) == 0)
        def _():
            dk_ref[...] = jnp.zeros_like(dk_ref)
            dv_ref[...] = jnp.zeros_like(dv_ref)

        dk_ref[...] += dk
        dv_ref[...] += dv

    dos = pl.BlockSpec((ts, X_WIDTH), lambda b, i: (b * ns + i, SEQ_MIX_WIDTH // X_WIDTH))
    dq, dk, dv = pl.pallas_call(
        body, grid=(B, ns), in_specs=[qs, ks, vs, dos],
        out_specs=(os, pl.BlockSpec((N_MEM, X_WIDTH), lambda b, i: (b, 0)), pl.BlockSpec((N_MEM, X_WIDTH), lambda b, i: (b, 0))),
        out_shape=(jax.ShapeDtypeStruct((B * S, X_WIDTH), _MXU_DTYPE),
                   jax.ShapeDtypeStruct((B * N_MEM, X_WIDTH), F32), jax.ShapeDtypeStruct((B * N_MEM, X_WIDTH), F32)),
        compiler_params=_cp(("parallel", "arbitrary")), name=name)(proj, mem_kv, mem_kv, d_cat)
    return dq, dk, dv


def _peers():
    x, y, c = lax.axis_index("x"), lax.axis_index("y"), lax.axis_index("c")
    me = 4 * x + 2 * y + c
    out = []
    for fx, fy, fc in [(0, 0, 1), (1, 0, 0), (0, 1, 0), (1, 1, 0), (1, 0, 1), (0, 1, 1), (1, 1, 1)]:
        px, py, pc = x ^ fx, y ^ fy, c ^ fc
        out.append(((px, py, pc), 4 * px + 2 * py + pc))
    return me, out


ANY = pl.BlockSpec(memory_space=pl.ANY)


def _remote(src, dst, send_sems, recv_sems, k, dev):
    return pltpu.make_async_remote_copy(src_ref=src, dst_ref=dst, send_sem=send_sems.at[k], recv_sem=recv_sems.at[k],
                                        device_id=dev, device_id_type=pl.DeviceIdType.MESH)


def _place():
    x, y, c = lax.axis_index("x"), lax.axis_index("y"), lax.axis_index("c")
    return x, y, c, [(1 - x, y), (x, 1 - y), (1 - x, 1 - y)]


def _all_gather(shard):
    R = shard.shape[0]

    def body(x_ref, o_ref, send_sems, recv_sems, local_sem):
        x, y, c, chips = _place()
        me, sibling = (x, y, c), (x, y, 1 - c)

        def slot(px, py, pc):
            return o_ref.at[4 * px + 2 * py + pc]

        def copy(k, block, to, src=None):
            return _remote(slot(*block) if src is None else src, slot(*block), send_sems, recv_sems, k, to)

        mine = pltpu.make_async_copy(x_ref, slot(*me), local_sem)
        mine.start()
        first = [copy(0, me, sibling, src=x_ref)] + [copy(1 + j, me, (*chip, c), src=x_ref) for j, chip in enumerate(chips)]
        for cp in first:
            cp.start()
        passed = [copy(4 + j, (*chip, c), sibling) for j, chip in enumerate(chips)]
        for j, chip in enumerate(chips):
            copy(1 + j, (*chip, c), me).wait_recv()
            passed[j].start()
        copy(0, sibling, me).wait_recv()
        for j, chip in enumerate(chips):
            copy(4 + j, (*chip, 1 - c), me).wait_recv()
        for cp in first + passed:
            cp.wait_send()
        mine.wait()

    return pl.pallas_call(body, in_specs=[ANY], out_specs=ANY,
                          out_shape=jax.ShapeDtypeStruct((N_DEV, R, LANE), shard.dtype),
                          scratch_shapes=[pltpu.SemaphoreType.DMA((7,)), pltpu.SemaphoreType.DMA((7,)),
                                          pltpu.SemaphoreType.DMA],
                          compiler_params=pltpu.CompilerParams(has_side_effects=True),
                          name="all_gather_weights")(shard)


N_CHIP = 4


def _exchange_sibling(big):
    R = big.shape[1]

    def body(b_ref, o_ref, send_sems, recv_sems):
        x, y, c, _ = _place()
        copies = [_remote(b_ref.at[2 * k + (1 - c)], o_ref.at[k], send_sems, recv_sems, k, (x, y, 1 - c))
                  for k in range(N_CHIP)]
        for cp in copies:
            cp.start()
        for cp in copies:
            cp.wait()

    return pl.pallas_call(body, in_specs=[ANY], out_specs=ANY,
                          out_shape=jax.ShapeDtypeStruct((N_CHIP, R, LANE), big.dtype),
                          scratch_shapes=[pltpu.SemaphoreType.DMA((N_CHIP,)), pltpu.SemaphoreType.DMA((N_CHIP,))],
                          compiler_params=pltpu.CompilerParams(has_side_effects=True),
                          name="exchange_sibling")(big)


def _partial_sum(g4, recv):
    R = g4.shape[2]
    tr = _pick(R, 6400)

    def body(g_ref, r_ref, pw_ref, po_ref):
        x, y, c, _ = _place()
        g = jnp.where(c == 0, g_ref[0], g_ref[1]).astype(F32) + r_ref[...].astype(F32)
        pw_ref[...] = g.astype(pw_ref.dtype)

        @pl.when(pl.program_id(1) == 2 * x + y)
        def _():
            po_ref[...] = g

    return pl.pallas_call(body, grid=(R // tr, N_CHIP),
                          in_specs=[pl.BlockSpec((None, 2, tr, LANE), lambda i, k: (k, 0, i, 0)),
                                    pl.BlockSpec((None, tr, LANE), lambda i, k: (k, i, 0))],
                          out_specs=(pl.BlockSpec((None, tr, LANE), lambda i, k: (k, i, 0)),
                                     pl.BlockSpec((tr, LANE), lambda i, k: (i, 0))),
                          out_shape=(jax.ShapeDtypeStruct((N_CHIP, R, LANE), recv.dtype),
                                     jax.ShapeDtypeStruct((R, LANE), F32)),
                          compiler_params=_cp(("parallel", "arbitrary")), name="partial_sum")(g4, recv)


def _exchange_chips(part, small):
    R = part.shape[1]
    K = small.shape[0]

    def body(p_ref, s_ref, ob_ref, os_ref, send_sems, recv_sems, local_sems):
        x, y, c, chips = _place()
        my_chip = 2 * x + y
        me, peers = _peers()
        own_b = pltpu.make_async_copy(p_ref.at[my_chip], ob_ref.at[my_chip], local_sems.at[0])
        own_s = pltpu.make_async_copy(s_ref, os_ref.at[me], local_sems.at[1])
        own_b.start()
        own_s.start()
        copies = [_remote(p_ref.at[2 * px + py], ob_ref.at[my_chip], send_sems, recv_sems, j, (px, py, c))
                  for j, (px, py) in enumerate(chips)]
        copies += [_remote(s_ref, os_ref.at[me], send_sems, recv_sems, 3 + k, dev) for k, (dev, _) in enumerate(peers)]
        for cp in copies:
            cp.start()
        for j, (px, py) in enumerate(chips):
            _remote(p_ref.at[my_chip], ob_ref.at[2 * px + py], send_sems, recv_sems, j, (px, py, c)).wait_recv()
        for k, (dev, idx) in enumerate(peers):
            _remote(s_ref, os_ref.at[idx], send_sems, recv_sems, 3 + k, dev).wait_recv()
        for cp in copies:
            cp.wait_send()
        own_b.wait()
        own_s.wait()

    return pl.pallas_call(body, in_specs=[ANY, ANY], out_specs=(ANY, ANY),
                          out_shape=(jax.ShapeDtypeStruct((N_CHIP, R, LANE), part.dtype),
                                     jax.ShapeDtypeStruct((N_DEV, K, LANE), small.dtype)),
                          scratch_shapes=[pltpu.SemaphoreType.DMA((10,)), pltpu.SemaphoreType.DMA((10,)),
                                          pltpu.SemaphoreType.DMA((2,))],
                          compiler_params=pltpu.CompilerParams(has_side_effects=True),
                          name="exchange_chips")(part, small)


def _adamw_math(w, g, m, v):
    m = ADAM_B1 * m + (1.0 - ADAM_B1) * g
    v = ADAM_B2 * v + (1.0 - ADAM_B2) * (g * g)
    m_hat = m / (1.0 - ADAM_B1 ** ADAM_STEP)
    v_hat = v / (1.0 - ADAM_B2 ** ADAM_STEP)
    delta = -ADAM_LR * (m_hat / (jnp.sqrt(v_hat) + ADAM_EPS) + ADAM_WD * w)
    return delta, m, v


def _adamw_shard(own, recv, w, m, v):
    R = own.shape[0]
    tr = _pick(R, 1024)

    def body(own_ref, recv_ref, w_ref, m_ref, v_ref, g_ref, d_ref, nm_ref, nv_ref):
        x, y, _, _ = _place()
        g = own_ref[...]
        for k in range(N_CHIP):
            g = g + jnp.where(k == 2 * x + y, 0.0, recv_ref[k].astype(F32))
        delta, nm, nv = _adamw_math(w_ref[...], g, m_ref[...], v_ref[...])
        g_ref[...] = g
        d_ref[...] = delta
        nm_ref[...] = nm
        nv_ref[...] = nv

    row = pl.BlockSpec((tr, LANE), lambda i: (i, 0))
    out = jax.ShapeDtypeStruct((R, LANE), F32)
    return pl.pallas_call(body, grid=(R // tr,),
                          in_specs=[row, pl.BlockSpec((N_CHIP, tr, LANE), lambda i: (0, i, 0)), row, row, row],
                          out_specs=(row,) * 4, out_shape=(out,) * 4,
                          compiler_params=_cp(("parallel",)), name="adamw_shard")(own, recv, w, m, v)


def _adamw_replicated(parts, w, m, v):
    K = w.shape[0]

    def body(p_ref, w_ref, m_ref, v_ref, g_ref, d_ref, nm_ref, nv_ref):
        g = p_ref[0]
        for p in range(1, N_DEV):
            g = g + p_ref[p]
        delta, nm, nv = _adamw_math(w_ref[...], g, m_ref[...], v_ref[...])
        g_ref[...] = g
        d_ref[...] = delta
        nm_ref[...] = nm
        nv_ref[...] = nv

    out = jax.ShapeDtypeStruct((K, LANE), F32)
    return pl.pallas_call(body, out_shape=(out,) * 4, compiler_params=_cp(), name="adamw_replicated")(parts, w, m, v)


_SHARDED = (("w_in_a", 1), ("conv_w_a", 2), ("w_in_b", 2), ("w_mem_kv", 1), ("w_out", 1), ("w_up", 2), ("w_down", 1))
_REPLICATED = ("mem_norm", "norm_pre_mix", "norm_post_mix", "norm_pre_mlp", "norm_post_mlp", "a_log_a", "dt_bias_a", "onorm_a")
ROW_ALIGN = 16


def _rows(n_elems, align=ROW_ALIGN):
    r = -(-n_elems // LANE)
    return -(-r // align) * align


def _pack(arrays, align=ROW_ALIGN, total=None, lead=()):
    parts = []
    for a in arrays:
        n = math.prod(a.shape[len(lead):])
        flat = a.reshape(lead + (n,))
        r = _rows(n, align)
        flat = jnp.pad(flat, [(0, 0)] * len(lead) + [(0, r * LANE - n)])
        parts.append(flat.reshape(lead + (r, LANE)))
    used = sum(part.shape[len(lead)] for part in parts)
    if total is not None and used < total:
        parts.append(jnp.zeros(lead + (total - used, LANE), parts[0].dtype))
    return jnp.concatenate(parts, axis=len(lead))


def _unpack(flat, shapes, align=ROW_ALIGN, lead=()):
    outs = []
    r0 = 0
    for shp in shapes:
        n = math.prod(shp)
        r = _rows(n, align)
        part = lax.slice_in_dim(flat, r0, r0 + r, axis=len(lead))
        part = part.reshape(lead + (r * LANE,))
        part = lax.slice_in_dim(part, 0, n, axis=len(lead))
        outs.append(part.reshape(lead + tuple(shp)))
        r0 += r
    return outs


def _to_full(gathered, axis):
    g = jnp.moveaxis(gathered, 0, axis)
    shp = g.shape
    return g.reshape(shp[:axis] + (shp[axis] * shp[axis + 1],) + shp[axis + 2:])


def _to_blocks(full, axis):
    shp = full.shape
    g = full.reshape(shp[:axis] + (N_DEV, shp[axis] // N_DEV) + shp[axis + 1:])
    return jnp.moveaxis(g, axis, 0)


def _widen_in_a(w):
    main = w[:, :4 * SEQ_MIX_WIDTH]
    small = w[:, 4 * SEQ_MIX_WIDTH:4 * SEQ_MIX_WIDTH + 2 * N_LIN_HEADS]
    memq = w[:, 4 * SEQ_MIX_WIDTH + 2 * N_LIN_HEADS:]
    pad = jnp.zeros((w.shape[0], IN_A_PAD - IN_A), w.dtype)
    return jnp.concatenate([main, memq, small, pad], axis=1)


def _narrow_in_a(g):
    main = g[:, :4 * SEQ_MIX_WIDTH]
    memq = g[:, 4 * SEQ_MIX_WIDTH:4 * SEQ_MIX_WIDTH + X_WIDTH]
    small = g[:, SM_COL:SM_COL + 2 * N_LIN_HEADS]
    return jnp.concatenate([main, small, memq], axis=1)


def _row128(v):
    return jnp.pad(v.reshape(1, -1), ((0, 0), (0, LANE - v.shape[-1])))


def _local_step(x, mem, target, p):
    B, S, D = x.shape
    T = B * S
    md = _MXU_DTYPE
    x0 = x.reshape(T, D)
    tgt = target.reshape(T, D)
    memf = mem.reshape(B * N_MEM, D)
    vec = lambda a: a.reshape(1, -1)

    mem_n = _norm_fwd(memf, vec(p["mem_norm"]), out_dtype=md, name="norm_mem")
    w_in = [_widen_in_a(p["w_in_a"][0]), p["w_in_b"][0]]
    memq_col = [4 * SEQ_MIX_WIDTH, 3 * SEQ_MIX_WIDTH]
    alog = _row128(p["a_log_a"][0])
    dtb = _row128(p["dt_bias_a"][0])
    onorm = vec(p["onorm_a"][0])
    conv_w = p["conv_w_a"][0]
    saved = []
    xi = x0
    h1 = _norm_fwd(xi, vec(p["norm_pre_mix"][0]), out_dtype=md, name="norm_pre_mix0")
    for i in range(2):
        s = {"x_in": xi}
        proj = _mm(h1, w_in[i], out_dtypes=(F32 if i == 0 else md,), name=f"in_proj{i}")
        mem_kv = _mm(mem_n, p["w_mem_kv"][i], out_dtypes=(md,), name=f"mem_kv{i}")
        if i == 0:
            act = _gdn_conv_fwd(proj, conv_w, B, S)
            beta, gc = _gdn_gates_fwd(proj, alog, dtb, B, S)
            u, w, qd, kd, intra = _gdn_prep_fwd(act, beta, gc, B, S)
            o, states = _gdn_scan_fwd(u, w, qd, kd, intra, gc, B, S)
            mix = _gdn_post_fwd(o, proj, onorm, T)
            s.update(act=act, beta=beta, gc=gc, u=u, w=w, qd=qd, kd=kd, intra=intra, o=o, states=states)
        else:
            mix, tot, first = _sb_fwd(proj, B, S)
            s.update(tot=tot, first=first)
        cross = _mem_fwd(proj, memq_col[i], mem_kv, B, S, name=f"mem_fwd{i}")
        cat = jnp.concatenate([mix.astype(md), cross.astype(md)], axis=1)
        y = _mm(cat, p["w_out"][i], name=f"out_proj{i}")
        x_mid, h2 = _norm_pair_fwd(y, vec(p["norm_post_mix"][i]), xi, vec(p["norm_pre_mlp"][i]), md,
                                   name=f"norm_post_mix_pre_mlp{i}")
        a_act, r = _mm(h2, p["w_up"][i], out_dtypes=(md, md), epilogue=_relu2_epilogue, name=f"up_proj{i}")
        y2 = _mm(a_act, p["w_down"][i], name=f"down_proj{i}")
        s.update(h1=h1, proj=proj, mem_kv=mem_kv, cat=cat, y=y, x_mid=x_mid, h2=h2, a_act=a_act, r=r, y2=y2)
        saved.append(s)
        if i == 0:
            xi, h1 = _norm_pair_fwd(y2, vec(p["norm_post_mlp"][0]), x_mid, vec(p["norm_pre_mix"][1]), md,
                                    name="norm_post_mlp0_pre_mix1")
        else:
            xi = _norm_fwd(y2, vec(p["norm_post_mlp"][1]), resid=x_mid, name="norm_post_mlp1")

    loss_row, dx = _loss_head(xi, tgt)

    g = {}
    d_mem_n = None
    gn = {k: [None, None] for k in ("norm_pre_mix", "norm_post_mix", "norm_pre_mlp", "norm_post_mlp")}
    g_w_mem_kv, g_w_out, g_w_up, g_w_down = [None, None], [None, None], [None, None], [None, None]
    d_y2, gn["norm_post_mlp"][1] = _norm_bwd(dx, saved[1]["y2"], vec(p["norm_post_mlp"][1]), name="norm_post_mlp_bwd1")
    for i in (1, 0):
        s = saved[i]
        g_w_down[i] = _mm(s["a_act"], d_y2, ta=True, out_dtypes=(_WIRE_DTYPE,), name=f"down_proj_dw{i}")
        d_u = _mm(d_y2, p["w_down"][i], tb=True, out_dtypes=(md,), epilogue=_drelu2_epilogue, extras=(s["r"],),
                  name=f"down_proj_dx{i}")
        g_w_up[i] = _mm(s["h2"], d_u, ta=True, out_dtypes=(_WIRE_DTYPE,), name=f"up_proj_dw{i}")
        d_h2 = _mm(d_u, p["w_up"][i], tb=True, name=f"up_proj_dx{i}")
        dx, d_y, gn["norm_pre_mlp"][i], gn["norm_post_mix"][i] = _norm_pair_bwd(
            d_h2, s["x_mid"], vec(p["norm_pre_mlp"][i]), dx, s["y"], vec(p["norm_post_mix"][i]),
            name=f"norm_pre_mlp_post_mix_bwd{i}")
        g_w_out[i] = _mm(s["cat"], d_y, ta=True, out_dtypes=(_WIRE_DTYPE,), name=f"out_proj_dw{i}")
        d_cat = _mm(d_y, p["w_out"][i], tb=True, name=f"out_proj_dx{i}")
        d_memq, d_mk, d_mv = _mem_bwd(s["proj"], memq_col[i], s["mem_kv"], d_cat, B, S, name=f"mem_bwd{i}")
        d_mem_kv = jnp.concatenate([d_mk.astype(md), d_mv.astype(md)], axis=1)
        g_w_mem_kv[i] = _mm(mem_n, d_mem_kv, ta=True, out_dtypes=(_WIRE_DTYPE,), name=f"mem_kv_dw{i}")
        d_mn = _mm(d_mem_kv, p["w_mem_kv"][i], tb=True, name=f"mem_kv_dx{i}")
        d_mem_n = d_mn if d_mem_n is None else d_mem_n + d_mn
        if i == 0:
            d_o, d_gate, g["onorm_a"] = _gdn_post_bwd(d_cat, s["o"], s["proj"], onorm, T)
            du, dw, dqd, dkd, da, dgc_s = _gdn_scan_bwd(s["u"], s["w"], s["qd"], s["kd"], s["intra"], s["gc"],
                                                         s["states"], d_o, B, S)
            dq, dk, dv, d_beta, d_gc = _gdn_prep_bwd(s["act"], s["beta"], s["gc"], du, dw, dqd, dkd, da, B, S)
            d_qkv, g["conv_w_a"] = _gdn_conv_bwd(dq, dk, dv, s["proj"], conv_w, B, S)
            d_sm, g["a_log_a"], g["dt_bias_a"] = _gdn_gates_bwd(d_beta, d_gc + dgc_s, s["proj"], alog, dtb, B, S)
            pad = jnp.zeros((T, IN_A_PAD - SM_COL - LANE), md)
            d_proj = jnp.concatenate([d_qkv, d_gate, d_memq, d_sm, pad], axis=1)
        else:
            dq, dk, dv = _sb_bwd(s["proj"], s["tot"], s["first"], d_cat, B, S)
            d_proj = jnp.concatenate([dq, dk, dv, d_memq], axis=1)
        g_w_in = _mm(s["h1"], d_proj, ta=True, out_dtypes=(_WIRE_DTYPE,), name=f"in_proj_dw{i}")
        d_h1 = _mm(d_proj, w_in[i], tb=True, name=f"in_proj_dx{i}")
        if i == 0:
            dx, gn["norm_pre_mix"][0] = _norm_bwd(d_h1, s["x_in"], vec(p["norm_pre_mix"][0]), resid=dx,
                                                  name="norm_pre_mix_bwd0")
            g["w_in_a"] = _narrow_in_a(g_w_in)[None]
        else:
            dx, d_y2, gn["norm_pre_mix"][1], gn["norm_post_mlp"][0] = _norm_pair_bwd(
                d_h1, s["x_in"], vec(p["norm_pre_mix"][1]), dx, saved[0]["y2"], vec(p["norm_post_mlp"][0]),
                name="norm_pre_mix1_post_mlp0_bwd")
            g["w_in_b"] = g_w_in[None]
    _, g_mem_norm = _norm_bwd(d_mem_n, memf, vec(p["mem_norm"]), name="norm_mem_bwd")
    g["mem_norm"] = g_mem_norm.reshape(-1)
    for k, v in gn.items():
        g[k] = jnp.concatenate(v, axis=0)
    g["w_mem_kv"] = jnp.stack(g_w_mem_kv)
    g["w_out"] = jnp.stack(g_w_out)
    g["w_up"] = jnp.stack(g_w_up)
    g["w_down"] = jnp.stack(g_w_down)
    g["conv_w_a"] = g["conv_w_a"][None]
    g["a_log_a"] = g["a_log_a"][:, :N_LIN_HEADS]
    g["dt_bias_a"] = g["dt_bias_a"][:, :N_LIN_HEADS]
    return loss_row, dx.reshape(B, S, D), g


def kernel(x, mem, mem_norm, norm_pre_mix, norm_post_mix, norm_pre_mlp, norm_post_mlp, w_in_a, conv_w_a, a_log_a, dt_bias_a, onorm_a, w_in_b, w_mem_kv, w_out, w_up, w_down, loss_target, m_mem_norm, m_norm_pre_mix, m_norm_post_mix, m_norm_pre_mlp, m_norm_post_mlp, m_w_in_a, m_conv_w_a, m_a_log_a, m_dt_bias_a, m_onorm_a, m_w_in_b, m_w_mem_kv, m_w_out, m_w_up, m_w_down, v_mem_norm, v_norm_pre_mix, v_norm_post_mix, v_norm_pre_mlp, v_norm_post_mlp, v_w_in_a, v_conv_w_a, v_a_log_a, v_dt_bias_a, v_onorm_a, v_w_in_b, v_w_mem_kv, v_w_out, v_w_up, v_w_down):
    args = dict(locals())
    big_names = [n for n, _ in _SHARDED]
    axes = dict(_SHARDED)
    shard_shapes = [args[n].shape for n in big_names]
    rows_total = -(-sum(_rows(math.prod(s)) for s in shard_shapes) // 1024) * 1024

    exact = ("conv_w_a",) if _WIRE_DTYPE != F32 else ()
    wire = [lax.bitcast_convert_type(args[n], _WIRE_DTYPE) if n in exact else args[n].astype(_WIRE_DTYPE)
            for n in big_names]
    wire_shapes = [a.shape for a in wire]
    wire_rows = -(-sum(_rows(math.prod(s)) for s in wire_shapes) // ROW_ALIGN) * ROW_ALIGN
    gathered = _all_gather(_pack(wire, total=wire_rows))
    p = {}
    for n, b in zip(big_names, _unpack(gathered, wire_shapes, lead=(N_DEV,))):
        b = lax.bitcast_convert_type(b, F32) if n in exact else b.astype(_MXU_DTYPE)
        p[n] = _to_full(b, axes[n])
    for n in _REPLICATED:
        p[n] = args[n]
    w_flat = _pack([args[n] for n in big_names], total=rows_total)

    loss_row, grad_x, g = _local_step(x, mem, loss_target, p)
    loss = lax.psum(loss_row[0, 0], ("x", "y", "c"))

    g_blocks = _pack([_to_blocks(g[n], axes[n]).astype(_WIRE_DTYPE) for n in big_names], total=rows_total, lead=(N_DEV,))
    rep_shapes = [args[n].shape for n in _REPLICATED]
    g_small = _pack([g[n] for n in _REPLICATED], align=8)
    recv_sib = _exchange_sibling(g_blocks)
    part, own = _partial_sum(g_blocks.reshape(N_CHIP, 2, rows_total, LANE), recv_sib)
    recv_big, recv_small = _exchange_chips(part, g_small)

    m_flat = _pack([args["m_" + n] for n in big_names], total=rows_total)
    v_flat = _pack([args["v_" + n] for n in big_names], total=rows_total)
    outs_big = [_unpack(f, shard_shapes) for f in _adamw_shard(own, recv_big, w_flat, m_flat, v_flat)]
    outs_small = [_unpack(f, rep_shapes, align=8) for f in _adamw_replicated(
        recv_small, _pack([args[n] for n in _REPLICATED], align=8),
        _pack([args["m_" + n] for n in _REPLICATED], align=8), _pack([args["v_" + n] for n in _REPLICATED], align=8))]

    order = ["mem_norm", "norm_pre_mix", "norm_post_mix", "norm_pre_mlp", "norm_post_mlp", "w_in_a", "conv_w_a",
             "a_log_a", "dt_bias_a", "onorm_a", "w_in_b", "w_mem_kv", "w_out", "w_up", "w_down"]
    result = [loss, grad_x]
    for kind in range(4):
        for n in order:
            if n in axes:
                result.append(outs_big[kind][big_names.index(n)])
            else:
                result.append(outs_small[kind][_REPLICATED.index(n)])
    return tuple(result)
```
